```python
import jax, jax.numpy as jnp
from jax import lax
import numpy as np

D_MODEL = 2048
BATCH = 8
SEQ = 4096
DEPTH = 4

CHUNK = 64
N_MIXERS = 2
N_CONV_LAYERS = (DEPTH + N_MIXERS - 1) // N_MIXERS
N_SSM_LAYERS = DEPTH // N_MIXERS

CONV_KERNEL = 31

SSM_EXPAND = 2
SSM_D_INNER = SSM_EXPAND * D_MODEL
SSM_HEAD_DIM = 64
SSM_N_HEADS = SSM_D_INNER // SSM_HEAD_DIM
SSM_N_GROUPS = 8
SSM_HEADS_PER_GROUP = SSM_N_HEADS // SSM_N_GROUPS
SSM_D_STATE = 128
SSM_CONV_KERNEL = 4
SSM_CONV_DIM = SSM_D_INNER + 2 * SSM_N_GROUPS * SSM_D_STATE
SSM_IN_DIM = SSM_D_INNER + SSM_CONV_DIM + SSM_N_HEADS

FFN_HIDDEN = 5632
FFN_CONV_KERNEL = 3

RMS_EPS = 1e-6
LN_EPS = 1e-5

kernel_name = "hybrid_conformer_mamba2_convffn_trunk"


def rms_norm(x, g, eps=RMS_EPS):
    xf = x.astype(jnp.float32)
    y = xf * lax.rsqrt(jnp.mean(xf * xf, axis=-1, keepdims=True) + eps)
    return (y * g.astype(jnp.float32)).astype(x.dtype)


def layer_norm(x, g, b, eps=LN_EPS):
    xf = x.astype(jnp.float32)
    mu = jnp.mean(xf, axis=-1, keepdims=True)
    xc = xf - mu
    var = jnp.mean(xc * xc, axis=-1, keepdims=True)
    y = xc * lax.rsqrt(var + eps) * g.astype(jnp.float32) + b.astype(jnp.float32)
    return y.astype(x.dtype)


def causal_depthwise_conv(x, w, b):
    k, c = w.shape
    xp = jnp.pad(x, ((0, 0), (k - 1, 0), (0, 0)))
    y = lax.conv_general_dilated(
        xp, w[:, None, :].astype(x.dtype), window_strides=(1,), padding="VALID",
        dimension_numbers=("NWC", "WIO", "NWC"), feature_group_count=c)
    return y + b.astype(x.dtype)


def conformer_conv_module(h, w_in, b_in, w_dw, b_dw, ln_g, ln_b, w_out, b_out):
    u = h @ w_in + b_in
    a, gate = jnp.split(u, 2, axis=-1)
    v = a * jax.nn.sigmoid(gate)
    v = causal_depthwise_conv(v, w_dw, b_dw)
    v = jax.nn.silu(layer_norm(v, ln_g, ln_b))
    return v @ w_out + b_out


def segsum(a):
    q = a.shape[-1]
    a_rep = jnp.broadcast_to(a[..., :, None], a.shape + (q,))
    strict = jnp.tril(jnp.ones((q, q), dtype=bool), -1)
    ss = jnp.cumsum(jnp.where(strict, a_rep, 0.0), axis=-2)
    return jnp.where(jnp.tril(jnp.ones((q, q), dtype=bool)), ss, -jnp.inf)


def ssd_chunked(x, dt, a_neg, bm, cm):
    bsz, seq, _, _ = x.shape
    nc = seq // CHUNK
    g, r, p, n = SSM_N_GROUPS, SSM_HEADS_PER_GROUP, SSM_HEAD_DIM, SSM_D_STATE
    xd = (x * dt[..., None]).reshape(bsz, nc, CHUNK, g, r, p)
    a = jnp.moveaxis((dt * a_neg).reshape(bsz, nc, CHUNK, g, r), 2, -1)
    bc = bm.reshape(bsz, nc, CHUNK, g, n)
    cc = cm.reshape(bsz, nc, CHUNK, g, n)
    a_cs = jnp.cumsum(a, axis=-1)
    decay_in = jnp.exp(segsum(a))
    cb = jnp.einsum("bclgn,bcsgn->bcgls", cc, bc)
    y_diag = jnp.einsum("bcgls,bcgrls,bcsgrp->bclgrp", cb, decay_in, xd)
    decay_to_end = jnp.exp(a_cs[..., -1:] - a_cs)
    states = jnp.einsum("bcsgn,bcgrs,bcsgrp->bcgrpn", bc, decay_to_end, xd)
    chunk_decay = jnp.exp(a_cs[..., -1])

    def step(carry, inp):
        st, dec = inp
        return carry * dec[..., None, None] + st, carry

    init = jnp.zeros((bsz, g, r, p, n), dtype=x.dtype)
    _, prev = lax.scan(step, init, (jnp.moveaxis(states, 1, 0), jnp.moveaxis(chunk_decay, 1, 0)))
    prev = jnp.moveaxis(prev, 0, 1)
    y_off = jnp.einsum("bclgn,bcgrpn,bcgrl->bclgrp", cc, prev, jnp.exp(a_cs))
    return (y_diag + y_off).reshape(bsz, seq, g * r, p)


def gated_group_rms_norm(y, z, g):
    yf = (y * jax.nn.silu(z)).astype(jnp.float32)
    shp = yf.shape
    yg = yf.reshape(shp[:-1] + (SSM_N_GROUPS, shp[-1] // SSM_N_GROUPS))
    yg = yg * lax.rsqrt(jnp.mean(yg * yg, axis=-1, keepdims=True) + RMS_EPS)
    return (yg.reshape(shp) * g.astype(jnp.float32)).astype(z.dtype)


def mamba2_mixer(h, w_in, w_conv, b_conv, dt_bias, a_log, d_skip, norm_g, w_out):
    bsz, seq, _ = h.shape
    zxbcdt = h @ w_in
    z, xbc, dt = jnp.split(zxbcdt, [SSM_D_INNER, SSM_D_INNER + SSM_CONV_DIM], axis=-1)
    xbc = jax.nn.silu(causal_depthwise_conv(xbc, w_conv, b_conv))
    gn = SSM_N_GROUPS * SSM_D_STATE
    xs, bm, cm = jnp.split(xbc, [SSM_D_INNER, SSM_D_INNER + gn], axis=-1)
    xs = xs.reshape(bsz, seq, SSM_N_HEADS, SSM_HEAD_DIM).astype(jnp.float32)
    bm = bm.reshape(bsz, seq, SSM_N_GROUPS, SSM_D_STATE).astype(jnp.float32)
    cm = cm.reshape(bsz, seq, SSM_N_GROUPS, SSM_D_STATE).astype(jnp.float32)
    dt = jax.nn.softplus(dt.astype(jnp.float32) + dt_bias.astype(jnp.float32))
    a_neg = -jnp.exp(a_log.astype(jnp.float32))
    y = ssd_chunked(xs, dt, a_neg, bm, cm)
    y = y + d_skip.astype(jnp.float32)[:, None] * xs
    y = y.reshape(bsz, seq, SSM_D_INNER).astype(h.dtype)
    return gated_group_rms_norm(y, z, norm_g) @ w_out


def conv_ffn(h, w_up, w_dw, b_dw, w_down):
    u = causal_depthwise_conv(h @ w_up, w_dw, b_dw)
    gate, val = jnp.split(u, 2, axis=-1)
    return (jax.nn.silu(gate) * val) @ w_down


def _fwd_setup_inputs(seed: int = 0) -> dict:
    key = jax.random.key(seed)
    ks = jax.random.split(key, 32)
    d, f = D_MODEL, FFN_HIDDEN
    nc, ns = N_CONV_LAYERS, N_SSM_LAYERS

    def nrm(k, shape, scale):
        return jax.random.normal(k, shape, dtype=jnp.float32) * scale

    dt0 = jnp.exp(jax.random.uniform(ks[16], (ns, SSM_N_HEADS), minval=math_log(1e-3), maxval=math_log(1e-1)))
    return {
        "x": nrm(ks[0], (BATCH, SEQ, d), 1.0),
        "norm_mix_g": 1.0 + nrm(ks[1], (DEPTH, d), 0.05),
        "norm_ffn_g": 1.0 + nrm(ks[2], (DEPTH, d), 0.05),
        "norm_final_g": 1.0 + nrm(ks[3], (d,), 0.05),
        "cv_w_in": nrm(ks[4], (nc, d, 2 * d), d ** -0.5),
        "cv_b_in": nrm(ks[5], (nc, 2 * d), 0.02),
        "cv_w_dw": nrm(ks[6], (nc, CONV_KERNEL, d), CONV_KERNEL ** -0.5),
        "cv_b_dw": nrm(ks[7], (nc, d), 0.02),
        "cv_ln_g": 1.0 + nrm(ks[8], (nc, d), 0.05),
        "cv_ln_b": nrm(ks[9], (nc, d), 0.02),
        "cv_w_out": nrm(ks[10], (nc, d, d), d ** -0.5),
        "cv_b_out": nrm(ks[11], (nc, d), 0.02),
        "ssm_w_in": nrm(ks[12], (ns, d, SSM_IN_DIM), d ** -0.5),
        "ssm_w_conv": nrm(ks[13], (ns, SSM_CONV_KERNEL, SSM_CONV_DIM), SSM_CONV_KERNEL ** -0.5),
        "ssm_b_conv": nrm(ks[14], (ns, SSM_CONV_DIM), 0.02),
        "ssm_dt_bias": dt0 + jnp.log(-jnp.expm1(-dt0)),
        "ssm_a_log": jnp.log(jax.random.uniform(ks[17], (ns, SSM_N_HEADS), minval=1.0, maxval=16.0)),
        "ssm_d": 1.0 + nrm(ks[18], (ns, SSM_N_HEADS), 0.1),
        "ssm_norm_g": 1.0 + nrm(ks[19], (ns, SSM_D_INNER), 0.05),
        "ssm_w_out": nrm(ks[20], (ns, SSM_D_INNER, d), SSM_D_INNER ** -0.5),
        "ffn_w_up": nrm(ks[21], (DEPTH, d, 2 * f), d ** -0.5),
        "ffn_w_dw": nrm(ks[22], (DEPTH, FFN_CONV_KERNEL, 2 * f), FFN_CONV_KERNEL ** -0.5),
        "ffn_b_dw": nrm(ks[23], (DEPTH, 2 * f), 0.02),
        "ffn_w_down": nrm(ks[24], (DEPTH, f, d), f ** -0.5),
    }


def math_log(v):
    return float(np.log(v))


def _fwd_reference(x, norm_mix_g, norm_ffn_g, norm_final_g,
              cv_w_in, cv_b_in, cv_w_dw, cv_b_dw, cv_ln_g, cv_ln_b, cv_w_out, cv_b_out,
              ssm_w_in, ssm_w_conv, ssm_b_conv, ssm_dt_bias, ssm_a_log, ssm_d, ssm_norm_g, ssm_w_out,
              ffn_w_up, ffn_w_dw, ffn_b_dw, ffn_w_down):
    for i in range(DEPTH):
        h = rms_norm(x, norm_mix_g[i])
        j = i // N_MIXERS
        if i % N_MIXERS == 0:
            x = x + conformer_conv_module(h, cv_w_in[j], cv_b_in[j], cv_w_dw[j], cv_b_dw[j],
                                          cv_ln_g[j], cv_ln_b[j], cv_w_out[j], cv_b_out[j])
        else:
            x = x + mamba2_mixer(h, ssm_w_in[j], ssm_w_conv[j], ssm_b_conv[j], ssm_dt_bias[j],
                                 ssm_a_log[j], ssm_d[j], ssm_norm_g[j], ssm_w_out[j])
        x = x + conv_ffn(rms_norm(x, norm_ffn_g[i]), ffn_w_up[i], ffn_w_dw[i], ffn_b_dw[i], ffn_w_down[i])
    return rms_norm(x, norm_final_g)


import jax as _jax
import jax.numpy as _jnp

TWIN_FORMAT = 'train_step'
FWD_PARAMS = ['x', 'norm_mix_g', 'norm_ffn_g', 'norm_final_g', 'cv_w_in', 'cv_b_in', 'cv_w_dw', 'cv_b_dw', 'cv_ln_g', 'cv_ln_b', 'cv_w_out', 'cv_b_out', 'ssm_w_in', 'ssm_w_conv', 'ssm_b_conv', 'ssm_dt_bias', 'ssm_a_log', 'ssm_d', 'ssm_norm_g', 'ssm_w_out', 'ffn_w_up', 'ffn_w_dw', 'ffn_b_dw', 'ffn_w_down']
TWIN_WEIGHTS = ['norm_mix_g', 'norm_ffn_g', 'norm_final_g', 'cv_w_in', 'cv_b_in', 'cv_w_dw', 'cv_b_dw', 'cv_ln_g', 'cv_ln_b', 'cv_w_out', 'cv_b_out', 'ssm_w_in', 'ssm_w_conv', 'ssm_b_conv', 'ssm_dt_bias', 'ssm_a_log', 'ssm_d', 'ssm_norm_g', 'ssm_w_out', 'ffn_w_up', 'ffn_w_dw', 'ffn_b_dw', 'ffn_w_down']
TWIN_DIFF_INPUT = 'x'
TWIN_INPUTS = ['x', 'norm_mix_g', 'norm_ffn_g', 'norm_final_g', 'cv_w_in', 'cv_b_in', 'cv_w_dw', 'cv_b_dw', 'cv_ln_g', 'cv_ln_b', 'cv_w_out', 'cv_b_out', 'ssm_w_in', 'ssm_w_conv', 'ssm_b_conv', 'ssm_dt_bias', 'ssm_a_log', 'ssm_d', 'ssm_norm_g', 'ssm_w_out', 'ffn_w_up', 'ffn_w_dw', 'ffn_b_dw', 'ffn_w_down', 'loss_target', 'm_norm_mix_g', 'm_norm_ffn_g', 'm_norm_final_g', 'm_cv_w_in', 'm_cv_b_in', 'm_cv_w_dw', 'm_cv_b_dw', 'm_cv_ln_g', 'm_cv_ln_b', 'm_cv_w_out', 'm_cv_b_out', 'm_ssm_w_in', 'm_ssm_w_conv', 'm_ssm_b_conv', 'm_ssm_dt_bias', 'm_ssm_a_log', 'm_ssm_d', 'm_ssm_norm_g', 'm_ssm_w_out', 'm_ffn_w_up', 'm_ffn_w_dw', 'm_ffn_b_dw', 'm_ffn_w_down', 'v_norm_mix_g', 'v_norm_ffn_g', 'v_norm_final_g', 'v_cv_w_in', 'v_cv_b_in', 'v_cv_w_dw', 'v_cv_b_dw', 'v_cv_ln_g', 'v_cv_ln_b', 'v_cv_w_out', 'v_cv_b_out', 'v_ssm_w_in', 'v_ssm_w_conv', 'v_ssm_b_conv', 'v_ssm_dt_bias', 'v_ssm_a_log', 'v_ssm_d', 'v_ssm_norm_g', 'v_ssm_w_out', 'v_ffn_w_up', 'v_ffn_w_dw', 'v_ffn_b_dw', 'v_ffn_w_down']
TWIN_OUTPUTS = ['loss', 'grad_x', 'grad_norm_mix_g', 'grad_norm_ffn_g', 'grad_norm_final_g', 'grad_cv_w_in', 'grad_cv_b_in', 'grad_cv_w_dw', 'grad_cv_b_dw', 'grad_cv_ln_g', 'grad_cv_ln_b', 'grad_cv_w_out', 'grad_cv_b_out', 'grad_ssm_w_in', 'grad_ssm_w_conv', 'grad_ssm_b_conv', 'grad_ssm_dt_bias', 'grad_ssm_a_log', 'grad_ssm_d', 'grad_ssm_norm_g', 'grad_ssm_w_out', 'grad_ffn_w_up', 'grad_ffn_w_dw', 'grad_ffn_b_dw', 'grad_ffn_w_down', 'delta_norm_mix_g', 'delta_norm_ffn_g', 'delta_norm_final_g', 'delta_cv_w_in', 'delta_cv_b_in', 'delta_cv_w_dw', 'delta_cv_b_dw', 'delta_cv_ln_g', 'delta_cv_ln_b', 'delta_cv_w_out', 'delta_cv_b_out', 'delta_ssm_w_in', 'delta_ssm_w_conv', 'delta_ssm_b_conv', 'delta_ssm_dt_bias', 'delta_ssm_a_log', 'delta_ssm_d', 'delta_ssm_norm_g', 'delta_ssm_w_out', 'delta_ffn_w_up', 'delta_ffn_w_dw', 'delta_ffn_b_dw', 'delta_ffn_w_down', 'new_m_norm_mix_g', 'new_m_norm_ffn_g', 'new_m_norm_final_g', 'new_m_cv_w_in', 'new_m_cv_b_in', 'new_m_cv_w_dw', 'new_m_cv_b_dw', 'new_m_cv_ln_g', 'new_m_cv_ln_b', 'new_m_cv_w_out', 'new_m_cv_b_out', 'new_m_ssm_w_in', 'new_m_ssm_w_conv', 'new_m_ssm_b_conv', 'new_m_ssm_dt_bias', 'new_m_ssm_a_log', 'new_m_ssm_d', 'new_m_ssm_norm_g', 'new_m_ssm_w_out', 'new_m_ffn_w_up', 'new_m_ffn_w_dw', 'new_m_ffn_b_dw', 'new_m_ffn_w_down', 'new_v_norm_mix_g', 'new_v_norm_ffn_g', 'new_v_norm_final_g', 'new_v_cv_w_in', 'new_v_cv_b_in', 'new_v_cv_w_dw', 'new_v_cv_b_dw', 'new_v_cv_ln_g', 'new_v_cv_ln_b', 'new_v_cv_w_out', 'new_v_cv_b_out', 'new_v_ssm_w_in', 'new_v_ssm_w_conv', 'new_v_ssm_b_conv', 'new_v_ssm_dt_bias', 'new_v_ssm_a_log', 'new_v_ssm_d', 'new_v_ssm_norm_g', 'new_v_ssm_w_out', 'new_v_ffn_w_up', 'new_v_ffn_w_dw', 'new_v_ffn_b_dw', 'new_v_ffn_w_down']
TWIN_LEAF_KINDS = {'loss': 'loss', 'grad_x': 'grad_x', 'grad_norm_mix_g': 'grad_w', 'grad_norm_ffn_g': 'grad_w', 'grad_norm_final_g': 'grad_w', 'grad_cv_w_in': 'grad_w', 'grad_cv_b_in': 'grad_w', 'grad_cv_w_dw': 'grad_w', 'grad_cv_b_dw': 'grad_w', 'grad_cv_ln_g': 'grad_w', 'grad_cv_ln_b': 'grad_w', 'grad_cv_w_out': 'grad_w', 'grad_cv_b_out': 'grad_w', 'grad_ssm_w_in': 'grad_w', 'grad_ssm_w_conv': 'grad_w', 'grad_ssm_b_conv': 'grad_w', 'grad_ssm_dt_bias': 'grad_w', 'grad_ssm_a_log': 'grad_w', 'grad_ssm_d': 'grad_w', 'grad_ssm_norm_g': 'grad_w', 'grad_ssm_w_out': 'grad_w', 'grad_ffn_w_up': 'grad_w', 'grad_ffn_w_dw': 'grad_w', 'grad_ffn_b_dw': 'grad_w', 'grad_ffn_w_down': 'grad_w', 'delta_norm_mix_g': 'delta_w', 'delta_norm_ffn_g': 'delta_w', 'delta_norm_final_g': 'delta_w', 'delta_cv_w_in': 'delta_w', 'delta_cv_b_in': 'delta_w', 'delta_cv_w_dw': 'delta_w', 'delta_cv_b_dw': 'delta_w', 'delta_cv_ln_g': 'delta_w', 'delta_cv_ln_b': 'delta_w', 'delta_cv_w_out': 'delta_w', 'delta_cv_b_out': 'delta_w', 'delta_ssm_w_in': 'delta_w', 'delta_ssm_w_conv': 'delta_w', 'delta_ssm_b_conv': 'delta_w', 'delta_ssm_dt_bias': 'delta_w', 'delta_ssm_a_log': 'delta_w', 'delta_ssm_d': 'delta_w', 'delta_ssm_norm_g': 'delta_w', 'delta_ssm_w_out': 'delta_w', 'delta_ffn_w_up': 'delta_w', 'delta_ffn_w_dw': 'delta_w', 'delta_ffn_b_dw': 'delta_w', 'delta_ffn_w_down': 'delta_w', 'new_m_norm_mix_g': 'new_m', 'new_m_norm_ffn_g': 'new_m', 'new_m_norm_final_g': 'new_m', 'new_m_cv_w_in': 'new_m', 'new_m_cv_b_in': 'new_m', 'new_m_cv_w_dw': 'new_m', 'new_m_cv_b_dw': 'new_m', 'new_m_cv_ln_g': 'new_m', 'new_m_cv_ln_b': 'new_m', 'new_m_cv_w_out': 'new_m', 'new_m_cv_b_out': 'new_m', 'new_m_ssm_w_in': 'new_m', 'new_m_ssm_w_conv': 'new_m', 'new_m_ssm_b_conv': 'new_m', 'new_m_ssm_dt_bias': 'new_m', 'new_m_ssm_a_log': 'new_m', 'new_m_ssm_d': 'new_m', 'new_m_ssm_norm_g': 'new_m', 'new_m_ssm_w_out': 'new_m', 'new_m_ffn_w_up': 'new_m', 'new_m_ffn_w_dw': 'new_m', 'new_m_ffn_b_dw': 'new_m', 'new_m_ffn_w_down': 'new_m', 'new_v_norm_mix_g': 'new_v', 'new_v_norm_ffn_g': 'new_v', 'new_v_norm_final_g': 'new_v', 'new_v_cv_w_in': 'new_v', 'new_v_cv_b_in': 'new_v', 'new_v_cv_w_dw': 'new_v', 'new_v_cv_b_dw': 'new_v', 'new_v_cv_ln_g': 'new_v', 'new_v_cv_ln_b': 'new_v', 'new_v_cv_w_out': 'new_v', 'new_v_cv_b_out': 'new_v', 'new_v_ssm_w_in': 'new_v', 'new_v_ssm_w_conv': 'new_v', 'new_v_ssm_b_conv': 'new_v', 'new_v_ssm_dt_bias': 'new_v', 'new_v_ssm_a_log': 'new_v', 'new_v_ssm_d': 'new_v', 'new_v_ssm_norm_g': 'new_v', 'new_v_ssm_w_out': 'new_v', 'new_v_ffn_w_up': 'new_v', 'new_v_ffn_w_dw': 'new_v', 'new_v_ffn_b_dw': 'new_v', 'new_v_ffn_w_down': 'new_v'}


def _forward(args):
    return _fwd_reference(*[args[k] for k in FWD_PARAMS])


def _output_shape():
    def fwd():
        inp = _fwd_setup_inputs(0)
        return _fwd_reference(*[inp[k] for k in FWD_PARAMS])
    out = _jax.eval_shape(fwd)
    return out.shape, out.dtype

N_MICROBATCH = 1
ADAM_LR = 0.001
ADAM_B1 = 0.9
ADAM_B2 = 0.999
ADAM_EPS = 1e-08
ADAM_WD = 0.01
ADAM_STEP = 10
PER_EXAMPLE_BATCH_AXIS = {'x': 0, 'loss_target': 0}
SHARED_INPUTS = []
_WEIGHT_DTYPES = {'norm_mix_g': _jnp.float32, 'norm_ffn_g': _jnp.float32, 'norm_final_g': _jnp.float32, 'cv_w_in': _jnp.float32, 'cv_b_in': _jnp.float32, 'cv_w_dw': _jnp.float32, 'cv_b_dw': _jnp.float32, 'cv_ln_g': _jnp.float32, 'cv_ln_b': _jnp.float32, 'cv_w_out': _jnp.float32, 'cv_b_out': _jnp.float32, 'ssm_w_in': _jnp.float32, 'ssm_w_conv': _jnp.float32, 'ssm_b_conv': _jnp.float32, 'ssm_dt_bias': _jnp.float32, 'ssm_a_log': _jnp.float32, 'ssm_d': _jnp.float32, 'ssm_norm_g': _jnp.float32, 'ssm_w_out': _jnp.float32, 'ffn_w_up': _jnp.float32, 'ffn_w_dw': _jnp.float32, 'ffn_b_dw': _jnp.float32, 'ffn_w_down': _jnp.float32}
MOMENT_SCALE = {'norm_mix_g': 7.922872e-02, 'norm_ffn_g': 6.176221e-02, 'norm_final_g': 1.601577e+01, 'cv_w_in': 4.879705e-02, 'cv_b_in': 6.471793e-02, 'cv_w_dw': 6.457089e-02, 'cv_b_dw': 1.541773e-01, 'cv_ln_g': 8.523626e-02, 'cv_ln_b': 8.888311e-02, 'cv_w_out': 6.690571e-02, 'cv_b_out': 1.602218e-01, 'ssm_w_in': 3.909523e-02, 'ssm_w_conv': 3.654991e-02, 'ssm_b_conv': 5.494256e-02, 'ssm_dt_bias': 8.220059e-02, 'ssm_a_log': 2.107580e-01, 'ssm_d': 2.445540e-01, 'ssm_norm_g': 4.249288e-02, 'ssm_w_out': 6.160064e-02, 'ffn_w_up': 2.645274e-02, 'ffn_w_dw': 2.650595e-02, 'ffn_b_dw': 2.814280e-02, 'ffn_w_down': 4.335713e-02}


def _to_microbatches(a, axis):
    t = _jnp.moveaxis(a, axis, 0)
    t = t.reshape((N_MICROBATCH, t.shape[0] // N_MICROBATCH) + t.shape[1:])
    return _jnp.moveaxis(t, 1, axis + 1)


def setup_inputs(seed: int = 0) -> dict:
    inp = _fwd_setup_inputs(seed)
    key = _jax.random.fold_in(_jax.random.key(seed), 7919)
    shape, _ = _output_shape()
    out = dict(inp)
    out["loss_target"] = _jax.random.normal(_jax.random.fold_in(key, 0), shape, _jnp.float32)
    for i, name in enumerate(TWIN_WEIGHTS):
        w = inp[name].astype(_jnp.float32)
        if MOMENT_SCALE is None:
            s = _jnp.sqrt(_jnp.mean(_jnp.square(w)) + 1e-30)
        else:
            s = MOMENT_SCALE[name]
        km, kv = _jax.random.split(_jax.random.fold_in(key, i + 1))
        out[name] = w
        out["m_" + name] = s * _jax.random.normal(km, w.shape, _jnp.float32)
        out["v_" + name] = (s * s) * _jax.random.uniform(kv, w.shape, _jnp.float32, 0.5, 1.5)
    if N_MICROBATCH > 1:
        for name, axis in PER_EXAMPLE_BATCH_AXIS.items():
            out[name] = _to_microbatches(out[name], axis)
    return {'x': out['x'], 'norm_mix_g': out['norm_mix_g'], 'norm_ffn_g': out['norm_ffn_g'], 'norm_final_g': out['norm_final_g'], 'cv_w_in': out['cv_w_in'], 'cv_b_in': out['cv_b_in'], 'cv_w_dw': out['cv_w_dw'], 'cv_b_dw': out['cv_b_dw'], 'cv_ln_g': out['cv_ln_g'], 'cv_ln_b': out['cv_ln_b'], 'cv_w_out': out['cv_w_out'], 'cv_b_out': out['cv_b_out'], 'ssm_w_in': out['ssm_w_in'], 'ssm_w_conv': out['ssm_w_conv'], 'ssm_b_conv': out['ssm_b_conv'], 'ssm_dt_bias': out['ssm_dt_bias'], 'ssm_a_log': out['ssm_a_log'], 'ssm_d': out['ssm_d'], 'ssm_norm_g': out['ssm_norm_g'], 'ssm_w_out': out['ssm_w_out'], 'ffn_w_up': out['ffn_w_up'], 'ffn_w_dw': out['ffn_w_dw'], 'ffn_b_dw': out['ffn_b_dw'], 'ffn_w_down': out['ffn_w_down'], 'loss_target': out['loss_target'], 'm_norm_mix_g': out['m_norm_mix_g'], 'm_norm_ffn_g': out['m_norm_ffn_g'], 'm_norm_final_g': out['m_norm_final_g'], 'm_cv_w_in': out['m_cv_w_in'], 'm_cv_b_in': out['m_cv_b_in'], 'm_cv_w_dw': out['m_cv_w_dw'], 'm_cv_b_dw': out['m_cv_b_dw'], 'm_cv_ln_g': out['m_cv_ln_g'], 'm_cv_ln_b': out['m_cv_ln_b'], 'm_cv_w_out': out['m_cv_w_out'], 'm_cv_b_out': out['m_cv_b_out'], 'm_ssm_w_in': out['m_ssm_w_in'], 'm_ssm_w_conv': out['m_ssm_w_conv'], 'm_ssm_b_conv': out['m_ssm_b_conv'], 'm_ssm_dt_bias': out['m_ssm_dt_bias'], 'm_ssm_a_log': out['m_ssm_a_log'], 'm_ssm_d': out['m_ssm_d'], 'm_ssm_norm_g': out['m_ssm_norm_g'], 'm_ssm_w_out': out['m_ssm_w_out'], 'm_ffn_w_up': out['m_ffn_w_up'], 'm_ffn_w_dw': out['m_ffn_w_dw'], 'm_ffn_b_dw': out['m_ffn_b_dw'], 'm_ffn_w_down': out['m_ffn_w_down'], 'v_norm_mix_g': out['v_norm_mix_g'], 'v_norm_ffn_g': out['v_norm_ffn_g'], 'v_norm_final_g': out['v_norm_final_g'], 'v_cv_w_in': out['v_cv_w_in'], 'v_cv_b_in': out['v_cv_b_in'], 'v_cv_w_dw': out['v_cv_w_dw'], 'v_cv_b_dw': out['v_cv_b_dw'], 'v_cv_ln_g': out['v_cv_ln_g'], 'v_cv_ln_b': out['v_cv_ln_b'], 'v_cv_w_out': out['v_cv_w_out'], 'v_cv_b_out': out['v_cv_b_out'], 'v_ssm_w_in': out['v_ssm_w_in'], 'v_ssm_w_conv': out['v_ssm_w_conv'], 'v_ssm_b_conv': out['v_ssm_b_conv'], 'v_ssm_dt_bias': out['v_ssm_dt_bias'], 'v_ssm_a_log': out['v_ssm_a_log'], 'v_ssm_d': out['v_ssm_d'], 'v_ssm_norm_g': out['v_ssm_norm_g'], 'v_ssm_w_out': out['v_ssm_w_out'], 'v_ffn_w_up': out['v_ffn_w_up'], 'v_ffn_w_dw': out['v_ffn_w_dw'], 'v_ffn_b_dw': out['v_ffn_b_dw'], 'v_ffn_w_down': out['v_ffn_w_down']}


def _loss(weights, diff, rest, loss_target):
    with _jax.named_scope("forward"):
        args = {**rest, TWIN_DIFF_INPUT: diff, **{k: w.astype(_WEIGHT_DTYPES[k]) for k, w in weights.items()}}
        y = _forward(args)
    with _jax.named_scope("loss_head"):
        err = _jnp.square(y.astype(_jnp.float32) - loss_target)
        return 0.5 * _jnp.sum(_jnp.mean(err, axis=-1)) if err.ndim else 0.5 * err


def _adamw(w, g, m, v):
    m = ADAM_B1 * m + (1.0 - ADAM_B1) * g
    v = ADAM_B2 * v + (1.0 - ADAM_B2) * _jnp.square(g)
    m_hat = m / (1.0 - ADAM_B1 ** ADAM_STEP)
    v_hat = v / (1.0 - ADAM_B2 ** ADAM_STEP)
    delta = -ADAM_LR * (m_hat / (_jnp.sqrt(v_hat) + ADAM_EPS) + ADAM_WD * w)
    return delta, m, v


def reference(x, norm_mix_g, norm_ffn_g, norm_final_g, cv_w_in, cv_b_in, cv_w_dw, cv_b_dw, cv_ln_g, cv_ln_b, cv_w_out, cv_b_out, ssm_w_in, ssm_w_conv, ssm_b_conv, ssm_dt_bias, ssm_a_log, ssm_d, ssm_norm_g, ssm_w_out, ffn_w_up, ffn_w_dw, ffn_b_dw, ffn_w_down, loss_target, m_norm_mix_g, m_norm_ffn_g, m_norm_final_g, m_cv_w_in, m_cv_b_in, m_cv_w_dw, m_cv_b_dw, m_cv_ln_g, m_cv_ln_b, m_cv_w_out, m_cv_b_out, m_ssm_w_in, m_ssm_w_conv, m_ssm_b_conv, m_ssm_dt_bias, m_ssm_a_log, m_ssm_d, m_ssm_norm_g, m_ssm_w_out, m_ffn_w_up, m_ffn_w_dw, m_ffn_b_dw, m_ffn_w_down, v_norm_mix_g, v_norm_ffn_g, v_norm_final_g, v_cv_w_in, v_cv_b_in, v_cv_w_dw, v_cv_b_dw, v_cv_ln_g, v_cv_ln_b, v_cv_w_out, v_cv_b_out, v_ssm_w_in, v_ssm_w_conv, v_ssm_b_conv, v_ssm_dt_bias, v_ssm_a_log, v_ssm_d, v_ssm_norm_g, v_ssm_w_out, v_ffn_w_up, v_ffn_w_dw, v_ffn_b_dw, v_ffn_w_down):
    given = dict(x=x, norm_mix_g=norm_mix_g, norm_ffn_g=norm_ffn_g, norm_final_g=norm_final_g, cv_w_in=cv_w_in, cv_b_in=cv_b_in, cv_w_dw=cv_w_dw, cv_b_dw=cv_b_dw, cv_ln_g=cv_ln_g, cv_ln_b=cv_ln_b, cv_w_out=cv_w_out, cv_b_out=cv_b_out, ssm_w_in=ssm_w_in, ssm_w_conv=ssm_w_conv, ssm_b_conv=ssm_b_conv, ssm_dt_bias=ssm_dt_bias, ssm_a_log=ssm_a_log, ssm_d=ssm_d, ssm_norm_g=ssm_norm_g, ssm_w_out=ssm_w_out, ffn_w_up=ffn_w_up, ffn_w_dw=ffn_w_dw, ffn_b_dw=ffn_b_dw, ffn_w_down=ffn_w_down, loss_target=loss_target, m_norm_mix_g=m_norm_mix_g, m_norm_ffn_g=m_norm_ffn_g, m_norm_final_g=m_norm_final_g, m_cv_w_in=m_cv_w_in, m_cv_b_in=m_cv_b_in, m_cv_w_dw=m_cv_w_dw, m_cv_b_dw=m_cv_b_dw, m_cv_ln_g=m_cv_ln_g, m_cv_ln_b=m_cv_ln_b, m_cv_w_out=m_cv_w_out, m_cv_b_out=m_cv_b_out, m_ssm_w_in=m_ssm_w_in, m_ssm_w_conv=m_ssm_w_conv, m_ssm_b_conv=m_ssm_b_conv, m_ssm_dt_bias=m_ssm_dt_bias, m_ssm_a_log=m_ssm_a_log, m_ssm_d=m_ssm_d, m_ssm_norm_g=m_ssm_norm_g, m_ssm_w_out=m_ssm_w_out, m_ffn_w_up=m_ffn_w_up, m_ffn_w_dw=m_ffn_w_dw, m_ffn_b_dw=m_ffn_b_dw, m_ffn_w_down=m_ffn_w_down, v_norm_mix_g=v_norm_mix_g, v_norm_ffn_g=v_norm_ffn_g, v_norm_final_g=v_norm_final_g, v_cv_w_in=v_cv_w_in, v_cv_b_in=v_cv_b_in, v_cv_w_dw=v_cv_w_dw, v_cv_b_dw=v_cv_b_dw, v_cv_ln_g=v_cv_ln_g, v_cv_ln_b=v_cv_ln_b, v_cv_w_out=v_cv_w_out, v_cv_b_out=v_cv_b_out, v_ssm_w_in=v_ssm_w_in, v_ssm_w_conv=v_ssm_w_conv, v_ssm_b_conv=v_ssm_b_conv, v_ssm_dt_bias=v_ssm_dt_bias, v_ssm_a_log=v_ssm_a_log, v_ssm_d=v_ssm_d, v_ssm_norm_g=v_ssm_norm_g, v_ssm_w_out=v_ssm_w_out, v_ffn_w_up=v_ffn_w_up, v_ffn_w_dw=v_ffn_w_dw, v_ffn_b_dw=v_ffn_b_dw, v_ffn_w_down=v_ffn_w_down)
    weights = {n: given[n] for n in TWIN_WEIGHTS}
    shared = {n: given[n] for n in SHARED_INPUTS}
    per_example = {n: given[n] for n in ['x']}
    grad_fn = _jax.value_and_grad(_loss, argnums=(0, 1))

    def one_microbatch(ex, loss_target):
        ex = dict(ex)
        diff = ex.pop(TWIN_DIFF_INPUT)
        return grad_fn(weights, diff, {**shared, **ex}, loss_target)

    if N_MICROBATCH == 1:
        loss, (grad_w, grad_x) = one_microbatch(per_example, given["loss_target"])
    else:
        def body(carry, xs):
            loss_sum, grad_sum = carry
            l_k, (gw_k, gx_k) = one_microbatch(xs[0], xs[1])
            with _jax.named_scope("update"):
                return (loss_sum + l_k, _jax.tree.map(_jnp.add, grad_sum, gw_k)), gx_k

        init = (_jnp.zeros((), _jnp.float32), _jax.tree.map(_jnp.zeros_like, weights))
        (loss, grad_w), grad_x = _jax.lax.scan(body, init, (per_example, given["loss_target"]))
    with _jax.named_scope("update"):
        delta_w, new_m, new_v = {}, {}, {}
        for n in TWIN_WEIGHTS:
            delta_w[n], new_m[n], new_v[n] = _adamw(weights[n], grad_w[n], given["m_" + n], given["v_" + n])
    return (loss, grad_x, *[grad_w[n] for n in TWIN_WEIGHTS], *[delta_w[n] for n in TWIN_WEIGHTS],
            *[new_m[n] for n in TWIN_WEIGHTS], *[new_v[n] for n in TWIN_WEIGHTS])
```

```python
import functools

import jax
import jax.numpy as jnp
from jax import lax
from jax.experimental import pallas as pl
from jax.experimental.pallas import tpu as pltpu

F32, BF16 = jnp.float32, jnp.bfloat16
HIGHEST = lax.Precision.HIGHEST
MESH = pl.DeviceIdType.MESH
ANY = pl.BlockSpec(memory_space=pl.ANY)

RMS_EPS = 1e-6
LN_EPS = 1e-5
HEAD_DIM = 64
N_GROUPS = 8
D_STATE = 128
ADAM_LR, ADAM_B1, ADAM_B2, ADAM_EPS, ADAM_WD, ADAM_STEP = 0.001, 0.9, 0.999, 1e-08, 0.01, 10

VMEM_LIMIT_BYTES = 56 * 1024 * 1024
LANES = 128
SSD_CHUNK = 128
PACK_COLS = 1024
PACK_QUANTUM = 1 << 20
ADAMW_BLOCK_ELEMS = 512 * 1024
N_CHIPS = 4


def _cparams(**kw):
    return pltpu.CompilerParams(vmem_limit_bytes=VMEM_LIMIT_BYTES, **kw)


def _pick(n, prefs):
    for p in prefs:
        if n % p == 0:
            return p
    return n


def _sigmoid(x):
    return 1.0 / (1.0 + jnp.exp(-x))


def _silu(x):
    return x * _sigmoid(x)


def _dsilu(x):
    s = _sigmoid(x)
    return s * (1.0 + x * (1.0 - s))


def _rowsum(x):
    return jnp.sum(x, axis=0, keepdims=True)


def _matmul(a, b, *, name, ta=False, tb=False, out_dtype=F32, bias=None, res=None):
    m, k = (a.shape[1], a.shape[0]) if ta else a.shape
    n = b.shape[0] if tb else b.shape[1]
    assert k == (b.shape[1] if tb else b.shape[0])
    tm = _pick(m, (1024, 512, 384, 256, 128))
    tn = _pick(n, (1024, 1152, 512, 384, 256, 128))
    tk = _pick(k, (512, 384, 256, 128))
    nk = k // tk
    dn = (((0 if ta else 1,), (1 if tb else 0,)), ((), ()))
    has_bias, has_res = bias is not None, res is not None

    def body(*refs):
        a_ref, b_ref = refs[0], refs[1]
        rest = list(refs[2:])
        bias_ref = rest.pop(0) if has_bias else None
        res_ref = rest.pop(0) if has_res else None
        o_ref, acc_ref = rest
        kk = pl.program_id(2)

        @pl.when(kk == 0)
        def _():
            acc_ref[...] = jnp.zeros_like(acc_ref)

        acc_ref[...] += lax.dot_general(a_ref[...].astype(BF16), b_ref[...].astype(BF16), dn,
                                        preferred_element_type=F32)

        @pl.when(kk == nk - 1)
        def _():
            r = acc_ref[...]
            if has_bias:
                r = r + bias_ref[...]
            if has_res:
                r = r + res_ref[...]
            o_ref[...] = r.astype(o_ref.dtype)

    a_spec = pl.BlockSpec((tk, tm), lambda i, j, kk: (kk, i)) if ta else pl.BlockSpec((tm, tk), lambda i, j, kk: (i, kk))
    b_spec = pl.BlockSpec((tn, tk), lambda i, j, kk: (j, kk)) if tb else pl.BlockSpec((tk, tn), lambda i, j, kk: (kk, j))
    in_specs, args = [a_spec, b_spec], [a, b]
    if has_bias:
        in_specs.append(pl.BlockSpec((1, tn), lambda i, j, kk: (0, j)))
        args.append(bias)
    if has_res:
        in_specs.append(pl.BlockSpec((tm, tn), lambda i, j, kk: (i, j)))
        args.append(res)
    return pl.pallas_call(
        body, name=name, grid=(m // tm, n // tn, nk),
        in_specs=in_specs, out_specs=pl.BlockSpec((tm, tn), lambda i, j, kk: (i, j)),
        out_shape=jax.ShapeDtypeStruct((m, n), out_dtype),
        scratch_shapes=[pltpu.VMEM((tm, tn), F32)],
        compiler_params=_cparams(dimension_semantics=("parallel", "parallel", "arbitrary")),
    )(*args)


def _rowwise(fn, rows, pars, outs, reds, *, tm, name):
    rows = [r if isinstance(r, tuple) else (r, r.shape[1], 0) for r in rows]
    t = rows[0][0].shape[0]
    assert t % tm == 0
    n_in, n_o = len(rows) + len(pars), len(outs)

    def body(*refs):
        i = pl.program_id(0)
        o, d = fn(*[r[...] for r in refs[:n_in]])
        for ref, val in zip(refs[n_in:n_in + n_o], o):
            ref[...] = val.astype(ref.dtype)
        d_refs = refs[n_in + n_o:]

        @pl.when(i == 0)
        def _():
            for ref in d_refs:
                ref[...] = jnp.zeros_like(ref)

        for ref, val in zip(d_refs, d):
            ref[...] += val

    in_specs = [pl.BlockSpec((tm, w), lambda i, b=blk: (i, b)) for _, w, blk in rows]
    in_specs += [pl.BlockSpec((1, p.shape[1]), lambda i: (0, 0)) for p in pars]
    out_specs = [pl.BlockSpec((tm, c), lambda i: (i, 0)) for c, _ in outs]
    out_specs += [pl.BlockSpec((1, c), lambda i: (0, 0)) for c in reds]
    out_shape = [jax.ShapeDtypeStruct((t, c), dt) for c, dt in outs] + [jax.ShapeDtypeStruct((1, c), F32) for c in reds]
    res = pl.pallas_call(
        body, name=name, grid=(t // tm,), in_specs=in_specs, out_specs=out_specs, out_shape=out_shape,
        compiler_params=_cparams(dimension_semantics=("arbitrary",)),
    )(*[r[0] for r in rows], *pars)
    return res[:n_o], res[n_o:]


def _rms_fwd(x, g, *, name):
    def fn(xb, gb):
        r = lax.rsqrt(jnp.mean(xb * xb, axis=-1, keepdims=True) + RMS_EPS)
        return [xb * r * gb], []
    (h,), _ = _rowwise(fn, [x], [g], [(x.shape[1], BF16)], [], tm=256, name=name)
    return h


def _rms_bwd(x, g, dh, dres, *, name):
    def fn(xb, dhb, drb, gb):
        r = lax.rsqrt(jnp.mean(xb * xb, axis=-1, keepdims=True) + RMS_EPS)
        xh = xb * r
        dxh = dhb * gb
        dx = r * (dxh - xh * jnp.mean(dxh * xh, axis=-1, keepdims=True))
        out = drb + dx
        return [out], [_rowsum(out), _rowsum(dhb * xh)]
    c = x.shape[1]
    (dx,), (colsum, dg) = _rowwise(fn, [x, dh, dres], [g], [(c, F32)], [c, c], tm=256, name=name)
    return dx, colsum, dg


def _loss_head(x, g, tgt, *, name):
    d_model = x.shape[1]

    def fn(xb, tb, gb):
        r = lax.rsqrt(jnp.mean(xb * xb, axis=-1, keepdims=True) + RMS_EPS)
        xh = xb * r
        e = xh * gb - tb
        dy = e * (1.0 / d_model)
        dxh = dy * gb
        dx = r * (dxh - xh * jnp.mean(dxh * xh, axis=-1, keepdims=True))
        return [dx], [_rowsum(e * e), _rowsum(dy * xh)]
    (dx,), (sq, dg) = _rowwise(fn, [x, tgt], [g], [(d_model, F32)], [d_model, d_model], tm=256, name=name)
    return dx, sq, dg


def _halo_rows(k):
    return 8 * ((k - 1 + 7) // 8) if k > 1 else 8


def _pad_taps(w):
    k = w.shape[0]
    kp = 8 * ((k + 7) // 8)
    return jnp.pad(w, ((0, kp - k), (0, 0)))


def _dwconv_fwd(x, w, b, *, name):
    t, c = x.shape
    k = w.shape[0]
    h = _halo_rows(k)
    tm = _pick(t, (256, 128))
    tc = _pick(c, (512, 256, 128))
    wp = _pad_taps(w)
    kp = wp.shape[0]
    rb = tm // h

    def body(x_ref, p_ref, w_ref, b_ref, o_ref, ext):
        i = pl.program_id(0)
        ext[pl.ds(h, tm), :] = x_ref[...]
        ext[pl.ds(0, h), :] = jnp.where(i > 0, p_ref[...], 0.0)
        acc = jnp.broadcast_to(b_ref[...], (tm, tc))
        for s in range(k):
            acc = acc + w_ref[k - 1 - s:k - s, :] * ext[pl.ds(h - s, tm), :]
        o_ref[...] = acc

    return pl.pallas_call(
        body, name=name, grid=(t // tm, c // tc),
        in_specs=[pl.BlockSpec((tm, tc), lambda i, j: (i, j)),
                  pl.BlockSpec((h, tc), lambda i, j: (jnp.maximum(i * rb - 1, 0), j)),
                  pl.BlockSpec((kp, tc), lambda i, j: (0, j)),
                  pl.BlockSpec((1, tc), lambda i, j: (0, j))],
        out_specs=pl.BlockSpec((tm, tc), lambda i, j: (i, j)),
        out_shape=jax.ShapeDtypeStruct((t, c), F32),
        scratch_shapes=[pltpu.VMEM((h + tm, tc), F32)],
        compiler_params=_cparams(dimension_semantics=("parallel", "parallel")),
    )(x, x, wp, b)


def _dwconv_bwd(x, dy, w, *, name):
    t, c = x.shape
    k = w.shape[0]
    h = _halo_rows(k)
    tm = _pick(t, (256, 128))
    tc = _pick(c, (512, 256, 128))
    wp = _pad_taps(w)
    kp = wp.shape[0]
    rb = tm // h
    nt = t // tm

    def body(x_ref, p_ref, dy_ref, n_ref, w_ref, dx_ref, dw_ref, db_ref, xext, dext):
        i = pl.program_id(1)

        @pl.when(i == 0)
        def _():
            dw_ref[...] = jnp.zeros_like(dw_ref)
            db_ref[...] = jnp.zeros_like(db_ref)

        xext[pl.ds(h, tm), :] = x_ref[...]
        xext[pl.ds(0, h), :] = jnp.where(i > 0, p_ref[...], 0.0)
        dyv = dy_ref[...]
        dext[pl.ds(0, tm), :] = dyv
        dext[pl.ds(tm, h), :] = jnp.where(i < nt - 1, n_ref[...], 0.0)
        acc = jnp.zeros((tm, tc), F32)
        for s in range(k):
            acc = acc + w_ref[k - 1 - s:k - s, :] * dext[pl.ds(s, tm), :]
            dw_ref[k - 1 - s:k - s, :] += _rowsum(xext[pl.ds(h - s, tm), :] * dyv)
        dx_ref[...] = acc
        db_ref[...] += _rowsum(dyv)

    dx, dw, db = pl.pallas_call(
        body, name=name, grid=(c // tc, nt),
        in_specs=[pl.BlockSpec((tm, tc), lambda j, i: (i, j)),
                  pl.BlockSpec((h, tc), lambda j, i: (jnp.maximum(i * rb - 1, 0), j)),
                  pl.BlockSpec((tm, tc), lambda j, i: (i, j)),
                  pl.BlockSpec((h, tc), lambda j, i: (jnp.minimum((i + 1) * rb, nt * rb - 1), j)),
                  pl.BlockSpec((kp, tc), lambda j, i: (0, j))],
        out_specs=[pl.BlockSpec((tm, tc), lambda j, i: (i, j)),
                   pl.BlockSpec((kp, tc), lambda j, i: (0, j)),
                   pl.BlockSpec((1, tc), lambda j, i: (0, j))],
        out_shape=[jax.ShapeDtypeStruct((t, c), F32), jax.ShapeDtypeStruct((kp, c), F32),
                   jax.ShapeDtypeStruct((1, c), F32)],
        scratch_shapes=[pltpu.VMEM((h + tm, tc), F32), pltpu.VMEM((tm + h, tc), F32)],
        compiler_params=_cparams(dimension_semantics=("parallel", "arbitrary")),
    )(x, x, dy, dy, wp)
    return dx, dw[:k], db


def _glu_fwd(u, b_in, *, name):
    d = u.shape[1] // 2

    def fn(ua, ug, ba, bg):
        return [(ua + ba) * _sigmoid(ug + bg)], []
    (v,), _ = _rowwise(fn, [(u, d, 0), (u, d, 1)], [b_in[:, :d], b_in[:, d:]], [(d, F32)], [], tm=256, name=name)
    return v


def _glu_bwd(u, b_in, dv, *, name):
    d = u.shape[1] // 2

    def fn(ua, ug, dvb, ba, bg):
        a = ua + ba
        s = _sigmoid(ug + bg)
        du = jnp.concatenate([dvb * s, dvb * a * s * (1.0 - s)], axis=1)
        return [du], [_rowsum(du)]
    (du,), (db,) = _rowwise(fn, [(u, d, 0), (u, d, 1), dv], [b_in[:, :d], b_in[:, d:]], [(2 * d, BF16)], [2 * d],
                            tm=256, name=name)
    return du, db


def _ln_silu_fwd(v, g, b, *, name):
    def fn(vb, gb, bb):
        mu = jnp.mean(vb, axis=-1, keepdims=True)
        xc = vb - mu
        rstd = lax.rsqrt(jnp.mean(xc * xc, axis=-1, keepdims=True) + LN_EPS)
        return [_silu(xc * rstd * gb + bb)], []
    (o,), _ = _rowwise(fn, [v], [g, b], [(v.shape[1], BF16)], [], tm=256, name=name)
    return o


def _ln_silu_bwd(v, g, b, do, *, name):
    def fn(vb, dob, gb, bb):
        mu = jnp.mean(vb, axis=-1, keepdims=True)
        xc = vb - mu
        rstd = lax.rsqrt(jnp.mean(xc * xc, axis=-1, keepdims=True) + LN_EPS)
        xh = xc * rstd
        dy = dob * _dsilu(xh * gb + bb)
        dxh = dy * gb
        dv = rstd * (dxh - jnp.mean(dxh, axis=-1, keepdims=True) - xh * jnp.mean(dxh * xh, axis=-1, keepdims=True))
        return [dv], [_rowsum(dy * xh), _rowsum(dy)]
    c = v.shape[1]
    (dv,), (dg, db) = _rowwise(fn, [v, do], [g, b], [(c, F32)], [c, c], tm=256, name=name)
    return dv, dg, db


def _ffn_gate_fwd(uc, *, name):
    f = uc.shape[1] // 2

    def fn(g, v):
        return [_silu(g) * v], []
    (hm,), _ = _rowwise(fn, [(uc, f, 0), (uc, f, 1)], [], [(f, BF16)], [], tm=128, name=name)
    return hm


def _ffn_gate_bwd(uc, dhm, *, name):
    f = uc.shape[1] // 2

    def fn(g, v, d):
        return [jnp.concatenate([d * v * _dsilu(g), d * _silu(g)], axis=1)], []
    (duc,), _ = _rowwise(fn, [(uc, f, 0), (uc, f, 1), dhm], [], [(2 * f, F32)], [], tm=128, name=name)
    return duc


def _ssm_act(conv, dtp_exp, bias_exp, aneg_exp, *, di, name):
    q = SSD_CHUNK
    gn = (conv.shape[1] - di) // 2

    def fn(cb, dtb, bb, ab):
        act = _silu(cb)
        dt = dtb + bb
        dt = jnp.maximum(dt, 0.0) + jnp.log(1.0 + jnp.exp(-jnp.abs(dt)))
        a = dt * ab
        tri = (lax.broadcasted_iota(jnp.int32, (q, q), 0) >= lax.broadcasted_iota(jnp.int32, (q, q), 1)).astype(F32)
        cs = _dot3(tri, a, (((1,), (0,)), ((), ())), 1)
        return [act[:, :di], act[:, di:di + gn], act[:, di + gn:], dt, cs], []
    outs, _ = _rowwise(fn, [conv, dtp_exp], [bias_exp, aneg_exp],
                       [(di, F32), (gn, F32), (gn, F32), (di, F32), (di, F32)], [], tm=q, name=name)
    return outs


def _head_masks(q):
    lane = lax.broadcasted_iota(jnp.int32, (q, LANES), 1)
    return lane < HEAD_DIM


def _pair_cols(cs, lo):
    sw = pltpu.roll(cs, HEAD_DIM, 1)
    return jnp.where(lo, cs, sw), jnp.where(lo, sw, cs)


def _ssd_fwd(xs, dt_exp, cs_exp, cs_rows, bm, cm, *, name):
    t, di = xs.shape
    q = SSD_CHUNK
    hg = di // N_GROUPS
    npair = hg // LANES
    nheads = hg // HEAD_DIM
    nc = t // q
    n = D_STATE

    def body(xs_ref, dt_ref, cs_ref, csr_ref, b_ref, c_ref, y_ref, st_ref, s_scr):
        ci = pl.program_id(1)

        @pl.when(ci == 0)
        def _():
            s_scr[...] = jnp.zeros_like(s_scr)

        bb = b_ref[...].astype(BF16)
        cb_ = c_ref[...].astype(BF16)
        cbm = lax.dot_general(cb_, bb, (((1,), (1,)), ((), ())), preferred_element_type=F32)
        tri = lax.broadcasted_iota(jnp.int32, (q, q), 0) >= lax.broadcasted_iota(jnp.int32, (q, q), 1)
        lo = _head_masks(q)
        csr = csr_ref[0]
        for p in range(npair):
            sl = pl.ds(p * LANES, LANES)
            x = xs_ref[:, sl] * dt_ref[:, sl]
            cs = cs_ref[:, sl]
            col0, col1 = _pair_cols(cs, lo)
            l0 = jnp.where(tri, jnp.exp(jnp.minimum(col0 - csr[2 * p:2 * p + 1, :], 0.0)), 0.0)
            l1 = jnp.where(tri, jnp.exp(jnp.minimum(col1 - csr[2 * p + 1:2 * p + 2, :], 0.0)), 0.0)
            xb = x.astype(BF16)
            yd = jnp.where(lo, jnp.dot((cbm * l0).astype(BF16), xb, preferred_element_type=F32),
                           jnp.dot((cbm * l1).astype(BF16), xb, preferred_element_type=F32))
            s = s_scr[p]
            st_ref[0, 0, p] = s
            yo = jnp.exp(cs) * jnp.dot(cb_, s.astype(BF16), preferred_element_type=F32)
            y_ref[:, sl] = yd + yo
            cs_end = cs[q - 1:q, :]
            xd = (x * jnp.exp(cs_end - cs)).astype(BF16)
            s_scr[p] = jnp.exp(cs_end) * s + lax.dot_general(bb, xd, (((0,), (0,)), ((), ())),
                                                             preferred_element_type=F32)

    return pl.pallas_call(
        body, name=name, grid=(N_GROUPS, nc),
        in_specs=[pl.BlockSpec((q, hg), lambda g, c: (c, g)),
                  pl.BlockSpec((q, hg), lambda g, c: (c, g)),
                  pl.BlockSpec((q, hg), lambda g, c: (c, g)),
                  pl.BlockSpec((1, nheads, q), lambda g, c: (g, 0, c)),
                  pl.BlockSpec((q, n), lambda g, c: (c, g)),
                  pl.BlockSpec((q, n), lambda g, c: (c, g))],
        out_specs=[pl.BlockSpec((q, hg), lambda g, c: (c, g)),
                   pl.BlockSpec((1, 1, npair, n, LANES), lambda g, c: (g, c, 0, 0, 0))],
        out_shape=[jax.ShapeDtypeStruct((t, di), F32),
                   jax.ShapeDtypeStruct((N_GROUPS, nc, npair, n, LANES), F32)],
        scratch_shapes=[pltpu.VMEM((npair, n, LANES), F32)],
        compiler_params=_cparams(dimension_semantics=("parallel", "arbitrary")),
    )(xs, dt_exp, cs_exp, cs_rows, bm, cm)


def _dot3(a, b, dims, split):
    rest = (a, b)[split].astype(F32)
    other = (a, b)[1 - split].astype(BF16)
    acc = None
    for _ in range(3):
        part = rest.astype(BF16)
        rest = rest - part.astype(F32)
        d = (lax.dot_general(part, other, dims, preferred_element_type=F32) if split == 0
             else lax.dot_general(other, part, dims, preferred_element_type=F32))
        acc = d if acc is None else acc + d
    return acc


def _ssd_bwd(xs, dt_exp, cs_exp, cs_rows, bm, cm, dy, states, aneg_exp, *, name):
    t, di = xs.shape
    q = SSD_CHUNK
    hg = di // N_GROUPS
    npair = hg // LANES
    nheads = hg // HEAD_DIM
    nc = t // q
    n = D_STATE
    nt_dims = (((1,), (1,)), ((), ()))
    tn_dims = (((0,), (0,)), ((), ()))

    mm_dims = (((1,), (0,)), ((), ()))

    def body(xs_ref, dt_ref, cs_ref, csr_ref, b_ref, c_ref, dy_ref, st_ref, an_ref,
             dxp_ref, db_ref, dc_ref, ddt_ref, dan_ref, r_scr):
        ci = pl.program_id(1)

        @pl.when(ci == 0)
        def _():
            r_scr[...] = jnp.zeros_like(r_scr)
            dan_ref[...] = jnp.zeros_like(dan_ref)

        bb = b_ref[...].astype(BF16)
        cb_ = c_ref[...].astype(BF16)
        cbm = lax.dot_general(cb_, bb, nt_dims, preferred_element_type=F32)
        row = lax.broadcasted_iota(jnp.int32, (q, q), 0)
        col = lax.broadcasted_iota(jnp.int32, (q, q), 1)
        tri = row >= col
        triu = (row <= col).astype(F32)
        trisl = (row > col).astype(F32)
        ones2 = (lax.broadcasted_iota(jnp.int32, (LANES, LANES), 0) // HEAD_DIM
                 == lax.broadcasted_iota(jnp.int32, (LANES, LANES), 1) // HEAD_DIM).astype(F32)
        onesq = jnp.ones((q, LANES), F32)
        last = lax.broadcasted_iota(jnp.int32, (q, LANES), 0) == q - 1
        lo = _head_masks(q)
        csr = csr_ref[0]
        dcb = jnp.zeros((q, q), F32)
        dc_acc = jnp.zeros((q, n), F32)
        db_acc = jnp.zeros((q, n), F32)
        for p in range(npair):
            sl = pl.ds(p * LANES, LANES)
            xsv = xs_ref[:, sl]
            dtv = dt_ref[:, sl]
            x = xsv * dtv
            cs = cs_ref[:, sl]
            dyv = dy_ref[:, sl]
            col0, col1 = _pair_cols(cs, lo)
            l0 = jnp.where(tri, jnp.exp(jnp.minimum(col0 - csr[2 * p:2 * p + 1, :], 0.0)), 0.0)
            l1 = jnp.where(tri, jnp.exp(jnp.minimum(col1 - csr[2 * p + 1:2 * p + 2, :], 0.0)), 0.0)
            xb = x.astype(BF16)
            dyb = dyv.astype(BF16)
            g0 = lax.dot_general(jnp.where(lo, dyv, 0.0).astype(BF16), xb, nt_dims, preferred_element_type=F32)
            g1 = lax.dot_general(jnp.where(lo, 0.0, dyv).astype(BF16), xb, nt_dims, preferred_element_type=F32)
            gl0, gl1 = g0 * l0, g1 * l1
            dcb = dcb + gl0 + gl1
            w0, w1 = cbm * gl0, cbm * gl1
            dxd = jnp.where(lo,
                            lax.dot_general((cbm * l0).astype(BF16), dyb, tn_dims, preferred_element_type=F32),
                            lax.dot_general((cbm * l1).astype(BF16), dyb, tn_dims, preferred_element_type=F32))
            e = jnp.exp(cs)
            cs_end = cs[q - 1:q, :]
            dte = jnp.exp(cs_end - cs)
            dend = jnp.exp(cs_end)
            sf = st_ref[0, 0, p]
            sb = sf.astype(BF16)
            r = r_scr[p]
            rb = r.astype(BF16)
            dyeb = (dyv * e).astype(BF16)
            dc_acc = dc_acc + lax.dot_general(dyeb, sb, nt_dims, preferred_element_type=F32)
            dxo = dte * jnp.dot(bb, rb, preferred_element_type=F32)
            db_acc = db_acc + lax.dot_general((x * dte).astype(BF16), rb, nt_dims, preferred_element_type=F32)
            r_scr[p] = dend * r + lax.dot_general(cb_, dyeb, tn_dims, preferred_element_type=F32)
            dx = dxd + dxo
            dxp_ref[:, sl] = dx
            yoff = e * jnp.dot(cb_, sb, preferred_element_type=F32)
            rw = jnp.where(lo, _dot3(w0, onesq, mm_dims, 0), _dot3(w1, onesq, mm_dims, 0))
            cw = jnp.where(lo, _dot3(w0, onesq, tn_dims, 0), _dot3(w1, onesq, tn_dims, 0))
            through = jnp.where(last, dend * _rowsum(r * sf), 0.0)
            suf = _dot3(dyv * yoff + through, ones2, mm_dims, 0) + rw - cw
            pre = _dot3(dxo * x, ones2, mm_dims, 0)
            da = _dot3(triu, suf, mm_dims, 1) + _dot3(trisl, pre, mm_dims, 1)
            qs = _dot3(dx * xsv, ones2, mm_dims, 0)
            ddt_ref[:, sl] = da * an_ref[:, sl] + qs
            dan_ref[:, sl] += _rowsum(da * dtv)
        dcbb = dcb.astype(BF16)
        dc_ref[...] = dc_acc + jnp.dot(dcbb, bb, preferred_element_type=F32)
        db_ref[...] = db_acc + lax.dot_general(dcbb, cb_, tn_dims, preferred_element_type=F32)

    rev = lambda g, c: (nc - 1 - c, g)
    return pl.pallas_call(
        body, name=name, grid=(N_GROUPS, nc),
        in_specs=[pl.BlockSpec((q, hg), rev), pl.BlockSpec((q, hg), rev), pl.BlockSpec((q, hg), rev),
                  pl.BlockSpec((1, nheads, q), lambda g, c: (g, 0, nc - 1 - c)),
                  pl.BlockSpec((q, n), rev), pl.BlockSpec((q, n), rev),
                  pl.BlockSpec((q, hg), rev),
                  pl.BlockSpec((1, 1, npair, n, LANES), lambda g, c: (g, nc - 1 - c, 0, 0, 0)),
                  pl.BlockSpec((1, hg), lambda g, c: (0, g))],
        out_specs=[pl.BlockSpec((q, hg), rev), pl.BlockSpec((q, n), rev), pl.BlockSpec((q, n), rev),
                   pl.BlockSpec((q, hg), rev), pl.BlockSpec((1, hg), lambda g, c: (0, g))],
        out_shape=[jax.ShapeDtypeStruct((t, di), F32), jax.ShapeDtypeStruct((t, N_GROUPS * n), F32),
                   jax.ShapeDtypeStruct((t, N_GROUPS * n), F32), jax.ShapeDtypeStruct((t, di), F32),
                   jax.ShapeDtypeStruct((1, di), F32)],
        scratch_shapes=[pltpu.VMEM((npair, n, LANES), F32)],
        compiler_params=_cparams(dimension_semantics=("parallel", "arbitrary")),
    )(xs, dt_exp, cs_exp, cs_rows, bm, cm, dy, states, aneg_exp)


def _group_stats(w, gw):
    return [lax.rsqrt(jnp.mean(w[:, i * gw:(i + 1) * gw] ** 2, axis=-1, keepdims=True) + RMS_EPS)
            for i in range(N_GROUPS)]


def _gated_norm_fwd(y_ssd, xs, z, d_exp, g, *, name):
    di = xs.shape[1]
    gw = di // N_GROUPS

    def fn(yb, xb, zb, db, gb):
        w = (yb + db * xb) * _silu(zb)
        rs = _group_stats(w, gw)
        return [jnp.concatenate([w[:, i * gw:(i + 1) * gw] * rs[i] for i in range(N_GROUPS)], axis=1) * gb], []
    (o,), _ = _rowwise(fn, [y_ssd, xs, (z, di, 0)], [d_exp, g], [(di, BF16)], [], tm=128, name=name)
    return o


def _gated_norm_bwd(y_ssd, xs, z, d_exp, g, do, *, name):
    di = xs.shape[1]
    gw = di // N_GROUPS

    def fn(yb, xb, zb, dob, db, gb):
        yy = yb + db * xb
        sz = _silu(zb)
        w = yy * sz
        rs = _group_stats(w, gw)
        dwh = dob * gb
        wh_parts, dw_parts = [], []
        for i in range(N_GROUPS):
            sl = slice(i * gw, (i + 1) * gw)
            wh = w[:, sl] * rs[i]
            wh_parts.append(wh)
            dw_parts.append(rs[i] * (dwh[:, sl] - wh * jnp.mean(dwh[:, sl] * wh, axis=-1, keepdims=True)))
        wh = jnp.concatenate(wh_parts, axis=1)
        dw = jnp.concatenate(dw_parts, axis=1)
        dy = dw * sz
        dz = dw * yy * _dsilu(zb)
        return [dy, dz], [_rowsum(dob * wh), _rowsum(dy * xb)]
    (dy, dz), (dg, dd) = _rowwise(fn, [y_ssd, xs, (z, di, 0), do], [d_exp, g], [(di, F32), (di, BF16)], [di, di],
                                  tm=128, name=name)
    return dy, dz, dg, dd


def _ssm_act_bwd(conv, dtp_exp, dxp, dy, dbm, dcm, ddt_exp, dt_exp, bias_exp, d_exp, *, di, name):
    gn = dbm.shape[1]

    def fn(cb, dtb, dxpb, dyb, dbb, dcb, ddtb, dteb, bb, db):
        dxs = dxpb * dteb + dyb * db
        dact = jnp.concatenate([dxs, dbb, dcb], axis=1)
        dconv = dact * _dsilu(cb)
        ddtp = ddtb * _sigmoid(dtb + bb)
        return [dconv, ddtp], [_rowsum(ddtp)]
    (dconv, ddtp), (dbias,) = _rowwise(fn, [conv, dtp_exp, dxp, dy, dbm, dcm, ddt_exp, dt_exp], [bias_exp, d_exp],
                                       [(di + 2 * gn, F32), (di, F32)], [di], tm=64, name=name)
    return dconv, ddtp, dbias


def _adamw(w, g, m, v, *, name):
    r, c = w.shape
    c1 = 1.0 / (1.0 - ADAM_B1 ** ADAM_STEP)
    c2 = 1.0 / (1.0 - ADAM_B2 ** ADAM_STEP)

    def fn(wb, gb, mb, vb):
        mn = ADAM_B1 * mb + (1.0 - ADAM_B1) * gb
        vn = ADAM_B2 * vb + (1.0 - ADAM_B2) * (gb * gb)
        delta = -ADAM_LR * ((mn * c1) / (jnp.sqrt(vn * c2) + ADAM_EPS) + ADAM_WD * wb)
        return [delta, mn, vn], []
    cap = max(8, ADAMW_BLOCK_ELEMS // c)
    tm = _pick(r, [p for p in (512, 256, 128, 64, 32, 16, 8) if p <= cap])
    (d, mn, vn), _ = _rowwise(fn, [w, g, m, v], [], [(c, F32)] * 3, [], tm=tm, name=name)
    return d, mn, vn


def _add_pair(sel, g, r, *, name):
    _, _, rows, cols = g.shape
    tm = _pick(rows, (512, 256, 128, 64, 32, 16))

    def body(s_ref, g_ref, r_ref, o_ref):
        o_ref[...] = (g_ref[...].astype(F32) + r_ref[...].astype(F32)).astype(BF16)

    return pl.pallas_call(
        body, name=name,
        grid_spec=pltpu.PrefetchScalarGridSpec(
            num_scalar_prefetch=1, grid=(N_CHIPS, rows // tm),
            in_specs=[pl.BlockSpec((None, None, tm, cols), lambda j, i, s: (j, s[0], i, 0)),
                      pl.BlockSpec((None, tm, cols), lambda j, i, s: (j, i, 0))],
            out_specs=pl.BlockSpec((None, tm, cols), lambda j, i, s: (j, i, 0))),
        out_shape=jax.ShapeDtypeStruct((N_CHIPS, rows, cols), BF16),
        compiler_params=_cparams(dimension_semantics=("parallel", "parallel")),
    )(sel, g, r)


def _add_four(sel, p, r, *, name):
    _, rows, cols = p.shape
    tm = _pick(rows, (512, 256, 128, 64, 32, 16))

    def body(s_ref, p_ref, r0, r1, r2, o_ref):
        o_ref[...] = ((p_ref[...].astype(F32) + r0[...].astype(F32)) + r1[...].astype(F32)) + r2[...].astype(F32)

    rspec = lambda k: pl.BlockSpec((None, tm, cols), lambda i, s, k=k: (k, i, 0))
    return pl.pallas_call(
        body, name=name,
        grid_spec=pltpu.PrefetchScalarGridSpec(
            num_scalar_prefetch=1, grid=(rows // tm,),
            in_specs=[pl.BlockSpec((None, tm, cols), lambda i, s: (s[0], i, 0)), rspec(0), rspec(1), rspec(2)],
            out_specs=pl.BlockSpec((tm, cols), lambda i, s: (i, 0))),
        out_shape=jax.ShapeDtypeStruct((rows, cols), F32),
        compiler_params=_cparams(dimension_semantics=("parallel",)),
    )(sel, p, r, r, r)


def _sum8(g, *, name):
    _, rows, cols = g.shape
    tm = _pick(rows, (512, 256, 128, 64, 32, 16, 8))

    def body(g_ref, o_ref):
        acc = g_ref[0]
        for k in range(1, 8):
            acc = acc + g_ref[k]
        o_ref[...] = acc

    return pl.pallas_call(
        body, name=name, grid=(rows // tm,),
        in_specs=[pl.BlockSpec((8, tm, cols), lambda i: (0, i, 0))],
        out_specs=pl.BlockSpec((tm, cols), lambda i: (i, 0)),
        out_shape=jax.ShapeDtypeStruct((rows, cols), F32),
        compiler_params=_cparams(dimension_semantics=("parallel",)),
    )(g)


def _place():
    x, y, c = lax.axis_index("x"), lax.axis_index("y"), lax.axis_index("c")
    chips = [(1 - x, y), (x, 1 - y), (1 - x, 1 - y)]
    return x, y, c, chips


def _rcopy(src, dst, send_sems, recv_sems, k, to):
    return pltpu.make_async_remote_copy(src_ref=src, dst_ref=dst, send_sem=send_sems.at[k], recv_sem=recv_sems.at[k],
                                        device_id=to, device_id_type=MESH)


def _gather_chips(pack, *, name):
    _, rows, cols = pack.shape

    def body(src, out, send_sems, recv_sems, local_sem):
        x, y, c, chips = _place()
        sibling = (x, y, 1 - c)
        me = 2 * x + y
        mine = pltpu.make_async_copy(src, out.at[me], local_sem)
        mine.start()
        first = [_rcopy(src.at[c], out.at[me, c], send_sems, recv_sems, k, (cx, cy, c)) for k, (cx, cy) in enumerate(chips)]
        for cp in first:
            cp.start()
        passed = []
        for k, (cx, cy) in enumerate(chips):
            blk = out.at[2 * cx + cy, c]
            _rcopy(blk, blk, send_sems, recv_sems, k, (cx, cy, c)).wait_recv()
            fw = _rcopy(blk, blk, send_sems, recv_sems, 3 + k, sibling)
            fw.start()
            passed.append(fw)
        for k, (cx, cy) in enumerate(chips):
            blk = out.at[2 * cx + cy, 1 - c]
            _rcopy(blk, blk, send_sems, recv_sems, 3 + k, sibling).wait_recv()
        for cp in first + passed:
            cp.wait_send()
        mine.wait()

    return pl.pallas_call(
        body, name=name, in_specs=[ANY], out_specs=ANY,
        out_shape=jax.ShapeDtypeStruct((N_CHIPS, 2, rows, cols), pack.dtype),
        scratch_shapes=[pltpu.SemaphoreType.DMA((6,)), pltpu.SemaphoreType.DMA((6,)), pltpu.SemaphoreType.DMA],
    )(pack)


def _gather_devices(pack, *, name):
    rows, cols = pack.shape

    def body(src, out, send_sems, recv_sems, local_sem):
        x, y, c, chips = _place()
        sibling = (x, y, 1 - c)

        def blk(px, py, pc):
            return out.at[4 * px + 2 * py + pc]

        mine = pltpu.make_async_copy(src, blk(x, y, c), local_sem)
        mine.start()
        first = [_rcopy(src, blk(x, y, c), send_sems, recv_sems, 0, sibling)]
        first += [_rcopy(src, blk(x, y, c), send_sems, recv_sems, 1 + k, (cx, cy, c)) for k, (cx, cy) in enumerate(chips)]
        for cp in first:
            cp.start()
        passed = []
        for k, (cx, cy) in enumerate(chips):
            b = blk(cx, cy, c)
            _rcopy(b, b, send_sems, recv_sems, 1 + k, (cx, cy, c)).wait_recv()
            fw = _rcopy(b, b, send_sems, recv_sems, 4 + k, sibling)
            fw.start()
            passed.append(fw)
        b = blk(x, y, 1 - c)
        _rcopy(b, b, send_sems, recv_sems, 0, sibling).wait_recv()
        for k, (cx, cy) in enumerate(chips):
            b = blk(cx, cy, 1 - c)
            _rcopy(b, b, send_sems, recv_sems, 4 + k, sibling).wait_recv()
        for cp in first + passed:
            cp.wait_send()
        mine.wait()

    return pl.pallas_call(
        body, name=name, in_specs=[ANY], out_specs=ANY,
        out_shape=jax.ShapeDtypeStruct((8, rows, cols), pack.dtype),
        scratch_shapes=[pltpu.SemaphoreType.DMA((7,)), pltpu.SemaphoreType.DMA((7,)), pltpu.SemaphoreType.DMA],
    )(pack)


def _swap_halves(g, *, name):
    _, _, rows, cols = g.shape

    def body(src, out, send_sems, recv_sems):
        x, y, c, _ = _place()
        cps = [_rcopy(src.at[j, 1 - c], out.at[j], send_sems, recv_sems, j, (x, y, 1 - c)) for j in range(N_CHIPS)]
        for cp in cps:
            cp.start()
        for cp in cps:
            cp.wait()

    return pl.pallas_call(
        body, name=name, in_specs=[ANY], out_specs=ANY,
        out_shape=jax.ShapeDtypeStruct((N_CHIPS, rows, cols), g.dtype),
        scratch_shapes=[pltpu.SemaphoreType.DMA((N_CHIPS,)), pltpu.SemaphoreType.DMA((N_CHIPS,))],
    )(g)


def _scatter_chips(p, *, name):
    _, rows, cols = p.shape

    def body(src, out, send_sems, recv_sems):
        x, y, c, chips = _place()
        cps = [_rcopy(src.at[2 * cx + cy], out.at[k], send_sems, recv_sems, k, (cx, cy, c))
               for k, (cx, cy) in enumerate(chips)]
        for cp in cps:
            cp.start()
        for cp in cps:
            cp.wait()

    return pl.pallas_call(
        body, name=name, in_specs=[ANY], out_specs=ANY,
        out_shape=jax.ShapeDtypeStruct((3, rows, cols), p.dtype),
        scratch_shapes=[pltpu.SemaphoreType.DMA((3,)), pltpu.SemaphoreType.DMA((3,))],
    )(p)


def _join_halves(r, *, name):
    rows, cols = r.shape

    def body(src, out, send_sems, recv_sems, local_sem):
        x, y, c, _ = _place()
        mine = pltpu.make_async_copy(src, out.at[c], local_sem)
        mine.start()
        cp = _rcopy(src, out.at[c], send_sems, recv_sems, 0, (x, y, 1 - c))
        cp.start()
        b = out.at[1 - c]
        _rcopy(b, b, send_sems, recv_sems, 0, (x, y, 1 - c)).wait_recv()
        cp.wait_send()
        mine.wait()

    return pl.pallas_call(
        body, name=name, in_specs=[ANY], out_specs=ANY,
        out_shape=jax.ShapeDtypeStruct((2, rows, cols), r.dtype),
        scratch_shapes=[pltpu.SemaphoreType.DMA((1,)), pltpu.SemaphoreType.DMA((1,)), pltpu.SemaphoreType.DMA],
    )(r)


def _reduce_scatter(g, *, name):
    x, y, c = lax.axis_index("x"), lax.axis_index("y"), lax.axis_index("c")
    sel_c = jnp.reshape(c, (1,)).astype(jnp.int32)
    sel_j = jnp.reshape(2 * x + y, (1,)).astype(jnp.int32)
    got = _swap_halves(g, name=name + "_swap")
    pair = _add_pair(sel_c, g, got, name=name + "_add2")
    got3 = _scatter_chips(pair, name=name + "_scatter")
    half = _add_four(sel_j, pair, got3, name=name + "_add4")
    return _join_halves(half, name=name + "_join")


def _col_to_chips(w, nq):
    k = w.shape[0]
    return w.reshape(k, N_CHIPS, nq).transpose(1, 0, 2).reshape(N_CHIPS, k * nq)


def _chips_to_col(g, k, nq):
    return g.reshape(N_CHIPS, k, nq).transpose(1, 0, 2).reshape(k, N_CHIPS * nq)


def _flat_rows(parts, cols):
    flat = jnp.concatenate([p.reshape(-1) for p in parts])
    n = flat.shape[0]
    rows = -(-n // cols)
    rows = 8 * (-(-rows // 8))
    return jnp.pad(flat, (0, rows * cols - n)).reshape(rows, cols)


def _expand(v, di):
    return jnp.repeat(v, HEAD_DIM).reshape(1, di)


def kernel(x, norm_mix_g, norm_ffn_g, norm_final_g, cv_w_in, cv_b_in, cv_w_dw, cv_b_dw, cv_ln_g, cv_ln_b, cv_w_out, cv_b_out, ssm_w_in, ssm_w_conv, ssm_b_conv, ssm_dt_bias, ssm_a_log, ssm_d, ssm_norm_g, ssm_w_out, ffn_w_up, ffn_w_dw, ffn_b_dw, ffn_w_down, loss_target, m_norm_mix_g, m_norm_ffn_g, m_norm_final_g, m_cv_w_in, m_cv_b_in, m_cv_w_dw, m_cv_b_dw, m_cv_ln_g, m_cv_ln_b, m_cv_w_out, m_cv_b_out, m_ssm_w_in, m_ssm_w_conv, m_ssm_b_conv, m_ssm_dt_bias, m_ssm_a_log, m_ssm_d, m_ssm_norm_g, m_ssm_w_out, m_ffn_w_up, m_ffn_w_dw, m_ffn_b_dw, m_ffn_w_down, v_norm_mix_g, v_norm_ffn_g, v_norm_final_g, v_cv_w_in, v_cv_b_in, v_cv_w_dw, v_cv_b_dw, v_cv_ln_g, v_cv_ln_b, v_cv_w_out, v_cv_b_out, v_ssm_w_in, v_ssm_w_conv, v_ssm_b_conv, v_ssm_dt_bias, v_ssm_a_log, v_ssm_d, v_ssm_norm_g, v_ssm_w_out, v_ffn_w_up, v_ffn_w_dw, v_ffn_b_dw, v_ffn_w_down):
    weights = dict(norm_mix_g=norm_mix_g, norm_ffn_g=norm_ffn_g, norm_final_g=norm_final_g, cv_w_in=cv_w_in, cv_b_in=cv_b_in, cv_w_dw=cv_w_dw, cv_b_dw=cv_b_dw, cv_ln_g=cv_ln_g, cv_ln_b=cv_ln_b, cv_w_out=cv_w_out, cv_b_out=cv_b_out, ssm_w_in=ssm_w_in, ssm_w_conv=ssm_w_conv, ssm_b_conv=ssm_b_conv, ssm_dt_bias=ssm_dt_bias, ssm_a_log=ssm_a_log, ssm_d=ssm_d, ssm_norm_g=ssm_norm_g, ssm_w_out=ssm_w_out, ffn_w_up=ffn_w_up, ffn_w_dw=ffn_w_dw, ffn_b_dw=ffn_b_dw, ffn_w_down=ffn_w_down)
    mom_m = dict(norm_mix_g=m_norm_mix_g, norm_ffn_g=m_norm_ffn_g, norm_final_g=m_norm_final_g, cv_w_in=m_cv_w_in, cv_b_in=m_cv_b_in, cv_w_dw=m_cv_w_dw, cv_b_dw=m_cv_b_dw, cv_ln_g=m_cv_ln_g, cv_ln_b=m_cv_ln_b, cv_w_out=m_cv_w_out, cv_b_out=m_cv_b_out, ssm_w_in=m_ssm_w_in, ssm_w_conv=m_ssm_w_conv, ssm_b_conv=m_ssm_b_conv, ssm_dt_bias=m_ssm_dt_bias, ssm_a_log=m_ssm_a_log, ssm_d=m_ssm_d, ssm_norm_g=m_ssm_norm_g, ssm_w_out=m_ssm_w_out, ffn_w_up=m_ffn_w_up, ffn_w_dw=m_ffn_w_dw, ffn_b_dw=m_ffn_b_dw, ffn_w_down=m_ffn_w_down)
    mom_v = dict(norm_mix_g=v_norm_mix_g, norm_ffn_g=v_norm_ffn_g, norm_final_g=v_norm_final_g, cv_w_in=v_cv_w_in, cv_b_in=v_cv_b_in, cv_w_dw=v_cv_w_dw, cv_b_dw=v_cv_b_dw, cv_ln_g=v_cv_ln_g, cv_ln_b=v_cv_ln_b, cv_w_out=v_cv_w_out, cv_b_out=v_cv_b_out, ssm_w_in=v_ssm_w_in, ssm_w_conv=v_ssm_w_conv, ssm_b_conv=v_ssm_b_conv, ssm_dt_bias=v_ssm_dt_bias, ssm_a_log=v_ssm_a_log, ssm_d=v_ssm_d, ssm_norm_g=v_ssm_norm_g, ssm_w_out=v_ssm_w_out, ffn_w_up=v_ffn_w_up, ffn_w_dw=v_ffn_w_dw, ffn_b_dw=v_ffn_b_dw, ffn_w_down=v_ffn_w_down)
    names = list(weights)

    xt = x[0]
    tgt = loss_target[0]
    t, d = xt.shape
    depth = norm_mix_g.shape[0]
    n_cv, n_ssm = cv_w_in.shape[0], ssm_w_in.shape[0]
    di = ssm_w_out.shape[1] * N_CHIPS
    n_heads = di // HEAD_DIM
    gn = N_GROUPS * D_STATE
    ssm_in = ssm_w_in.shape[2] * N_CHIPS
    ssm_in_pad = LANES * (-(-ssm_in // LANES))
    f2 = ffn_w_up.shape[2] * N_CHIPS
    chip = 2 * lax.axis_index("x") + lax.axis_index("y")

    big = []
    for i in range(depth):
        j = i // 2
        if i % 2 == 0:
            big += [("cv_w_in", j, d, cv_w_in.shape[2], True), ("cv_w_out", j, cv_w_out.shape[1], d, False)]
        else:
            big += [("ssm_w_in", j, d, ssm_w_in.shape[2], True), ("ssm_w_out", j, ssm_w_out.shape[1], d, False)]
        big += [("ffn_w_up", i, d, ffn_w_up.shape[2], True), ("ffn_w_down", i, ffn_w_down.shape[1], d, False)]
    offs, o = [], 0
    for _, _, r, c, _ in big:
        offs.append(o)
        o += r * c
    pack_n = PACK_QUANTUM * (-(-o // PACK_QUANTUM))
    pack_pad = pack_n - o
    half_rows = pack_n // (2 * PACK_COLS)
    wpack = jnp.concatenate([weights[nm][l].reshape(-1) for nm, l, _, _, _ in big]
                            + [jnp.zeros((pack_pad,), F32)]).astype(BF16)
    gathered = _gather_chips(wpack.reshape(2, half_rows, PACK_COLS), name="gather_weights").reshape(N_CHIPS, pack_n)
    full = {}
    for (nm, l, r, c, by_col), o in zip(big, offs):
        blk = gathered[:, o:o + r * c]
        full[nm, l] = _chips_to_col(blk, r, c) if by_col else blk.reshape(N_CHIPS * r, c)
    for l in range(n_ssm):
        full["ssm_w_in", l] = jnp.pad(full["ssm_w_in", l], ((0, 0), (0, ssm_in_pad - ssm_in)))

    small_sharded = ["cv_w_dw", "ssm_w_conv", "ssm_b_conv", "ssm_norm_g", "ffn_w_dw"]
    spack = _flat_rows([weights[nm] for nm in small_sharded], LANES)
    sg = _gather_devices(spack, name="gather_small").reshape(8, -1)[::2]
    o = 0
    for nm in small_sharded:
        shp = weights[nm].shape
        n = weights[nm].size
        full[nm] = jnp.concatenate([sg[j, o:o + n].reshape(shp) for j in range(N_CHIPS)], axis=-1)
        o += n

    row = lambda v: v.reshape(1, -1)

    saved = []
    xc = xt
    for i in range(depth):
        j = i // 2
        s = {"x_in": xc}
        h = _rms_fwd(xc, row(norm_mix_g[i]), name="rms_mix_fwd")
        s["h"] = h
        if i % 2 == 0:
            u = _matmul(h, full["cv_w_in", j], name="cv_in_fwd")
            v1 = _glu_fwd(u, row(cv_b_in[j]), name="cv_glu_fwd")
            v2 = _dwconv_fwd(v1, full["cv_w_dw"][j], row(cv_b_dw[j]), name="cv_dw_fwd")
            v4 = _ln_silu_fwd(v2, row(cv_ln_g[j]), row(cv_ln_b[j]), name="cv_ln_fwd")
            xc = _matmul(v4, full["cv_w_out", j], bias=row(cv_b_out[j]), res=xc, name="cv_out_fwd")
            s.update(u=u, v1=v1, v2=v2, v4=v4)
        else:
            zx = _matmul(h, full["ssm_w_in", j], name="ssm_in_fwd")
            xbc_pre = zx[:, di:di + di + 2 * gn]
            dtp_exp = jnp.repeat(zx[:, 2 * di + 2 * gn:2 * di + 2 * gn + n_heads], HEAD_DIM, axis=1)
            conv = _dwconv_fwd(xbc_pre, full["ssm_w_conv"][j], row(full["ssm_b_conv"][j]), name="ssm_dw_fwd")
            bias_exp = _expand(ssm_dt_bias[j], di)
            aneg_exp = _expand(-jnp.exp(ssm_a_log[j]), di)
            d_exp = _expand(ssm_d[j], di)
            xs, bm, cm, dt_exp, cs_exp = _ssm_act(conv, dtp_exp, bias_exp, aneg_exp, di=di, name="ssm_act_fwd")
            cs_rows = cs_exp[:, ::HEAD_DIM].T.reshape(N_GROUPS, n_heads // N_GROUPS, t)
            y_ssd, states = _ssd_fwd(xs, dt_exp, cs_exp, cs_rows, bm, cm, name="ssd_fwd")
            gnrm = _gated_norm_fwd(y_ssd, xs, zx, d_exp, row(full["ssm_norm_g"][j]), name="ssm_norm_fwd")
            xc = _matmul(gnrm, full["ssm_w_out", j], res=xc, name="ssm_out_fwd")
            s.update(zx=zx, xbc_pre=xbc_pre, dtp_exp=dtp_exp, conv=conv, bias_exp=bias_exp, aneg_exp=aneg_exp,
                     d_exp=d_exp, xs=xs, bm=bm, cm=cm, dt_exp=dt_exp, cs_exp=cs_exp, cs_rows=cs_rows, y_ssd=y_ssd,
                     states=states, gnrm=gnrm)
        s["x_mid"] = xc
        h2 = _rms_fwd(xc, row(norm_ffn_g[i]), name="rms_ffn_fwd")
        u2 = _matmul(h2, full["ffn_w_up", i], name="ffn_up_fwd")
        uc = _dwconv_fwd(u2, full["ffn_w_dw"][i], row(ffn_b_dw[i]), name="ffn_dw_fwd")
        hm = _ffn_gate_fwd(uc, name="ffn_gate_fwd")
        xc = _matmul(hm, full["ffn_w_down", i], res=xc, name="ffn_down_fwd")
        s.update(h2=h2, u2=u2, uc=uc, hm=hm)
        saved.append(s)

    dx, sq, dg_final = _loss_head(xc, row(norm_final_g), tgt, name="loss_head")
    loss_part = 0.5 / d * jnp.sum(sq)
    gr = {nm: [None] * weights[nm].shape[0] for nm in names if nm != "norm_final_g"}
    gbig = {}
    for i in reversed(range(depth)):
        j = i // 2
        s = saved[i]
        dxb = dx.astype(BF16)
        gbig["ffn_w_down", i] = _matmul(s["hm"], dxb, ta=True, out_dtype=BF16, name="ffn_down_dw")
        dhm = _matmul(dxb, full["ffn_w_down", i], tb=True, name="ffn_down_dx")
        duc = _ffn_gate_bwd(s["uc"], dhm, name="ffn_gate_bwd")
        du2, dw_dw, db_dw = _dwconv_bwd(s["u2"], duc, full["ffn_w_dw"][i], name="ffn_dw_bwd")
        gr["ffn_w_dw"][i], gr["ffn_b_dw"][i] = dw_dw, db_dw[0]
        du2b = du2.astype(BF16)
        gbig["ffn_w_up", i] = _matmul(s["h2"], du2b, ta=True, out_dtype=BF16, name="ffn_up_dw")
        dh2 = _matmul(du2b, full["ffn_w_up", i], tb=True, name="ffn_up_dx")
        dx, colsum, dg = _rms_bwd(s["x_mid"], row(norm_ffn_g[i]), dh2, dx, name="rms_ffn_bwd")
        gr["norm_ffn_g"][i] = dg[0]
        dxb = dx.astype(BF16)
        if i % 2 == 0:
            gr["cv_b_out"][j] = colsum[0]
            gbig["cv_w_out", j] = _matmul(s["v4"], dxb, ta=True, out_dtype=BF16, name="cv_out_dw")
            dv4 = _matmul(dxb, full["cv_w_out", j], tb=True, name="cv_out_dx")
            dv2, dlg, dlb = _ln_silu_bwd(s["v2"], row(cv_ln_g[j]), row(cv_ln_b[j]), dv4, name="cv_ln_bwd")
            gr["cv_ln_g"][j], gr["cv_ln_b"][j] = dlg[0], dlb[0]
            dv1, dw_dw, db_dw = _dwconv_bwd(s["v1"], dv2, full["cv_w_dw"][j], name="cv_dw_bwd")
            gr["cv_w_dw"][j], gr["cv_b_dw"][j] = dw_dw, db_dw[0]
            du, db_in = _glu_bwd(s["u"], row(cv_b_in[j]), dv1, name="cv_glu_bwd")
            gr["cv_b_in"][j] = db_in[0]
            gbig["cv_w_in", j] = _matmul(s["h"], du, ta=True, out_dtype=BF16, name="cv_in_dw")
            dh = _matmul(du, full["cv_w_in", j], tb=True, name="cv_in_dx")
        else:
            gbig["ssm_w_out", j] = _matmul(s["gnrm"], dxb, ta=True, out_dtype=BF16, name="ssm_out_dw")
            dgn = _matmul(dxb, full["ssm_w_out", j], tb=True, name="ssm_out_dx")
            dy, dz, dng, ddl = _gated_norm_bwd(s["y_ssd"], s["xs"], s["zx"], s["d_exp"], row(full["ssm_norm_g"][j]),
                                               dgn, name="ssm_norm_bwd")
            gr["ssm_norm_g"][j] = dng[0]
            gr["ssm_d"][j] = ddl.reshape(n_heads, HEAD_DIM).sum(axis=1)
            dxp, dbm, dcm, ddt_exp, dan = _ssd_bwd(s["xs"], s["dt_exp"], s["cs_exp"], s["cs_rows"], s["bm"], s["cm"],
                                                   dy, s["states"], s["aneg_exp"], name="ssd_bwd")
            gr["ssm_a_log"][j] = dan[0, ::HEAD_DIM] * s["aneg_exp"][0, ::HEAD_DIM]
            dconv, ddtp, dbias = _ssm_act_bwd(s["conv"], s["dtp_exp"], dxp, dy, dbm, dcm, ddt_exp, s["dt_exp"],
                                              s["bias_exp"], s["d_exp"], di=di, name="ssm_act_bwd")
            gr["ssm_dt_bias"][j] = dbias[0, ::HEAD_DIM]
            dxbc, dw_c, db_c = _dwconv_bwd(s["xbc_pre"], dconv, full["ssm_w_conv"][j], name="ssm_dw_bwd")
            gr["ssm_w_conv"][j], gr["ssm_b_conv"][j] = dw_c, db_c[0]
            dzx = jnp.concatenate([dz, dxbc.astype(BF16), ddtp[:, ::HEAD_DIM].astype(BF16),
                                   jnp.zeros((t, ssm_in_pad - ssm_in), BF16)], axis=1)
            gw = _matmul(s["h"], dzx, ta=True, out_dtype=BF16, name="ssm_in_dw")
            gbig["ssm_w_in", j] = gw[:, :ssm_in]
            dh = _matmul(dzx, full["ssm_w_in", j], tb=True, name="ssm_in_dx")
        dx, _, dg = _rms_bwd(s["x_in"], row(norm_mix_g[i]), dh, dx, name="rms_mix_bwd")
        gr["norm_mix_g"][i] = dg[0]

    parts = []
    for nm, l, r, c, by_col in big:
        gfull = gbig[nm, l]
        parts.append(_col_to_chips(gfull, c) if by_col else gfull.reshape(N_CHIPS, r * c))
    parts.append(jnp.zeros((N_CHIPS, pack_pad), BF16))
    gpack = jnp.concatenate(parts, axis=1).reshape(N_CHIPS, 2, half_rows, PACK_COLS)
    gsum = _reduce_scatter(gpack, name="grads").reshape(pack_n)
    grads = {}
    for nm in ("cv_w_in", "cv_w_out", "ssm_w_in", "ssm_w_out", "ffn_w_up", "ffn_w_down"):
        grads[nm] = jnp.stack([gsum[o:o + r * c].reshape(r, c) for (n2, l, r, c, _), o in zip(big, offs) if n2 == nm])

    small = [nm for nm in names if nm not in grads]
    small_parts = []
    for nm in small:
        small_parts.append(dg_final[0] if nm == "norm_final_g" else jnp.stack(gr[nm]))
    gs_pack = _flat_rows(small_parts + [loss_part.reshape(1)], LANES)
    gs = _sum8(_gather_devices(gs_pack, name="gather_small_grads"), name="sum_small_grads").reshape(-1)
    o = 0
    for nm, p in zip(small, small_parts):
        gfull = gs[o:o + p.size].reshape(p.shape)
        o += p.size
        if nm in small_sharded:
            cq = weights[nm].shape[-1]
            gfull = lax.dynamic_slice_in_dim(gfull, chip * cq, cq, axis=gfull.ndim - 1)
        grads[nm] = gfull
    loss = gs[o]

    delta, new_m, new_v = {}, {}, {}
    for nm in ("cv_w_in", "cv_w_out", "ssm_w_in", "ssm_w_out", "ffn_w_up", "ffn_w_down"):
        shp = weights[nm].shape
        as2d = lambda a: a.reshape(-1, shp[-1])
        dl, mn, vn = _adamw(as2d(weights[nm]), as2d(grads[nm]), as2d(mom_m[nm]), as2d(mom_v[nm]), name="adamw_" + nm)
        delta[nm], new_m[nm], new_v[nm] = dl.reshape(shp), mn.reshape(shp), vn.reshape(shp)
    pk = lambda dct: _flat_rows([dct[nm] for nm in small], LANES)
    dl, mn, vn = _adamw(pk(weights), pk(grads), pk(mom_m), pk(mom_v), name="adamw_small")
    dl, mn, vn = dl.reshape(-1), mn.reshape(-1), vn.reshape(-1)
    o = 0
    for nm in small:
        shp, n = weights[nm].shape, weights[nm].size
        delta[nm], new_m[nm], new_v[nm] = (a[o:o + n].reshape(shp) for a in (dl, mn, vn))
        o += n

    return (loss, dx[None], *[grads[nm] for nm in names], *[delta[nm] for nm in names],
            *[new_m[nm] for nm in names], *[new_v[nm] for nm in names])
```

```python
import functools

import jax
import jax.numpy as jnp
from jax import lax
from jax.experimental import pallas as pl
from jax.experimental.pallas import tpu as pltpu

F32, BF16 = jnp.float32, jnp.bfloat16
HIGHEST = lax.Precision.HIGHEST
MESH = pl.DeviceIdType.MESH
ANY = pl.BlockSpec(memory_space=pl.ANY)

RMS_EPS = 1e-6
LN_EPS = 1e-5
HEAD_DIM = 64
N_GROUPS = 8
D_STATE = 128
ADAM_LR, ADAM_B1, ADAM_B2, ADAM_EPS, ADAM_WD, ADAM_STEP = 0.001, 0.9, 0.999, 1e-08, 0.01, 10

VMEM_LIMIT_BYTES = 56 * 1024 * 1024
LANES = 128
SSD_CHUNK = 128
PACK_COLS = 1024
PACK_QUANTUM = 1 << 20
ADAMW_BLOCK_ELEMS = 512 * 1024
N_CHIPS = 4


def _cparams(**kw):
    return pltpu.CompilerParams(vmem_limit_bytes=VMEM_LIMIT_BYTES, **kw)


def _pick(n, prefs):
    for p in prefs:
        if n % p == 0:
            return p
    return n


def _sigmoid(x):
    return 1.0 / (1.0 + jnp.exp(-x))


def _silu(x):
    return x * _sigmoid(x)


def _dsilu(x):
    s = _sigmoid(x)
    return s * (1.0 + x * (1.0 - s))


def _rowsum(x):
    return jnp.sum(x, axis=0, keepdims=True)


def _matmul(a, b, *, name, ta=False, tb=False, out_dtype=F32, bias=None, res=None):
    m, k = (a.shape[1], a.shape[0]) if ta else a.shape
    n = b.shape[0] if tb else b.shape[1]
    assert k == (b.shape[1] if tb else b.shape[0])
    tm = _pick(m, (1024, 512, 384, 256, 128))
    tn = _pick(n, (1024, 1536, 1152, 512, 384, 256, 128))
    tk = _pick(k, (512, 384, 256, 128))
    nk = k // tk
    dn = (((0 if ta else 1,), (1 if tb else 0,)), ((), ()))
    has_bias, has_res = bias is not None, res is not None

    def body(*refs):
        a_ref, b_ref = refs[0], refs[1]
        rest = list(refs[2:])
        bias_ref = rest.pop(0) if has_bias else None
        res_ref = rest.pop(0) if has_res else None
        o_ref, acc_ref = rest
        kk = pl.program_id(2)

        @pl.when(kk == 0)
        def _():
            acc_ref[...] = jnp.zeros_like(acc_ref)

        acc_ref[...] += lax.dot_general(a_ref[...].astype(BF16), b_ref[...].astype(BF16), dn,
                                        preferred_element_type=F32)

        @pl.when(kk == nk - 1)
        def _():
            r = acc_ref[...]
            if has_bias:
                r = r + bias_ref[...]
            if has_res:
                r = r + res_ref[...]
            o_ref[...] = r.astype(o_ref.dtype)

    a_spec = pl.BlockSpec((tk, tm), lambda i, j, kk: (kk, i)) if ta else pl.BlockSpec((tm, tk), lambda i, j, kk: (i, kk))
    b_spec = pl.BlockSpec((tn, tk), lambda i, j, kk: (j, kk)) if tb else pl.BlockSpec((tk, tn), lambda i, j, kk: (kk, j))
    in_specs, args = [a_spec, b_spec], [a, b]
    if has_bias:
        in_specs.append(pl.BlockSpec((1, tn), lambda i, j, kk: (0, j)))
        args.append(bias)
    if has_res:
        in_specs.append(pl.BlockSpec((tm, tn), lambda i, j, kk: (i, j)))
        args.append(res)
    return pl.pallas_call(
        body, name=name, grid=(m // tm, n // tn, nk),
        in_specs=in_specs, out_specs=pl.BlockSpec((tm, tn), lambda i, j, kk: (i, j)),
        out_shape=jax.ShapeDtypeStruct((m, n), out_dtype),
        scratch_shapes=[pltpu.VMEM((tm, tn), F32)],
        compiler_params=_cparams(dimension_semantics=("parallel", "parallel", "arbitrary")),
    )(*args)


def _rowwise(fn, rows, pars, outs, reds, *, tm, name):
    rows = [r if isinstance(r, tuple) else (r, r.shape[1], 0) for r in rows]
    t = rows[0][0].shape[0]
    assert t % tm == 0
    n_in, n_o = len(rows) + len(pars), len(outs)

    def body(*refs):
        i = pl.program_id(0)
        o, d = fn(*[r[...] for r in refs[:n_in]])
        for ref, val in zip(refs[n_in:n_in + n_o], o):
            ref[...] = val.astype(ref.dtype)
        d_refs = refs[n_in + n_o:]

        @pl.when(i == 0)
        def _():
            for ref in d_refs:
                ref[...] = jnp.zeros_like(ref)

        for ref, val in zip(d_refs, d):
            ref[...] += val

    in_specs = [pl.BlockSpec((tm, w), lambda i, b=blk: (i, b)) for _, w, blk in rows]
    in_specs += [pl.BlockSpec((1, p.shape[1]), lambda i: (0, 0)) for p in pars]
    out_specs = [pl.BlockSpec((tm, c), lambda i: (i, 0)) for c, _ in outs]
    out_specs += [pl.BlockSpec((1, c), lambda i: (0, 0)) for c in reds]
    out_shape = [jax.ShapeDtypeStruct((t, c), dt) for c, dt in outs] + [jax.ShapeDtypeStruct((1, c), F32) for c in reds]
    res = pl.pallas_call(
        body, name=name, grid=(t // tm,), in_specs=in_specs, out_specs=out_specs, out_shape=out_shape,
        compiler_params=_cparams(dimension_semantics=("arbitrary",)),
    )(*[r[0] for r in rows], *pars)
    return res[:n_o], res[n_o:]


def _rms_fwd(x, g, *, name):
    def fn(xb, gb):
        r = lax.rsqrt(jnp.mean(xb * xb, axis=-1, keepdims=True) + RMS_EPS)
        return [xb * r * gb], []
    (h,), _ = _rowwise(fn, [x], [g], [(x.shape[1], BF16)], [], tm=256, name=name)
    return h


def _rms_bwd(x, g, dh, dres, *, name):
    def fn(xb, dhb, drb, gb):
        r = lax.rsqrt(jnp.mean(xb * xb, axis=-1, keepdims=True) + RMS_EPS)
        xh = xb * r
        dxh = dhb * gb
        dx = r * (dxh - xh * jnp.mean(dxh * xh, axis=-1, keepdims=True))
        out = drb + dx
        return [out], [_rowsum(out), _rowsum(dhb * xh)]
    c = x.shape[1]
    (dx,), (colsum, dg) = _rowwise(fn, [x, dh, dres], [g], [(c, F32)], [c, c], tm=256, name=name)
    return dx, colsum, dg


def _loss_head(x, g, tgt, *, name):
    d_model = x.shape[1]

    def fn(xb, tb, gb):
        r = lax.rsqrt(jnp.mean(xb * xb, axis=-1, keepdims=True) + RMS_EPS)
        xh = xb * r
        e = xh * gb - tb
        dy = e * (1.0 / d_model)
        dxh = dy * gb
        dx = r * (dxh - xh * jnp.mean(dxh * xh, axis=-1, keepdims=True))
        return [dx], [_rowsum(e * e), _rowsum(dy * xh)]
    (dx,), (sq, dg) = _rowwise(fn, [x, tgt], [g], [(d_model, F32)], [d_model, d_model], tm=256, name=name)
    return dx, sq, dg


def _halo_rows(k):
    return 8 * ((k - 1 + 7) // 8) if k > 1 else 8


def _pad_taps(w):
    k = w.shape[0]
    kp = 8 * ((k + 7) // 8)
    return jnp.pad(w, ((0, kp - k), (0, 0)))


def _dwconv_fwd(x, w, b, *, name):
    t, c = x.shape
    k = w.shape[0]
    h = _halo_rows(k)
    tm = _pick(t, (256, 128))
    tc = _pick(c, (512, 256, 128))
    wp = _pad_taps(w)
    kp = wp.shape[0]
    rb = tm // h

    def body(x_ref, p_ref, w_ref, b_ref, o_ref, ext):
        i = pl.program_id(0)
        ext[pl.ds(h, tm), :] = x_ref[...]
        ext[pl.ds(0, h), :] = jnp.where(i > 0, p_ref[...], 0.0)
        acc = jnp.broadcast_to(b_ref[...], (tm, tc))
        for s in range(k):
            acc = acc + w_ref[k - 1 - s:k - s, :] * ext[pl.ds(h - s, tm), :]
        o_ref[...] = acc

    return pl.pallas_call(
        body, name=name, grid=(t // tm, c // tc),
        in_specs=[pl.BlockSpec((tm, tc), lambda i, j: (i, j)),
                  pl.BlockSpec((h, tc), lambda i, j: (jnp.maximum(i * rb - 1, 0), j)),
                  pl.BlockSpec((kp, tc), lambda i, j: (0, j)),
                  pl.BlockSpec((1, tc), lambda i, j: (0, j))],
        out_specs=pl.BlockSpec((tm, tc), lambda i, j: (i, j)),
        out_shape=jax.ShapeDtypeStruct((t, c), F32),
        scratch_shapes=[pltpu.VMEM((h + tm, tc), F32)],
        compiler_params=_cparams(dimension_semantics=("parallel", "parallel")),
    )(x, x, wp, b)


def _dwconv_bwd(x, dy, w, *, name):
    t, c = x.shape
    k = w.shape[0]
    h = _halo_rows(k)
    tm = _pick(t, (256, 128))
    tc = _pick(c, (512, 256, 128))
    wp = _pad_taps(w)
    kp = wp.shape[0]
    rb = tm // h
    nt = t // tm

    def body(x_ref, p_ref, dy_ref, n_ref, w_ref, dx_ref, dw_ref, db_ref, xext, dext):
        i = pl.program_id(1)

        @pl.when(i == 0)
        def _():
            dw_ref[...] = jnp.zeros_like(dw_ref)
            db_ref[...] = jnp.zeros_like(db_ref)

        xext[pl.ds(h, tm), :] = x_ref[...]
        xext[pl.ds(0, h), :] = jnp.where(i > 0, p_ref[...], 0.0)
        dyv = dy_ref[...]
        dext[pl.ds(0, tm), :] = dyv
        dext[pl.ds(tm, h), :] = jnp.where(i < nt - 1, n_ref[...], 0.0)
        acc = jnp.zeros((tm, tc), F32)
        for s in range(k):
            acc = acc + w_ref[k - 1 - s:k - s, :] * dext[pl.ds(s, tm), :]
            dw_ref[k - 1 - s:k - s, :] += _rowsum(xext[pl.ds(h - s, tm), :] * dyv)
        dx_ref[...] = acc
        db_ref[...] += _rowsum(dyv)

    dx, dw, db = pl.pallas_call(
        body, name=name, grid=(c // tc, nt),
        in_specs=[pl.BlockSpec((tm, tc), lambda j, i: (i, j)),
                  pl.BlockSpec((h, tc), lambda j, i: (jnp.maximum(i * rb - 1, 0), j)),
                  pl.BlockSpec((tm, tc), lambda j, i: (i, j)),
                  pl.BlockSpec((h, tc), lambda j, i: (jnp.minimum((i + 1) * rb, nt * rb - 1), j)),
                  pl.BlockSpec((kp, tc), lambda j, i: (0, j))],
        out_specs=[pl.BlockSpec((tm, tc), lambda j, i: (i, j)),
                   pl.BlockSpec((kp, tc), lambda j, i: (0, j)),
                   pl.BlockSpec((1, tc), lambda j, i: (0, j))],
        out_shape=[jax.ShapeDtypeStruct((t, c), F32), jax.ShapeDtypeStruct((kp, c), F32),
                   jax.ShapeDtypeStruct((1, c), F32)],
        scratch_shapes=[pltpu.VMEM((h + tm, tc), F32), pltpu.VMEM((tm + h, tc), F32)],
        compiler_params=_cparams(dimension_semantics=("parallel", "arbitrary")),
    )(x, x, dy, dy, wp)
    return dx, dw[:k], db


def _glu_fwd(u, b_in, *, name):
    d = u.shape[1] // 2

    def fn(ua, ug, ba, bg):
        return [(ua + ba) * _sigmoid(ug + bg)], []
    (v,), _ = _rowwise(fn, [(u, d, 0), (u, d, 1)], [b_in[:, :d], b_in[:, d:]], [(d, F32)], [], tm=256, name=name)
    return v


def _glu_bwd(u, b_in, dv, *, name):
    d = u.shape[1] // 2

    def fn(ua, ug, dvb, ba, bg):
        a = ua + ba
        s = _sigmoid(ug + bg)
        du = jnp.concatenate([dvb * s, dvb * a * s * (1.0 - s)], axis=1)
        return [du], [_rowsum(du)]
    (du,), (db,) = _rowwise(fn, [(u, d, 0), (u, d, 1), dv], [b_in[:, :d], b_in[:, d:]], [(2 * d, BF16)], [2 * d],
                            tm=256, name=name)
    return du, db


def _ln_silu_fwd(v, g, b, *, name):
    def fn(vb, gb, bb):
        mu = jnp.mean(vb, axis=-1, keepdims=True)
        xc = vb - mu
        rstd = lax.rsqrt(jnp.mean(xc * xc, axis=-1, keepdims=True) + LN_EPS)
        return [_silu(xc * rstd * gb + bb)], []
    (o,), _ = _rowwise(fn, [v], [g, b], [(v.shape[1], BF16)], [], tm=256, name=name)
    return o


def _ln_silu_bwd(v, g, b, do, *, name):
    def fn(vb, dob, gb, bb):
        mu = jnp.mean(vb, axis=-1, keepdims=True)
        xc = vb - mu
        rstd = lax.rsqrt(jnp.mean(xc * xc, axis=-1, keepdims=True) + LN_EPS)
        xh = xc * rstd
        dy = dob * _dsilu(xh * gb + bb)
        dxh = dy * gb
        dv = rstd * (dxh - jnp.mean(dxh, axis=-1, keepdims=True) - xh * jnp.mean(dxh * xh, axis=-1, keepdims=True))
        return [dv], [_rowsum(dy * xh), _rowsum(dy)]
    c = v.shape[1]
    (dv,), (dg, db) = _rowwise(fn, [v, do], [g, b], [(c, F32)], [c, c], tm=256, name=name)
    return dv, dg, db


def _ffn_gate_fwd(uc, *, name):
    f = uc.shape[1] // 2

    def fn(g, v):
        return [_silu(g) * v], []
    (hm,), _ = _rowwise(fn, [(uc, f, 0), (uc, f, 1)], [], [(f, BF16)], [], tm=128, name=name)
    return hm


def _ffn_gate_bwd(uc, dhm, *, name):
    f = uc.shape[1] // 2

    def fn(g, v, d):
        return [jnp.concatenate([d * v * _dsilu(g), d * _silu(g)], axis=1)], []
    (duc,), _ = _rowwise(fn, [(uc, f, 0), (uc, f, 1), dhm], [], [(2 * f, F32)], [], tm=128, name=name)
    return duc


def _ssm_act(conv, dtp_exp, bias_exp, aneg_exp, *, di, name):
    q = SSD_CHUNK
    gn = (conv.shape[1] - di) // 2

    def fn(cb, dtb, bb, ab):
        act = _silu(cb)
        dt = dtb + bb
        dt = jnp.maximum(dt, 0.0) + jnp.log(1.0 + jnp.exp(-jnp.abs(dt)))
        a = dt * ab
        tri = (lax.broadcasted_iota(jnp.int32, (q, q), 0) >= lax.broadcasted_iota(jnp.int32, (q, q), 1)).astype(F32)
        cs = _dot3(tri, a, (((1,), (0,)), ((), ())), 1)
        return [act[:, :di], act[:, di:di + gn], act[:, di + gn:], dt, cs], []
    outs, _ = _rowwise(fn, [conv, dtp_exp], [bias_exp, aneg_exp],
                       [(di, F32), (gn, F32), (gn, F32), (di, F32), (di, F32)], [], tm=q, name=name)
    return outs


def _head_masks(q):
    lane = lax.broadcasted_iota(jnp.int32, (q, LANES), 1)
    return lane < HEAD_DIM


def _pair_cols(cs, lo):
    sw = pltpu.roll(cs, HEAD_DIM, 1)
    return jnp.where(lo, cs, sw), jnp.where(lo, sw, cs)


def _ssd_fwd(xs, dt_exp, cs_exp, cs_rows, bm, cm, *, name):
    t, di = xs.shape
    q = SSD_CHUNK
    hg = di // N_GROUPS
    npair = hg // LANES
    nheads = hg // HEAD_DIM
    nc = t // q
    n = D_STATE

    def body(xs_ref, dt_ref, cs_ref, csr_ref, b_ref, c_ref, y_ref, st_ref, s_scr):
        ci = pl.program_id(1)

        @pl.when(ci == 0)
        def _():
            s_scr[...] = jnp.zeros_like(s_scr)

        bb = b_ref[...].astype(BF16)
        cb_ = c_ref[...].astype(BF16)
        cbm = lax.dot_general(cb_, bb, (((1,), (1,)), ((), ())), preferred_element_type=F32)
        tri = lax.broadcasted_iota(jnp.int32, (q, q), 0) >= lax.broadcasted_iota(jnp.int32, (q, q), 1)
        lo = _head_masks(q)
        csr = csr_ref[0]
        for p in range(npair):
            sl = pl.ds(p * LANES, LANES)
            x = xs_ref[:, sl] * dt_ref[:, sl]
            cs = cs_ref[:, sl]
            col0, col1 = _pair_cols(cs, lo)
            l0 = jnp.where(tri, jnp.exp(jnp.minimum(col0 - csr[2 * p:2 * p + 1, :], 0.0)), 0.0)
            l1 = jnp.where(tri, jnp.exp(jnp.minimum(col1 - csr[2 * p + 1:2 * p + 2, :], 0.0)), 0.0)
            xb = x.astype(BF16)
            yd = jnp.where(lo, jnp.dot((cbm * l0).astype(BF16), xb, preferred_element_type=F32),
                           jnp.dot((cbm * l1).astype(BF16), xb, preferred_element_type=F32))
            s = s_scr[p]
            st_ref[0, 0, p] = s
            yo = jnp.exp(cs) * jnp.dot(cb_, s.astype(BF16), preferred_element_type=F32)
            y_ref[:, sl] = yd + yo
            cs_end = cs[q - 1:q, :]
            xd = (x * jnp.exp(cs_end - cs)).astype(BF16)
            s_scr[p] = jnp.exp(cs_end) * s + lax.dot_general(bb, xd, (((0,), (0,)), ((), ())),
                                                             preferred_element_type=F32)

    return pl.pallas_call(
        body, name=name, grid=(N_GROUPS, nc),
        in_specs=[pl.BlockSpec((q, hg), lambda g, c: (c, g)),
                  pl.BlockSpec((q, hg), lambda g, c: (c, g)),
                  pl.BlockSpec((q, hg), lambda g, c: (c, g)),
                  pl.BlockSpec((1, nheads, q), lambda g, c: (g, 0, c)),
                  pl.BlockSpec((q, n), lambda g, c: (c, g)),
                  pl.BlockSpec((q, n), lambda g, c: (c, g))],
        out_specs=[pl.BlockSpec((q, hg), lambda g, c: (c, g)),
                   pl.BlockSpec((1, 1, npair, n, LANES), lambda g, c: (g, c, 0, 0, 0))],
        out_shape=[jax.ShapeDtypeStruct((t, di), F32),
                   jax.ShapeDtypeStruct((N_GROUPS, nc, npair, n, LANES), F32)],
        scratch_shapes=[pltpu.VMEM((npair, n, LANES), F32)],
        compiler_params=_cparams(dimension_semantics=("parallel", "arbitrary")),
    )(xs, dt_exp, cs_exp, cs_rows, bm, cm)


def _dot3(a, b, dims, split):
    rest = (a, b)[split].astype(F32)
    other = (a, b)[1 - split].astype(BF16)
    acc = None
    for _ in range(3):
        part = rest.astype(BF16)
        rest = rest - part.astype(F32)
        d = (lax.dot_general(part, other, dims, preferred_element_type=F32) if split == 0
             else lax.dot_general(other, part, dims, preferred_element_type=F32))
        acc = d if acc is None else acc + d
    return acc


def _ssd_bwd(xs, dt_exp, cs_exp, cs_rows, bm, cm, dy, states, aneg_exp, *, name):
    t, di = xs.shape
    q = SSD_CHUNK
    hg = di // N_GROUPS
    npair = hg // LANES
    nheads = hg // HEAD_DIM
    nc = t // q
    n = D_STATE
    nt_dims = (((1,), (1,)), ((), ()))
    tn_dims = (((0,), (0,)), ((), ()))

    mm_dims = (((1,), (0,)), ((), ()))

    def body(xs_ref, dt_ref, cs_ref, csr_ref, b_ref, c_ref, dy_ref, st_ref, an_ref,
             dxp_ref, db_ref, dc_ref, ddt_ref, dan_ref, r_scr):
        ci = pl.program_id(1)

        @pl.when(ci == 0)
        def _():
            r_scr[...] = jnp.zeros_like(r_scr)
            dan_ref[...] = jnp.zeros_like(dan_ref)

        bb = b_ref[...].astype(BF16)
        cb_ = c_ref[...].astype(BF16)
        cbm = lax.dot_general(cb_, bb, nt_dims, preferred_element_type=F32)
        row = lax.broadcasted_iota(jnp.int32, (q, q), 0)
        col = lax.broadcasted_iota(jnp.int32, (q, q), 1)
        tri = row >= col
        triu = (row <= col).astype(F32)
        trisl = (row > col).astype(F32)
        ones2 = (lax.broadcasted_iota(jnp.int32, (LANES, LANES), 0) // HEAD_DIM
                 == lax.broadcasted_iota(jnp.int32, (LANES, LANES), 1) // HEAD_DIM).astype(F32)
        onesq = jnp.ones((q, LANES), F32)
        last = lax.broadcasted_iota(jnp.int32, (q, LANES), 0) == q - 1
        lo = _head_masks(q)
        csr = csr_ref[0]
        dcb = jnp.zeros((q, q), F32)
        dc_acc = jnp.zeros((q, n), F32)
        db_acc = jnp.zeros((q, n), F32)
        for p in range(npair):
            sl = pl.ds(p * LANES, LANES)
            xsv = xs_ref[:, sl]
            dtv = dt_ref[:, sl]
            x = xsv * dtv
            cs = cs_ref[:, sl]
            dyv = dy_ref[:, sl]
            col0, col1 = _pair_cols(cs, lo)
            l0 = jnp.where(tri, jnp.exp(jnp.minimum(col0 - csr[2 * p:2 * p + 1, :], 0.0)), 0.0)
            l1 = jnp.where(tri, jnp.exp(jnp.minimum(col1 - csr[2 * p + 1:2 * p + 2, :], 0.0)), 0.0)
            xb = x.astype(BF16)
            dyb = dyv.astype(BF16)
            g0 = lax.dot_general(jnp.where(lo, dyv, 0.0).astype(BF16), xb, nt_dims, preferred_element_type=F32)
            g1 = lax.dot_general(jnp.where(lo, 0.0, dyv).astype(BF16), xb, nt_dims, preferred_element_type=F32)
            gl0, gl1 = g0 * l0, g1 * l1
            dcb = dcb + gl0 + gl1
            w0, w1 = cbm * gl0, cbm * gl1
            dxd = jnp.where(lo,
                            lax.dot_general((cbm * l0).astype(BF16), dyb, tn_dims, preferred_element_type=F32),
                            lax.dot_general((cbm * l1).astype(BF16), dyb, tn_dims, preferred_element_type=F32))
            e = jnp.exp(cs)
            cs_end = cs[q - 1:q, :]
            dte = jnp.exp(cs_end - cs)
            dend = jnp.exp(cs_end)
            sf = st_ref[0, 0, p]
            sb = sf.astype(BF16)
            r = r_scr[p]
            rb = r.astype(BF16)
            dyeb = (dyv * e).astype(BF16)
            dc_acc = dc_acc + lax.dot_general(dyeb, sb, nt_dims, preferred_element_type=F32)
            dxo = dte * jnp.dot(bb, rb, preferred_element_type=F32)
            db_acc = db_acc + lax.dot_general((x * dte).astype(BF16), rb, nt_dims, preferred_element_type=F32)
            r_scr[p] = dend * r + lax.dot_general(cb_, dyeb, tn_dims, preferred_element_type=F32)
            dx = dxd + dxo
            dxp_ref[:, sl] = dx
            yoff = e * jnp.dot(cb_, sb, preferred_element_type=F32)
            rw = jnp.where(lo, _dot3(w0, onesq, mm_dims, 0), _dot3(w1, onesq, mm_dims, 0))
            cw = jnp.where(lo, _dot3(w0, onesq, tn_dims, 0), _dot3(w1, onesq, tn_dims, 0))
            through = jnp.where(last, dend * _rowsum(r * sf), 0.0)
            suf = _dot3(dyv * yoff + through, ones2, mm_dims, 0) + rw - cw
            pre = _dot3(dxo * x, ones2, mm_dims, 0)
            da = _dot3(triu, suf, mm_dims, 1) + _dot3(trisl, pre, mm_dims, 1)
            qs = _dot3(dx * xsv, ones2, mm_dims, 0)
            ddt_ref[:, sl] = da * an_ref[:, sl] + qs
            dan_ref[:, sl] += _rowsum(da * dtv)
        dcbb = dcb.astype(BF16)
        dc_ref[...] = dc_acc + jnp.dot(dcbb, bb, preferred_element_type=F32)
        db_ref[...] = db_acc + lax.dot_general(dcbb, cb_, tn_dims, preferred_element_type=F32)

    rev = lambda g, c: (nc - 1 - c, g)
    return pl.pallas_call(
        body, name=name, grid=(N_GROUPS, nc),
        in_specs=[pl.BlockSpec((q, hg), rev), pl.BlockSpec((q, hg), rev), pl.BlockSpec((q, hg), rev),
                  pl.BlockSpec((1, nheads, q), lambda g, c: (g, 0, nc - 1 - c)),
                  pl.BlockSpec((q, n), rev), pl.BlockSpec((q, n), rev),
                  pl.BlockSpec((q, hg), rev),
                  pl.BlockSpec((1, 1, npair, n, LANES), lambda g, c: (g, nc - 1 - c, 0, 0, 0)),
                  pl.BlockSpec((1, hg), lambda g, c: (0, g))],
        out_specs=[pl.BlockSpec((q, hg), rev), pl.BlockSpec((q, n), rev), pl.BlockSpec((q, n), rev),
                   pl.BlockSpec((q, hg), rev), pl.BlockSpec((1, hg), lambda g, c: (0, g))],
        out_shape=[jax.ShapeDtypeStruct((t, di), F32), jax.ShapeDtypeStruct((t, N_GROUPS * n), F32),
                   jax.ShapeDtypeStruct((t, N_GROUPS * n), F32), jax.ShapeDtypeStruct((t, di), F32),
                   jax.ShapeDtypeStruct((1, di), F32)],
        scratch_shapes=[pltpu.VMEM((npair, n, LANES), F32)],
        compiler_params=_cparams(dimension_semantics=("parallel", "arbitrary")),
    )(xs, dt_exp, cs_exp, cs_rows, bm, cm, dy, states, aneg_exp)


def _group_stats(w, gw):
    return [lax.rsqrt(jnp.mean(w[:, i * gw:(i + 1) * gw] ** 2, axis=-1, keepdims=True) + RMS_EPS)
            for i in range(N_GROUPS)]


def _gated_norm_fwd(y_ssd, xs, z, d_exp, g, *, name):
    di = xs.shape[1]
    gw = di // N_GROUPS

    def fn(yb, xb, zb, db, gb):
        w = (yb + db * xb) * _silu(zb)
        rs = _group_stats(w, gw)
        return [jnp.concatenate([w[:, i * gw:(i + 1) * gw] * rs[i] for i in range(N_GROUPS)], axis=1) * gb], []
    (o,), _ = _rowwise(fn, [y_ssd, xs, (z, di, 0)], [d_exp, g], [(di, BF16)], [], tm=128, name=name)
    return o


def _gated_norm_bwd(y_ssd, xs, z, d_exp, g, do, *, name):
    di = xs.shape[1]
    gw = di // N_GROUPS

    def fn(yb, xb, zb, dob, db, gb):
        yy = yb + db * xb
        sz = _silu(zb)
        w = yy * sz
        rs = _group_stats(w, gw)
        dwh = dob * gb
        wh_parts, dw_parts = [], []
        for i in range(N_GROUPS):
            sl = slice(i * gw, (i + 1) * gw)
            wh = w[:, sl] * rs[i]
            wh_parts.append(wh)
            dw_parts.append(rs[i] * (dwh[:, sl] - wh * jnp.mean(dwh[:, sl] * wh, axis=-1, keepdims=True)))
        wh = jnp.concatenate(wh_parts, axis=1)
        dw = jnp.concatenate(dw_parts, axis=1)
        dy = dw * sz
        dz = dw * yy * _dsilu(zb)
        return [dy, dz], [_rowsum(dob * wh), _rowsum(dy * xb)]
    (dy, dz), (dg, dd) = _rowwise(fn, [y_ssd, xs, (z, di, 0), do], [d_exp, g], [(di, F32), (di, BF16)], [di, di],
                                  tm=128, name=name)
    return dy, dz, dg, dd


def _ssm_act_bwd(conv, dtp_exp, dxp, dy, dbm, dcm, ddt_exp, dt_exp, bias_exp, d_exp, *, di, name):
    gn = dbm.shape[1]

    def fn(cb, dtb, dxpb, dyb, dbb, dcb, ddtb, dteb, bb, db):
        dxs = dxpb * dteb + dyb * db
        dact = jnp.concatenate([dxs, dbb, dcb], axis=1)
        dconv = dact * _dsilu(cb)
        ddtp = ddtb * _sigmoid(dtb + bb)
        return [dconv, ddtp], [_rowsum(ddtp)]
    (dconv, ddtp), (dbias,) = _rowwise(fn, [conv, dtp_exp, dxp, dy, dbm, dcm, ddt_exp, dt_exp], [bias_exp, d_exp],
                                       [(di + 2 * gn, F32), (di, F32)], [di], tm=64, name=name)
    return dconv, ddtp, dbias


def _adamw(w, g, m, v, *, name):
    r, c = w.shape
    c1 = 1.0 / (1.0 - ADAM_B1 ** ADAM_STEP)
    c2 = 1.0 / (1.0 - ADAM_B2 ** ADAM_STEP)

    def fn(wb, gb, mb, vb):
        mn = ADAM_B1 * mb + (1.0 - ADAM_B1) * gb
        vn = ADAM_B2 * vb + (1.0 - ADAM_B2) * (gb * gb)
        delta = -ADAM_LR * ((mn * c1) / (jnp.sqrt(vn * c2) + ADAM_EPS) + ADAM_WD * wb)
        return [delta, mn, vn], []
    cap = max(8, ADAMW_BLOCK_ELEMS // c)
    tm = _pick(r, [p for p in (512, 256, 128, 64, 32, 16, 8) if p <= cap])
    (d, mn, vn), _ = _rowwise(fn, [w, g, m, v], [], [(c, F32)] * 3, [], tm=tm, name=name)
    return d, mn, vn


def _add_pair(sel, g, r, *, name):
    _, _, rows, cols = g.shape
    tm = _pick(rows, (512, 256, 128, 64, 32, 16))

    def body(s_ref, g_ref, r_ref, o_ref):
        o_ref[...] = (g_ref[...].astype(F32) + r_ref[...].astype(F32)).astype(BF16)

    return pl.pallas_call(
        body, name=name,
        grid_spec=pltpu.PrefetchScalarGridSpec(
            num_scalar_prefetch=1, grid=(N_CHIPS, rows // tm),
            in_specs=[pl.BlockSpec((None, None, tm, cols), lambda j, i, s: (j, s[0], i, 0)),
                      pl.BlockSpec((None, tm, cols), lambda j, i, s: (j, i, 0))],
            out_specs=pl.BlockSpec((None, tm, cols), lambda j, i, s: (j, i, 0))),
        out_shape=jax.ShapeDtypeStruct((N_CHIPS, rows, cols), BF16),
        compiler_params=_cparams(dimension_semantics=("parallel", "parallel")),
    )(sel, g, r)


def _add_four(sel, p, r, *, name):
    _, rows, cols = p.shape
    tm = _pick(rows, (512, 256, 128, 64, 32, 16))

    def body(s_ref, p_ref, r0, r1, r2, o_ref):
        o_ref[...] = ((p_ref[...].astype(F32) + r0[...].astype(F32)) + r1[...].astype(F32)) + r2[...].astype(F32)

    rspec = lambda k: pl.BlockSpec((None, tm, cols), lambda i, s, k=k: (k, i, 0))
    return pl.pallas_call(
        body, name=name,
        grid_spec=pltpu.PrefetchScalarGridSpec(
            num_scalar_prefetch=1, grid=(rows // tm,),
            in_specs=[pl.BlockSpec((None, tm, cols), lambda i, s: (s[0], i, 0)), rspec(0), rspec(1), rspec(2)],
            out_specs=pl.BlockSpec((None, tm, cols), lambda i, s: (s[1], i, 0))),
        out_shape=jax.ShapeDtypeStruct((2, rows, cols), F32),
        compiler_params=_cparams(dimension_semantics=("parallel",)),
    )(sel, p, r, r, r)


def _sum8(g, *, name):
    _, rows, cols = g.shape
    tm = _pick(rows, (512, 256, 128, 64, 32, 16, 8))

    def body(g_ref, o_ref):
        acc = g_ref[0]
        for k in range(1, 8):
            acc = acc + g_ref[k]
        o_ref[...] = acc

    return pl.pallas_call(
        body, name=name, grid=(rows // tm,),
        in_specs=[pl.BlockSpec((8, tm, cols), lambda i: (0, i, 0))],
        out_specs=pl.BlockSpec((tm, cols), lambda i: (i, 0)),
        out_shape=jax.ShapeDtypeStruct((rows, cols), F32),
        compiler_params=_cparams(dimension_semantics=("parallel",)),
    )(g)


def _place():
    x, y, c = lax.axis_index("x"), lax.axis_index("y"), lax.axis_index("c")
    chips = [(1 - x, y), (x, 1 - y), (1 - x, 1 - y)]
    return x, y, c, chips


def _rcopy(src, dst, send_sems, recv_sems, k, to):
    return pltpu.make_async_remote_copy(src_ref=src, dst_ref=dst, send_sem=send_sems.at[k], recv_sem=recv_sems.at[k],
                                        device_id=to, device_id_type=MESH)


def _gather_chips(pack, *, name):
    _, rows, cols = pack.shape

    def body(src, out, send_sems, recv_sems):
        x, y, c, chips = _place()
        sibling = (x, y, 1 - c)
        me = 2 * x + y
        first = [_rcopy(src.at[c], out.at[me, c], send_sems, recv_sems, k, (cx, cy, c)) for k, (cx, cy) in enumerate(chips)]
        for cp in first:
            cp.start()
        passed = []
        for k, (cx, cy) in enumerate(chips):
            blk = out.at[2 * cx + cy, c]
            _rcopy(blk, blk, send_sems, recv_sems, k, (cx, cy, c)).wait_recv()
            fw = _rcopy(blk, blk, send_sems, recv_sems, 3 + k, sibling)
            fw.start()
            passed.append(fw)
        for k, (cx, cy) in enumerate(chips):
            blk = out.at[2 * cx + cy, 1 - c]
            _rcopy(blk, blk, send_sems, recv_sems, 3 + k, sibling).wait_recv()
        for cp in first + passed:
            cp.wait_send()

    return pl.pallas_call(
        body, name=name, in_specs=[ANY], out_specs=ANY,
        out_shape=jax.ShapeDtypeStruct((N_CHIPS, 2, rows, cols), pack.dtype),
        scratch_shapes=[pltpu.SemaphoreType.DMA((6,)), pltpu.SemaphoreType.DMA((6,))],
    )(pack)


def _gather_devices(pack, *, name):
    rows, cols = pack.shape

    def body(src, out, send_sems, recv_sems, local_sem):
        x, y, c, chips = _place()
        sibling = (x, y, 1 - c)

        def blk(px, py, pc):
            return out.at[4 * px + 2 * py + pc]

        mine = pltpu.make_async_copy(src, blk(x, y, c), local_sem)
        mine.start()
        first = [_rcopy(src, blk(x, y, c), send_sems, recv_sems, 0, sibling)]
        first += [_rcopy(src, blk(x, y, c), send_sems, recv_sems, 1 + k, (cx, cy, c)) for k, (cx, cy) in enumerate(chips)]
        for cp in first:
            cp.start()
        passed = []
        for k, (cx, cy) in enumerate(chips):
            b = blk(cx, cy, c)
            _rcopy(b, b, send_sems, recv_sems, 1 + k, (cx, cy, c)).wait_recv()
            fw = _rcopy(b, b, send_sems, recv_sems, 4 + k, sibling)
            fw.start()
            passed.append(fw)
        b = blk(x, y, 1 - c)
        _rcopy(b, b, send_sems, recv_sems, 0, sibling).wait_recv()
        for k, (cx, cy) in enumerate(chips):
            b = blk(cx, cy, 1 - c)
            _rcopy(b, b, send_sems, recv_sems, 4 + k, sibling).wait_recv()
        for cp in first + passed:
            cp.wait_send()
        mine.wait()

    return pl.pallas_call(
        body, name=name, in_specs=[ANY], out_specs=ANY,
        out_shape=jax.ShapeDtypeStruct((8, rows, cols), pack.dtype),
        scratch_shapes=[pltpu.SemaphoreType.DMA((7,)), pltpu.SemaphoreType.DMA((7,)), pltpu.SemaphoreType.DMA],
    )(pack)


def _swap_halves(g, *, name):
    _, _, rows, cols = g.shape

    def body(src, out, send_sems, recv_sems):
        x, y, c, _ = _place()
        cps = [_rcopy(src.at[j, 1 - c], out.at[j], send_sems, recv_sems, j, (x, y, 1 - c)) for j in range(N_CHIPS)]
        for cp in cps:
            cp.start()
        for cp in cps:
            cp.wait()

    return pl.pallas_call(
        body, name=name, in_specs=[ANY], out_specs=ANY,
        out_shape=jax.ShapeDtypeStruct((N_CHIPS, rows, cols), g.dtype),
        scratch_shapes=[pltpu.SemaphoreType.DMA((N_CHIPS,)), pltpu.SemaphoreType.DMA((N_CHIPS,))],
    )(g)


def _scatter_chips(p, *, name):
    _, rows, cols = p.shape

    def body(src, out, send_sems, recv_sems):
        x, y, c, chips = _place()
        cps = [_rcopy(src.at[2 * cx + cy], out.at[k], send_sems, recv_sems, k, (cx, cy, c))
               for k, (cx, cy) in enumerate(chips)]
        for cp in cps:
            cp.start()
        for cp in cps:
            cp.wait()

    return pl.pallas_call(
        body, name=name, in_specs=[ANY], out_specs=ANY,
        out_shape=jax.ShapeDtypeStruct((3, rows, cols), p.dtype),
        scratch_shapes=[pltpu.SemaphoreType.DMA((3,)), pltpu.SemaphoreType.DMA((3,))],
    )(p)


def _join_halves(r, *, name):
    def body(src, out, send_sems, recv_sems):
        x, y, c, _ = _place()
        cp = _rcopy(src.at[c], out.at[c], send_sems, recv_sems, 0, (x, y, 1 - c))
        cp.start()
        b = out.at[1 - c]
        _rcopy(b, b, send_sems, recv_sems, 0, (x, y, 1 - c)).wait_recv()
        cp.wait_send()

    return pl.pallas_call(
        body, name=name, in_specs=[ANY], out_specs=ANY,
        out_shape=jax.ShapeDtypeStruct(r.shape, r.dtype), input_output_aliases={0: 0},
        scratch_shapes=[pltpu.SemaphoreType.DMA((1,)), pltpu.SemaphoreType.DMA((1,))],
    )(r)


def _reduce_scatter(g, *, name):
    x, y, c = lax.axis_index("x"), lax.axis_index("y"), lax.axis_index("c")
    sel_c = jnp.reshape(c, (1,)).astype(jnp.int32)
    sel_j = jnp.stack([2 * x + y, c]).astype(jnp.int32)
    got = _swap_halves(g, name=name + "_swap")
    pair = _add_pair(sel_c, g, got, name=name + "_add2")
    got3 = _scatter_chips(pair, name=name + "_scatter")
    half = _add_four(sel_j, pair, got3, name=name + "_add4")
    return _join_halves(half, name=name + "_join")


def _col_to_chips(w, nq):
    k = w.shape[0]
    return w.reshape(k, N_CHIPS, nq).transpose(1, 0, 2).reshape(N_CHIPS, k * nq)


def _chips_to_col(g, k, nq):
    return g.reshape(N_CHIPS, k, nq).transpose(1, 0, 2).reshape(k, N_CHIPS * nq)


def _flat_rows(parts, cols):
    flat = jnp.concatenate([p.reshape(-1) for p in parts])
    n = flat.shape[0]
    rows = -(-n // cols)
    rows = 8 * (-(-rows // 8))
    return jnp.pad(flat, (0, rows * cols - n)).reshape(rows, cols)


def _expand(v, di):
    return jnp.repeat(v, HEAD_DIM).reshape(1, di)


def kernel(x, norm_mix_g, norm_ffn_g, norm_final_g, cv_w_in, cv_b_in, cv_w_dw, cv_b_dw, cv_ln_g, cv_ln_b, cv_w_out, cv_b_out, ssm_w_in, ssm_w_conv, ssm_b_conv, ssm_dt_bias, ssm_a_log, ssm_d, ssm_norm_g, ssm_w_out, ffn_w_up, ffn_w_dw, ffn_b_dw, ffn_w_down, loss_target, m_norm_mix_g, m_norm_ffn_g, m_norm_final_g, m_cv_w_in, m_cv_b_in, m_cv_w_dw, m_cv_b_dw, m_cv_ln_g, m_cv_ln_b, m_cv_w_out, m_cv_b_out, m_ssm_w_in, m_ssm_w_conv, m_ssm_b_conv, m_ssm_dt_bias, m_ssm_a_log, m_ssm_d, m_ssm_norm_g, m_ssm_w_out, m_ffn_w_up, m_ffn_w_dw, m_ffn_b_dw, m_ffn_w_down, v_norm_mix_g, v_norm_ffn_g, v_norm_final_g, v_cv_w_in, v_cv_b_in, v_cv_w_dw, v_cv_b_dw, v_cv_ln_g, v_cv_ln_b, v_cv_w_out, v_cv_b_out, v_ssm_w_in, v_ssm_w_conv, v_ssm_b_conv, v_ssm_dt_bias, v_ssm_a_log, v_ssm_d, v_ssm_norm_g, v_ssm_w_out, v_ffn_w_up, v_ffn_w_dw, v_ffn_b_dw, v_ffn_w_down):
    weights = dict(norm_mix_g=norm_mix_g, norm_ffn_g=norm_ffn_g, norm_final_g=norm_final_g, cv_w_in=cv_w_in, cv_b_in=cv_b_in, cv_w_dw=cv_w_dw, cv_b_dw=cv_b_dw, cv_ln_g=cv_ln_g, cv_ln_b=cv_ln_b, cv_w_out=cv_w_out, cv_b_out=cv_b_out, ssm_w_in=ssm_w_in, ssm_w_conv=ssm_w_conv, ssm_b_conv=ssm_b_conv, ssm_dt_bias=ssm_dt_bias, ssm_a_log=ssm_a_log, ssm_d=ssm_d, ssm_norm_g=ssm_norm_g, ssm_w_out=ssm_w_out, ffn_w_up=ffn_w_up, ffn_w_dw=ffn_w_dw, ffn_b_dw=ffn_b_dw, ffn_w_down=ffn_w_down)
    mom_m = dict(norm_mix_g=m_norm_mix_g, norm_ffn_g=m_norm_ffn_g, norm_final_g=m_norm_final_g, cv_w_in=m_cv_w_in, cv_b_in=m_cv_b_in, cv_w_dw=m_cv_w_dw, cv_b_dw=m_cv_b_dw, cv_ln_g=m_cv_ln_g, cv_ln_b=m_cv_ln_b, cv_w_out=m_cv_w_out, cv_b_out=m_cv_b_out, ssm_w_in=m_ssm_w_in, ssm_w_conv=m_ssm_w_conv, ssm_b_conv=m_ssm_b_conv, ssm_dt_bias=m_ssm_dt_bias, ssm_a_log=m_ssm_a_log, ssm_d=m_ssm_d, ssm_norm_g=m_ssm_norm_g, ssm_w_out=m_ssm_w_out, ffn_w_up=m_ffn_w_up, ffn_w_dw=m_ffn_w_dw, ffn_b_dw=m_ffn_b_dw, ffn_w_down=m_ffn_w_down)
    mom_v = dict(norm_mix_g=v_norm_mix_g, norm_ffn_g=v_norm_ffn_g, norm_final_g=v_norm_final_g, cv_w_in=v_cv_w_in, cv_b_in=v_cv_b_in, cv_w_dw=v_cv_w_dw, cv_b_dw=v_cv_b_dw, cv_ln_g=v_cv_ln_g, cv_ln_b=v_cv_ln_b, cv_w_out=v_cv_w_out, cv_b_out=v_cv_b_out, ssm_w_in=v_ssm_w_in, ssm_w_conv=v_ssm_w_conv, ssm_b_conv=v_ssm_b_conv, ssm_dt_bias=v_ssm_dt_bias, ssm_a_log=v_ssm_a_log, ssm_d=v_ssm_d, ssm_norm_g=v_ssm_norm_g, ssm_w_out=v_ssm_w_out, ffn_w_up=v_ffn_w_up, ffn_w_dw=v_ffn_w_dw, ffn_b_dw=v_ffn_b_dw, ffn_w_down=v_ffn_w_down)
    names = list(weights)

    xt = x[0]
    tgt = loss_target[0]
    t, d = xt.shape
    depth = norm_mix_g.shape[0]
    n_cv, n_ssm = cv_w_in.shape[0], ssm_w_in.shape[0]
    di = ssm_w_out.shape[1] * N_CHIPS
    n_heads = di // HEAD_DIM
    gn = N_GROUPS * D_STATE
    ssm_in = ssm_w_in.shape[2] * N_CHIPS
    chip = 2 * lax.axis_index("x") + lax.axis_index("y")

    cq = ssm_w_in.shape[2]
    cqp = LANES * (-(-cq // LANES))
    big = []
    for i in range(depth):
        j = i // 2
        if i % 2 == 0:
            big += [("cv_w_in", j, d, cv_w_in.shape[2], True), ("cv_w_out", j, cv_w_out.shape[1], d, False)]
        else:
            big += [("ssm_w_in", j, d, cqp, True), ("ssm_w_out", j, ssm_w_out.shape[1], d, False)]
        big += [("ffn_w_up", i, d, ffn_w_up.shape[2], True), ("ffn_w_down", i, ffn_w_down.shape[1], d, False)]
    offs, o = [], 0
    for _, _, r, c, _ in big:
        offs.append(o)
        o += r * c
    pack_n = PACK_QUANTUM * (-(-o // PACK_QUANTUM))
    pack_pad = pack_n - o
    half_rows = pack_n // (2 * PACK_COLS)

    def shard_flat(nm, l):
        w = weights[nm][l]
        if nm == "ssm_w_in":
            w = jnp.pad(w, ((0, 0), (0, cqp - cq)))
        return w.reshape(-1)

    wpack = jnp.concatenate([shard_flat(nm, l) for nm, l, _, _, _ in big] + [jnp.zeros((pack_pad,), F32)]).astype(BF16)
    wpack = lax.optimization_barrier(wpack.reshape(2, half_rows, PACK_COLS))
    gathered = _gather_chips(wpack, name="gather_weights")
    gathered = lax.dynamic_update_index_in_dim(gathered, wpack, chip, 0).reshape(N_CHIPS, pack_n)
    full = {}
    for (nm, l, r, c, by_col), o in zip(big, offs):
        blk = gathered[:, o:o + r * c]
        full[nm, l] = _chips_to_col(blk, r, c) if by_col else blk.reshape(N_CHIPS * r, c)

    def ssm_cols(a, lo, hi):
        parts = []
        for jj in range(N_CHIPS):
            s0, s1 = max(lo, jj * cq), min(hi, (jj + 1) * cq)
            if s0 < s1:
                parts.append(a[:, jj * cqp + s0 - jj * cq:jj * cqp + s1 - jj * cq])
        return parts[0] if len(parts) == 1 else jnp.concatenate(parts, axis=1)

    def ssm_cols_back(a):
        return jnp.concatenate([jnp.pad(a[:, jj * cq:(jj + 1) * cq], ((0, 0), (0, cqp - cq))) for jj in range(N_CHIPS)],
                               axis=1)

    small_sharded = ["cv_w_dw", "ssm_w_conv", "ssm_b_conv", "ssm_norm_g", "ffn_w_dw"]
    spack = _flat_rows([weights[nm] for nm in small_sharded], LANES)
    sg = _gather_devices(spack, name="gather_small").reshape(8, -1)[::2]
    o = 0
    for nm in small_sharded:
        shp = weights[nm].shape
        n = weights[nm].size
        full[nm] = jnp.concatenate([sg[j, o:o + n].reshape(shp) for j in range(N_CHIPS)], axis=-1)
        o += n

    row = lambda v: v.reshape(1, -1)

    saved = []
    xc = xt
    for i in range(depth):
        j = i // 2
        s = {"x_in": xc}
        h = _rms_fwd(xc, row(norm_mix_g[i]), name="rms_mix_fwd")
        s["h"] = h
        if i % 2 == 0:
            u = _matmul(h, full["cv_w_in", j], name="cv_in_fwd")
            v1 = _glu_fwd(u, row(cv_b_in[j]), name="cv_glu_fwd")
            v2 = _dwconv_fwd(v1, full["cv_w_dw"][j], row(cv_b_dw[j]), name="cv_dw_fwd")
            v4 = _ln_silu_fwd(v2, row(cv_ln_g[j]), row(cv_ln_b[j]), name="cv_ln_fwd")
            xc = _matmul(v4, full["cv_w_out", j], bias=row(cv_b_out[j]), res=xc, name="cv_out_fwd")
            s.update(u=u, v1=v1, v2=v2, v4=v4)
        else:
            zx = _matmul(h, full["ssm_w_in", j], name="ssm_in_fwd")
            z = ssm_cols(zx, 0, di)
            xbc_pre = ssm_cols(zx, di, 2 * di + 2 * gn)
            dtp_exp = jnp.repeat(ssm_cols(zx, 2 * di + 2 * gn, ssm_in), HEAD_DIM, axis=1)
            conv = _dwconv_fwd(xbc_pre, full["ssm_w_conv"][j], row(full["ssm_b_conv"][j]), name="ssm_dw_fwd")
            bias_exp = _expand(ssm_dt_bias[j], di)
            aneg_exp = _expand(-jnp.exp(ssm_a_log[j]), di)
            d_exp = _expand(ssm_d[j], di)
            xs, bm, cm, dt_exp, cs_exp = _ssm_act(conv, dtp_exp, bias_exp, aneg_exp, di=di, name="ssm_act_fwd")
            cs_rows = cs_exp[:, ::HEAD_DIM].T.reshape(N_GROUPS, n_heads // N_GROUPS, t)
            y_ssd, states = _ssd_fwd(xs, dt_exp, cs_exp, cs_rows, bm, cm, name="ssd_fwd")
            gnrm = _gated_norm_fwd(y_ssd, xs, z, d_exp, row(full["ssm_norm_g"][j]), name="ssm_norm_fwd")
            xc = _matmul(gnrm, full["ssm_w_out", j], res=xc, name="ssm_out_fwd")
            s.update(z=z, xbc_pre=xbc_pre, dtp_exp=dtp_exp, conv=conv, bias_exp=bias_exp, aneg_exp=aneg_exp,
                     d_exp=d_exp, xs=xs, bm=bm, cm=cm, dt_exp=dt_exp, cs_exp=cs_exp, cs_rows=cs_rows, y_ssd=y_ssd,
                     states=states, gnrm=gnrm)
        s["x_mid"] = xc
        h2 = _rms_fwd(xc, row(norm_ffn_g[i]), name="rms_ffn_fwd")
        u2 = _matmul(h2, full["ffn_w_up", i], name="ffn_up_fwd")
        uc = _dwconv_fwd(u2, full["ffn_w_dw"][i], row(ffn_b_dw[i]), name="ffn_dw_fwd")
        hm = _ffn_gate_fwd(uc, name="ffn_gate_fwd")
        xc = _matmul(hm, full["ffn_w_down", i], res=xc, name="ffn_down_fwd")
        s.update(h2=h2, u2=u2, uc=uc, hm=hm)
        saved.append(s)

    dx, sq, dg_final = _loss_head(xc, row(norm_final_g), tgt, name="loss_head")
    loss_part = 0.5 / d * jnp.sum(sq)
    gr = {nm: [None] * weights[nm].shape[0] for nm in names if nm != "norm_final_g"}
    gbig = {}
    for i in reversed(range(depth)):
        j = i // 2
        s = saved[i]
        dxb = dx.astype(BF16)
        gbig["ffn_w_down", i] = _matmul(s["hm"], dxb, ta=True, out_dtype=BF16, name="ffn_down_dw")
        dhm = _matmul(dxb, full["ffn_w_down", i], tb=True, name="ffn_down_dx")
        duc = _ffn_gate_bwd(s["uc"], dhm, name="ffn_gate_bwd")
        du2, dw_dw, db_dw = _dwconv_bwd(s["u2"], duc, full["ffn_w_dw"][i], name="ffn_dw_bwd")
        gr["ffn_w_dw"][i], gr["ffn_b_dw"][i] = dw_dw, db_dw[0]
        du2b = du2.astype(BF16)
        gbig["ffn_w_up", i] = _matmul(s["h2"], du2b, ta=True, out_dtype=BF16, name="ffn_up_dw")
        dh2 = _matmul(du2b, full["ffn_w_up", i], tb=True, name="ffn_up_dx")
        dx, colsum, dg = _rms_bwd(s["x_mid"], row(norm_ffn_g[i]), dh2, dx, name="rms_ffn_bwd")
        gr["norm_ffn_g"][i] = dg[0]
        dxb = dx.astype(BF16)
        if i % 2 == 0:
            gr["cv_b_out"][j] = colsum[0]
            gbig["cv_w_out", j] = _matmul(s["v4"], dxb, ta=True, out_dtype=BF16, name="cv_out_dw")
            dv4 = _matmul(dxb, full["cv_w_out", j], tb=True, name="cv_out_dx")
            dv2, dlg, dlb = _ln_silu_bwd(s["v2"], row(cv_ln_g[j]), row(cv_ln_b[j]), dv4, name="cv_ln_bwd")
            gr["cv_ln_g"][j], gr["cv_ln_b"][j] = dlg[0], dlb[0]
            dv1, dw_dw, db_dw = _dwconv_bwd(s["v1"], dv2, full["cv_w_dw"][j], name="cv_dw_bwd")
            gr["cv_w_dw"][j], gr["cv_b_dw"][j] = dw_dw, db_dw[0]
            du, db_in = _glu_bwd(s["u"], row(cv_b_in[j]), dv1, name="cv_glu_bwd")
            gr["cv_b_in"][j] = db_in[0]
            gbig["cv_w_in", j] = _matmul(s["h"], du, ta=True, out_dtype=BF16, name="cv_in_dw")
            dh = _matmul(du, full["cv_w_in", j], tb=True, name="cv_in_dx")
        else:
            gbig["ssm_w_out", j] = _matmul(s["gnrm"], dxb, ta=True, out_dtype=BF16, name="ssm_out_dw")
            dgn = _matmul(dxb, full["ssm_w_out", j], tb=True, name="ssm_out_dx")
            dy, dz, dng, ddl = _gated_norm_bwd(s["y_ssd"], s["xs"], s["z"], s["d_exp"], row(full["ssm_norm_g"][j]),
                                               dgn, name="ssm_norm_bwd")
            gr["ssm_norm_g"][j] = dng[0]
            gr["ssm_d"][j] = ddl.reshape(n_heads, HEAD_DIM).sum(axis=1)
            dxp, dbm, dcm, ddt_exp, dan = _ssd_bwd(s["xs"], s["dt_exp"], s["cs_exp"], s["cs_rows"], s["bm"], s["cm"],
                                                   dy, s["states"], s["aneg_exp"], name="ssd_bwd")
            gr["ssm_a_log"][j] = dan[0, ::HEAD_DIM] * s["aneg_exp"][0, ::HEAD_DIM]
            dconv, ddtp, dbias = _ssm_act_bwd(s["conv"], s["dtp_exp"], dxp, dy, dbm, dcm, ddt_exp, s["dt_exp"],
                                              s["bias_exp"], s["d_exp"], di=di, name="ssm_act_bwd")
            gr["ssm_dt_bias"][j] = dbias[0, ::HEAD_DIM]
            dxbc, dw_c, db_c = _dwconv_bwd(s["xbc_pre"], dconv, full["ssm_w_conv"][j], name="ssm_dw_bwd")
            gr["ssm_w_conv"][j], gr["ssm_b_conv"][j] = dw_c, db_c[0]
            dzx = ssm_cols_back(jnp.concatenate([dz, dxbc.astype(BF16), ddtp[:, ::HEAD_DIM].astype(BF16)], axis=1))
            gbig["ssm_w_in", j] = _matmul(s["h"], dzx, ta=True, out_dtype=BF16, name="ssm_in_dw")
            dh = _matmul(dzx, full["ssm_w_in", j], tb=True, name="ssm_in_dx")
        dx, _, dg = _rms_bwd(s["x_in"], row(norm_mix_g[i]), dh, dx, name="rms_mix_bwd")
        gr["norm_mix_g"][i] = dg[0]

    parts = []
    for nm, l, r, c, by_col in big:
        gfull = gbig[nm, l]
        parts.append(_col_to_chips(gfull, c) if by_col else gfull.reshape(N_CHIPS, r * c))
    parts.append(jnp.zeros((N_CHIPS, pack_pad), BF16))
    gpack = jnp.concatenate(parts, axis=1).reshape(N_CHIPS, 2, half_rows, PACK_COLS)
    gsum = _reduce_scatter(gpack, name="grads").reshape(pack_n)
    grads = {}
    for nm in ("cv_w_in", "cv_w_out", "ssm_w_in", "ssm_w_out", "ffn_w_up", "ffn_w_down"):
        grads[nm] = jnp.stack([gsum[o:o + r * c].reshape(r, c)[:, :weights[nm].shape[2]]
                               for (n2, l, r, c, _), o in zip(big, offs) if n2 == nm])

    small = [nm for nm in names if nm not in grads]
    small_parts = []
    for nm in small:
        small_parts.append(dg_final[0] if nm == "norm_final_g" else jnp.stack(gr[nm]))
    gs_pack = _flat_rows(small_parts + [loss_part.reshape(1)], LANES)
    gs = _sum8(_gather_devices(gs_pack, name="gather_small_grads"), name="sum_small_grads").reshape(-1)
    o = 0
    for nm, p in zip(small, small_parts):
        gfull = gs[o:o + p.size].reshape(p.shape)
        o += p.size
        if nm in small_sharded:
            width = weights[nm].shape[-1]
            gfull = lax.dynamic_slice_in_dim(gfull, chip * width, width, axis=gfull.ndim - 1)
        grads[nm] = gfull
    loss = gs[o]

    delta, new_m, new_v = {}, {}, {}
    for nm in ("cv_w_in", "cv_w_out", "ssm_w_in", "ssm_w_out", "ffn_w_up", "ffn_w_down"):
        shp = weights[nm].shape
        as2d = lambda a: a.reshape(-1, shp[-1])
        dl, mn, vn = _adamw(as2d(weights[nm]), as2d(grads[nm]), as2d(mom_m[nm]), as2d(mom_v[nm]), name="adamw_" + nm)
        delta[nm], new_m[nm], new_v[nm] = dl.reshape(shp), mn.reshape(shp), vn.reshape(shp)
    pk = lambda dct: _flat_rows([dct[nm] for nm in small], LANES)
    dl, mn, vn = _adamw(pk(weights), pk(grads), pk(mom_m), pk(mom_v), name="adamw_small")
    dl, mn, vn = dl.reshape(-1), mn.reshape(-1), vn.reshape(-1)
    o = 0
    for nm in small:
        shp, n = weights[nm].shape, weights[nm].size
        delta[nm], new_m[nm], new_v[nm] = (a[o:o + n].reshape(shp) for a in (dl, mn, vn))
        o += n

    return (loss, dx[None], *[grads[nm] for nm in names], *[delta[nm] for nm in names],
            *[new_m[nm] for nm in names], *[new_v[nm] for nm in names])
```

```python
import functools

import jax
import jax.numpy as jnp
from jax import lax
from jax.experimental import pallas as pl
from jax.experimental.pallas import tpu as pltpu

F32, BF16 = jnp.float32, jnp.bfloat16
HIGHEST = lax.Precision.HIGHEST
MESH = pl.DeviceIdType.MESH
ANY = pl.BlockSpec(memory_space=pl.ANY)

RMS_EPS = 1e-6
LN_EPS = 1e-5
HEAD_DIM = 64
N_GROUPS = 8
D_STATE = 128
ADAM_LR, ADAM_B1, ADAM_B2, ADAM_EPS, ADAM_WD, ADAM_STEP = 0.001, 0.9, 0.999, 1e-08, 0.01, 10

VMEM_LIMIT_BYTES = 56 * 1024 * 1024
LANES = 128
SSD_CHUNK = 128
PACK_COLS = 1024
PACK_QUANTUM = 1 << 20
ADAMW_BLOCK_ELEMS = 512 * 1024
N_CHIPS = 4


def _cparams(**kw):
    return pltpu.CompilerParams(vmem_limit_bytes=VMEM_LIMIT_BYTES, **kw)


def _pick(n, prefs):
    for p in prefs:
        if n % p == 0:
            return p
    return n


def _sigmoid(x):
    return 1.0 / (1.0 + jnp.exp(-x))


def _silu(x):
    return x * _sigmoid(x)


def _dsilu(x):
    s = _sigmoid(x)
    return s * (1.0 + x * (1.0 - s))


def _rowsum(x):
    return jnp.sum(x, axis=0, keepdims=True)


def _matmul(a, b, *, name, ta=False, tb=False, b_chips=False, out_chips=False, out_dtype=F32, bias=None, res=None):
    m, k = (a.shape[1], a.shape[0]) if ta else a.shape
    if b_chips:
        nq = b.shape[2]
        n = b.shape[1] if tb else N_CHIPS * nq
        assert k == (N_CHIPS * nq if tb else b.shape[1])
    else:
        n = b.shape[0] if tb else b.shape[1]
        assert k == (b.shape[1] if tb else b.shape[0])
        nq = n // N_CHIPS
    whole = (2816, 2688, 1408, 1344)
    if (b_chips and not tb) or out_chips:
        tn = _pick(nq, (1024,) + whole + (512, 384, 256, 128))
    else:
        tn = _pick(n, (1024, 1536, 1152, 512, 384, 256, 128))
    if b_chips and tb:
        tk = _pick(nq, (512,) + whole + (384, 256, 128))
    else:
        tk = _pick(k, (512, 384, 256, 128))
    tm = _pick(m, (1024, 512, 384, 256, 128)) if tn <= 1024 else _pick(m, (512, 384, 256, 128))
    nk = k // tk
    nbq = nq // (tk if tb else tn) if (b_chips or out_chips) else 1
    dn = (((0 if ta else 1,), (1 if tb else 0,)), ((), ()))
    has_bias, has_res = bias is not None, res is not None

    def body(*refs):
        a_ref, b_ref = refs[0], refs[1]
        rest = list(refs[2:])
        bias_ref = rest.pop(0) if has_bias else None
        res_ref = rest.pop(0) if has_res else None
        o_ref, acc_ref = rest
        kk = pl.program_id(2)

        @pl.when(kk == 0)
        def _():
            acc_ref[...] = jnp.zeros_like(acc_ref)

        acc_ref[...] += lax.dot_general(a_ref[...].astype(BF16), b_ref[...].astype(BF16), dn,
                                        preferred_element_type=F32)

        @pl.when(kk == nk - 1)
        def _():
            r = acc_ref[...]
            if has_bias:
                r = r + bias_ref[...]
            if has_res:
                r = r + res_ref[...]
            o_ref[...] = r.astype(o_ref.dtype)

    a_spec = pl.BlockSpec((tk, tm), lambda i, j, kk: (kk, i)) if ta else pl.BlockSpec((tm, tk), lambda i, j, kk: (i, kk))
    if b_chips and tb:
        b_spec = pl.BlockSpec((None, tn, tk), lambda i, j, kk: (kk // nbq, j, kk % nbq))
    elif b_chips:
        b_spec = pl.BlockSpec((None, tk, tn), lambda i, j, kk: (j // nbq, kk, j % nbq))
    elif tb:
        b_spec = pl.BlockSpec((tn, tk), lambda i, j, kk: (j, kk))
    else:
        b_spec = pl.BlockSpec((tk, tn), lambda i, j, kk: (kk, j))
    if out_chips:
        out_spec = pl.BlockSpec((None, tm, tn), lambda i, j, kk: (j // nbq, i, j % nbq))
        out_shape = jax.ShapeDtypeStruct((N_CHIPS, m, nq), out_dtype)
    else:
        out_spec = pl.BlockSpec((tm, tn), lambda i, j, kk: (i, j))
        out_shape = jax.ShapeDtypeStruct((m, n), out_dtype)
    in_specs, args = [a_spec, b_spec], [a, b]
    if has_bias:
        in_specs.append(pl.BlockSpec((1, tn), lambda i, j, kk: (0, j)))
        args.append(bias)
    if has_res:
        in_specs.append(pl.BlockSpec((tm, tn), lambda i, j, kk: (i, j)))
        args.append(res)
    return pl.pallas_call(
        body, name=name, grid=(m // tm, n // tn, nk),
        in_specs=in_specs, out_specs=out_spec, out_shape=out_shape,
        scratch_shapes=[pltpu.VMEM((tm, tn), F32)],
        compiler_params=_cparams(dimension_semantics=("parallel", "parallel", "arbitrary")),
    )(*args)


def _rowwise(fn, rows, pars, outs, reds, *, tm, name):
    rows = [r if isinstance(r, tuple) else (r, r.shape[1], 0) for r in rows]
    t = rows[0][0].shape[0]
    assert t % tm == 0
    n_in, n_o = len(rows) + len(pars), len(outs)

    def body(*refs):
        i = pl.program_id(0)
        o, d = fn(*[r[...] for r in refs[:n_in]])
        for ref, val in zip(refs[n_in:n_in + n_o], o):
            ref[...] = val.astype(ref.dtype)
        d_refs = refs[n_in + n_o:]

        @pl.when(i == 0)
        def _():
            for ref in d_refs:
                ref[...] = jnp.zeros_like(ref)

        for ref, val in zip(d_refs, d):
            ref[...] += val

    in_specs = [pl.BlockSpec((tm, w), lambda i, b=blk: (i, b)) for _, w, blk in rows]
    in_specs += [pl.BlockSpec((1, p.shape[1]), lambda i: (0, 0)) for p in pars]
    out_specs = [pl.BlockSpec((tm, c), lambda i: (i, 0)) for c, _ in outs]
    out_specs += [pl.BlockSpec((1, c), lambda i: (0, 0)) for c in reds]
    out_shape = [jax.ShapeDtypeStruct((t, c), dt) for c, dt in outs] + [jax.ShapeDtypeStruct((1, c), F32) for c in reds]
    res = pl.pallas_call(
        body, name=name, grid=(t // tm,), in_specs=in_specs, out_specs=out_specs, out_shape=out_shape,
        compiler_params=_cparams(dimension_semantics=("arbitrary",)),
    )(*[r[0] for r in rows], *pars)
    return res[:n_o], res[n_o:]


def _rms_fwd(x, g, *, name):
    def fn(xb, gb):
        r = lax.rsqrt(jnp.mean(xb * xb, axis=-1, keepdims=True) + RMS_EPS)
        return [xb * r * gb], []
    (h,), _ = _rowwise(fn, [x], [g], [(x.shape[1], BF16)], [], tm=256, name=name)
    return h


def _rms_bwd(x, g, dh, dres, *, name):
    def fn(xb, dhb, drb, gb):
        r = lax.rsqrt(jnp.mean(xb * xb, axis=-1, keepdims=True) + RMS_EPS)
        xh = xb * r
        dxh = dhb * gb
        dx = r * (dxh - xh * jnp.mean(dxh * xh, axis=-1, keepdims=True))
        out = drb + dx
        return [out, out], [_rowsum(out), _rowsum(dhb * xh)]
    c = x.shape[1]
    (dx, dxb), (colsum, dg) = _rowwise(fn, [x, dh, dres], [g], [(c, F32), (c, BF16)], [c, c], tm=256, name=name)
    return dx, dxb, colsum, dg


def _loss_head(x, g, tgt, *, name):
    d_model = x.shape[1]

    def fn(xb, tb, gb):
        r = lax.rsqrt(jnp.mean(xb * xb, axis=-1, keepdims=True) + RMS_EPS)
        xh = xb * r
        e = xh * gb - tb
        dy = e * (1.0 / d_model)
        dxh = dy * gb
        dx = r * (dxh - xh * jnp.mean(dxh * xh, axis=-1, keepdims=True))
        return [dx, dx], [_rowsum(e * e), _rowsum(dy * xh)]
    (dx, dxb), (sq, dg) = _rowwise(fn, [x, tgt], [g], [(d_model, F32), (d_model, BF16)], [d_model, d_model], tm=256,
                                   name=name)
    return dx, dxb, sq, dg


def _halo_rows(k):
    return 8 * ((k - 1 + 7) // 8) if k > 1 else 8


def _pad_taps(w):
    k = w.shape[0]
    kp = 8 * ((k + 7) // 8)
    return jnp.pad(w, ((0, kp - k), (0, 0)))


def _dwconv_fwd(x, w, b, *, name):
    t, c = x.shape
    k = w.shape[0]
    h = _halo_rows(k)
    tm = _pick(t, (256, 128))
    tc = _pick(c, (512, 256, 128))
    wp = _pad_taps(w)
    kp = wp.shape[0]
    rb = tm // h

    def body(x_ref, p_ref, w_ref, b_ref, o_ref, ext):
        i = pl.program_id(0)
        ext[pl.ds(h, tm), :] = x_ref[...]
        ext[pl.ds(0, h), :] = jnp.where(i > 0, p_ref[...], 0.0)
        acc = jnp.broadcast_to(b_ref[...], (tm, tc))
        for s in range(k):
            acc = acc + w_ref[k - 1 - s:k - s, :] * ext[pl.ds(h - s, tm), :]
        o_ref[...] = acc

    return pl.pallas_call(
        body, name=name, grid=(t // tm, c // tc),
        in_specs=[pl.BlockSpec((tm, tc), lambda i, j: (i, j)),
                  pl.BlockSpec((h, tc), lambda i, j: (jnp.maximum(i * rb - 1, 0), j)),
                  pl.BlockSpec((kp, tc), lambda i, j: (0, j)),
                  pl.BlockSpec((1, tc), lambda i, j: (0, j))],
        out_specs=pl.BlockSpec((tm, tc), lambda i, j: (i, j)),
        out_shape=jax.ShapeDtypeStruct((t, c), F32),
        scratch_shapes=[pltpu.VMEM((h + tm, tc), F32)],
        compiler_params=_cparams(dimension_semantics=("parallel", "parallel")),
    )(x, x, wp, b)


def _dwconv_bwd(x, dy, w, *, name, dx_dtype=F32):
    t, c = x.shape
    k = w.shape[0]
    h = _halo_rows(k)
    tm = _pick(t, (256, 128))
    tc = _pick(c, (512, 256, 128))
    wp = _pad_taps(w)
    kp = wp.shape[0]
    rb = tm // h
    nt = t // tm

    def body(x_ref, p_ref, dy_ref, n_ref, w_ref, dx_ref, dw_ref, db_ref, xext, dext):
        i = pl.program_id(1)

        @pl.when(i == 0)
        def _():
            dw_ref[...] = jnp.zeros_like(dw_ref)
            db_ref[...] = jnp.zeros_like(db_ref)

        xext[pl.ds(h, tm), :] = x_ref[...]
        xext[pl.ds(0, h), :] = jnp.where(i > 0, p_ref[...], 0.0)
        dyv = dy_ref[...]
        dext[pl.ds(0, tm), :] = dyv
        dext[pl.ds(tm, h), :] = jnp.where(i < nt - 1, n_ref[...], 0.0)
        acc = jnp.zeros((tm, tc), F32)
        for s in range(k):
            acc = acc + w_ref[k - 1 - s:k - s, :] * dext[pl.ds(s, tm), :]
            dw_ref[k - 1 - s:k - s, :] += _rowsum(xext[pl.ds(h - s, tm), :] * dyv)
        dx_ref[...] = acc.astype(dx_ref.dtype)
        db_ref[...] += _rowsum(dyv)

    dx, dw, db = pl.pallas_call(
        body, name=name, grid=(c // tc, nt),
        in_specs=[pl.BlockSpec((tm, tc), lambda j, i: (i, j)),
                  pl.BlockSpec((h, tc), lambda j, i: (jnp.maximum(i * rb - 1, 0), j)),
                  pl.BlockSpec((tm, tc), lambda j, i: (i, j)),
                  pl.BlockSpec((h, tc), lambda j, i: (jnp.minimum((i + 1) * rb, nt * rb - 1), j)),
                  pl.BlockSpec((kp, tc), lambda j, i: (0, j))],
        out_specs=[pl.BlockSpec((tm, tc), lambda j, i: (i, j)),
                   pl.BlockSpec((kp, tc), lambda j, i: (0, j)),
                   pl.BlockSpec((1, tc), lambda j, i: (0, j))],
        out_shape=[jax.ShapeDtypeStruct((t, c), dx_dtype), jax.ShapeDtypeStruct((kp, c), F32),
                   jax.ShapeDtypeStruct((1, c), F32)],
        scratch_shapes=[pltpu.VMEM((h + tm, tc), F32), pltpu.VMEM((tm + h, tc), F32)],
        compiler_params=_cparams(dimension_semantics=("parallel", "arbitrary")),
    )(x, x, dy, dy, wp)
    return dx, dw[:k], db


def _glu_fwd(u, b_in, *, name):
    d = u.shape[1] // 2

    def fn(ua, ug, ba, bg):
        return [(ua + ba) * _sigmoid(ug + bg)], []
    (v,), _ = _rowwise(fn, [(u, d, 0), (u, d, 1)], [b_in[:, :d], b_in[:, d:]], [(d, F32)], [], tm=256, name=name)
    return v


def _glu_bwd(u, b_in, dv, *, name):
    d = u.shape[1] // 2

    def fn(ua, ug, dvb, ba, bg):
        a = ua + ba
        s = _sigmoid(ug + bg)
        du = jnp.concatenate([dvb * s, dvb * a * s * (1.0 - s)], axis=1)
        return [du], [_rowsum(du)]
    (du,), (db,) = _rowwise(fn, [(u, d, 0), (u, d, 1), dv], [b_in[:, :d], b_in[:, d:]], [(2 * d, BF16)], [2 * d],
                            tm=256, name=name)
    return du, db


def _ln_silu_fwd(v, g, b, *, name):
    def fn(vb, gb, bb):
        mu = jnp.mean(vb, axis=-1, keepdims=True)
        xc = vb - mu
        rstd = lax.rsqrt(jnp.mean(xc * xc, axis=-1, keepdims=True) + LN_EPS)
        return [_silu(xc * rstd * gb + bb)], []
    (o,), _ = _rowwise(fn, [v], [g, b], [(v.shape[1], BF16)], [], tm=256, name=name)
    return o


def _ln_silu_bwd(v, g, b, do, *, name):
    def fn(vb, dob, gb, bb):
        mu = jnp.mean(vb, axis=-1, keepdims=True)
        xc = vb - mu
        rstd = lax.rsqrt(jnp.mean(xc * xc, axis=-1, keepdims=True) + LN_EPS)
        xh = xc * rstd
        dy = dob * _dsilu(xh * gb + bb)
        dxh = dy * gb
        dv = rstd * (dxh - jnp.mean(dxh, axis=-1, keepdims=True) - xh * jnp.mean(dxh * xh, axis=-1, keepdims=True))
        return [dv], [_rowsum(dy * xh), _rowsum(dy)]
    c = v.shape[1]
    (dv,), (dg, db) = _rowwise(fn, [v, do], [g, b], [(c, F32)], [c, c], tm=256, name=name)
    return dv, dg, db


def _ffn_gate_fwd(uc, *, name):
    f = uc.shape[1] // 2

    def fn(g, v):
        return [_silu(g) * v], []
    (hm,), _ = _rowwise(fn, [(uc, f, 0), (uc, f, 1)], [], [(f, BF16)], [], tm=128, name=name)
    return hm


def _ffn_gate_bwd(uc, dhm, *, name):
    f = uc.shape[1] // 2

    def fn(g, v, d):
        return [jnp.concatenate([d * v * _dsilu(g), d * _silu(g)], axis=1)], []
    (duc,), _ = _rowwise(fn, [(uc, f, 0), (uc, f, 1), dhm], [], [(2 * f, F32)], [], tm=128, name=name)
    return duc


def _ssm_act(conv, dtp_exp, bias_exp, aneg_exp, *, di, name):
    q = SSD_CHUNK
    gn = (conv.shape[1] - di) // 2

    def fn(cb, dtb, bb, ab):
        act = _silu(cb)
        dt = dtb + bb
        dt = jnp.maximum(dt, 0.0) + jnp.log(1.0 + jnp.exp(-jnp.abs(dt)))
        a = dt * ab
        tri = (lax.broadcasted_iota(jnp.int32, (q, q), 0) >= lax.broadcasted_iota(jnp.int32, (q, q), 1)).astype(F32)
        cs = _dot3(tri, a, (((1,), (0,)), ((), ())), 1)
        return [act[:, :di], act[:, di:di + gn], act[:, di + gn:], dt, cs], []
    outs, _ = _rowwise(fn, [conv, dtp_exp], [bias_exp, aneg_exp],
                       [(di, F32), (gn, F32), (gn, F32), (di, F32), (di, F32)], [], tm=q, name=name)
    return outs


def _head_masks(q):
    lane = lax.broadcasted_iota(jnp.int32, (q, LANES), 1)
    return lane < HEAD_DIM


def _pair_cols(cs, lo):
    sw = pltpu.roll(cs, HEAD_DIM, 1)
    return jnp.where(lo, cs, sw), jnp.where(lo, sw, cs)


def _ssd_fwd(xs, dt_exp, cs_exp, cs_rows, bm, cm, *, name):
    t, di = xs.shape
    q = SSD_CHUNK
    hg = di // N_GROUPS
    npair = hg // LANES
    nheads = hg // HEAD_DIM
    nc = t // q
    n = D_STATE

    def body(xs_ref, dt_ref, cs_ref, csr_ref, b_ref, c_ref, y_ref, st_ref, s_scr):
        ci = pl.program_id(1)

        @pl.when(ci == 0)
        def _():
            s_scr[...] = jnp.zeros_like(s_scr)

        bb = b_ref[...].astype(BF16)
        cb_ = c_ref[...].astype(BF16)
        cbm = lax.dot_general(cb_, bb, (((1,), (1,)), ((), ())), preferred_element_type=F32)
        tri = lax.broadcasted_iota(jnp.int32, (q, q), 0) >= lax.broadcasted_iota(jnp.int32, (q, q), 1)
        lo = _head_masks(q)
        csr = csr_ref[0]
        for p in range(npair):
            sl = pl.ds(p * LANES, LANES)
            x = xs_ref[:, sl] * dt_ref[:, sl]
            cs = cs_ref[:, sl]
            col0, col1 = _pair_cols(cs, lo)
            l0 = jnp.where(tri, jnp.exp(jnp.minimum(col0 - csr[2 * p:2 * p + 1, :], 0.0)), 0.0)
            l1 = jnp.where(tri, jnp.exp(jnp.minimum(col1 - csr[2 * p + 1:2 * p + 2, :], 0.0)), 0.0)
            xb = x.astype(BF16)
            yd = jnp.where(lo, jnp.dot((cbm * l0).astype(BF16), xb, preferred_element_type=F32),
                           jnp.dot((cbm * l1).astype(BF16), xb, preferred_element_type=F32))
            s = s_scr[p]
            st_ref[0, 0, p] = s
            yo = jnp.exp(cs) * jnp.dot(cb_, s.astype(BF16), preferred_element_type=F32)
            y_ref[:, sl] = yd + yo
            cs_end = cs[q - 1:q, :]
            xd = (x * jnp.exp(cs_end - cs)).astype(BF16)
            s_scr[p] = jnp.exp(cs_end) * s + lax.dot_general(bb, xd, (((0,), (0,)), ((), ())),
                                                             preferred_element_type=F32)

    return pl.pallas_call(
        body, name=name, grid=(N_GROUPS, nc),
        in_specs=[pl.BlockSpec((q, hg), lambda g, c: (c, g)),
                  pl.BlockSpec((q, hg), lambda g, c: (c, g)),
                  pl.BlockSpec((q, hg), lambda g, c: (c, g)),
                  pl.BlockSpec((1, nheads, q), lambda g, c: (g, 0, c)),
                  pl.BlockSpec((q, n), lambda g, c: (c, g)),
                  pl.BlockSpec((q, n), lambda g, c: (c, g))],
        out_specs=[pl.BlockSpec((q, hg), lambda g, c: (c, g)),
                   pl.BlockSpec((1, 1, npair, n, LANES), lambda g, c: (g, c, 0, 0, 0))],
        out_shape=[jax.ShapeDtypeStruct((t, di), F32),
                   jax.ShapeDtypeStruct((N_GROUPS, nc, npair, n, LANES), F32)],
        scratch_shapes=[pltpu.VMEM((npair, n, LANES), F32)],
        compiler_params=_cparams(dimension_semantics=("parallel", "arbitrary")),
    )(xs, dt_exp, cs_exp, cs_rows, bm, cm)


def _dot3(a, b, dims, split):
    rest = (a, b)[split].astype(F32)
    other = (a, b)[1 - split].astype(BF16)
    acc = None
    for _ in range(3):
        part = rest.astype(BF16)
        rest = rest - part.astype(F32)
        d = (lax.dot_general(part, other, dims, preferred_element_type=F32) if split == 0
             else lax.dot_general(other, part, dims, preferred_element_type=F32))
        acc = d if acc is None else acc + d
    return acc


def _ssd_bwd(xs, dt_exp, cs_exp, cs_rows, bm, cm, dy, states, aneg_exp, *, name):
    t, di = xs.shape
    q = SSD_CHUNK
    hg = di // N_GROUPS
    npair = hg // LANES
    nheads = hg // HEAD_DIM
    nc = t // q
    n = D_STATE
    nt_dims = (((1,), (1,)), ((), ()))
    tn_dims = (((0,), (0,)), ((), ()))

    mm_dims = (((1,), (0,)), ((), ()))

    def body(xs_ref, dt_ref, cs_ref, csr_ref, b_ref, c_ref, dy_ref, st_ref, an_ref,
             dxp_ref, db_ref, dc_ref, ddt_ref, dan_ref, r_scr):
        ci = pl.program_id(1)

        @pl.when(ci == 0)
        def _():
            r_scr[...] = jnp.zeros_like(r_scr)
            dan_ref[...] = jnp.zeros_like(dan_ref)

        bb = b_ref[...].astype(BF16)
        cb_ = c_ref[...].astype(BF16)
        cbm = lax.dot_general(cb_, bb, nt_dims, preferred_element_type=F32)
        row = lax.broadcasted_iota(jnp.int32, (q, q), 0)
        col = lax.broadcasted_iota(jnp.int32, (q, q), 1)
        tri = row >= col
        triu = (row <= col).astype(F32)
        trisl = (row > col).astype(F32)
        ones2 = (lax.broadcasted_iota(jnp.int32, (LANES, LANES), 0) // HEAD_DIM
                 == lax.broadcasted_iota(jnp.int32, (LANES, LANES), 1) // HEAD_DIM).astype(F32)
        onesq = jnp.ones((q, LANES), F32)
        last = lax.broadcasted_iota(jnp.int32, (q, LANES), 0) == q - 1
        lo = _head_masks(q)
        csr = csr_ref[0]
        dcb = jnp.zeros((q, q), F32)
        dc_acc = jnp.zeros((q, n), F32)
        db_acc = jnp.zeros((q, n), F32)
        for p in range(npair):
            sl = pl.ds(p * LANES, LANES)
            xsv = xs_ref[:, sl]
            dtv = dt_ref[:, sl]
            x = xsv * dtv
            cs = cs_ref[:, sl]
            dyv = dy_ref[:, sl]
            col0, col1 = _pair_cols(cs, lo)
            l0 = jnp.where(tri, jnp.exp(jnp.minimum(col0 - csr[2 * p:2 * p + 1, :], 0.0)), 0.0)
            l1 = jnp.where(tri, jnp.exp(jnp.minimum(col1 - csr[2 * p + 1:2 * p + 2, :], 0.0)), 0.0)
            xb = x.astype(BF16)
            dyb = dyv.astype(BF16)
            g0 = lax.dot_general(jnp.where(lo, dyv, 0.0).astype(BF16), xb, nt_dims, preferred_element_type=F32)
            g1 = lax.dot_general(jnp.where(lo, 0.0, dyv).astype(BF16), xb, nt_dims, preferred_element_type=F32)
            gl0, gl1 = g0 * l0, g1 * l1
            dcb = dcb + gl0 + gl1
            w0, w1 = cbm * gl0, cbm * gl1
            dxd = jnp.where(lo,
                            lax.dot_general((cbm * l0).astype(BF16), dyb, tn_dims, preferred_element_type=F32),
                            lax.dot_general((cbm * l1).astype(BF16), dyb, tn_dims, preferred_element_type=F32))
            e = jnp.exp(cs)
            cs_end = cs[q - 1:q, :]
            dte = jnp.exp(cs_end - cs)
            dend = jnp.exp(cs_end)
            sf = st_ref[0, 0, p]
            sb = sf.astype(BF16)
            r = r_scr[p]
            rb = r.astype(BF16)
            dyeb = (dyv * e).astype(BF16)
            dc_acc = dc_acc + lax.dot_general(dyeb, sb, nt_dims, preferred_element_type=F32)
            dxo = dte * jnp.dot(bb, rb, preferred_element_type=F32)
            db_acc = db_acc + lax.dot_general((x * dte).astype(BF16), rb, nt_dims, preferred_element_type=F32)
            r_scr[p] = dend * r + lax.dot_general(cb_, dyeb, tn_dims, preferred_element_type=F32)
            dx = dxd + dxo
            dxp_ref[:, sl] = dx
            yoff = e * jnp.dot(cb_, sb, preferred_element_type=F32)
            rw = jnp.where(lo, _dot3(w0, onesq, mm_dims, 0), _dot3(w1, onesq, mm_dims, 0))
            cw = jnp.where(lo, _dot3(w0, onesq, tn_dims, 0), _dot3(w1, onesq, tn_dims, 0))
            through = jnp.where(last, dend * _rowsum(r * sf), 0.0)
            suf = _dot3(dyv * yoff + through, ones2, mm_dims, 0) + rw - cw
            pre = _dot3(dxo * x, ones2, mm_dims, 0)
            da = _dot3(triu, suf, mm_dims, 1) + _dot3(trisl, pre, mm_dims, 1)
            qs = _dot3(dx * xsv, ones2, mm_dims, 0)
            ddt_ref[:, sl] = da * an_ref[:, sl] + qs
            dan_ref[:, sl] += _rowsum(da * dtv)
        dcbb = dcb.astype(BF16)
        dc_ref[...] = dc_acc + jnp.dot(dcbb, bb, preferred_element_type=F32)
        db_ref[...] = db_acc + lax.dot_general(dcbb, cb_, tn_dims, preferred_element_type=F32)

    rev = lambda g, c: (nc - 1 - c, g)
    return pl.pallas_call(
        body, name=name, grid=(N_GROUPS, nc),
        in_specs=[pl.BlockSpec((q, hg), rev), pl.BlockSpec((q, hg), rev), pl.BlockSpec((q, hg), rev),
                  pl.BlockSpec((1, nheads, q), lambda g, c: (g, 0, nc - 1 - c)),
                  pl.BlockSpec((q, n), rev), pl.BlockSpec((q, n), rev),
                  pl.BlockSpec((q, hg), rev),
                  pl.BlockSpec((1, 1, npair, n, LANES), lambda g, c: (g, nc - 1 - c, 0, 0, 0)),
                  pl.BlockSpec((1, hg), lambda g, c: (0, g))],
        out_specs=[pl.BlockSpec((q, hg), rev), pl.BlockSpec((q, n), rev), pl.BlockSpec((q, n), rev),
                   pl.BlockSpec((q, hg), rev), pl.BlockSpec((1, hg), lambda g, c: (0, g))],
        out_shape=[jax.ShapeDtypeStruct((t, di), F32), jax.ShapeDtypeStruct((t, N_GROUPS * n), F32),
                   jax.ShapeDtypeStruct((t, N_GROUPS * n), F32), jax.ShapeDtypeStruct((t, di), F32),
                   jax.ShapeDtypeStruct((1, di), F32)],
        scratch_shapes=[pltpu.VMEM((npair, n, LANES), F32)],
        compiler_params=_cparams(dimension_semantics=("parallel", "arbitrary")),
    )(xs, dt_exp, cs_exp, cs_rows, bm, cm, dy, states, aneg_exp)


def _group_stats(w, gw):
    return [lax.rsqrt(jnp.mean(w[:, i * gw:(i + 1) * gw] ** 2, axis=-1, keepdims=True) + RMS_EPS)
            for i in range(N_GROUPS)]


def _gated_norm_fwd(y_ssd, xs, z, d_exp, g, *, name):
    di = xs.shape[1]
    gw = di // N_GROUPS

    def fn(yb, xb, zb, db, gb):
        w = (yb + db * xb) * _silu(zb)
        rs = _group_stats(w, gw)
        return [jnp.concatenate([w[:, i * gw:(i + 1) * gw] * rs[i] for i in range(N_GROUPS)], axis=1) * gb], []
    (o,), _ = _rowwise(fn, [y_ssd, xs, (z, di, 0)], [d_exp, g], [(di, BF16)], [], tm=128, name=name)
    return o


def _gated_norm_bwd(y_ssd, xs, z, d_exp, g, do, *, name):
    di = xs.shape[1]
    gw = di // N_GROUPS

    def fn(yb, xb, zb, dob, db, gb):
        yy = yb + db * xb
        sz = _silu(zb)
        w = yy * sz
        rs = _group_stats(w, gw)
        dwh = dob * gb
        wh_parts, dw_parts = [], []
        for i in range(N_GROUPS):
            sl = slice(i * gw, (i + 1) * gw)
            wh = w[:, sl] * rs[i]
            wh_parts.append(wh)
            dw_parts.append(rs[i] * (dwh[:, sl] - wh * jnp.mean(dwh[:, sl] * wh, axis=-1, keepdims=True)))
        wh = jnp.concatenate(wh_parts, axis=1)
        dw = jnp.concatenate(dw_parts, axis=1)
        dy = dw * sz
        dz = dw * yy * _dsilu(zb)
        return [dy, dz], [_rowsum(dob * wh), _rowsum(dy * xb)]
    (dy, dz), (dg, dd) = _rowwise(fn, [y_ssd, xs, (z, di, 0), do], [d_exp, g], [(di, F32), (di, BF16)], [di, di],
                                  tm=128, name=name)
    return dy, dz, dg, dd


def _ssm_act_bwd(conv, dtp_exp, dxp, dy, dbm, dcm, ddt_exp, dt_exp, bias_exp, d_exp, *, di, name):
    gn = dbm.shape[1]

    def fn(cb, dtb, dxpb, dyb, dbb, dcb, ddtb, dteb, bb, db):
        dxs = dxpb * dteb + dyb * db
        dact = jnp.concatenate([dxs, dbb, dcb], axis=1)
        dconv = dact * _dsilu(cb)
        ddtp = ddtb * _sigmoid(dtb + bb)
        return [dconv, ddtp], [_rowsum(ddtp)]
    (dconv, ddtp), (dbias,) = _rowwise(fn, [conv, dtp_exp, dxp, dy, dbm, dcm, ddt_exp, dt_exp], [bias_exp, d_exp],
                                       [(di + 2 * gn, F32), (di, F32)], [di], tm=64, name=name)
    return dconv, ddtp, dbias


def _adamw(w, g, m, v, *, name):
    r, c = w.shape
    c1 = 1.0 / (1.0 - ADAM_B1 ** ADAM_STEP)
    c2 = 1.0 / (1.0 - ADAM_B2 ** ADAM_STEP)

    def fn(wb, gb, mb, vb):
        mn = ADAM_B1 * mb + (1.0 - ADAM_B1) * gb
        vn = ADAM_B2 * vb + (1.0 - ADAM_B2) * (gb * gb)
        delta = -ADAM_LR * ((mn * c1) / (jnp.sqrt(vn * c2) + ADAM_EPS) + ADAM_WD * wb)
        return [delta, mn, vn], []
    cap = max(8, ADAMW_BLOCK_ELEMS // c)
    tm = _pick(r, [p for p in (512, 256, 128, 64, 32, 16, 8) if p <= cap])
    (d, mn, vn), _ = _rowwise(fn, [w, g, m, v], [], [(c, F32)] * 3, [], tm=tm, name=name)
    return d, mn, vn


def _add_pair(sel, g, r, *, name):
    _, _, rows, cols = g.shape
    tm = _pick(rows, (256, 128, 64, 32, 16))

    def body(s_ref, g_ref, r_ref, o_ref):
        o_ref[...] = (g_ref[...].astype(F32) + r_ref[...].astype(F32)).astype(BF16)

    return pl.pallas_call(
        body, name=name,
        grid_spec=pltpu.PrefetchScalarGridSpec(
            num_scalar_prefetch=1, grid=(N_CHIPS, rows // tm),
            in_specs=[pl.BlockSpec((None, None, tm, cols), lambda j, i, s: (j, s[0], i, 0)),
                      pl.BlockSpec((None, tm, cols), lambda j, i, s: (j, i, 0))],
            out_specs=pl.BlockSpec((None, tm, cols), lambda j, i, s: (j, i, 0))),
        out_shape=jax.ShapeDtypeStruct((N_CHIPS, rows, cols), BF16),
        compiler_params=_cparams(dimension_semantics=("parallel", "parallel")),
    )(sel, g, r)


def _add_four(sel, p, r, *, name):
    _, rows, cols = p.shape
    tm = _pick(rows, (256, 128, 64, 32, 16))

    def body(s_ref, p_ref, r0, r1, r2, o_ref):
        o_ref[...] = ((p_ref[...].astype(F32) + r0[...].astype(F32)) + r1[...].astype(F32)) + r2[...].astype(F32)

    rspec = lambda k: pl.BlockSpec((None, tm, cols), lambda i, s, k=k: (k, i, 0))
    return pl.pallas_call(
        body, name=name,
        grid_spec=pltpu.PrefetchScalarGridSpec(
            num_scalar_prefetch=1, grid=(rows // tm,),
            in_specs=[pl.BlockSpec((None, tm, cols), lambda i, s: (s[0], i, 0)), rspec(0), rspec(1), rspec(2)],
            out_specs=pl.BlockSpec((None, tm, cols), lambda i, s: (s[1], i, 0))),
        out_shape=jax.ShapeDtypeStruct((2, rows, cols), F32),
        compiler_params=_cparams(dimension_semantics=("parallel",)),
    )(sel, p, r, r, r)


def _sum8(g, *, name):
    _, rows, cols = g.shape
    tm = _pick(rows, (512, 256, 128, 64, 32, 16, 8))

    def body(g_ref, o_ref):
        acc = g_ref[0]
        for k in range(1, 8):
            acc = acc + g_ref[k]
        o_ref[...] = acc

    return pl.pallas_call(
        body, name=name, grid=(rows // tm,),
        in_specs=[pl.BlockSpec((8, tm, cols), lambda i: (0, i, 0))],
        out_specs=pl.BlockSpec((tm, cols), lambda i: (i, 0)),
        out_shape=jax.ShapeDtypeStruct((rows, cols), F32),
        compiler_params=_cparams(dimension_semantics=("parallel",)),
    )(g)


def _place():
    x, y, c = lax.axis_index("x"), lax.axis_index("y"), lax.axis_index("c")
    chips = [(1 - x, y), (x, 1 - y), (1 - x, 1 - y)]
    return x, y, c, chips


def _rcopy(src, dst, send_sems, recv_sems, k, to):
    return pltpu.make_async_remote_copy(src_ref=src, dst_ref=dst, send_sem=send_sems.at[k], recv_sem=recv_sems.at[k],
                                        device_id=to, device_id_type=MESH)


def _gather_chips(packs, *, name):
    n = len(packs)

    def body(*refs):
        srcs, outs, (send_sems, recv_sems) = refs[:n], refs[n:2 * n], refs[2 * n:]
        x, y, c, chips = _place()
        sibling = (x, y, 1 - c)
        me = 2 * x + y
        first, passed = [], []
        for t, (src, out) in enumerate(zip(srcs, outs)):
            for k, (cx, cy) in enumerate(chips):
                cp = _rcopy(src.at[c], out.at[me, c], send_sems, recv_sems, 6 * t + k, (cx, cy, c))
                cp.start()
                first.append(cp)
        for t, out in enumerate(outs):
            for k, (cx, cy) in enumerate(chips):
                blk = out.at[2 * cx + cy, c]
                _rcopy(blk, blk, send_sems, recv_sems, 6 * t + k, (cx, cy, c)).wait_recv()
                fw = _rcopy(blk, blk, send_sems, recv_sems, 6 * t + 3 + k, sibling)
                fw.start()
                passed.append(fw)
        for t, out in enumerate(outs):
            for k, (cx, cy) in enumerate(chips):
                blk = out.at[2 * cx + cy, 1 - c]
                _rcopy(blk, blk, send_sems, recv_sems, 6 * t + 3 + k, sibling).wait_recv()
        for cp in first + passed:
            cp.wait_send()

    return pl.pallas_call(
        body, name=name, in_specs=[ANY] * n, out_specs=[ANY] * n,
        out_shape=[jax.ShapeDtypeStruct((N_CHIPS,) + p.shape, p.dtype) for p in packs],
        scratch_shapes=[pltpu.SemaphoreType.DMA((6 * n,)), pltpu.SemaphoreType.DMA((6 * n,))],
    )(*packs)


def _gather_devices(pack, *, name):
    rows, cols = pack.shape

    def body(src, out, send_sems, recv_sems, local_sem):
        x, y, c, chips = _place()
        sibling = (x, y, 1 - c)

        def blk(px, py, pc):
            return out.at[4 * px + 2 * py + pc]

        mine = pltpu.make_async_copy(src, blk(x, y, c), local_sem)
        mine.start()
        first = [_rcopy(src, blk(x, y, c), send_sems, recv_sems, 0, sibling)]
        first += [_rcopy(src, blk(x, y, c), send_sems, recv_sems, 1 + k, (cx, cy, c)) for k, (cx, cy) in enumerate(chips)]
        for cp in first:
            cp.start()
        passed = []
        for k, (cx, cy) in enumerate(chips):
            b = blk(cx, cy, c)
            _rcopy(b, b, send_sems, recv_sems, 1 + k, (cx, cy, c)).wait_recv()
            fw = _rcopy(b, b, send_sems, recv_sems, 4 + k, sibling)
            fw.start()
            passed.append(fw)
        b = blk(x, y, 1 - c)
        _rcopy(b, b, send_sems, recv_sems, 0, sibling).wait_recv()
        for k, (cx, cy) in enumerate(chips):
            b = blk(cx, cy, 1 - c)
            _rcopy(b, b, send_sems, recv_sems, 4 + k, sibling).wait_recv()
        for cp in first + passed:
            cp.wait_send()
        mine.wait()

    return pl.pallas_call(
        body, name=name, in_specs=[ANY], out_specs=ANY,
        out_shape=jax.ShapeDtypeStruct((8, rows, cols), pack.dtype),
        scratch_shapes=[pltpu.SemaphoreType.DMA((7,)), pltpu.SemaphoreType.DMA((7,)), pltpu.SemaphoreType.DMA],
    )(pack)


def _swap_halves(gs, *, name):
    n = len(gs)

    def body(*refs):
        srcs, outs, (send_sems, recv_sems) = refs[:n], refs[n:2 * n], refs[2 * n:]
        x, y, c, _ = _place()
        cps = [_rcopy(src.at[j, 1 - c], out.at[j], send_sems, recv_sems, N_CHIPS * t + j, (x, y, 1 - c))
               for t, (src, out) in enumerate(zip(srcs, outs)) for j in range(N_CHIPS)]
        for cp in cps:
            cp.start()
        for cp in cps:
            cp.wait()

    return pl.pallas_call(
        body, name=name, in_specs=[ANY] * n, out_specs=[ANY] * n,
        out_shape=[jax.ShapeDtypeStruct((N_CHIPS,) + g.shape[2:], g.dtype) for g in gs],
        scratch_shapes=[pltpu.SemaphoreType.DMA((N_CHIPS * n,)), pltpu.SemaphoreType.DMA((N_CHIPS * n,))],
    )(*gs)


def _scatter_chips(ps, *, name):
    n = len(ps)

    def body(*refs):
        srcs, outs, (send_sems, recv_sems) = refs[:n], refs[n:2 * n], refs[2 * n:]
        x, y, c, chips = _place()
        cps = [_rcopy(src.at[2 * cx + cy], out.at[k], send_sems, recv_sems, 3 * t + k, (cx, cy, c))
               for t, (src, out) in enumerate(zip(srcs, outs)) for k, (cx, cy) in enumerate(chips)]
        for cp in cps:
            cp.start()
        for cp in cps:
            cp.wait()

    return pl.pallas_call(
        body, name=name, in_specs=[ANY] * n, out_specs=[ANY] * n,
        out_shape=[jax.ShapeDtypeStruct((3,) + p.shape[1:], p.dtype) for p in ps],
        scratch_shapes=[pltpu.SemaphoreType.DMA((3 * n,)), pltpu.SemaphoreType.DMA((3 * n,))],
    )(*ps)


def _join_halves(rs, *, name):
    n = len(rs)

    def body(*refs):
        srcs, outs, (send_sems, recv_sems) = refs[:n], refs[n:2 * n], refs[2 * n:]
        x, y, c, _ = _place()
        cps = [_rcopy(src.at[c], out.at[c], send_sems, recv_sems, t, (x, y, 1 - c))
               for t, (src, out) in enumerate(zip(srcs, outs))]
        for cp in cps:
            cp.start()
        for t, out in enumerate(outs):
            b = out.at[1 - c]
            _rcopy(b, b, send_sems, recv_sems, t, (x, y, 1 - c)).wait_recv()
        for cp in cps:
            cp.wait_send()

    return pl.pallas_call(
        body, name=name, in_specs=[ANY] * n, out_specs=[ANY] * n,
        out_shape=[jax.ShapeDtypeStruct(r.shape, r.dtype) for r in rs], input_output_aliases={t: t for t in range(n)},
        scratch_shapes=[pltpu.SemaphoreType.DMA((n,)), pltpu.SemaphoreType.DMA((n,))],
    )(*rs)


def _reduce_scatter(gs, *, name):
    x, y, c = lax.axis_index("x"), lax.axis_index("y"), lax.axis_index("c")
    sel_c = jnp.reshape(c, (1,)).astype(jnp.int32)
    sel_j = jnp.stack([2 * x + y, c]).astype(jnp.int32)
    got = _swap_halves(gs, name=name + "_swap")
    pairs = [_add_pair(sel_c, g, r, name=name + "_add2") for g, r in zip(gs, got)]
    got3 = _scatter_chips(pairs, name=name + "_scatter")
    halves = [_add_four(sel_j, p, r, name=name + "_add4") for p, r in zip(pairs, got3)]
    return _join_halves(halves, name=name + "_join")


def _col_to_chips(w, nq):
    k = w.shape[0]
    return w.reshape(k, N_CHIPS, nq).transpose(1, 0, 2).reshape(N_CHIPS, k * nq)


def _chips_to_col(g, k, nq):
    return g.reshape(N_CHIPS, k, nq).transpose(1, 0, 2).reshape(k, N_CHIPS * nq)


def _flat_rows(parts, cols):
    flat = jnp.concatenate([p.reshape(-1) for p in parts])
    n = flat.shape[0]
    rows = -(-n // cols)
    rows = 8 * (-(-rows // 8))
    return jnp.pad(flat, (0, rows * cols - n)).reshape(rows, cols)


def _expand(v, di):
    return jnp.repeat(v, HEAD_DIM).reshape(1, di)


def kernel(x, norm_mix_g, norm_ffn_g, norm_final_g, cv_w_in, cv_b_in, cv_w_dw, cv_b_dw, cv_ln_g, cv_ln_b, cv_w_out, cv_b_out, ssm_w_in, ssm_w_conv, ssm_b_conv, ssm_dt_bias, ssm_a_log, ssm_d, ssm_norm_g, ssm_w_out, ffn_w_up, ffn_w_dw, ffn_b_dw, ffn_w_down, loss_target, m_norm_mix_g, m_norm_ffn_g, m_norm_final_g, m_cv_w_in, m_cv_b_in, m_cv_w_dw, m_cv_b_dw, m_cv_ln_g, m_cv_ln_b, m_cv_w_out, m_cv_b_out, m_ssm_w_in, m_ssm_w_conv, m_ssm_b_conv, m_ssm_dt_bias, m_ssm_a_log, m_ssm_d, m_ssm_norm_g, m_ssm_w_out, m_ffn_w_up, m_ffn_w_dw, m_ffn_b_dw, m_ffn_w_down, v_norm_mix_g, v_norm_ffn_g, v_norm_final_g, v_cv_w_in, v_cv_b_in, v_cv_w_dw, v_cv_b_dw, v_cv_ln_g, v_cv_ln_b, v_cv_w_out, v_cv_b_out, v_ssm_w_in, v_ssm_w_conv, v_ssm_b_conv, v_ssm_dt_bias, v_ssm_a_log, v_ssm_d, v_ssm_norm_g, v_ssm_w_out, v_ffn_w_up, v_ffn_w_dw, v_ffn_b_dw, v_ffn_w_down):
    weights = dict(norm_mix_g=norm_mix_g, norm_ffn_g=norm_ffn_g, norm_final_g=norm_final_g, cv_w_in=cv_w_in, cv_b_in=cv_b_in, cv_w_dw=cv_w_dw, cv_b_dw=cv_b_dw, cv_ln_g=cv_ln_g, cv_ln_b=cv_ln_b, cv_w_out=cv_w_out, cv_b_out=cv_b_out, ssm_w_in=ssm_w_in, ssm_w_conv=ssm_w_conv, ssm_b_conv=ssm_b_conv, ssm_dt_bias=ssm_dt_bias, ssm_a_log=ssm_a_log, ssm_d=ssm_d, ssm_norm_g=ssm_norm_g, ssm_w_out=ssm_w_out, ffn_w_up=ffn_w_up, ffn_w_dw=ffn_w_dw, ffn_b_dw=ffn_b_dw, ffn_w_down=ffn_w_down)
    mom_m = dict(norm_mix_g=m_norm_mix_g, norm_ffn_g=m_norm_ffn_g, norm_final_g=m_norm_final_g, cv_w_in=m_cv_w_in, cv_b_in=m_cv_b_in, cv_w_dw=m_cv_w_dw, cv_b_dw=m_cv_b_dw, cv_ln_g=m_cv_ln_g, cv_ln_b=m_cv_ln_b, cv_w_out=m_cv_w_out, cv_b_out=m_cv_b_out, ssm_w_in=m_ssm_w_in, ssm_w_conv=m_ssm_w_conv, ssm_b_conv=m_ssm_b_conv, ssm_dt_bias=m_ssm_dt_bias, ssm_a_log=m_ssm_a_log, ssm_d=m_ssm_d, ssm_norm_g=m_ssm_norm_g, ssm_w_out=m_ssm_w_out, ffn_w_up=m_ffn_w_up, ffn_w_dw=m_ffn_w_dw, ffn_b_dw=m_ffn_b_dw, ffn_w_down=m_ffn_w_down)
    mom_v = dict(norm_mix_g=v_norm_mix_g, norm_ffn_g=v_norm_ffn_g, norm_final_g=v_norm_final_g, cv_w_in=v_cv_w_in, cv_b_in=v_cv_b_in, cv_w_dw=v_cv_w_dw, cv_b_dw=v_cv_b_dw, cv_ln_g=v_cv_ln_g, cv_ln_b=v_cv_ln_b, cv_w_out=v_cv_w_out, cv_b_out=v_cv_b_out, ssm_w_in=v_ssm_w_in, ssm_w_conv=v_ssm_w_conv, ssm_b_conv=v_ssm_b_conv, ssm_dt_bias=v_ssm_dt_bias, ssm_a_log=v_ssm_a_log, ssm_d=v_ssm_d, ssm_norm_g=v_ssm_norm_g, ssm_w_out=v_ssm_w_out, ffn_w_up=v_ffn_w_up, ffn_w_dw=v_ffn_w_dw, ffn_b_dw=v_ffn_b_dw, ffn_w_down=v_ffn_w_down)
    names = list(weights)

    xt = x[0]
    tgt = loss_target[0]
    t, d = xt.shape
    depth = norm_mix_g.shape[0]
    n_cv, n_ssm = cv_w_in.shape[0], ssm_w_in.shape[0]
    di = ssm_w_out.shape[1] * N_CHIPS
    n_heads = di // HEAD_DIM
    gn = N_GROUPS * D_STATE
    ssm_in = ssm_w_in.shape[2] * N_CHIPS
    chip = 2 * lax.axis_index("x") + lax.axis_index("y")

    cq = ssm_w_in.shape[2]
    cqp = LANES * (-(-cq // LANES))
    by_col = ("cv_w_in", "ssm_w_in", "ffn_w_up")
    big_names = ("cv_w_in", "cv_w_out", "ssm_w_in", "ssm_w_out", "ffn_w_up", "ffn_w_down")

    def layer_tensors(i):
        mixer = [("cv_w_in", i // 2), ("cv_w_out", i // 2)] if i % 2 == 0 else [("ssm_w_in", i // 2), ("ssm_w_out", i // 2)]
        return mixer + [("ffn_w_up", i), ("ffn_w_down", i)]

    def halves(a):
        return a.reshape((2, a.shape[0] // 2) + a.shape[1:])

    full = {}
    for i in range(depth):
        keys = layer_tensors(i)
        shards = []
        for nm, l in keys:
            w = weights[nm][l]
            if nm == "ssm_w_in":
                w = jnp.pad(w, ((0, 0), (0, cqp - cq)))
            shards.append(halves(w.astype(BF16)))
        shards = lax.optimization_barrier(shards)
        got = _gather_chips(shards, name="gather_weights")
        for (nm, l), g, s in zip(keys, got, shards):
            g = lax.dynamic_update_index_in_dim(g, s, chip, 0)
            g = g.reshape((N_CHIPS, 2 * g.shape[2], g.shape[3]))
            full[nm, l] = g if nm in by_col else g.reshape(N_CHIPS * g.shape[1], g.shape[2])

    def ssm_cols(a, lo, hi):
        parts = []
        for jj in range(N_CHIPS):
            s0, s1 = max(lo, jj * cq), min(hi, (jj + 1) * cq)
            if s0 < s1:
                parts.append(a[:, jj * cqp + s0 - jj * cq:jj * cqp + s1 - jj * cq])
        return parts[0] if len(parts) == 1 else jnp.concatenate(parts, axis=1)

    def ssm_cols_back(a):
        return jnp.concatenate([jnp.pad(a[:, jj * cq:(jj + 1) * cq], ((0, 0), (0, cqp - cq))) for jj in range(N_CHIPS)],
                               axis=1)

    small_sharded = ["cv_w_dw", "ssm_w_conv", "ssm_b_conv", "ssm_norm_g", "ffn_w_dw"]
    spack = _flat_rows([weights[nm] for nm in small_sharded], LANES)
    sg = _gather_devices(spack, name="gather_small").reshape(8, -1)[::2]
    o = 0
    for nm in small_sharded:
        shp = weights[nm].shape
        n = weights[nm].size
        full[nm] = jnp.concatenate([sg[j, o:o + n].reshape(shp) for j in range(N_CHIPS)], axis=-1)
        o += n

    row = lambda v: v.reshape(1, -1)

    saved = []
    xc = xt
    for i in range(depth):
        j = i // 2
        s = {"x_in": xc}
        h = _rms_fwd(xc, row(norm_mix_g[i]), name="rms_mix_fwd")
        s["h"] = h
        if i % 2 == 0:
            u = _matmul(h, full["cv_w_in", j], b_chips=True, name="cv_in_fwd")
            v1 = _glu_fwd(u, row(cv_b_in[j]), name="cv_glu_fwd")
            v2 = _dwconv_fwd(v1, full["cv_w_dw"][j], row(cv_b_dw[j]), name="cv_dw_fwd")
            v4 = _ln_silu_fwd(v2, row(cv_ln_g[j]), row(cv_ln_b[j]), name="cv_ln_fwd")
            xc = _matmul(v4, full["cv_w_out", j], bias=row(cv_b_out[j]), res=xc, name="cv_out_fwd")
            s.update(u=u, v1=v1, v2=v2, v4=v4)
        else:
            zx = _matmul(h, full["ssm_w_in", j], b_chips=True, name="ssm_in_fwd")
            z = ssm_cols(zx, 0, di)
            xbc_pre = ssm_cols(zx, di, 2 * di + 2 * gn)
            dtp_exp = jnp.repeat(ssm_cols(zx, 2 * di + 2 * gn, ssm_in), HEAD_DIM, axis=1)
            conv = _dwconv_fwd(xbc_pre, full["ssm_w_conv"][j], row(full["ssm_b_conv"][j]), name="ssm_dw_fwd")
            bias_exp = _expand(ssm_dt_bias[j], di)
            aneg_exp = _expand(-jnp.exp(ssm_a_log[j]), di)
            d_exp = _expand(ssm_d[j], di)
            xs, bm, cm, dt_exp, cs_exp = _ssm_act(conv, dtp_exp, bias_exp, aneg_exp, di=di, name="ssm_act_fwd")
            cs_rows = cs_exp[:, ::HEAD_DIM].T.reshape(N_GROUPS, n_heads // N_GROUPS, t)
            y_ssd, states = _ssd_fwd(xs, dt_exp, cs_exp, cs_rows, bm, cm, name="ssd_fwd")
            gnrm = _gated_norm_fwd(y_ssd, xs, z, d_exp, row(full["ssm_norm_g"][j]), name="ssm_norm_fwd")
            xc = _matmul(gnrm, full["ssm_w_out", j], res=xc, name="ssm_out_fwd")
            s.update(z=z, xbc_pre=xbc_pre, dtp_exp=dtp_exp, conv=conv, bias_exp=bias_exp, aneg_exp=aneg_exp,
                     d_exp=d_exp, xs=xs, bm=bm, cm=cm, dt_exp=dt_exp, cs_exp=cs_exp, cs_rows=cs_rows, y_ssd=y_ssd,
                     states=states, gnrm=gnrm)
        s["x_mid"] = xc
        h2 = _rms_fwd(xc, row(norm_ffn_g[i]), name="rms_ffn_fwd")
        u2 = _matmul(h2, full["ffn_w_up", i], b_chips=True, name="ffn_up_fwd")
        uc = _dwconv_fwd(u2, full["ffn_w_dw"][i], row(ffn_b_dw[i]), name="ffn_dw_fwd")
        hm = _ffn_gate_fwd(uc, name="ffn_gate_fwd")
        xc = _matmul(hm, full["ffn_w_down", i], res=xc, name="ffn_down_fwd")
        s.update(h2=h2, u2=u2, uc=uc, hm=hm)
        saved.append(s)

    dx, dxb, sq, dg_final = _loss_head(xc, row(norm_final_g), tgt, name="loss_head")
    loss_part = 0.5 / d * jnp.sum(sq)
    gr = {nm: [None] * weights[nm].shape[0] for nm in names if nm != "norm_final_g"}
    gbig = {}
    reduced = {}
    chips_rows = lambda g: g.reshape(N_CHIPS, g.shape[0] // N_CHIPS, g.shape[1])
    for i in reversed(range(depth)):
        j = i // 2
        s = saved[i]
        gbig["ffn_w_down", i] = chips_rows(_matmul(s["hm"], dxb, ta=True, out_dtype=BF16, name="ffn_down_dw"))
        dhm = _matmul(dxb, full["ffn_w_down", i], tb=True, name="ffn_down_dx")
        duc = _ffn_gate_bwd(s["uc"], dhm, name="ffn_gate_bwd")
        du2b, dw_dw, db_dw = _dwconv_bwd(s["u2"], duc, full["ffn_w_dw"][i], dx_dtype=BF16, name="ffn_dw_bwd")
        gr["ffn_w_dw"][i], gr["ffn_b_dw"][i] = dw_dw, db_dw[0]
        gbig["ffn_w_up", i] = _matmul(s["h2"], du2b, ta=True, out_chips=True, out_dtype=BF16, name="ffn_up_dw")
        dh2 = _matmul(du2b, full["ffn_w_up", i], tb=True, b_chips=True, name="ffn_up_dx")
        dx, dxb, colsum, dg = _rms_bwd(s["x_mid"], row(norm_ffn_g[i]), dh2, dx, name="rms_ffn_bwd")
        gr["norm_ffn_g"][i] = dg[0]
        if i % 2 == 0:
            gr["cv_b_out"][j] = colsum[0]
            gbig["cv_w_out", j] = chips_rows(_matmul(s["v4"], dxb, ta=True, out_dtype=BF16, name="cv_out_dw"))
            dv4 = _matmul(dxb, full["cv_w_out", j], tb=True, name="cv_out_dx")
            dv2, dlg, dlb = _ln_silu_bwd(s["v2"], row(cv_ln_g[j]), row(cv_ln_b[j]), dv4, name="cv_ln_bwd")
            gr["cv_ln_g"][j], gr["cv_ln_b"][j] = dlg[0], dlb[0]
            dv1, dw_dw, db_dw = _dwconv_bwd(s["v1"], dv2, full["cv_w_dw"][j], name="cv_dw_bwd")
            gr["cv_w_dw"][j], gr["cv_b_dw"][j] = dw_dw, db_dw[0]
            du, db_in = _glu_bwd(s["u"], row(cv_b_in[j]), dv1, name="cv_glu_bwd")
            gr["cv_b_in"][j] = db_in[0]
            gbig["cv_w_in", j] = _matmul(s["h"], du, ta=True, out_chips=True, out_dtype=BF16, name="cv_in_dw")
            dh = _matmul(du, full["cv_w_in", j], tb=True, b_chips=True, name="cv_in_dx")
        else:
            gbig["ssm_w_out", j] = chips_rows(_matmul(s["gnrm"], dxb, ta=True, out_dtype=BF16, name="ssm_out_dw"))
            dgn = _matmul(dxb, full["ssm_w_out", j], tb=True, name="ssm_out_dx")
            dy, dz, dng, ddl = _gated_norm_bwd(s["y_ssd"], s["xs"], s["z"], s["d_exp"], row(full["ssm_norm_g"][j]),
                                               dgn, name="ssm_norm_bwd")
            gr["ssm_norm_g"][j] = dng[0]
            gr["ssm_d"][j] = ddl.reshape(n_heads, HEAD_DIM).sum(axis=1)
            dxp, dbm, dcm, ddt_exp, dan = _ssd_bwd(s["xs"], s["dt_exp"], s["cs_exp"], s["cs_rows"], s["bm"], s["cm"],
                                                   dy, s["states"], s["aneg_exp"], name="ssd_bwd")
            gr["ssm_a_log"][j] = dan[0, ::HEAD_DIM] * s["aneg_exp"][0, ::HEAD_DIM]
            dconv, ddtp, dbias = _ssm_act_bwd(s["conv"], s["dtp_exp"], dxp, dy, dbm, dcm, ddt_exp, s["dt_exp"],
                                              s["bias_exp"], s["d_exp"], di=di, name="ssm_act_bwd")
            gr["ssm_dt_bias"][j] = dbias[0, ::HEAD_DIM]
            dxbc, dw_c, db_c = _dwconv_bwd(s["xbc_pre"], dconv, full["ssm_w_conv"][j], dx_dtype=BF16, name="ssm_dw_bwd")
            gr["ssm_w_conv"][j], gr["ssm_b_conv"][j] = dw_c, db_c[0]
            dzx = ssm_cols_back(jnp.concatenate([dz, dxbc, ddtp[:, ::HEAD_DIM].astype(BF16)], axis=1))
            gbig["ssm_w_in", j] = _matmul(s["h"], dzx, ta=True, out_chips=True, out_dtype=BF16, name="ssm_in_dw")
            dh = _matmul(dzx, full["ssm_w_in", j], tb=True, b_chips=True, name="ssm_in_dx")
        dx, dxb, _, dg = _rms_bwd(s["x_in"], row(norm_mix_g[i]), dh, dx, name="rms_mix_bwd")
        gr["norm_mix_g"][i] = dg[0]

        keys = layer_tensors(i)
        sums = _reduce_scatter([gbig[k].reshape((N_CHIPS, 2, gbig[k].shape[1] // 2, gbig[k].shape[2])) for k in keys],
                               name="grads")
        for k, r in zip(keys, sums):
            reduced[k] = r.reshape(2 * r.shape[1], r.shape[2])[:, :weights[k[0]].shape[2]]

    grads = {nm: jnp.stack([reduced[nm, l] for l in range(weights[nm].shape[0])]) for nm in big_names}

    small = [nm for nm in names if nm not in grads]
    small_parts = []
    for nm in small:
        small_parts.append(dg_final[0] if nm == "norm_final_g" else jnp.stack(gr[nm]))
    gs_pack = _flat_rows(small_parts + [loss_part.reshape(1)], LANES)
    gs = _sum8(_gather_devices(gs_pack, name="gather_small_grads"), name="sum_small_grads").reshape(-1)
    o = 0
    for nm, p in zip(small, small_parts):
        gfull = gs[o:o + p.size].reshape(p.shape)
        o += p.size
        if nm in small_sharded:
            width = weights[nm].shape[-1]
            gfull = lax.dynamic_slice_in_dim(gfull, chip * width, width, axis=gfull.ndim - 1)
        grads[nm] = gfull
    loss = gs[o]

    delta, new_m, new_v = {}, {}, {}
    for nm in big_names:
        shp = weights[nm].shape
        as2d = lambda a: a.reshape(-1, shp[-1])
        dl, mn, vn = _adamw(as2d(weights[nm]), as2d(grads[nm]), as2d(mom_m[nm]), as2d(mom_v[nm]), name="adamw_" + nm)
        delta[nm], new_m[nm], new_v[nm] = dl.reshape(shp), mn.reshape(shp), vn.reshape(shp)
    pk = lambda dct: _flat_rows([dct[nm] for nm in small], LANES)
    dl, mn, vn = _adamw(pk(weights), pk(grads), pk(mom_m), pk(mom_v), name="adamw_small")
    dl, mn, vn = dl.reshape(-1), mn.reshape(-1), vn.reshape(-1)
    o = 0
    for nm in small:
        shp, n = weights[nm].shape, weights[nm].size
        delta[nm], new_m[nm], new_v[nm] = (a[o:o + n].reshape(shp) for a in (dl, mn, vn))
        o += n

    return (loss, dx[None], *[grads[nm] for nm in names], *[delta[nm] for nm in names],
            *[new_m[nm] for nm in names], *[new_v[nm] for nm in names])
```

```python
import functools

import jax
import jax.numpy as jnp
from jax import lax
from jax.experimental import pallas as pl
from jax.experimental.pallas import tpu as pltpu

F32, BF16 = jnp.float32, jnp.bfloat16
MESH = pl.DeviceIdType.MESH
ANY = pl.BlockSpec(memory_space=pl.ANY)

RMS_EPS = 1e-6
LN_EPS = 1e-5
HEAD_DIM = 64
N_GROUPS = 8
D_STATE = 128
ADAM_LR, ADAM_B1, ADAM_B2, ADAM_EPS, ADAM_WD, ADAM_STEP = 0.001, 0.9, 0.999, 1e-08, 0.01, 10

VMEM_LIMIT_BYTES = 56 * 1024 * 1024
LANES = 128
SSD_CHUNK = 128
ADAMW_BLOCK_ELEMS = 512 * 1024
N_CHIPS = 4


def _cparams(**kw):
    return pltpu.CompilerParams(vmem_limit_bytes=VMEM_LIMIT_BYTES, **kw)


def _pick(n, prefs):
    for p in prefs:
        if n % p == 0:
            return p
    return n


def _sigmoid(x):
    return 1.0 / (1.0 + jnp.exp(-x))


def _silu(x):
    return x * _sigmoid(x)


def _dsilu(x):
    s = _sigmoid(x)
    return s * (1.0 + x * (1.0 - s))


def _rowsum(x):
    return jnp.sum(x, axis=0, keepdims=True)


MM_TILES = (2816, 2688, 2048, 1408, 1024, 896, 512, 384, 256, 128)
MM_VMEM_BUDGET = 40 * 1024 * 1024


def _mm_tiles(m, n, k, n_unit, k_unit, out_bytes, has_res):
    best = None
    for tk in [t for t in MM_TILES if k_unit % t == 0]:
        for tm in [t for t in (1024, 512, 256, 128) if m % t == 0] or [m]:
            for tn in [t for t in MM_TILES if n_unit % t == 0]:
                vmem = 2 * 2 * (tm * tk + tk * tn) + 2 * tm * tn * out_bytes
                vmem += tm * tn * 4 if k // tk > 1 else 0
                vmem += 2 * tm * tn * 4 if has_res else 0
                if vmem > MM_VMEM_BUDGET:
                    continue
                traffic = m * k * (n // tn) + k * n * (m // tm)
                if best is None or traffic < best[0]:
                    best = (traffic, tm, tn, tk)
        if best is not None:
            return best[1:]
    raise ValueError((m, n, k))


def _matmul(a, b, *, name, ta=False, tb=False, b_chips=False, out_chips=False, out_dtype=F32, bias=None, res=None,
            carry=None):
    m, k = (a.shape[1], a.shape[0]) if ta else a.shape
    if b_chips:
        nq = b.shape[2]
        n = b.shape[1] if tb else N_CHIPS * nq
        assert k == (N_CHIPS * nq if tb else b.shape[1])
    else:
        n = b.shape[0] if tb else b.shape[1]
        assert k == (b.shape[1] if tb else b.shape[0])
        nq = n // N_CHIPS
    has_bias, has_res = bias is not None, res is not None
    tm, tn, tk = _mm_tiles(m, n, k, nq if ((b_chips and not tb) or out_chips) else n, nq if (b_chips and tb) else k,
                           jnp.dtype(out_dtype).itemsize, has_res)
    gm, gn, nk = m // tm, n // tn, k // tk
    nbq = nq // (tk if tb else tn) if (b_chips or out_chips) else 1
    dn = (((0 if ta else 1,), (1 if tb else 0,)), ((), ()))
    kind = carry[0] if carry else None

    def body(*refs):
        a_ref, b_ref = refs[0], refs[1]
        rest = list(refs[2:])
        bias_ref = rest.pop(0) if has_bias else None
        res_ref = rest.pop(0) if has_res else None
        src = rest.pop(0) if carry else None
        o_ref = rest.pop(0)
        dst = rest.pop(0) if carry else None
        acc_ref = rest.pop(0) if nk > 1 else None
        i, j, kk = pl.program_id(0), pl.program_id(1), pl.program_id(2)

        if carry:
            send_sems, recv_sems = rest
            x, y, c, chips = _place()
            if kind == "gather":
                cps = [_rcopy(src.at[c], dst.at[2 * x + y, c], send_sems, recv_sems, q, (cx, cy, c))
                       for q, (cx, cy) in enumerate(chips)]
            else:
                cps = [_rcopy(src.at[2 * cx + cy], dst.at[q], send_sems, recv_sems, q, (cx, cy, c))
                       for q, (cx, cy) in enumerate(chips)]

            @pl.when((i == 0) & (j == 0) & (kk == 0))
            def _():
                for cp in cps:
                    cp.start()

        def finish(r):
            if has_bias:
                r = r + bias_ref[...]
            if has_res:
                r = r + res_ref[...]
            o_ref[...] = r.astype(o_ref.dtype)

        part = lax.dot_general(a_ref[...].astype(BF16), b_ref[...].astype(BF16), dn, preferred_element_type=F32)
        if nk == 1:
            finish(part)
        else:
            @pl.when(kk == 0)
            def _():
                acc_ref[...] = part

            @pl.when(kk > 0)
            def _():
                acc_ref[...] += part

            @pl.when(kk == nk - 1)
            def _():
                finish(acc_ref[...])

        if carry:
            @pl.when((i == gm - 1) & (j == gn - 1) & (kk == nk - 1))
            def _():
                if kind == "gather":
                    for q, (cx, cy) in enumerate(chips):
                        blk = dst.at[2 * cx + cy, c]
                        _rcopy(blk, blk, send_sems, recv_sems, q, (cx, cy, c)).wait_recv()
                    for cp in cps:
                        cp.wait_send()
                else:
                    for cp in cps:
                        cp.wait()

    a_spec = pl.BlockSpec((tk, tm), lambda i, j, kk: (kk, i)) if ta else pl.BlockSpec((tm, tk), lambda i, j, kk: (i, kk))
    if b_chips and tb:
        b_spec = pl.BlockSpec((None, tn, tk), lambda i, j, kk: (kk // nbq, j, kk % nbq))
    elif b_chips:
        b_spec = pl.BlockSpec((None, tk, tn), lambda i, j, kk: (j // nbq, kk, j % nbq))
    elif tb:
        b_spec = pl.BlockSpec((tn, tk), lambda i, j, kk: (j, kk))
    else:
        b_spec = pl.BlockSpec((tk, tn), lambda i, j, kk: (kk, j))
    if out_chips:
        out_spec = pl.BlockSpec((None, tm, tn), lambda i, j, kk: (j // nbq, i, j % nbq))
        out_shape = jax.ShapeDtypeStruct((N_CHIPS, m, nq), out_dtype)
    else:
        out_spec = pl.BlockSpec((tm, tn), lambda i, j, kk: (i, j))
        out_shape = jax.ShapeDtypeStruct((m, n), out_dtype)
    in_specs, args = [a_spec, b_spec], [a, b]
    if has_bias:
        in_specs.append(pl.BlockSpec((1, tn), lambda i, j, kk: (0, j)))
        args.append(bias)
    if has_res:
        in_specs.append(pl.BlockSpec((tm, tn), lambda i, j, kk: (i, j)))
        args.append(res)
    scratch = [pltpu.VMEM((tm, tn), F32)] if nk > 1 else []
    if not carry:
        return pl.pallas_call(
            body, name=name, grid=(gm, gn, nk), in_specs=in_specs, out_specs=out_spec, out_shape=out_shape,
            scratch_shapes=scratch,
            compiler_params=_cparams(dimension_semantics=("parallel", "parallel", "arbitrary")),
        )(*args)
    moved = carry[1]
    land = (N_CHIPS,) + moved.shape if kind == "gather" else (3,) + moved.shape[1:]
    return pl.pallas_call(
        body, name=name, grid=(gm, gn, nk), in_specs=in_specs + [ANY], out_specs=[out_spec, ANY],
        out_shape=[out_shape, jax.ShapeDtypeStruct(land, moved.dtype)],
        scratch_shapes=scratch + [pltpu.SemaphoreType.DMA((3,)), pltpu.SemaphoreType.DMA((3,))],
        compiler_params=_cparams(dimension_semantics=("arbitrary", "arbitrary", "arbitrary")),
    )(*args, moved)


def _rowwise(fn, rows, pars, outs, reds, *, tm, name):
    rows = [r if isinstance(r, tuple) else (r, r.shape[1], 0) for r in rows]
    t = rows[0][0].shape[0]
    assert t % tm == 0
    n_in, n_o = len(rows) + len(pars), len(outs)

    def body(*refs):
        i = pl.program_id(0)
        o, d = fn(*[r[...] for r in refs[:n_in]])
        for ref, val in zip(refs[n_in:n_in + n_o], o):
            ref[...] = val.astype(ref.dtype)
        d_refs = refs[n_in + n_o:]

        @pl.when(i == 0)
        def _():
            for ref in d_refs:
                ref[...] = jnp.zeros_like(ref)

        for ref, val in zip(d_refs, d):
            ref[...] += val

    in_specs = [pl.BlockSpec((tm, w), lambda i, b=blk: (i, b)) for _, w, blk in rows]
    in_specs += [pl.BlockSpec((1, p.shape[1]), lambda i: (0, 0)) for p in pars]
    out_specs = [pl.BlockSpec((tm, c), lambda i: (i, 0)) for c, _ in outs]
    out_specs += [pl.BlockSpec((1, c), lambda i: (0, 0)) for c in reds]
    out_shape = [jax.ShapeDtypeStruct((t, c), dt) for c, dt in outs] + [jax.ShapeDtypeStruct((1, c), F32) for c in reds]
    res = pl.pallas_call(
        body, name=name, grid=(t // tm,), in_specs=in_specs, out_specs=out_specs, out_shape=out_shape,
        compiler_params=_cparams(dimension_semantics=("arbitrary",)),
    )(*[r[0] for r in rows], *pars)
    return res[:n_o], res[n_o:]


def _rms_fwd(x, g, *, name):
    def fn(xb, gb):
        r = lax.rsqrt(jnp.mean(xb * xb, axis=-1, keepdims=True) + RMS_EPS)
        return [xb * r * gb], []
    (h,), _ = _rowwise(fn, [x], [g], [(x.shape[1], BF16)], [], tm=256, name=name)
    return h


def _rms_bwd(x, g, dh, dres, *, name):
    def fn(xb, dhb, drb, gb):
        r = lax.rsqrt(jnp.mean(xb * xb, axis=-1, keepdims=True) + RMS_EPS)
        xh = xb * r
        dxh = dhb * gb
        dx = r * (dxh - xh * jnp.mean(dxh * xh, axis=-1, keepdims=True))
        out = drb + dx
        return [out, out], [_rowsum(out), _rowsum(dhb * xh)]
    c = x.shape[1]
    (dx, dxb), (colsum, dg) = _rowwise(fn, [x, dh, dres], [g], [(c, F32), (c, BF16)], [c, c], tm=256, name=name)
    return dx, dxb, colsum, dg


def _loss_head(x, g, tgt, *, name):
    d_model = x.shape[1]

    def fn(xb, tb, gb):
        r = lax.rsqrt(jnp.mean(xb * xb, axis=-1, keepdims=True) + RMS_EPS)
        xh = xb * r
        e = xh * gb - tb
        dy = e * (1.0 / d_model)
        dxh = dy * gb
        dx = r * (dxh - xh * jnp.mean(dxh * xh, axis=-1, keepdims=True))
        return [dx, dx], [_rowsum(e * e), _rowsum(dy * xh)]
    (dx, dxb), (sq, dg) = _rowwise(fn, [x, tgt], [g], [(d_model, F32), (d_model, BF16)], [d_model, d_model], tm=256,
                                   name=name)
    return dx, dxb, sq, dg


def _halo_rows(k):
    return 8 * ((k - 1 + 7) // 8) if k > 1 else 8


def _pad_taps(w):
    k = w.shape[0]
    kp = 8 * ((k + 7) // 8)
    return jnp.pad(w, ((0, kp - k), (0, 0)))


def _dwconv_fwd(x, w, b, *, name):
    t, c = x.shape
    k = w.shape[0]
    h = _halo_rows(k)
    tm = _pick(t, (256, 128))
    tc = _pick(c, (512, 256, 128))
    wp = _pad_taps(w)
    kp = wp.shape[0]
    rb = tm // h

    def body(x_ref, p_ref, w_ref, b_ref, o_ref, ext):
        i = pl.program_id(0)
        ext[pl.ds(h, tm), :] = x_ref[...]
        ext[pl.ds(0, h), :] = jnp.where(i > 0, p_ref[...], 0.0)
        acc = jnp.broadcast_to(b_ref[...], (tm, tc))
        for s in range(k):
            acc = acc + w_ref[k - 1 - s:k - s, :] * ext[pl.ds(h - s, tm), :]
        o_ref[...] = acc

    return pl.pallas_call(
        body, name=name, grid=(t // tm, c // tc),
        in_specs=[pl.BlockSpec((tm, tc), lambda i, j: (i, j)),
                  pl.BlockSpec((h, tc), lambda i, j: (jnp.maximum(i * rb - 1, 0), j)),
                  pl.BlockSpec((kp, tc), lambda i, j: (0, j)),
                  pl.BlockSpec((1, tc), lambda i, j: (0, j))],
        out_specs=pl.BlockSpec((tm, tc), lambda i, j: (i, j)),
        out_shape=jax.ShapeDtypeStruct((t, c), F32),
        scratch_shapes=[pltpu.VMEM((h + tm, tc), F32)],
        compiler_params=_cparams(dimension_semantics=("parallel", "parallel")),
    )(x, x, wp, b)


def _dwconv_bwd(x, dy, w, *, name, dx_dtype=F32):
    t, c = x.shape
    k = w.shape[0]
    h = _halo_rows(k)
    tm = _pick(t, (256, 128))
    tc = _pick(c, (512, 256, 128))
    wp = _pad_taps(w)
    kp = wp.shape[0]
    rb = tm // h
    nt = t // tm

    def body(x_ref, p_ref, dy_ref, n_ref, w_ref, dx_ref, dw_ref, db_ref, xext, dext):
        i = pl.program_id(1)

        @pl.when(i == 0)
        def _():
            dw_ref[...] = jnp.zeros_like(dw_ref)
            db_ref[...] = jnp.zeros_like(db_ref)

        xext[pl.ds(h, tm), :] = x_ref[...]
        xext[pl.ds(0, h), :] = jnp.where(i > 0, p_ref[...], 0.0)
        dyv = dy_ref[...]
        dext[pl.ds(0, tm), :] = dyv
        dext[pl.ds(tm, h), :] = jnp.where(i < nt - 1, n_ref[...], 0.0)
        acc = jnp.zeros((tm, tc), F32)
        for s in range(k):
            acc = acc + w_ref[k - 1 - s:k - s, :] * dext[pl.ds(s, tm), :]
            dw_ref[k - 1 - s:k - s, :] += _rowsum(xext[pl.ds(h - s, tm), :] * dyv)
        dx_ref[...] = acc.astype(dx_ref.dtype)
        db_ref[...] += _rowsum(dyv)

    dx, dw, db = pl.pallas_call(
        body, name=name, grid=(c // tc, nt),
        in_specs=[pl.BlockSpec((tm, tc), lambda j, i: (i, j)),
                  pl.BlockSpec((h, tc), lambda j, i: (jnp.maximum(i * rb - 1, 0), j)),
                  pl.BlockSpec((tm, tc), lambda j, i: (i, j)),
                  pl.BlockSpec((h, tc), lambda j, i: (jnp.minimum((i + 1) * rb, nt * rb - 1), j)),
                  pl.BlockSpec((kp, tc), lambda j, i: (0, j))],
        out_specs=[pl.BlockSpec((tm, tc), lambda j, i: (i, j)),
                   pl.BlockSpec((kp, tc), lambda j, i: (0, j)),
                   pl.BlockSpec((1, tc), lambda j, i: (0, j))],
        out_shape=[jax.ShapeDtypeStruct((t, c), dx_dtype), jax.ShapeDtypeStruct((kp, c), F32),
                   jax.ShapeDtypeStruct((1, c), F32)],
        scratch_shapes=[pltpu.VMEM((h + tm, tc), F32), pltpu.VMEM((tm + h, tc), F32)],
        compiler_params=_cparams(dimension_semantics=("parallel", "arbitrary")),
    )(x, x, dy, dy, wp)
    return dx, dw[:k], db


def _glu_fwd(u, b_in, *, name):
    d = u.shape[1] // 2

    def fn(ua, ug, ba, bg):
        return [(ua + ba) * _sigmoid(ug + bg)], []
    (v,), _ = _rowwise(fn, [(u, d, 0), (u, d, 1)], [b_in[:, :d], b_in[:, d:]], [(d, F32)], [], tm=256, name=name)
    return v


def _glu_bwd(u, b_in, dv, *, name):
    d = u.shape[1] // 2

    def fn(ua, ug, dvb, ba, bg):
        a = ua + ba
        s = _sigmoid(ug + bg)
        du = jnp.concatenate([dvb * s, dvb * a * s * (1.0 - s)], axis=1)
        return [du], [_rowsum(du)]
    (du,), (db,) = _rowwise(fn, [(u, d, 0), (u, d, 1), dv], [b_in[:, :d], b_in[:, d:]], [(2 * d, BF16)], [2 * d],
                            tm=256, name=name)
    return du, db


def _ln_silu_fwd(v, g, b, *, name):
    def fn(vb, gb, bb):
        mu = jnp.mean(vb, axis=-1, keepdims=True)
        xc = vb - mu
        rstd = lax.rsqrt(jnp.mean(xc * xc, axis=-1, keepdims=True) + LN_EPS)
        return [_silu(xc * rstd * gb + bb)], []
    (o,), _ = _rowwise(fn, [v], [g, b], [(v.shape[1], BF16)], [], tm=256, name=name)
    return o


def _ln_silu_bwd(v, g, b, do, *, name):
    def fn(vb, dob, gb, bb):
        mu = jnp.mean(vb, axis=-1, keepdims=True)
        xc = vb - mu
        rstd = lax.rsqrt(jnp.mean(xc * xc, axis=-1, keepdims=True) + LN_EPS)
        xh = xc * rstd
        dy = dob * _dsilu(xh * gb + bb)
        dxh = dy * gb
        dv = rstd * (dxh - jnp.mean(dxh, axis=-1, keepdims=True) - xh * jnp.mean(dxh * xh, axis=-1, keepdims=True))
        return [dv], [_rowsum(dy * xh), _rowsum(dy)]
    c = v.shape[1]
    (dv,), (dg, db) = _rowwise(fn, [v, do], [g, b], [(c, F32)], [c, c], tm=256, name=name)
    return dv, dg, db


def _ffn_gate_fwd(uc, *, name):
    f = uc.shape[1] // 2

    def fn(g, v):
        return [_silu(g) * v], []
    (hm,), _ = _rowwise(fn, [(uc, f, 0), (uc, f, 1)], [], [(f, BF16)], [], tm=128, name=name)
    return hm


def _ffn_gate_bwd(uc, dhm, *, name):
    f = uc.shape[1] // 2

    def fn(g, v, d):
        return [jnp.concatenate([d * v * _dsilu(g), d * _silu(g)], axis=1)], []
    (duc,), _ = _rowwise(fn, [(uc, f, 0), (uc, f, 1), dhm], [], [(2 * f, F32)], [], tm=128, name=name)
    return duc


def _ssm_act(conv, dtp_exp, bias_exp, aneg_exp, *, di, name):
    q = SSD_CHUNK
    gn = (conv.shape[1] - di) // 2

    def fn(cb, dtb, bb, ab):
        act = _silu(cb)
        dt = dtb + bb
        dt = jnp.maximum(dt, 0.0) + jnp.log(1.0 + jnp.exp(-jnp.abs(dt)))
        a = dt * ab
        tri = (lax.broadcasted_iota(jnp.int32, (q, q), 0) >= lax.broadcasted_iota(jnp.int32, (q, q), 1)).astype(F32)
        cs = _dot3(tri, a, (((1,), (0,)), ((), ())), 1)
        return [act[:, :di], act[:, di:di + gn], act[:, di + gn:], dt, cs], []
    outs, _ = _rowwise(fn, [conv, dtp_exp], [bias_exp, aneg_exp],
                       [(di, F32), (gn, F32), (gn, F32), (di, F32), (di, F32)], [], tm=q, name=name)
    return outs


def _head_masks(q):
    lane = lax.broadcasted_iota(jnp.int32, (q, LANES), 1)
    return lane < HEAD_DIM


def _pair_cols(cs, lo):
    sw = pltpu.roll(cs, HEAD_DIM, 1)
    return jnp.where(lo, cs, sw), jnp.where(lo, sw, cs)


def _ssd_fwd(xs, dt_exp, cs_exp, cs_rows, bm, cm, *, name):
    t, di = xs.shape
    q = SSD_CHUNK
    hg = di // N_GROUPS
    npair = hg // LANES
    nheads = hg // HEAD_DIM
    nc = t // q
    n = D_STATE

    def body(xs_ref, dt_ref, cs_ref, csr_ref, b_ref, c_ref, y_ref, st_ref, s_scr):
        ci = pl.program_id(1)

        @pl.when(ci == 0)
        def _():
            s_scr[...] = jnp.zeros_like(s_scr)

        bb = b_ref[...].astype(BF16)
        cb_ = c_ref[...].astype(BF16)
        cbm = lax.dot_general(cb_, bb, (((1,), (1,)), ((), ())), preferred_element_type=F32)
        tri = lax.broadcasted_iota(jnp.int32, (q, q), 0) >= lax.broadcasted_iota(jnp.int32, (q, q), 1)
        lo = _head_masks(q)
        csr = csr_ref[0]
        for p in range(npair):
            sl = pl.ds(p * LANES, LANES)
            x = xs_ref[:, sl] * dt_ref[:, sl]
            cs = cs_ref[:, sl]
            col0, col1 = _pair_cols(cs, lo)
            l0 = jnp.where(tri, jnp.exp(jnp.minimum(col0 - csr[2 * p:2 * p + 1, :], 0.0)), 0.0)
            l1 = jnp.where(tri, jnp.exp(jnp.minimum(col1 - csr[2 * p + 1:2 * p + 2, :], 0.0)), 0.0)
            xb = x.astype(BF16)
            yd = jnp.where(lo, jnp.dot((cbm * l0).astype(BF16), xb, preferred_element_type=F32),
                           jnp.dot((cbm * l1).astype(BF16), xb, preferred_element_type=F32))
            s = s_scr[p]
            st_ref[0, 0, p] = s
            yo = jnp.exp(cs) * jnp.dot(cb_, s.astype(BF16), preferred_element_type=F32)
            y_ref[:, sl] = yd + yo
            cs_end = cs[q - 1:q, :]
            xd = (x * jnp.exp(cs_end - cs)).astype(BF16)
            s_scr[p] = jnp.exp(cs_end) * s + lax.dot_general(bb, xd, (((0,), (0,)), ((), ())),
                                                             preferred_element_type=F32)

    return pl.pallas_call(
        body, name=name, grid=(N_GROUPS, nc),
        in_specs=[pl.BlockSpec((q, hg), lambda g, c: (c, g)),
                  pl.BlockSpec((q, hg), lambda g, c: (c, g)),
                  pl.BlockSpec((q, hg), lambda g, c: (c, g)),
                  pl.BlockSpec((1, nheads, q), lambda g, c: (g, 0, c)),
                  pl.BlockSpec((q, n), lambda g, c: (c, g)),
                  pl.BlockSpec((q, n), lambda g, c: (c, g))],
        out_specs=[pl.BlockSpec((q, hg), lambda g, c: (c, g)),
                   pl.BlockSpec((1, 1, npair, n, LANES), lambda g, c: (g, c, 0, 0, 0))],
        out_shape=[jax.ShapeDtypeStruct((t, di), F32),
                   jax.ShapeDtypeStruct((N_GROUPS, nc, npair, n, LANES), F32)],
        scratch_shapes=[pltpu.VMEM((npair, n, LANES), F32)],
        compiler_params=_cparams(dimension_semantics=("parallel", "arbitrary")),
    )(xs, dt_exp, cs_exp, cs_rows, bm, cm)


def _dot3(a, b, dims, split):
    rest = (a, b)[split].astype(F32)
    other = (a, b)[1 - split].astype(BF16)
    acc = None
    for _ in range(3):
        part = rest.astype(BF16)
        rest = rest - part.astype(F32)
        d = (lax.dot_general(part, other, dims, preferred_element_type=F32) if split == 0
             else lax.dot_general(other, part, dims, preferred_element_type=F32))
        acc = d if acc is None else acc + d
    return acc


def _ssd_bwd(xs, dt_exp, cs_exp, cs_rows, bm, cm, dy, states, aneg_exp, *, name):
    t, di = xs.shape
    q = SSD_CHUNK
    hg = di // N_GROUPS
    npair = hg // LANES
    nheads = hg // HEAD_DIM
    nc = t // q
    n = D_STATE
    nt_dims = (((1,), (1,)), ((), ()))
    tn_dims = (((0,), (0,)), ((), ()))

    mm_dims = (((1,), (0,)), ((), ()))

    def body(xs_ref, dt_ref, cs_ref, csr_ref, b_ref, c_ref, dy_ref, st_ref, an_ref,
             dxp_ref, db_ref, dc_ref, ddt_ref, dan_ref, r_scr):
        ci = pl.program_id(1)

        @pl.when(ci == 0)
        def _():
            r_scr[...] = jnp.zeros_like(r_scr)
            dan_ref[...] = jnp.zeros_like(dan_ref)

        bb = b_ref[...].astype(BF16)
        cb_ = c_ref[...].astype(BF16)
        cbm = lax.dot_general(cb_, bb, nt_dims, preferred_element_type=F32)
        row = lax.broadcasted_iota(jnp.int32, (q, q), 0)
        col = lax.broadcasted_iota(jnp.int32, (q, q), 1)
        tri = row >= col
        triu = (row <= col).astype(F32)
        trisl = (row > col).astype(F32)
        ones2 = (lax.broadcasted_iota(jnp.int32, (LANES, LANES), 0) // HEAD_DIM
                 == lax.broadcasted_iota(jnp.int32, (LANES, LANES), 1) // HEAD_DIM).astype(F32)
        onesq = jnp.ones((q, LANES), F32)
        last = lax.broadcasted_iota(jnp.int32, (q, LANES), 0) == q - 1
        lo = _head_masks(q)
        csr = csr_ref[0]
        dcb = jnp.zeros((q, q), F32)
        dc_acc = jnp.zeros((q, n), F32)
        db_acc = jnp.zeros((q, n), F32)
        for p in range(npair):
            sl = pl.ds(p * LANES, LANES)
            xsv = xs_ref[:, sl]
            dtv = dt_ref[:, sl]
            x = xsv * dtv
            cs = cs_ref[:, sl]
            dyv = dy_ref[:, sl]
            col0, col1 = _pair_cols(cs, lo)
            l0 = jnp.where(tri, jnp.exp(jnp.minimum(col0 - csr[2 * p:2 * p + 1, :], 0.0)), 0.0)
            l1 = jnp.where(tri, jnp.exp(jnp.minimum(col1 - csr[2 * p + 1:2 * p + 2, :], 0.0)), 0.0)
            xb = x.astype(BF16)
            dyb = dyv.astype(BF16)
            g0 = lax.dot_general(jnp.where(lo, dyv, 0.0).astype(BF16), xb, nt_dims, preferred_element_type=F32)
            g1 = lax.dot_general(jnp.where(lo, 0.0, dyv).astype(BF16), xb, nt_dims, preferred_element_type=F32)
            gl0, gl1 = g0 * l0, g1 * l1
            dcb = dcb + gl0 + gl1
            w0, w1 = cbm * gl0, cbm * gl1
            dxd = jnp.where(lo,
                            lax.dot_general((cbm * l0).astype(BF16), dyb, tn_dims, preferred_element_type=F32),
                            lax.dot_general((cbm * l1).astype(BF16), dyb, tn_dims, preferred_element_type=F32))
            e = jnp.exp(cs)
            cs_end = cs[q - 1:q, :]
            dte = jnp.exp(cs_end - cs)
            dend = jnp.exp(cs_end)
            sf = st_ref[0, 0, p]
            sb = sf.astype(BF16)
            r = r_scr[p]
            rb = r.astype(BF16)
            dyeb = (dyv * e).astype(BF16)
            dc_acc = dc_acc + lax.dot_general(dyeb, sb, nt_dims, preferred_element_type=F32)
            dxo = dte * jnp.dot(bb, rb, preferred_element_type=F32)
            db_acc = db_acc + lax.dot_general((x * dte).astype(BF16), rb, nt_dims, preferred_element_type=F32)
            r_scr[p] = dend * r + lax.dot_general(cb_, dyeb, tn_dims, preferred_element_type=F32)
            dx = dxd + dxo
            dxp_ref[:, sl] = dx
            yoff = e * jnp.dot(cb_, sb, preferred_element_type=F32)
            rw = jnp.where(lo, _dot3(w0, onesq, mm_dims, 0), _dot3(w1, onesq, mm_dims, 0))
            cw = jnp.where(lo, _dot3(w0, onesq, tn_dims, 0), _dot3(w1, onesq, tn_dims, 0))
            through = jnp.where(last, dend * _rowsum(r * sf), 0.0)
            suf = _dot3(dyv * yoff + through, ones2, mm_dims, 0) + rw - cw
            pre = _dot3(dxo * x, ones2, mm_dims, 0)
            da = _dot3(triu, suf, mm_dims, 1) + _dot3(trisl, pre, mm_dims, 1)
            qs = _dot3(dx * xsv, ones2, mm_dims, 0)
            ddt_ref[:, sl] = da * an_ref[:, sl] + qs
            dan_ref[:, sl] += _rowsum(da * dtv)
        dcbb = dcb.astype(BF16)
        dc_ref[...] = dc_acc + jnp.dot(dcbb, bb, preferred_element_type=F32)
        db_ref[...] = db_acc + lax.dot_general(dcbb, cb_, tn_dims, preferred_element_type=F32)

    rev = lambda g, c: (nc - 1 - c, g)
    return pl.pallas_call(
        body, name=name, grid=(N_GROUPS, nc),
        in_specs=[pl.BlockSpec((q, hg), rev), pl.BlockSpec((q, hg), rev), pl.BlockSpec((q, hg), rev),
                  pl.BlockSpec((1, nheads, q), lambda g, c: (g, 0, nc - 1 - c)),
                  pl.BlockSpec((q, n), rev), pl.BlockSpec((q, n), rev),
                  pl.BlockSpec((q, hg), rev),
                  pl.BlockSpec((1, 1, npair, n, LANES), lambda g, c: (g, nc - 1 - c, 0, 0, 0)),
                  pl.BlockSpec((1, hg), lambda g, c: (0, g))],
        out_specs=[pl.BlockSpec((q, hg), rev), pl.BlockSpec((q, n), rev), pl.BlockSpec((q, n), rev),
                   pl.BlockSpec((q, hg), rev), pl.BlockSpec((1, hg), lambda g, c: (0, g))],
        out_shape=[jax.ShapeDtypeStruct((t, di), F32), jax.ShapeDtypeStruct((t, N_GROUPS * n), F32),
                   jax.ShapeDtypeStruct((t, N_GROUPS * n), F32), jax.ShapeDtypeStruct((t, di), F32),
                   jax.ShapeDtypeStruct((1, di), F32)],
        scratch_shapes=[pltpu.VMEM((npair, n, LANES), F32)],
        compiler_params=_cparams(dimension_semantics=("parallel", "arbitrary")),
    )(xs, dt_exp, cs_exp, cs_rows, bm, cm, dy, states, aneg_exp)


def _group_stats(w, gw):
    return [lax.rsqrt(jnp.mean(w[:, i * gw:(i + 1) * gw] ** 2, axis=-1, keepdims=True) + RMS_EPS)
            for i in range(N_GROUPS)]


def _gated_norm_fwd(y_ssd, xs, z, d_exp, g, *, name):
    di = xs.shape[1]
    gw = di // N_GROUPS

    def fn(yb, xb, zb, db, gb):
        w = (yb + db * xb) * _silu(zb)
        rs = _group_stats(w, gw)
        return [jnp.concatenate([w[:, i * gw:(i + 1) * gw] * rs[i] for i in range(N_GROUPS)], axis=1) * gb], []
    (o,), _ = _rowwise(fn, [y_ssd, xs, (z, di, 0)], [d_exp, g], [(di, BF16)], [], tm=128, name=name)
    return o


def _gated_norm_bwd(y_ssd, xs, z, d_exp, g, do, *, name):
    di = xs.shape[1]
    gw = di // N_GROUPS

    def fn(yb, xb, zb, dob, db, gb):
        yy = yb + db * xb
        sz = _silu(zb)
        w = yy * sz
        rs = _group_stats(w, gw)
        dwh = dob * gb
        wh_parts, dw_parts = [], []
        for i in range(N_GROUPS):
            sl = slice(i * gw, (i + 1) * gw)
            wh = w[:, sl] * rs[i]
            wh_parts.append(wh)
            dw_parts.append(rs[i] * (dwh[:, sl] - wh * jnp.mean(dwh[:, sl] * wh, axis=-1, keepdims=True)))
        wh = jnp.concatenate(wh_parts, axis=1)
        dw = jnp.concatenate(dw_parts, axis=1)
        dy = dw * sz
        dz = dw * yy * _dsilu(zb)
        return [dy, dz], [_rowsum(dob * wh), _rowsum(dy * xb)]
    (dy, dz), (dg, dd) = _rowwise(fn, [y_ssd, xs, (z, di, 0), do], [d_exp, g], [(di, F32), (di, BF16)], [di, di],
                                  tm=128, name=name)
    return dy, dz, dg, dd


def _ssm_act_bwd(conv, dtp_exp, dxp, dy, dbm, dcm, ddt_exp, dt_exp, bias_exp, d_exp, *, di, name):
    gn = dbm.shape[1]

    def fn(cb, dtb, dxpb, dyb, dbb, dcb, ddtb, dteb, bb, db):
        dxs = dxpb * dteb + dyb * db
        dact = jnp.concatenate([dxs, dbb, dcb], axis=1)
        dconv = dact * _dsilu(cb)
        ddtp = ddtb * _sigmoid(dtb + bb)
        return [dconv, ddtp], [_rowsum(ddtp)]
    (dconv, ddtp), (dbias,) = _rowwise(fn, [conv, dtp_exp, dxp, dy, dbm, dcm, ddt_exp, dt_exp], [bias_exp, d_exp],
                                       [(di + 2 * gn, F32), (di, F32)], [di], tm=64, name=name)
    return dconv, ddtp, dbias


def _adamw(w, g, m, v, *, name):
    r, c = w.shape
    c1 = 1.0 / (1.0 - ADAM_B1 ** ADAM_STEP)
    c2 = 1.0 / (1.0 - ADAM_B2 ** ADAM_STEP)

    def fn(wb, gb, mb, vb):
        mn = ADAM_B1 * mb + (1.0 - ADAM_B1) * gb
        vn = ADAM_B2 * vb + (1.0 - ADAM_B2) * (gb * gb)
        delta = -ADAM_LR * ((mn * c1) / (jnp.sqrt(vn * c2) + ADAM_EPS) + ADAM_WD * wb)
        return [delta, mn, vn], []
    cap = max(8, ADAMW_BLOCK_ELEMS // c)
    tm = _pick(r, [p for p in (512, 256, 128, 64, 32, 16, 8) if p <= cap])
    (d, mn, vn), _ = _rowwise(fn, [w, g, m, v], [], [(c, F32)] * 3, [], tm=tm, name=name)
    return d, mn, vn


def _add_pair(sel, g, r, *, name):
    _, _, rows, cols = g.shape
    tm = _pick(rows, (256, 128, 64, 32, 16))

    def body(s_ref, g_ref, r_ref, o_ref):
        o_ref[...] = (g_ref[...].astype(F32) + r_ref[...].astype(F32)).astype(BF16)

    return pl.pallas_call(
        body, name=name,
        grid_spec=pltpu.PrefetchScalarGridSpec(
            num_scalar_prefetch=1, grid=(N_CHIPS, rows // tm),
            in_specs=[pl.BlockSpec((None, None, tm, cols), lambda j, i, s: (j, s[0], i, 0)),
                      pl.BlockSpec((None, tm, cols), lambda j, i, s: (j, i, 0))],
            out_specs=pl.BlockSpec((None, tm, cols), lambda j, i, s: (j, i, 0))),
        out_shape=jax.ShapeDtypeStruct((N_CHIPS, rows, cols), BF16),
        compiler_params=_cparams(dimension_semantics=("parallel", "parallel")),
    )(sel, g, r)


def _add_four(sel, p, r, *, name):
    _, rows, cols = p.shape
    tm = _pick(rows, (256, 128, 64, 32, 16))

    def body(s_ref, p_ref, r0, r1, r2, o_ref):
        o_ref[...] = ((p_ref[...].astype(F32) + r0[...].astype(F32)) + r1[...].astype(F32)) + r2[...].astype(F32)

    rspec = lambda k: pl.BlockSpec((None, tm, cols), lambda i, s, k=k: (k, i, 0))
    return pl.pallas_call(
        body, name=name,
        grid_spec=pltpu.PrefetchScalarGridSpec(
            num_scalar_prefetch=1, grid=(rows // tm,),
            in_specs=[pl.BlockSpec((None, tm, cols), lambda i, s: (s[0], i, 0)), rspec(0), rspec(1), rspec(2)],
            out_specs=pl.BlockSpec((None, tm, cols), lambda i, s: (s[1], i, 0))),
        out_shape=jax.ShapeDtypeStruct((2, rows, cols), F32),
        compiler_params=_cparams(dimension_semantics=("parallel",)),
    )(sel, p, r, r, r)


def _sum8(g, *, name):
    _, rows, cols = g.shape
    tm = _pick(rows, (512, 256, 128, 64, 32, 16, 8))

    def body(g_ref, o_ref):
        acc = g_ref[0]
        for k in range(1, 8):
            acc = acc + g_ref[k]
        o_ref[...] = acc

    return pl.pallas_call(
        body, name=name, grid=(rows // tm,),
        in_specs=[pl.BlockSpec((8, tm, cols), lambda i: (0, i, 0))],
        out_specs=pl.BlockSpec((tm, cols), lambda i: (i, 0)),
        out_shape=jax.ShapeDtypeStruct((rows, cols), F32),
        compiler_params=_cparams(dimension_semantics=("parallel",)),
    )(g)


def _place():
    x, y, c = lax.axis_index("x"), lax.axis_index("y"), lax.axis_index("c")
    chips = [(1 - x, y), (x, 1 - y), (1 - x, 1 - y)]
    return x, y, c, chips


def _rcopy(src, dst, send_sems, recv_sems, k, to):
    return pltpu.make_async_remote_copy(src_ref=src, dst_ref=dst, send_sem=send_sems.at[k], recv_sem=recv_sems.at[k],
                                        device_id=to, device_id_type=MESH)


def _gather_chips(packs, *, name):
    n = len(packs)

    def body(*refs):
        srcs, outs, (send_sems, recv_sems) = refs[:n], refs[n:2 * n], refs[2 * n:]
        x, y, c, chips = _place()
        sibling = (x, y, 1 - c)
        me = 2 * x + y
        first, passed = [], []
        for t, (src, out) in enumerate(zip(srcs, outs)):
            for k, (cx, cy) in enumerate(chips):
                cp = _rcopy(src.at[c], out.at[me, c], send_sems, recv_sems, 6 * t + k, (cx, cy, c))
                cp.start()
                first.append(cp)
        for t, out in enumerate(outs):
            for k, (cx, cy) in enumerate(chips):
                blk = out.at[2 * cx + cy, c]
                _rcopy(blk, blk, send_sems, recv_sems, 6 * t + k, (cx, cy, c)).wait_recv()
                fw = _rcopy(blk, blk, send_sems, recv_sems, 6 * t + 3 + k, sibling)
                fw.start()
                passed.append(fw)
        for t, out in enumerate(outs):
            for k, (cx, cy) in enumerate(chips):
                blk = out.at[2 * cx + cy, 1 - c]
                _rcopy(blk, blk, send_sems, recv_sems, 6 * t + 3 + k, sibling).wait_recv()
        for cp in first + passed:
            cp.wait_send()

    return pl.pallas_call(
        body, name=name, in_specs=[ANY] * n, out_specs=[ANY] * n,
        out_shape=[jax.ShapeDtypeStruct((N_CHIPS,) + p.shape, p.dtype) for p in packs],
        scratch_shapes=[pltpu.SemaphoreType.DMA((6 * n,)), pltpu.SemaphoreType.DMA((6 * n,))],
    )(*packs)


def _pass_on(gs, *, name):
    n = len(gs)

    def body(*refs):
        srcs, outs, (send_sems, recv_sems) = refs[:n], refs[n:2 * n], refs[2 * n:]
        x, y, c, chips = _place()
        sibling = (x, y, 1 - c)
        cps = [_rcopy(src.at[2 * cx + cy, c], out.at[2 * cx + cy, c], send_sems, recv_sems, 3 * t + k, sibling)
               for t, (src, out) in enumerate(zip(srcs, outs)) for k, (cx, cy) in enumerate(chips)]
        for cp in cps:
            cp.start()
        for t, out in enumerate(outs):
            for k, (cx, cy) in enumerate(chips):
                b = out.at[2 * cx + cy, 1 - c]
                _rcopy(b, b, send_sems, recv_sems, 3 * t + k, sibling).wait_recv()
        for cp in cps:
            cp.wait_send()

    return pl.pallas_call(
        body, name=name, in_specs=[ANY] * n, out_specs=[ANY] * n,
        out_shape=[jax.ShapeDtypeStruct(g.shape, g.dtype) for g in gs], input_output_aliases={t: t for t in range(n)},
        scratch_shapes=[pltpu.SemaphoreType.DMA((3 * n,)), pltpu.SemaphoreType.DMA((3 * n,))],
    )(*gs)


def _gather_devices(pack, *, name):
    rows, cols = pack.shape

    def body(src, out, send_sems, recv_sems, local_sem):
        x, y, c, chips = _place()
        sibling = (x, y, 1 - c)

        def blk(px, py, pc):
            return out.at[4 * px + 2 * py + pc]

        mine = pltpu.make_async_copy(src, blk(x, y, c), local_sem)
        mine.start()
        first = [_rcopy(src, blk(x, y, c), send_sems, recv_sems, 0, sibling)]
        first += [_rcopy(src, blk(x, y, c), send_sems, recv_sems, 1 + k, (cx, cy, c)) for k, (cx, cy) in enumerate(chips)]
        for cp in first:
            cp.start()
        passed = []
        for k, (cx, cy) in enumerate(chips):
            b = blk(cx, cy, c)
            _rcopy(b, b, send_sems, recv_sems, 1 + k, (cx, cy, c)).wait_recv()
            fw = _rcopy(b, b, send_sems, recv_sems, 4 + k, sibling)
            fw.start()
            passed.append(fw)
        b = blk(x, y, 1 - c)
        _rcopy(b, b, send_sems, recv_sems, 0, sibling).wait_recv()
        for k, (cx, cy) in enumerate(chips):
            b = blk(cx, cy, 1 - c)
            _rcopy(b, b, send_sems, recv_sems, 4 + k, sibling).wait_recv()
        for cp in first + passed:
            cp.wait_send()
        mine.wait()

    return pl.pallas_call(
        body, name=name, in_specs=[ANY], out_specs=ANY,
        out_shape=jax.ShapeDtypeStruct((8, rows, cols), pack.dtype),
        scratch_shapes=[pltpu.SemaphoreType.DMA((7,)), pltpu.SemaphoreType.DMA((7,)), pltpu.SemaphoreType.DMA],
    )(pack)


def _swap_halves(gs, *, name):
    n = len(gs)

    def body(*refs):
        srcs, outs, (send_sems, recv_sems) = refs[:n], refs[n:2 * n], refs[2 * n:]
        x, y, c, _ = _place()
        cps = [_rcopy(src.at[j, 1 - c], out.at[j], send_sems, recv_sems, N_CHIPS * t + j, (x, y, 1 - c))
               for t, (src, out) in enumerate(zip(srcs, outs)) for j in range(N_CHIPS)]
        for cp in cps:
            cp.start()
        for cp in cps:
            cp.wait()

    return pl.pallas_call(
        body, name=name, in_specs=[ANY] * n, out_specs=[ANY] * n,
        out_shape=[jax.ShapeDtypeStruct((N_CHIPS,) + g.shape[2:], g.dtype) for g in gs],
        scratch_shapes=[pltpu.SemaphoreType.DMA((N_CHIPS * n,)), pltpu.SemaphoreType.DMA((N_CHIPS * n,))],
    )(*gs)


def _join_halves(rs, *, name):
    n = len(rs)

    def body(*refs):
        srcs, outs, (send_sems, recv_sems) = refs[:n], refs[n:2 * n], refs[2 * n:]
        x, y, c, _ = _place()
        cps = [_rcopy(src.at[c], out.at[c], send_sems, recv_sems, t, (x, y, 1 - c))
               for t, (src, out) in enumerate(zip(srcs, outs))]
        for cp in cps:
            cp.start()
        for t, out in enumerate(outs):
            b = out.at[1 - c]
            _rcopy(b, b, send_sems, recv_sems, t, (x, y, 1 - c)).wait_recv()
        for cp in cps:
            cp.wait_send()

    return pl.pallas_call(
        body, name=name, in_specs=[ANY] * n, out_specs=[ANY] * n,
        out_shape=[jax.ShapeDtypeStruct(r.shape, r.dtype) for r in rs], input_output_aliases={t: t for t in range(n)},
        scratch_shapes=[pltpu.SemaphoreType.DMA((n,)), pltpu.SemaphoreType.DMA((n,))],
    )(*rs)


def _flat_rows(parts, cols):
    flat = jnp.concatenate([p.reshape(-1) for p in parts])
    n = flat.shape[0]
    rows = -(-n // cols)
    rows = 8 * (-(-rows // 8))
    return jnp.pad(flat, (0, rows * cols - n)).reshape(rows, cols)


def _expand(v, di):
    return jnp.repeat(v, HEAD_DIM).reshape(1, di)


def kernel(x, norm_mix_g, norm_ffn_g, norm_final_g, cv_w_in, cv_b_in, cv_w_dw, cv_b_dw, cv_ln_g, cv_ln_b, cv_w_out, cv_b_out, ssm_w_in, ssm_w_conv, ssm_b_conv, ssm_dt_bias, ssm_a_log, ssm_d, ssm_norm_g, ssm_w_out, ffn_w_up, ffn_w_dw, ffn_b_dw, ffn_w_down, loss_target, m_norm_mix_g, m_norm_ffn_g, m_norm_final_g, m_cv_w_in, m_cv_b_in, m_cv_w_dw, m_cv_b_dw, m_cv_ln_g, m_cv_ln_b, m_cv_w_out, m_cv_b_out, m_ssm_w_in, m_ssm_w_conv, m_ssm_b_conv, m_ssm_dt_bias, m_ssm_a_log, m_ssm_d, m_ssm_norm_g, m_ssm_w_out, m_ffn_w_up, m_ffn_w_dw, m_ffn_b_dw, m_ffn_w_down, v_norm_mix_g, v_norm_ffn_g, v_norm_final_g, v_cv_w_in, v_cv_b_in, v_cv_w_dw, v_cv_b_dw, v_cv_ln_g, v_cv_ln_b, v_cv_w_out, v_cv_b_out, v_ssm_w_in, v_ssm_w_conv, v_ssm_b_conv, v_ssm_dt_bias, v_ssm_a_log, v_ssm_d, v_ssm_norm_g, v_ssm_w_out, v_ffn_w_up, v_ffn_w_dw, v_ffn_b_dw, v_ffn_w_down):
    weights = dict(norm_mix_g=norm_mix_g, norm_ffn_g=norm_ffn_g, norm_final_g=norm_final_g, cv_w_in=cv_w_in, cv_b_in=cv_b_in, cv_w_dw=cv_w_dw, cv_b_dw=cv_b_dw, cv_ln_g=cv_ln_g, cv_ln_b=cv_ln_b, cv_w_out=cv_w_out, cv_b_out=cv_b_out, ssm_w_in=ssm_w_in, ssm_w_conv=ssm_w_conv, ssm_b_conv=ssm_b_conv, ssm_dt_bias=ssm_dt_bias, ssm_a_log=ssm_a_log, ssm_d=ssm_d, ssm_norm_g=ssm_norm_g, ssm_w_out=ssm_w_out, ffn_w_up=ffn_w_up, ffn_w_dw=ffn_w_dw, ffn_b_dw=ffn_b_dw, ffn_w_down=ffn_w_down)
    mom_m = dict(norm_mix_g=m_norm_mix_g, norm_ffn_g=m_norm_ffn_g, norm_final_g=m_norm_final_g, cv_w_in=m_cv_w_in, cv_b_in=m_cv_b_in, cv_w_dw=m_cv_w_dw, cv_b_dw=m_cv_b_dw, cv_ln_g=m_cv_ln_g, cv_ln_b=m_cv_ln_b, cv_w_out=m_cv_w_out, cv_b_out=m_cv_b_out, ssm_w_in=m_ssm_w_in, ssm_w_conv=m_ssm_w_conv, ssm_b_conv=m_ssm_b_conv, ssm_dt_bias=m_ssm_dt_bias, ssm_a_log=m_ssm_a_log, ssm_d=m_ssm_d, ssm_norm_g=m_ssm_norm_g, ssm_w_out=m_ssm_w_out, ffn_w_up=m_ffn_w_up, ffn_w_dw=m_ffn_w_dw, ffn_b_dw=m_ffn_b_dw, ffn_w_down=m_ffn_w_down)
    mom_v = dict(norm_mix_g=v_norm_mix_g, norm_ffn_g=v_norm_ffn_g, norm_final_g=v_norm_final_g, cv_w_in=v_cv_w_in, cv_b_in=v_cv_b_in, cv_w_dw=v_cv_w_dw, cv_b_dw=v_cv_b_dw, cv_ln_g=v_cv_ln_g, cv_ln_b=v_cv_ln_b, cv_w_out=v_cv_w_out, cv_b_out=v_cv_b_out, ssm_w_in=v_ssm_w_in, ssm_w_conv=v_ssm_w_conv, ssm_b_conv=v_ssm_b_conv, ssm_dt_bias=v_ssm_dt_bias, ssm_a_log=v_ssm_a_log, ssm_d=v_ssm_d, ssm_norm_g=v_ssm_norm_g, ssm_w_out=v_ssm_w_out, ffn_w_up=v_ffn_w_up, ffn_w_dw=v_ffn_w_dw, ffn_b_dw=v_ffn_b_dw, ffn_w_down=v_ffn_w_down)
    names = list(weights)

    xt = x[0]
    tgt = loss_target[0]
    t, d = xt.shape
    depth = norm_mix_g.shape[0]
    n_cv, n_ssm = cv_w_in.shape[0], ssm_w_in.shape[0]
    di = ssm_w_out.shape[1] * N_CHIPS
    n_heads = di // HEAD_DIM
    gn = N_GROUPS * D_STATE
    ssm_in = ssm_w_in.shape[2] * N_CHIPS
    chip = 2 * lax.axis_index("x") + lax.axis_index("y")

    cq = ssm_w_in.shape[2]
    cqp = LANES * (-(-cq // LANES))
    by_col = ("cv_w_in", "ssm_w_in", "ffn_w_up")
    big_names = ("cv_w_in", "cv_w_out", "ssm_w_in", "ssm_w_out", "ffn_w_up", "ffn_w_down")

    def layer_tensors(i):
        mixer = [("cv_w_in", i // 2), ("cv_w_out", i // 2)] if i % 2 == 0 else [("ssm_w_in", i // 2), ("ssm_w_out", i // 2)]
        return mixer + [("ffn_w_up", i), ("ffn_w_down", i)]

    def halves(a):
        return a.reshape((2, a.shape[0] // 2) + a.shape[1:])

    order = [key for i in range(depth) for key in layer_tensors(i)]
    shards = []
    for nm, l in order:
        w = weights[nm][l]
        if nm == "ssm_w_in":
            w = jnp.pad(w, ((0, 0), (0, cqp - cq)))
        shards.append(halves(w.astype(BF16)))
    shards = dict(zip(order, lax.optimization_barrier(shards)))
    full = {}

    def arrived(key, g):
        g = lax.dynamic_update_index_in_dim(g, shards[key], chip, 0)
        g = g.reshape((N_CHIPS, 2 * g.shape[2], g.shape[3]))
        full[key] = g if key[0] in by_col else g.reshape(N_CHIPS * g.shape[1], g.shape[2])

    ahead = 2
    for key, g in zip(order[:ahead], _gather_chips([shards[k] for k in order[:ahead]], name="gather_weights")):
        arrived(key, g)

    def mm_fwd(key, a, **kw):
        pos = order.index(key)
        kw["b_chips"] = key[0] in by_col
        if pos + ahead >= len(order):
            return _matmul(a, full[key], **kw)
        nxt = order[pos + ahead]
        out, land = _matmul(a, full[key], carry=("gather", shards[nxt]), **kw)
        arrived(nxt, _pass_on([land], name="pass_on")[0])
        return out

    def ssm_cols(a, lo, hi):
        parts = []
        for jj in range(N_CHIPS):
            s0, s1 = max(lo, jj * cq), min(hi, (jj + 1) * cq)
            if s0 < s1:
                parts.append(a[:, jj * cqp + s0 - jj * cq:jj * cqp + s1 - jj * cq])
        return parts[0] if len(parts) == 1 else jnp.concatenate(parts, axis=1)

    def ssm_cols_back(a):
        return jnp.concatenate([jnp.pad(a[:, jj * cq:(jj + 1) * cq], ((0, 0), (0, cqp - cq))) for jj in range(N_CHIPS)],
                               axis=1)

    small_sharded = ["cv_w_dw", "ssm_w_conv", "ssm_b_conv", "ssm_norm_g", "ffn_w_dw"]
    spack = _flat_rows([weights[nm] for nm in small_sharded], LANES)
    sg = _gather_devices(spack, name="gather_small").reshape(8, -1)[::2]
    o = 0
    for nm in small_sharded:
        shp = weights[nm].shape
        n = weights[nm].size
        full[nm] = jnp.concatenate([sg[j, o:o + n].reshape(shp) for j in range(N_CHIPS)], axis=-1)
        o += n

    row = lambda v: v.reshape(1, -1)

    saved = []
    xc = xt
    for i in range(depth):
        j = i // 2
        s = {"x_in": xc}
        h = _rms_fwd(xc, row(norm_mix_g[i]), name="rms_mix_fwd")
        s["h"] = h
        if i % 2 == 0:
            u = mm_fwd(("cv_w_in", j), h, name="cv_in_fwd")
            v1 = _glu_fwd(u, row(cv_b_in[j]), name="cv_glu_fwd")
            v2 = _dwconv_fwd(v1, full["cv_w_dw"][j], row(cv_b_dw[j]), name="cv_dw_fwd")
            v4 = _ln_silu_fwd(v2, row(cv_ln_g[j]), row(cv_ln_b[j]), name="cv_ln_fwd")
            xc = mm_fwd(("cv_w_out", j), v4, bias=row(cv_b_out[j]), res=xc, name="cv_out_fwd")
            s.update(u=u, v1=v1, v2=v2, v4=v4)
        else:
            zx = mm_fwd(("ssm_w_in", j), h, name="ssm_in_fwd")
            z = ssm_cols(zx, 0, di)
            xbc_pre = ssm_cols(zx, di, 2 * di + 2 * gn)
            dtp_exp = jnp.repeat(ssm_cols(zx, 2 * di + 2 * gn, ssm_in), HEAD_DIM, axis=1)
            conv = _dwconv_fwd(xbc_pre, full["ssm_w_conv"][j], row(full["ssm_b_conv"][j]), name="ssm_dw_fwd")
            bias_exp = _expand(ssm_dt_bias[j], di)
            aneg_exp = _expand(-jnp.exp(ssm_a_log[j]), di)
            d_exp = _expand(ssm_d[j], di)
            xs, bm, cm, dt_exp, cs_exp = _ssm_act(conv, dtp_exp, bias_exp, aneg_exp, di=di, name="ssm_act_fwd")
            cs_rows = cs_exp[:, ::HEAD_DIM].T.reshape(N_GROUPS, n_heads // N_GROUPS, t)
            y_ssd, states = _ssd_fwd(xs, dt_exp, cs_exp, cs_rows, bm, cm, name="ssd_fwd")
            gnrm = _gated_norm_fwd(y_ssd, xs, z, d_exp, row(full["ssm_norm_g"][j]), name="ssm_norm_fwd")
            xc = mm_fwd(("ssm_w_out", j), gnrm, res=xc, name="ssm_out_fwd")
            s.update(z=z, xbc_pre=xbc_pre, dtp_exp=dtp_exp, conv=conv, bias_exp=bias_exp, aneg_exp=aneg_exp,
                     d_exp=d_exp, xs=xs, bm=bm, cm=cm, dt_exp=dt_exp, cs_exp=cs_exp, cs_rows=cs_rows, y_ssd=y_ssd,
                     states=states, gnrm=gnrm)
        s["x_mid"] = xc
        h2 = _rms_fwd(xc, row(norm_ffn_g[i]), name="rms_ffn_fwd")
        u2 = mm_fwd(("ffn_w_up", i), h2, name="ffn_up_fwd")
        uc = _dwconv_fwd(u2, full["ffn_w_dw"][i], row(ffn_b_dw[i]), name="ffn_dw_fwd")
        hm = _ffn_gate_fwd(uc, name="ffn_gate_fwd")
        xc = mm_fwd(("ffn_w_down", i), hm, res=xc, name="ffn_down_fwd")
        s.update(h2=h2, u2=u2, uc=uc, hm=hm)
        saved.append(s)

    dx, dxb, sq, dg_final = _loss_head(xc, row(norm_final_g), tgt, name="loss_head")
    loss_part = 0.5 / d * jnp.sum(sq)
    gr = {nm: [None] * weights[nm].shape[0] for nm in names if nm != "norm_final_g"}
    reduced = {}
    sel_c = jnp.reshape(lax.axis_index("c"), (1,)).astype(jnp.int32)
    sel_j = jnp.stack([chip, lax.axis_index("c")]).astype(jnp.int32)

    def backward_pair(key, act, dout, dw_name, dx_name):
        col = key[0] in by_col
        g = _matmul(act, dout, ta=True, out_chips=col, out_dtype=BF16, name=dw_name)
        if not col:
            g = g.reshape(N_CHIPS, g.shape[0] // N_CHIPS, g.shape[1])
        g = g.reshape(N_CHIPS, 2, g.shape[1] // 2, g.shape[2])
        (got,) = _swap_halves([g], name="grads_swap")
        pair = _add_pair(sel_c, g, got, name="grads_add2")
        dact, got3 = _matmul(dout, full[key], tb=True, b_chips=col, carry=("scatter", pair), name=dx_name)
        half = _add_four(sel_j, pair, got3, name="grads_add4")
        (r,) = _join_halves([half], name="grads_join")
        reduced[key] = r.reshape(2 * r.shape[1], r.shape[2])[:, :weights[key[0]].shape[2]]
        return dact

    for i in reversed(range(depth)):
        j = i // 2
        s = saved[i]
        dhm = backward_pair(("ffn_w_down", i), s["hm"], dxb, "ffn_down_dw", "ffn_down_dx")
        duc = _ffn_gate_bwd(s["uc"], dhm, name="ffn_gate_bwd")
        du2b, dw_dw, db_dw = _dwconv_bwd(s["u2"], duc, full["ffn_w_dw"][i], dx_dtype=BF16, name="ffn_dw_bwd")
        gr["ffn_w_dw"][i], gr["ffn_b_dw"][i] = dw_dw, db_dw[0]
        dh2 = backward_pair(("ffn_w_up", i), s["h2"], du2b, "ffn_up_dw", "ffn_up_dx")
        dx, dxb, colsum, dg = _rms_bwd(s["x_mid"], row(norm_ffn_g[i]), dh2, dx, name="rms_ffn_bwd")
        gr["norm_ffn_g"][i] = dg[0]
        if i % 2 == 0:
            gr["cv_b_out"][j] = colsum[0]
            dv4 = backward_pair(("cv_w_out", j), s["v4"], dxb, "cv_out_dw", "cv_out_dx")
            dv2, dlg, dlb = _ln_silu_bwd(s["v2"], row(cv_ln_g[j]), row(cv_ln_b[j]), dv4, name="cv_ln_bwd")
            gr["cv_ln_g"][j], gr["cv_ln_b"][j] = dlg[0], dlb[0]
            dv1, dw_dw, db_dw = _dwconv_bwd(s["v1"], dv2, full["cv_w_dw"][j], name="cv_dw_bwd")
            gr["cv_w_dw"][j], gr["cv_b_dw"][j] = dw_dw, db_dw[0]
            du, db_in = _glu_bwd(s["u"], row(cv_b_in[j]), dv1, name="cv_glu_bwd")
            gr["cv_b_in"][j] = db_in[0]
            dh = backward_pair(("cv_w_in", j), s["h"], du, "cv_in_dw", "cv_in_dx")
        else:
            dgn = backward_pair(("ssm_w_out", j), s["gnrm"], dxb, "ssm_out_dw", "ssm_out_dx")
            dy, dz, dng, ddl = _gated_norm_bwd(s["y_ssd"], s["xs"], s["z"], s["d_exp"], row(full["ssm_norm_g"][j]),
                                               dgn, name="ssm_norm_bwd")
            gr["ssm_norm_g"][j] = dng[0]
            gr["ssm_d"][j] = ddl.reshape(n_heads, HEAD_DIM).sum(axis=1)
            dxp, dbm, dcm, ddt_exp, dan = _ssd_bwd(s["xs"], s["dt_exp"], s["cs_exp"], s["cs_rows"], s["bm"], s["cm"],
                                                   dy, s["states"], s["aneg_exp"], name="ssd_bwd")
            gr["ssm_a_log"][j] = dan[0, ::HEAD_DIM] * s["aneg_exp"][0, ::HEAD_DIM]
            dconv, ddtp, dbias = _ssm_act_bwd(s["conv"], s["dtp_exp"], dxp, dy, dbm, dcm, ddt_exp, s["dt_exp"],
                                              s["bias_exp"], s["d_exp"], di=di, name="ssm_act_bwd")
            gr["ssm_dt_bias"][j] = dbias[0, ::HEAD_DIM]
            dxbc, dw_c, db_c = _dwconv_bwd(s["xbc_pre"], dconv, full["ssm_w_conv"][j], dx_dtype=BF16, name="ssm_dw_bwd")
            gr["ssm_w_conv"][j], gr["ssm_b_conv"][j] = dw_c, db_c[0]
            dzx = ssm_cols_back(jnp.concatenate([dz, dxbc, ddtp[:, ::HEAD_DIM].astype(BF16)], axis=1))
            dh = backward_pair(("ssm_w_in", j), s["h"], dzx, "ssm_in_dw", "ssm_in_dx")
        dx, dxb, _, dg = _rms_bwd(s["x_in"], row(norm_mix_g[i]), dh, dx, name="rms_mix_bwd")
        gr["norm_mix_g"][i] = dg[0]

    grads = {nm: jnp.stack([reduced[nm, l] for l in range(weights[nm].shape[0])]) for nm in big_names}

    small = [nm for nm in names if nm not in grads]
    small_parts = []
    for nm in small:
        small_parts.append(dg_final[0] if nm == "norm_final_g" else jnp.stack(gr[nm]))
    gs_pack = _flat_rows(small_parts + [loss_part.reshape(1)], LANES)
    gs = _sum8(_gather_devices(gs_pack, name="gather_small_grads"), name="sum_small_grads").reshape(-1)
    o = 0
    for nm, p in zip(small, small_parts):
        gfull = gs[o:o + p.size].reshape(p.shape)
        o += p.size
        if nm in small_sharded:
            width = weights[nm].shape[-1]
            gfull = lax.dynamic_slice_in_dim(gfull, chip * width, width, axis=gfull.ndim - 1)
        grads[nm] = gfull
    loss = gs[o]

    delta, new_m, new_v = {}, {}, {}
    for nm in big_names:
        shp = weights[nm].shape
        as2d = lambda a: a.reshape(-1, shp[-1])
        dl, mn, vn = _adamw(as2d(weights[nm]), as2d(grads[nm]), as2d(mom_m[nm]), as2d(mom_v[nm]), name="adamw_" + nm)
        delta[nm], new_m[nm], new_v[nm] = dl.reshape(shp), mn.reshape(shp), vn.reshape(shp)
    pk = lambda dct: _flat_rows([dct[nm] for nm in small], LANES)
    dl, mn, vn = _adamw(pk(weights), pk(grads), pk(mom_m), pk(mom_v), name="adamw_small")
    dl, mn, vn = dl.reshape(-1), mn.reshape(-1), vn.reshape(-1)
    o = 0
    for nm in small:
        shp, n = weights[nm].shape, weights[nm].size
        delta[nm], new_m[nm], new_v[nm] = (a[o:o + n].reshape(shp) for a in (dl, mn, vn))
        o += n

    return (loss, dx[None], *[grads[nm] for nm in names], *[delta[nm] for nm in names],
            *[new_m[nm] for nm in names], *[new_v[nm] for nm in names])
```

```python
import functools

import jax
import jax.numpy as jnp
from jax import lax
from jax.experimental import pallas as pl
from jax.experimental.pallas import tpu as pltpu

F32, BF16 = jnp.float32, jnp.bfloat16
MESH = pl.DeviceIdType.MESH
ANY = pl.BlockSpec(memory_space=pl.ANY)

RMS_EPS = 1e-6
LN_EPS = 1e-5
HEAD_DIM = 64
N_GROUPS = 8
D_STATE = 128
ADAM_LR, ADAM_B1, ADAM_B2, ADAM_EPS, ADAM_WD, ADAM_STEP = 0.001, 0.9, 0.999, 1e-08, 0.01, 10

VMEM_LIMIT_BYTES = 56 * 1024 * 1024
LANES = 128
SSD_CHUNK = 128
ADAMW_BLOCK_ELEMS = 512 * 1024
N_CHIPS = 4


def _cparams(**kw):
    return pltpu.CompilerParams(vmem_limit_bytes=VMEM_LIMIT_BYTES, **kw)


def _pick(n, prefs):
    for p in prefs:
        if n % p == 0:
            return p
    return n


def _sigmoid(x):
    return 1.0 / (1.0 + jnp.exp(-x))


def _silu(x):
    return x * _sigmoid(x)


def _dsilu(x):
    s = _sigmoid(x)
    return s * (1.0 + x * (1.0 - s))


def _rowsum(x):
    return jnp.sum(x, axis=0, keepdims=True)


MM_TILES = (2816, 2688, 2048, 1408, 1024, 896, 512, 384, 256, 128)
MM_VMEM_BUDGET = 40 * 1024 * 1024


def _mm_tiles(m, n, k, n_unit, k_unit, out_bytes, has_res):
    best = None
    for tk in [t for t in MM_TILES if k_unit % t == 0]:
        for tm in [t for t in (1024, 512, 256, 128) if m % t == 0] or [m]:
            for tn in [t for t in MM_TILES if n_unit % t == 0]:
                vmem = 2 * 2 * (tm * tk + tk * tn) + 2 * tm * tn * out_bytes
                vmem += tm * tn * 4 if k // tk > 1 else 0
                vmem += 2 * tm * tn * 4 if has_res else 0
                if vmem > MM_VMEM_BUDGET:
                    continue
                traffic = m * k * (n // tn) + k * n * (m // tm)
                if best is None or traffic < best[0]:
                    best = (traffic, tm, tn, tk)
        if best is not None:
            return best[1:]
    raise ValueError((m, n, k))


def _matmul(a, b, *, name, ta=False, tb=False, a_chips=False, b_chips=False, out_chips=False, out_dtype=F32, bias=None,
            res=None, carry=None):
    if a_chips:
        assert tb and b_chips and not ta
        m, k = a.shape[1], N_CHIPS * a.shape[2]
    else:
        m, k = (a.shape[1], a.shape[0]) if ta else a.shape
    if b_chips:
        nq = b.shape[2]
        n = b.shape[1] if tb else N_CHIPS * nq
        assert k == (N_CHIPS * nq if tb else b.shape[1])
    else:
        n = b.shape[0] if tb else b.shape[1]
        assert k == (b.shape[1] if tb else b.shape[0])
        nq = n // N_CHIPS
    has_bias, has_res = bias is not None, res is not None
    tm, tn, tk = _mm_tiles(m, n, k, nq if ((b_chips and not tb) or out_chips) else n, nq if (b_chips and tb) else k,
                           jnp.dtype(out_dtype).itemsize, has_res)
    gm, gn, nk = m // tm, n // tn, k // tk
    nbq = nq // (tk if tb else tn) if (b_chips or out_chips) else 1
    dn = (((0 if ta else 1,), (1 if tb else 0,)), ((), ()))
    kind = carry[0] if carry else None

    def body(*refs):
        a_ref, b_ref = refs[0], refs[1]
        rest = list(refs[2:])
        bias_ref = rest.pop(0) if has_bias else None
        res_ref = rest.pop(0) if has_res else None
        src = rest.pop(0) if carry else None
        o_ref = rest.pop(0)
        dst = rest.pop(0) if carry else None
        acc_ref = rest.pop(0) if nk > 1 else None
        i, j, kk = pl.program_id(0), pl.program_id(1), pl.program_id(2)

        if carry:
            send_sems, recv_sems = rest
            x, y, c, chips = _place()
            if kind == "gather":
                cps = [_rcopy(src.at[c], dst.at[2 * x + y, c], send_sems, recv_sems, q, (cx, cy, c))
                       for q, (cx, cy) in enumerate(chips)]
            else:
                cps = [_rcopy(src.at[2 * cx + cy], dst.at[q], send_sems, recv_sems, q, (cx, cy, c))
                       for q, (cx, cy) in enumerate(chips)]

            @pl.when((i == 0) & (j == 0) & (kk == 0))
            def _():
                for cp in cps:
                    cp.start()

        def finish(r):
            if has_bias:
                r = r + bias_ref[...]
            if has_res:
                r = r + res_ref[...]
            o_ref[...] = r.astype(o_ref.dtype)

        part = lax.dot_general(a_ref[...].astype(BF16), b_ref[...].astype(BF16), dn, preferred_element_type=F32)
        if nk == 1:
            finish(part)
        else:
            @pl.when(kk == 0)
            def _():
                acc_ref[...] = part

            @pl.when(kk > 0)
            def _():
                acc_ref[...] += part

            @pl.when(kk == nk - 1)
            def _():
                finish(acc_ref[...])

        if carry:
            @pl.when((i == gm - 1) & (j == gn - 1) & (kk == nk - 1))
            def _():
                if kind == "gather":
                    for q, (cx, cy) in enumerate(chips):
                        blk = dst.at[2 * cx + cy, c]
                        _rcopy(blk, blk, send_sems, recv_sems, q, (cx, cy, c)).wait_recv()
                    for cp in cps:
                        cp.wait_send()
                else:
                    for cp in cps:
                        cp.wait()

    if a_chips:
        a_spec = pl.BlockSpec((None, tm, tk), lambda i, j, kk: (kk // nbq, i, kk % nbq))
    elif ta:
        a_spec = pl.BlockSpec((tk, tm), lambda i, j, kk: (kk, i))
    else:
        a_spec = pl.BlockSpec((tm, tk), lambda i, j, kk: (i, kk))
    if b_chips and tb:
        b_spec = pl.BlockSpec((None, tn, tk), lambda i, j, kk: (kk // nbq, j, kk % nbq))
    elif b_chips:
        b_spec = pl.BlockSpec((None, tk, tn), lambda i, j, kk: (j // nbq, kk, j % nbq))
    elif tb:
        b_spec = pl.BlockSpec((tn, tk), lambda i, j, kk: (j, kk))
    else:
        b_spec = pl.BlockSpec((tk, tn), lambda i, j, kk: (kk, j))
    if out_chips:
        out_spec = pl.BlockSpec((None, tm, tn), lambda i, j, kk: (j // nbq, i, j % nbq))
        out_shape = jax.ShapeDtypeStruct((N_CHIPS, m, nq), out_dtype)
    else:
        out_spec = pl.BlockSpec((tm, tn), lambda i, j, kk: (i, j))
        out_shape = jax.ShapeDtypeStruct((m, n), out_dtype)
    in_specs, args = [a_spec, b_spec], [a, b]
    if has_bias:
        in_specs.append(pl.BlockSpec((1, tn), lambda i, j, kk: (0, j)))
        args.append(bias)
    if has_res:
        in_specs.append(pl.BlockSpec((tm, tn), lambda i, j, kk: (i, j)))
        args.append(res)
    scratch = [pltpu.VMEM((tm, tn), F32)] if nk > 1 else []
    if not carry:
        return pl.pallas_call(
            body, name=name, grid=(gm, gn, nk), in_specs=in_specs, out_specs=out_spec, out_shape=out_shape,
            scratch_shapes=scratch,
            compiler_params=_cparams(dimension_semantics=("parallel", "parallel", "arbitrary")),
        )(*args)
    moved = carry[1]
    land = (N_CHIPS,) + moved.shape if kind == "gather" else (3,) + moved.shape[1:]
    return pl.pallas_call(
        body, name=name, grid=(gm, gn, nk), in_specs=in_specs + [ANY], out_specs=[out_spec, ANY],
        out_shape=[out_shape, jax.ShapeDtypeStruct(land, moved.dtype)],
        scratch_shapes=scratch + [pltpu.SemaphoreType.DMA((3,)), pltpu.SemaphoreType.DMA((3,))],
        compiler_params=_cparams(dimension_semantics=("arbitrary", "arbitrary", "arbitrary")),
    )(*args, moved)


def _rowwise(fn, rows, pars, outs, reds, *, tm, name):
    rows = [r if isinstance(r, tuple) else (r, r.shape[1], 0) for r in rows]
    t = rows[0][0].shape[0]
    assert t % tm == 0
    n_in, n_o = len(rows) + len(pars), len(outs)

    def body(*refs):
        i = pl.program_id(0)
        o, d = fn(*[r[...] for r in refs[:n_in]])
        for ref, val in zip(refs[n_in:n_in + n_o], o):
            ref[...] = val.astype(ref.dtype)
        d_refs = refs[n_in + n_o:]

        @pl.when(i == 0)
        def _():
            for ref in d_refs:
                ref[...] = jnp.zeros_like(ref)

        for ref, val in zip(d_refs, d):
            ref[...] += val

    in_specs = [pl.BlockSpec((tm, w), lambda i, b=blk: (i, b)) for _, w, blk in rows]
    in_specs += [pl.BlockSpec((1, p.shape[1]), lambda i: (0, 0)) for p in pars]
    out_specs = [pl.BlockSpec((tm, c), lambda i: (i, 0)) for c, _ in outs]
    out_specs += [pl.BlockSpec((1, c), lambda i: (0, 0)) for c in reds]
    out_shape = [jax.ShapeDtypeStruct((t, c), dt) for c, dt in outs] + [jax.ShapeDtypeStruct((1, c), F32) for c in reds]
    res = pl.pallas_call(
        body, name=name, grid=(t // tm,), in_specs=in_specs, out_specs=out_specs, out_shape=out_shape,
        compiler_params=_cparams(dimension_semantics=("arbitrary",)),
    )(*[r[0] for r in rows], *pars)
    return res[:n_o], res[n_o:]


def _rms_fwd(x, g, *, name):
    def fn(xb, gb):
        r = lax.rsqrt(jnp.mean(xb * xb, axis=-1, keepdims=True) + RMS_EPS)
        return [xb * r * gb], []
    (h,), _ = _rowwise(fn, [x], [g], [(x.shape[1], BF16)], [], tm=256, name=name)
    return h


def _rms_bwd(x, g, dh, dres, *, name):
    def fn(xb, dhb, drb, gb):
        r = lax.rsqrt(jnp.mean(xb * xb, axis=-1, keepdims=True) + RMS_EPS)
        xh = xb * r
        dxh = dhb * gb
        dx = r * (dxh - xh * jnp.mean(dxh * xh, axis=-1, keepdims=True))
        out = drb + dx
        return [out, out], [_rowsum(out), _rowsum(dhb * xh)]
    c = x.shape[1]
    (dx, dxb), (colsum, dg) = _rowwise(fn, [x, dh, dres], [g], [(c, F32), (c, BF16)], [c, c], tm=256, name=name)
    return dx, dxb, colsum, dg


def _loss_head(x, g, tgt, *, name):
    d_model = x.shape[1]

    def fn(xb, tb, gb):
        r = lax.rsqrt(jnp.mean(xb * xb, axis=-1, keepdims=True) + RMS_EPS)
        xh = xb * r
        e = xh * gb - tb
        dy = e * (1.0 / d_model)
        dxh = dy * gb
        dx = r * (dxh - xh * jnp.mean(dxh * xh, axis=-1, keepdims=True))
        return [dx, dx], [_rowsum(e * e), _rowsum(dy * xh)]
    (dx, dxb), (sq, dg) = _rowwise(fn, [x, tgt], [g], [(d_model, F32), (d_model, BF16)], [d_model, d_model], tm=256,
                                   name=name)
    return dx, dxb, sq, dg


def _halo_rows(k):
    return 8 * ((k - 1 + 7) // 8) if k > 1 else 8


def _pad_taps(w):
    k = w.shape[0]
    kp = 8 * ((k + 7) // 8)
    return jnp.pad(w, ((0, kp - k), (0, 0)))


def _conv_chunks(k, tc):
    rc, lw = (32, 256) if k > 9 else (16, 512)
    lanes, l0 = [], 0
    while l0 < tc:
        lanes.append((l0, min(lw, tc - l0)))
        l0 += lw
    return rc, lanes


def _fold8(v):
    acc = v[0:8]
    for q in range(1, v.shape[0] // 8):
        acc = acc + v[8 * q:8 * q + 8]
    return acc


def _taps(src_ref, lead, base, rc, lanes, w_ref, k, sign, acc):
    for s in range(k):
        rows = pl.ds(base + sign * s, rc)
        acc = acc + w_ref[k - 1 - s:k - s, lanes] * src_ref[lead + (rows, lanes)]
    return acc


def _dwconv_fwd(x, w, b, *, name):
    t, c = x.shape
    k = w.shape[0]
    h = _halo_rows(k)
    tm = _pick(t, (256, 128))
    tc = _pick(c, (512, 256, 128) if k > 9 else (1536, 1024, 512, 256, 128))
    wp = _pad_taps(w)
    kp = wp.shape[0]
    rb = tm // h
    rc, lane_chunks = _conv_chunks(k, tc)

    def body(x_ref, p_ref, w_ref, b_ref, o_ref, ext):
        i = pl.program_id(0)
        ext[pl.ds(h, tm), :] = x_ref[...]
        ext[pl.ds(0, h), :] = jnp.where(i > 0, p_ref[...], 0.0)
        for l0, lw in lane_chunks:
            lanes = pl.ds(l0, lw)
            for r0 in range(0, tm, rc):
                acc = jnp.broadcast_to(b_ref[:, lanes], (rc, lw))
                o_ref[pl.ds(r0, rc), lanes] = _taps(ext, (), h + r0, rc, lanes, w_ref, k, -1, acc)

    return pl.pallas_call(
        body, name=name, grid=(t // tm, c // tc),
        in_specs=[pl.BlockSpec((tm, tc), lambda i, j: (i, j)),
                  pl.BlockSpec((h, tc), lambda i, j: (jnp.maximum(i * rb - 1, 0), j)),
                  pl.BlockSpec((kp, tc), lambda i, j: (0, j)),
                  pl.BlockSpec((1, tc), lambda i, j: (0, j))],
        out_specs=pl.BlockSpec((tm, tc), lambda i, j: (i, j)),
        out_shape=jax.ShapeDtypeStruct((t, c), F32),
        scratch_shapes=[pltpu.VMEM((h + tm, tc), F32)],
        compiler_params=_cparams(dimension_semantics=("parallel", "parallel")),
    )(x, x, wp, b)


def _dwconv_bwd(x, dy, w, *, name, dx_dtype=F32):
    t, c = x.shape
    k = w.shape[0]
    h = _halo_rows(k)
    tm = _pick(t, (256, 128))
    tc = _pick(c, (512, 256, 128) if k > 9 else (1536, 1024, 512, 256, 128))
    wp = _pad_taps(w)
    kp = wp.shape[0]
    rb = tm // h
    nt = t // tm
    rc, lane_chunks = _conv_chunks(k, tc)

    def body(x_ref, p_ref, dy_ref, n_ref, w_ref, dx_ref, dw_ref, db_ref, xext, dext):
        i = pl.program_id(1)

        @pl.when(i == 0)
        def _():
            dw_ref[...] = jnp.zeros_like(dw_ref)
            db_ref[...] = jnp.zeros_like(db_ref)

        xext[pl.ds(h, tm), :] = x_ref[...]
        xext[pl.ds(0, h), :] = jnp.where(i > 0, p_ref[...], 0.0)
        dext[pl.ds(0, tm), :] = dy_ref[...]
        dext[pl.ds(tm, h), :] = jnp.where(i < nt - 1, n_ref[...], 0.0)
        for l0, lw in lane_chunks:
            lanes = pl.ds(l0, lw)
            for r0 in range(0, tm, rc):
                acc = _taps(dext, (), r0, rc, lanes, w_ref, k, +1, jnp.zeros((rc, lw), F32))
                dx_ref[pl.ds(r0, rc), lanes] = acc.astype(dx_ref.dtype)
            for s in range(k):
                a8 = jnp.zeros((8, lw), F32)
                for r0 in range(0, tm, rc):
                    a8 = a8 + _fold8(xext[pl.ds(h + r0 - s, rc), lanes] * dext[pl.ds(r0, rc), lanes])
                dw_ref[k - 1 - s:k - s, lanes] += _rowsum(a8)
            b8 = jnp.zeros((8, lw), F32)
            for r0 in range(0, tm, rc):
                b8 = b8 + _fold8(dext[pl.ds(r0, rc), lanes])
            db_ref[:, lanes] += _rowsum(b8)

    dx, dw, db = pl.pallas_call(
        body, name=name, grid=(c // tc, nt),
        in_specs=[pl.BlockSpec((tm, tc), lambda j, i: (i, j)),
                  pl.BlockSpec((h, tc), lambda j, i: (jnp.maximum(i * rb - 1, 0), j)),
                  pl.BlockSpec((tm, tc), lambda j, i: (i, j)),
                  pl.BlockSpec((h, tc), lambda j, i: (jnp.minimum((i + 1) * rb, nt * rb - 1), j)),
                  pl.BlockSpec((kp, tc), lambda j, i: (0, j))],
        out_specs=[pl.BlockSpec((tm, tc), lambda j, i: (i, j)),
                   pl.BlockSpec((kp, tc), lambda j, i: (0, j)),
                   pl.BlockSpec((1, tc), lambda j, i: (0, j))],
        out_shape=[jax.ShapeDtypeStruct((t, c), dx_dtype), jax.ShapeDtypeStruct((kp, c), F32),
                   jax.ShapeDtypeStruct((1, c), F32)],
        scratch_shapes=[pltpu.VMEM((h + tm, tc), F32), pltpu.VMEM((tm + h, tc), F32)],
        compiler_params=_cparams(dimension_semantics=("parallel", "arbitrary")),
    )(x, x, dy, dy, wp)
    return dx, dw[:k], db


def _glu_fwd(u, b_in, *, name):
    d = u.shape[1] // 2

    def fn(ua, ug, ba, bg):
        return [(ua + ba) * _sigmoid(ug + bg)], []
    (v,), _ = _rowwise(fn, [(u, d, 0), (u, d, 1)], [b_in[:, :d], b_in[:, d:]], [(d, F32)], [], tm=256, name=name)
    return v


def _glu_bwd(u, b_in, dv, *, name):
    d = u.shape[1] // 2

    def fn(ua, ug, dvb, ba, bg):
        a = ua + ba
        s = _sigmoid(ug + bg)
        du = jnp.concatenate([dvb * s, dvb * a * s * (1.0 - s)], axis=1)
        return [du], [_rowsum(du)]
    (du,), (db,) = _rowwise(fn, [(u, d, 0), (u, d, 1), dv], [b_in[:, :d], b_in[:, d:]], [(2 * d, BF16)], [2 * d],
                            tm=256, name=name)
    return du, db


def _ln_silu_fwd(v, g, b, *, name):
    def fn(vb, gb, bb):
        mu = jnp.mean(vb, axis=-1, keepdims=True)
        xc = vb - mu
        rstd = lax.rsqrt(jnp.mean(xc * xc, axis=-1, keepdims=True) + LN_EPS)
        return [_silu(xc * rstd * gb + bb)], []
    (o,), _ = _rowwise(fn, [v], [g, b], [(v.shape[1], BF16)], [], tm=256, name=name)
    return o


def _ln_silu_bwd(v, g, b, do, *, name):
    def fn(vb, dob, gb, bb):
        mu = jnp.mean(vb, axis=-1, keepdims=True)
        xc = vb - mu
        rstd = lax.rsqrt(jnp.mean(xc * xc, axis=-1, keepdims=True) + LN_EPS)
        xh = xc * rstd
        dy = dob * _dsilu(xh * gb + bb)
        dxh = dy * gb
        dv = rstd * (dxh - jnp.mean(dxh, axis=-1, keepdims=True) - xh * jnp.mean(dxh * xh, axis=-1, keepdims=True))
        return [dv], [_rowsum(dy * xh), _rowsum(dy)]
    c = v.shape[1]
    (dv,), (dg, db) = _rowwise(fn, [v, do], [g, b], [(c, F32)], [c, c], tm=256, name=name)
    return dv, dg, db


def _ffn_mid_setup(u2c, w, b):
    _, t, nq = u2c.shape
    f = 2 * nq
    k = w.shape[0]
    h = _halo_rows(k)
    tm = _pick(t, (256, 128))
    tc = _pick(nq, (1408, 1024, 512, 256, 128))
    rc, lane_chunks = _conv_chunks(k, tc)
    return dict(t=t, nq=nq, f=f, k=k, h=h, tm=tm, tc=tc, npq=nq // tc, rb=tm // h, nt=t // tm, rc=rc,
                lane_chunks=lane_chunks, u4=u2c.reshape(2, 2, t, nq), wg=_pad_taps(w[:, :f]), wv=_pad_taps(w[:, f:]),
                bg=b[:, :f], bv=b[:, f:])


def _ffn_mid_fwd(u2c, w, b, *, name):
    p = _ffn_mid_setup(u2c, w, b)
    t, f, k, h, tm, tc, npq, rb, rc = (p[n] for n in ("t", "f", "k", "h", "tm", "tc", "npq", "rb", "rc"))
    kp = p["wg"].shape[0]

    def body(u_ref, p_ref, wg_ref, wv_ref, bg_ref, bv_ref, o_ref, ext):
        i = pl.program_id(2)
        for kind in range(2):
            ext[kind, pl.ds(h, tm), :] = u_ref[kind]
            ext[kind, pl.ds(0, h), :] = jnp.where(i > 0, p_ref[kind], 0.0)
        for l0, lw in p["lane_chunks"]:
            lanes = pl.ds(l0, lw)
            for r0 in range(0, tm, rc):
                g = _taps(ext, (0,), h + r0, rc, lanes, wg_ref, k, -1, jnp.broadcast_to(bg_ref[:, lanes], (rc, lw)))
                v = _taps(ext, (1,), h + r0, rc, lanes, wv_ref, k, -1, jnp.broadcast_to(bv_ref[:, lanes], (rc, lw)))
                o_ref[pl.ds(r0, rc), lanes] = (_silu(g) * v).astype(BF16)

    col = lambda q, jj, i: (0, q * npq + jj)
    return pl.pallas_call(
        body, name=name, grid=(2, npq, t // tm),
        in_specs=[pl.BlockSpec((2, None, tm, tc), lambda q, jj, i: (0, q, i, jj)),
                  pl.BlockSpec((2, None, h, tc), lambda q, jj, i: (0, q, jnp.maximum(i * rb - 1, 0), jj)),
                  pl.BlockSpec((kp, tc), col), pl.BlockSpec((kp, tc), col),
                  pl.BlockSpec((1, tc), col), pl.BlockSpec((1, tc), col)],
        out_specs=pl.BlockSpec((tm, tc), lambda q, jj, i: (i, q * npq + jj)),
        out_shape=jax.ShapeDtypeStruct((t, f), BF16),
        scratch_shapes=[pltpu.VMEM((2, h + tm, tc), F32)],
        compiler_params=_cparams(dimension_semantics=("parallel", "parallel", "parallel")),
    )(p["u4"], p["u4"], p["wg"], p["wv"], p["bg"], p["bv"])


def _ffn_mid_bwd(u2c, dhm, w, b, *, name):
    p = _ffn_mid_setup(u2c, w, b)
    t, nq, f, k, h, tm, tc, npq, rb, nt, rc = (p[n] for n in ("t", "nq", "f", "k", "h", "tm", "tc", "npq", "rb", "nt", "rc"))
    kp = p["wg"].shape[0]
    chunks1 = [(r0, rc) for r0 in range(0, tm, rc)] + [(tm, h)]

    def body(u_ref, p_ref, n_ref, dh_ref, nd_ref, wg_ref, wv_ref, bg_ref, bv_ref,
             du_ref, dwg_ref, dwv_ref, dbg_ref, dbv_ref, uext, dsc):
        i = pl.program_id(2)
        w_refs, dw_refs, db_refs = (wg_ref, wv_ref), (dwg_ref, dwv_ref), (dbg_ref, dbv_ref)

        @pl.when(i == 0)
        def _():
            for ref in dw_refs + db_refs:
                ref[...] = jnp.zeros_like(ref)

        for kind in range(2):
            uext[kind, pl.ds(0, h), :] = jnp.where(i > 0, p_ref[kind], 0.0)
            uext[kind, pl.ds(h, tm), :] = u_ref[kind]
            uext[kind, pl.ds(h + tm, h), :] = jnp.where(i < nt - 1, n_ref[kind], 0.0)
        for l0, lw in p["lane_chunks"]:
            lanes = pl.ds(l0, lw)
            for r0, rr in chunks1:
                g = _taps(uext, (0,), h + r0, rr, lanes, wg_ref, k, -1, jnp.broadcast_to(bg_ref[:, lanes], (rr, lw)))
                v = _taps(uext, (1,), h + r0, rr, lanes, wv_ref, k, -1, jnp.broadcast_to(bv_ref[:, lanes], (rr, lw)))
                dh = dh_ref[pl.ds(r0, rr), lanes] if r0 < tm else jnp.where(i < nt - 1, nd_ref[:, lanes], 0.0)
                sg = _sigmoid(g)
                dsc[0, pl.ds(r0, rr), lanes] = dh * v * (sg * (1.0 + g * (1.0 - sg)))
                dsc[1, pl.ds(r0, rr), lanes] = dh * (g * sg)
            for kind in range(2):
                for r0 in range(0, tm, rc):
                    acc = _taps(dsc, (kind,), r0, rc, lanes, w_refs[kind], k, +1, jnp.zeros((rc, lw), F32))
                    du_ref[kind, pl.ds(r0, rc), lanes] = acc.astype(BF16)
                for s in range(k):
                    a8 = jnp.zeros((8, lw), F32)
                    for r0 in range(0, tm, rc):
                        a8 = a8 + _fold8(uext[kind, pl.ds(h + r0 - s, rc), lanes] * dsc[kind, pl.ds(r0, rc), lanes])
                    dw_refs[kind][k - 1 - s:k - s, lanes] += _rowsum(a8)
                b8 = jnp.zeros((8, lw), F32)
                for r0 in range(0, tm, rc):
                    b8 = b8 + _fold8(dsc[kind, pl.ds(r0, rc), lanes])
                db_refs[kind][:, lanes] += _rowsum(b8)

    col = lambda q, jj, i: (0, q * npq + jj)
    nxt = lambda i: jnp.minimum((i + 1) * rb, nt * rb - 1)
    du, dwg, dwv, dbg, dbv = pl.pallas_call(
        body, name=name, grid=(2, npq, nt),
        in_specs=[pl.BlockSpec((2, None, tm, tc), lambda q, jj, i: (0, q, i, jj)),
                  pl.BlockSpec((2, None, h, tc), lambda q, jj, i: (0, q, jnp.maximum(i * rb - 1, 0), jj)),
                  pl.BlockSpec((2, None, h, tc), lambda q, jj, i: (0, q, nxt(i), jj)),
                  pl.BlockSpec((tm, tc), lambda q, jj, i: (i, q * npq + jj)),
                  pl.BlockSpec((h, tc), lambda q, jj, i: (nxt(i), q * npq + jj)),
                  pl.BlockSpec((kp, tc), col), pl.BlockSpec((kp, tc), col),
                  pl.BlockSpec((1, tc), col), pl.BlockSpec((1, tc), col)],
        out_specs=[pl.BlockSpec((2, None, tm, tc), lambda q, jj, i: (0, q, i, jj)),
                   pl.BlockSpec((kp, tc), col), pl.BlockSpec((kp, tc), col),
                   pl.BlockSpec((1, tc), col), pl.BlockSpec((1, tc), col)],
        out_shape=[jax.ShapeDtypeStruct((2, 2, t, nq), BF16), jax.ShapeDtypeStruct((kp, f), F32),
                   jax.ShapeDtypeStruct((kp, f), F32), jax.ShapeDtypeStruct((1, f), F32), jax.ShapeDtypeStruct((1, f), F32)],
        scratch_shapes=[pltpu.VMEM((2, h + tm + h, tc), F32), pltpu.VMEM((2, tm + h, tc), F32)],
        compiler_params=_cparams(dimension_semantics=("parallel", "parallel", "arbitrary")),
    )(p["u4"], p["u4"], p["u4"], dhm, dhm, p["wg"], p["wv"], p["bg"], p["bv"])
    return (du.reshape(N_CHIPS, t, nq), jnp.concatenate([dwg[:k], dwv[:k]], axis=1), jnp.concatenate([dbg, dbv], axis=1))


def _ssm_act(conv, dtp_exp, bias_exp, aneg_exp, *, di, name):
    q = SSD_CHUNK
    gn = (conv.shape[1] - di) // 2

    def fn(cb, dtb, bb, ab):
        act = _silu(cb)
        dt = dtb + bb
        dt = jnp.maximum(dt, 0.0) + jnp.log(1.0 + jnp.exp(-jnp.abs(dt)))
        a = dt * ab
        tri = (lax.broadcasted_iota(jnp.int32, (q, q), 0) >= lax.broadcasted_iota(jnp.int32, (q, q), 1)).astype(F32)
        cs = _dot3(tri, a, (((1,), (0,)), ((), ())), 1)
        return [act[:, :di], act[:, di:di + gn], act[:, di + gn:], dt, cs], []
    outs, _ = _rowwise(fn, [conv, dtp_exp], [bias_exp, aneg_exp],
                       [(di, F32), (gn, F32), (gn, F32), (di, F32), (di, F32)], [], tm=q, name=name)
    return outs


def _head_masks(q):
    lane = lax.broadcasted_iota(jnp.int32, (q, LANES), 1)
    return lane < HEAD_DIM


def _pair_cols(cs, lo):
    sw = pltpu.roll(cs, HEAD_DIM, 1)
    return jnp.where(lo, cs, sw), jnp.where(lo, sw, cs)


def _ssd_fwd(xs, dt_exp, cs_exp, cs_rows, bm, cm, *, name):
    t, di = xs.shape
    q = SSD_CHUNK
    hg = di // N_GROUPS
    npair = hg // LANES
    nheads = hg // HEAD_DIM
    nc = t // q
    n = D_STATE

    def body(xs_ref, dt_ref, cs_ref, csr_ref, b_ref, c_ref, y_ref, st_ref, s_scr):
        ci = pl.program_id(1)

        @pl.when(ci == 0)
        def _():
            s_scr[...] = jnp.zeros_like(s_scr)

        bb = b_ref[...].astype(BF16)
        cb_ = c_ref[...].astype(BF16)
        cbm = lax.dot_general(cb_, bb, (((1,), (1,)), ((), ())), preferred_element_type=F32)
        tri = lax.broadcasted_iota(jnp.int32, (q, q), 0) >= lax.broadcasted_iota(jnp.int32, (q, q), 1)
        lo = _head_masks(q)
        csr = csr_ref[0]
        for p in range(npair):
            sl = pl.ds(p * LANES, LANES)
            x = xs_ref[:, sl] * dt_ref[:, sl]
            cs = cs_ref[:, sl]
            col0, col1 = _pair_cols(cs, lo)
            l0 = jnp.where(tri, jnp.exp(jnp.minimum(col0 - csr[2 * p:2 * p + 1, :], 0.0)), 0.0)
            l1 = jnp.where(tri, jnp.exp(jnp.minimum(col1 - csr[2 * p + 1:2 * p + 2, :], 0.0)), 0.0)
            xb = x.astype(BF16)
            yd = jnp.where(lo, jnp.dot((cbm * l0).astype(BF16), xb, preferred_element_type=F32),
                           jnp.dot((cbm * l1).astype(BF16), xb, preferred_element_type=F32))
            s = s_scr[p]
            st_ref[0, 0, p] = s
            yo = jnp.exp(cs) * jnp.dot(cb_, s.astype(BF16), preferred_element_type=F32)
            y_ref[:, sl] = yd + yo
            cs_end = cs[q - 1:q, :]
            xd = (x * jnp.exp(cs_end - cs)).astype(BF16)
            s_scr[p] = jnp.exp(cs_end) * s + lax.dot_general(bb, xd, (((0,), (0,)), ((), ())),
                                                             preferred_element_type=F32)

    return pl.pallas_call(
        body, name=name, grid=(N_GROUPS, nc),
        in_specs=[pl.BlockSpec((q, hg), lambda g, c: (c, g)),
                  pl.BlockSpec((q, hg), lambda g, c: (c, g)),
                  pl.BlockSpec((q, hg), lambda g, c: (c, g)),
                  pl.BlockSpec((1, nheads, q), lambda g, c: (g, 0, c)),
                  pl.BlockSpec((q, n), lambda g, c: (c, g)),
                  pl.BlockSpec((q, n), lambda g, c: (c, g))],
        out_specs=[pl.BlockSpec((q, hg), lambda g, c: (c, g)),
                   pl.BlockSpec((1, 1, npair, n, LANES), lambda g, c: (g, c, 0, 0, 0))],
        out_shape=[jax.ShapeDtypeStruct((t, di), F32),
                   jax.ShapeDtypeStruct((N_GROUPS, nc, npair, n, LANES), F32)],
        scratch_shapes=[pltpu.VMEM((npair, n, LANES), F32)],
        compiler_params=_cparams(dimension_semantics=("parallel", "arbitrary")),
    )(xs, dt_exp, cs_exp, cs_rows, bm, cm)


def _dot3(a, b, dims, split):
    rest = (a, b)[split].astype(F32)
    other = (a, b)[1 - split].astype(BF16)
    acc = None
    for _ in range(3):
        part = rest.astype(BF16)
        rest = rest - part.astype(F32)
        d = (lax.dot_general(part, other, dims, preferred_element_type=F32) if split == 0
             else lax.dot_general(other, part, dims, preferred_element_type=F32))
        acc = d if acc is None else acc + d
    return acc


def _ssd_bwd(xs, dt_exp, cs_exp, cs_rows, bm, cm, dy, states, aneg_exp, *, name):
    t, di = xs.shape
    q = SSD_CHUNK
    hg = di // N_GROUPS
    npair = hg // LANES
    nheads = hg // HEAD_DIM
    nc = t // q
    n = D_STATE
    nt_dims = (((1,), (1,)), ((), ()))
    tn_dims = (((0,), (0,)), ((), ()))

    mm_dims = (((1,), (0,)), ((), ()))

    def body(xs_ref, dt_ref, cs_ref, csr_ref, b_ref, c_ref, dy_ref, st_ref, an_ref,
             dxp_ref, db_ref, dc_ref, ddt_ref, dan_ref, r_scr):
        ci = pl.program_id(1)

        @pl.when(ci == 0)
        def _():
            r_scr[...] = jnp.zeros_like(r_scr)
            dan_ref[...] = jnp.zeros_like(dan_ref)

        bb = b_ref[...].astype(BF16)
        cb_ = c_ref[...].astype(BF16)
        cbm = lax.dot_general(cb_, bb, nt_dims, preferred_element_type=F32)
        row = lax.broadcasted_iota(jnp.int32, (q, q), 0)
        col = lax.broadcasted_iota(jnp.int32, (q, q), 1)
        tri = row >= col
        triu = (row <= col).astype(F32)
        trisl = (row > col).astype(F32)
        ones2 = (lax.broadcasted_iota(jnp.int32, (LANES, LANES), 0) // HEAD_DIM
                 == lax.broadcasted_iota(jnp.int32, (LANES, LANES), 1) // HEAD_DIM).astype(F32)
        onesq = jnp.ones((q, LANES), F32)
        last = lax.broadcasted_iota(jnp.int32, (q, LANES), 0) == q - 1
        lo = _head_masks(q)
        csr = csr_ref[0]
        dcb = jnp.zeros((q, q), F32)
        dc_acc = jnp.zeros((q, n), F32)
        db_acc = jnp.zeros((q, n), F32)
        for p in range(npair):
            sl = pl.ds(p * LANES, LANES)
            xsv = xs_ref[:, sl]
            dtv = dt_ref[:, sl]
            x = xsv * dtv
            cs = cs_ref[:, sl]
            dyv = dy_ref[:, sl]
            col0, col1 = _pair_cols(cs, lo)
            l0 = jnp.where(tri, jnp.exp(jnp.minimum(col0 - csr[2 * p:2 * p + 1, :], 0.0)), 0.0)
            l1 = jnp.where(tri, jnp.exp(jnp.minimum(col1 - csr[2 * p + 1:2 * p + 2, :], 0.0)), 0.0)
            xb = x.astype(BF16)
            dyb = dyv.astype(BF16)
            g0 = lax.dot_general(jnp.where(lo, dyv, 0.0).astype(BF16), xb, nt_dims, preferred_element_type=F32)
            g1 = lax.dot_general(jnp.where(lo, 0.0, dyv).astype(BF16), xb, nt_dims, preferred_element_type=F32)
            gl0, gl1 = g0 * l0, g1 * l1
            dcb = dcb + gl0 + gl1
            w0, w1 = cbm * gl0, cbm * gl1
            dxd = jnp.where(lo,
                            lax.dot_general((cbm * l0).astype(BF16), dyb, tn_dims, preferred_element_type=F32),
                            lax.dot_general((cbm * l1).astype(BF16), dyb, tn_dims, preferred_element_type=F32))
            e = jnp.exp(cs)
            cs_end = cs[q - 1:q, :]
            dte = jnp.exp(cs_end - cs)
            dend = jnp.exp(cs_end)
            sf = st_ref[0, 0, p]
            sb = sf.astype(BF16)
            r = r_scr[p]
            rb = r.astype(BF16)
            dyeb = (dyv * e).astype(BF16)
            dc_acc = dc_acc + lax.dot_general(dyeb, sb, nt_dims, preferred_element_type=F32)
            dxo = dte * jnp.dot(bb, rb, preferred_element_type=F32)
            db_acc = db_acc + lax.dot_general((x * dte).astype(BF16), rb, nt_dims, preferred_element_type=F32)
            r_scr[p] = dend * r + lax.dot_general(cb_, dyeb, tn_dims, preferred_element_type=F32)
            dx = dxd + dxo
            dxp_ref[:, sl] = dx
            yoff = e * jnp.dot(cb_, sb, preferred_element_type=F32)
            rw = jnp.where(lo, _dot3(w0, onesq, mm_dims, 0), _dot3(w1, onesq, mm_dims, 0))
            cw = jnp.where(lo, _dot3(w0, onesq, tn_dims, 0), _dot3(w1, onesq, tn_dims, 0))
            through = jnp.where(last, dend * _rowsum(r * sf), 0.0)
            suf = _dot3(dyv * yoff + through, ones2, mm_dims, 0) + rw - cw
            pre = _dot3(dxo * x, ones2, mm_dims, 0)
            da = _dot3(triu, suf, mm_dims, 1) + _dot3(trisl, pre, mm_dims, 1)
            qs = _dot3(dx * xsv, ones2, mm_dims, 0)
            ddt_ref[:, sl] = da * an_ref[:, sl] + qs
            dan_ref[:, sl] += _rowsum(da * dtv)
        dcbb = dcb.astype(BF16)
        dc_ref[...] = dc_acc + jnp.dot(dcbb, bb, preferred_element_type=F32)
        db_ref[...] = db_acc + lax.dot_general(dcbb, cb_, tn_dims, preferred_element_type=F32)

    rev = lambda g, c: (nc - 1 - c, g)
    return pl.pallas_call(
        body, name=name, grid=(N_GROUPS, nc),
        in_specs=[pl.BlockSpec((q, hg), rev), pl.BlockSpec((q, hg), rev), pl.BlockSpec((q, hg), rev),
                  pl.BlockSpec((1, nheads, q), lambda g, c: (g, 0, nc - 1 - c)),
                  pl.BlockSpec((q, n), rev), pl.BlockSpec((q, n), rev),
                  pl.BlockSpec((q, hg), rev),
                  pl.BlockSpec((1, 1, npair, n, LANES), lambda g, c: (g, nc - 1 - c, 0, 0, 0)),
                  pl.BlockSpec((1, hg), lambda g, c: (0, g))],
        out_specs=[pl.BlockSpec((q, hg), rev), pl.BlockSpec((q, n), rev), pl.BlockSpec((q, n), rev),
                   pl.BlockSpec((q, hg), rev), pl.BlockSpec((1, hg), lambda g, c: (0, g))],
        out_shape=[jax.ShapeDtypeStruct((t, di), F32), jax.ShapeDtypeStruct((t, N_GROUPS * n), F32),
                   jax.ShapeDtypeStruct((t, N_GROUPS * n), F32), jax.ShapeDtypeStruct((t, di), F32),
                   jax.ShapeDtypeStruct((1, di), F32)],
        scratch_shapes=[pltpu.VMEM((npair, n, LANES), F32)],
        compiler_params=_cparams(dimension_semantics=("parallel", "arbitrary")),
    )(xs, dt_exp, cs_exp, cs_rows, bm, cm, dy, states, aneg_exp)


def _group_stats(w, gw):
    return [lax.rsqrt(jnp.mean(w[:, i * gw:(i + 1) * gw] ** 2, axis=-1, keepdims=True) + RMS_EPS)
            for i in range(N_GROUPS)]


def _gated_norm_fwd(y_ssd, xs, z, d_exp, g, *, name):
    di = xs.shape[1]
    gw = di // N_GROUPS

    def fn(yb, xb, zb, db, gb):
        w = (yb + db * xb) * _silu(zb)
        rs = _group_stats(w, gw)
        return [jnp.concatenate([w[:, i * gw:(i + 1) * gw] * rs[i] for i in range(N_GROUPS)], axis=1) * gb], []
    (o,), _ = _rowwise(fn, [y_ssd, xs, (z, di, 0)], [d_exp, g], [(di, BF16)], [], tm=128, name=name)
    return o


def _gated_norm_bwd(y_ssd, xs, z, d_exp, g, do, *, name):
    di = xs.shape[1]
    gw = di // N_GROUPS

    def fn(yb, xb, zb, dob, db, gb):
        yy = yb + db * xb
        sz = _silu(zb)
        w = yy * sz
        rs = _group_stats(w, gw)
        dwh = dob * gb
        wh_parts, dw_parts = [], []
        for i in range(N_GROUPS):
            sl = slice(i * gw, (i + 1) * gw)
            wh = w[:, sl] * rs[i]
            wh_parts.append(wh)
            dw_parts.append(rs[i] * (dwh[:, sl] - wh * jnp.mean(dwh[:, sl] * wh, axis=-1, keepdims=True)))
        wh = jnp.concatenate(wh_parts, axis=1)
        dw = jnp.concatenate(dw_parts, axis=1)
        dy = dw * sz
        dz = dw * yy * _dsilu(zb)
        return [dy, dz], [_rowsum(dob * wh), _rowsum(dy * xb)]
    (dy, dz), (dg, dd) = _rowwise(fn, [y_ssd, xs, (z, di, 0), do], [d_exp, g], [(di, F32), (di, BF16)], [di, di],
                                  tm=128, name=name)
    return dy, dz, dg, dd


def _ssm_act_bwd(conv, dtp_exp, dxp, dy, dbm, dcm, ddt_exp, dt_exp, bias_exp, d_exp, *, di, name):
    gn = dbm.shape[1]

    def fn(cb, dtb, dxpb, dyb, dbb, dcb, ddtb, dteb, bb, db):
        dxs = dxpb * dteb + dyb * db
        dact = jnp.concatenate([dxs, dbb, dcb], axis=1)
        dconv = dact * _dsilu(cb)
        ddtp = ddtb * _sigmoid(dtb + bb)
        return [dconv, ddtp], [_rowsum(ddtp)]
    (dconv, ddtp), (dbias,) = _rowwise(fn, [conv, dtp_exp, dxp, dy, dbm, dcm, ddt_exp, dt_exp], [bias_exp, d_exp],
                                       [(di + 2 * gn, F32), (di, F32)], [di], tm=64, name=name)
    return dconv, ddtp, dbias


def _adamw(w, g, m, v, *, name):
    r, c = w.shape
    c1 = 1.0 / (1.0 - ADAM_B1 ** ADAM_STEP)
    c2 = 1.0 / (1.0 - ADAM_B2 ** ADAM_STEP)

    def fn(wb, gb, mb, vb):
        mn = ADAM_B1 * mb + (1.0 - ADAM_B1) * gb
        vn = ADAM_B2 * vb + (1.0 - ADAM_B2) * (gb * gb)
        delta = -ADAM_LR * ((mn * c1) / (jnp.sqrt(vn * c2) + ADAM_EPS) + ADAM_WD * wb)
        return [delta, mn, vn], []
    cap = max(8, ADAMW_BLOCK_ELEMS // c)
    tm = _pick(r, [p for p in (512, 256, 128, 64, 32, 16, 8) if p <= cap])
    (d, mn, vn), _ = _rowwise(fn, [w, g, m, v], [], [(c, F32)] * 3, [], tm=tm, name=name)
    return d, mn, vn


def _add_pair(sel, g, r, *, name):
    _, _, rows, cols = g.shape
    tm = _pick(rows, (256, 128, 64, 32, 16))

    def body(s_ref, g_ref, r_ref, o_ref):
        o_ref[...] = (g_ref[...].astype(F32) + r_ref[...].astype(F32)).astype(BF16)

    return pl.pallas_call(
        body, name=name,
        grid_spec=pltpu.PrefetchScalarGridSpec(
            num_scalar_prefetch=1, grid=(N_CHIPS, rows // tm),
            in_specs=[pl.BlockSpec((None, None, tm, cols), lambda j, i, s: (j, s[0], i, 0)),
                      pl.BlockSpec((None, tm, cols), lambda j, i, s: (j, i, 0))],
            out_specs=pl.BlockSpec((None, tm, cols), lambda j, i, s: (j, i, 0))),
        out_shape=jax.ShapeDtypeStruct((N_CHIPS, rows, cols), BF16),
        compiler_params=_cparams(dimension_semantics=("parallel", "parallel")),
    )(sel, g, r)


def _add_four(sel, p, r, *, name):
    _, rows, cols = p.shape
    tm = _pick(rows, (256, 128, 64, 32, 16))

    def body(s_ref, p_ref, r0, r1, r2, o_ref):
        o_ref[...] = ((p_ref[...].astype(F32) + r0[...].astype(F32)) + r1[...].astype(F32)) + r2[...].astype(F32)

    rspec = lambda k: pl.BlockSpec((None, tm, cols), lambda i, s, k=k: (k, i, 0))
    return pl.pallas_call(
        body, name=name,
        grid_spec=pltpu.PrefetchScalarGridSpec(
            num_scalar_prefetch=1, grid=(rows // tm,),
            in_specs=[pl.BlockSpec((None, tm, cols), lambda i, s: (s[0], i, 0)), rspec(0), rspec(1), rspec(2)],
            out_specs=pl.BlockSpec((None, tm, cols), lambda i, s: (s[1], i, 0))),
        out_shape=jax.ShapeDtypeStruct((2, rows, cols), F32),
        compiler_params=_cparams(dimension_semantics=("parallel",)),
    )(sel, p, r, r, r)


def _sum8(g, *, name):
    _, rows, cols = g.shape
    tm = _pick(rows, (512, 256, 128, 64, 32, 16, 8))

    def body(g_ref, o_ref):
        acc = g_ref[0]
        for k in range(1, 8):
            acc = acc + g_ref[k]
        o_ref[...] = acc

    return pl.pallas_call(
        body, name=name, grid=(rows // tm,),
        in_specs=[pl.BlockSpec((8, tm, cols), lambda i: (0, i, 0))],
        out_specs=pl.BlockSpec((tm, cols), lambda i: (i, 0)),
        out_shape=jax.ShapeDtypeStruct((rows, cols), F32),
        compiler_params=_cparams(dimension_semantics=("parallel",)),
    )(g)


def _place():
    x, y, c = lax.axis_index("x"), lax.axis_index("y"), lax.axis_index("c")
    chips = [(1 - x, y), (x, 1 - y), (1 - x, 1 - y)]
    return x, y, c, chips


def _rcopy(src, dst, send_sems, recv_sems, k, to):
    return pltpu.make_async_remote_copy(src_ref=src, dst_ref=dst, send_sem=send_sems.at[k], recv_sem=recv_sems.at[k],
                                        device_id=to, device_id_type=MESH)


def _gather_chips(packs, *, name):
    n = len(packs)

    def body(*refs):
        srcs, outs, (send_sems, recv_sems) = refs[:n], refs[n:2 * n], refs[2 * n:]
        x, y, c, chips = _place()
        sibling = (x, y, 1 - c)
        me = 2 * x + y
        first, passed = [], []
        for t, (src, out) in enumerate(zip(srcs, outs)):
            for k, (cx, cy) in enumerate(chips):
                cp = _rcopy(src.at[c], out.at[me, c], send_sems, recv_sems, 6 * t + k, (cx, cy, c))
                cp.start()
                first.append(cp)
        for t, out in enumerate(outs):
            for k, (cx, cy) in enumerate(chips):
                blk = out.at[2 * cx + cy, c]
                _rcopy(blk, blk, send_sems, recv_sems, 6 * t + k, (cx, cy, c)).wait_recv()
                fw = _rcopy(blk, blk, send_sems, recv_sems, 6 * t + 3 + k, sibling)
                fw.start()
                passed.append(fw)
        for t, out in enumerate(outs):
            for k, (cx, cy) in enumerate(chips):
                blk = out.at[2 * cx + cy, 1 - c]
                _rcopy(blk, blk, send_sems, recv_sems, 6 * t + 3 + k, sibling).wait_recv()
        for cp in first + passed:
            cp.wait_send()

    return pl.pallas_call(
        body, name=name, in_specs=[ANY] * n, out_specs=[ANY] * n,
        out_shape=[jax.ShapeDtypeStruct((N_CHIPS,) + p.shape, p.dtype) for p in packs],
        scratch_shapes=[pltpu.SemaphoreType.DMA((6 * n,)), pltpu.SemaphoreType.DMA((6 * n,))],
    )(*packs)


def _pass_on(gs, *, name):
    n = len(gs)

    def body(*refs):
        srcs, outs, (send_sems, recv_sems) = refs[:n], refs[n:2 * n], refs[2 * n:]
        x, y, c, chips = _place()
        sibling = (x, y, 1 - c)
        cps = [_rcopy(src.at[2 * cx + cy, c], out.at[2 * cx + cy, c], send_sems, recv_sems, 3 * t + k, sibling)
               for t, (src, out) in enumerate(zip(srcs, outs)) for k, (cx, cy) in enumerate(chips)]
        for cp in cps:
            cp.start()
        for t, out in enumerate(outs):
            for k, (cx, cy) in enumerate(chips):
                b = out.at[2 * cx + cy, 1 - c]
                _rcopy(b, b, send_sems, recv_sems, 3 * t + k, sibling).wait_recv()
        for cp in cps:
            cp.wait_send()

    return pl.pallas_call(
        body, name=name, in_specs=[ANY] * n, out_specs=[ANY] * n,
        out_shape=[jax.ShapeDtypeStruct(g.shape, g.dtype) for g in gs], input_output_aliases={t: t for t in range(n)},
        scratch_shapes=[pltpu.SemaphoreType.DMA((3 * n,)), pltpu.SemaphoreType.DMA((3 * n,))],
    )(*gs)


def _gather_devices(pack, *, name):
    rows, cols = pack.shape

    def body(src, out, send_sems, recv_sems, local_sem):
        x, y, c, chips = _place()
        sibling = (x, y, 1 - c)

        def blk(px, py, pc):
            return out.at[4 * px + 2 * py + pc]

        mine = pltpu.make_async_copy(src, blk(x, y, c), local_sem)
        mine.start()
        first = [_rcopy(src, blk(x, y, c), send_sems, recv_sems, 0, sibling)]
        first += [_rcopy(src, blk(x, y, c), send_sems, recv_sems, 1 + k, (cx, cy, c)) for k, (cx, cy) in enumerate(chips)]
        for cp in first:
            cp.start()
        passed = []
        for k, (cx, cy) in enumerate(chips):
            b = blk(cx, cy, c)
            _rcopy(b, b, send_sems, recv_sems, 1 + k, (cx, cy, c)).wait_recv()
            fw = _rcopy(b, b, send_sems, recv_sems, 4 + k, sibling)
            fw.start()
            passed.append(fw)
        b = blk(x, y, 1 - c)
        _rcopy(b, b, send_sems, recv_sems, 0, sibling).wait_recv()
        for k, (cx, cy) in enumerate(chips):
            b = blk(cx, cy, 1 - c)
            _rcopy(b, b, send_sems, recv_sems, 4 + k, sibling).wait_recv()
        for cp in first + passed:
            cp.wait_send()
        mine.wait()

    return pl.pallas_call(
        body, name=name, in_specs=[ANY], out_specs=ANY,
        out_shape=jax.ShapeDtypeStruct((8, rows, cols), pack.dtype),
        scratch_shapes=[pltpu.SemaphoreType.DMA((7,)), pltpu.SemaphoreType.DMA((7,)), pltpu.SemaphoreType.DMA],
    )(pack)


def _swap_halves(gs, *, name):
    n = len(gs)

    def body(*refs):
        srcs, outs, (send_sems, recv_sems) = refs[:n], refs[n:2 * n], refs[2 * n:]
        x, y, c, _ = _place()
        cps = [_rcopy(src.at[j, 1 - c], out.at[j], send_sems, recv_sems, N_CHIPS * t + j, (x, y, 1 - c))
               for t, (src, out) in enumerate(zip(srcs, outs)) for j in range(N_CHIPS)]
        for cp in cps:
            cp.start()
        for cp in cps:
            cp.wait()

    return pl.pallas_call(
        body, name=name, in_specs=[ANY] * n, out_specs=[ANY] * n,
        out_shape=[jax.ShapeDtypeStruct((N_CHIPS,) + g.shape[2:], g.dtype) for g in gs],
        scratch_shapes=[pltpu.SemaphoreType.DMA((N_CHIPS * n,)), pltpu.SemaphoreType.DMA((N_CHIPS * n,))],
    )(*gs)


def _join_halves(rs, *, name):
    n = len(rs)

    def body(*refs):
        srcs, outs, (send_sems, recv_sems) = refs[:n], refs[n:2 * n], refs[2 * n:]
        x, y, c, _ = _place()
        cps = [_rcopy(src.at[c], out.at[c], send_sems, recv_sems, t, (x, y, 1 - c))
               for t, (src, out) in enumerate(zip(srcs, outs))]
        for cp in cps:
            cp.start()
        for t, out in enumerate(outs):
            b = out.at[1 - c]
            _rcopy(b, b, send_sems, recv_sems, t, (x, y, 1 - c)).wait_recv()
        for cp in cps:
            cp.wait_send()

    return pl.pallas_call(
        body, name=name, in_specs=[ANY] * n, out_specs=[ANY] * n,
        out_shape=[jax.ShapeDtypeStruct(r.shape, r.dtype) for r in rs], input_output_aliases={t: t for t in range(n)},
        scratch_shapes=[pltpu.SemaphoreType.DMA((n,)), pltpu.SemaphoreType.DMA((n,))],
    )(*rs)


def _flat_rows(parts, cols):
    flat = jnp.concatenate([p.reshape(-1) for p in parts])
    n = flat.shape[0]
    rows = -(-n // cols)
    rows = 8 * (-(-rows // 8))
    return jnp.pad(flat, (0, rows * cols - n)).reshape(rows, cols)


def _expand(v, di):
    return jnp.repeat(v, HEAD_DIM).reshape(1, di)


def kernel(x, norm_mix_g, norm_ffn_g, norm_final_g, cv_w_in, cv_b_in, cv_w_dw, cv_b_dw, cv_ln_g, cv_ln_b, cv_w_out, cv_b_out, ssm_w_in, ssm_w_conv, ssm_b_conv, ssm_dt_bias, ssm_a_log, ssm_d, ssm_norm_g, ssm_w_out, ffn_w_up, ffn_w_dw, ffn_b_dw, ffn_w_down, loss_target, m_norm_mix_g, m_norm_ffn_g, m_norm_final_g, m_cv_w_in, m_cv_b_in, m_cv_w_dw, m_cv_b_dw, m_cv_ln_g, m_cv_ln_b, m_cv_w_out, m_cv_b_out, m_ssm_w_in, m_ssm_w_conv, m_ssm_b_conv, m_ssm_dt_bias, m_ssm_a_log, m_ssm_d, m_ssm_norm_g, m_ssm_w_out, m_ffn_w_up, m_ffn_w_dw, m_ffn_b_dw, m_ffn_w_down, v_norm_mix_g, v_norm_ffn_g, v_norm_final_g, v_cv_w_in, v_cv_b_in, v_cv_w_dw, v_cv_b_dw, v_cv_ln_g, v_cv_ln_b, v_cv_w_out, v_cv_b_out, v_ssm_w_in, v_ssm_w_conv, v_ssm_b_conv, v_ssm_dt_bias, v_ssm_a_log, v_ssm_d, v_ssm_norm_g, v_ssm_w_out, v_ffn_w_up, v_ffn_w_dw, v_ffn_b_dw, v_ffn_w_down):
    weights = dict(norm_mix_g=norm_mix_g, norm_ffn_g=norm_ffn_g, norm_final_g=norm_final_g, cv_w_in=cv_w_in, cv_b_in=cv_b_in, cv_w_dw=cv_w_dw, cv_b_dw=cv_b_dw, cv_ln_g=cv_ln_g, cv_ln_b=cv_ln_b, cv_w_out=cv_w_out, cv_b_out=cv_b_out, ssm_w_in=ssm_w_in, ssm_w_conv=ssm_w_conv, ssm_b_conv=ssm_b_conv, ssm_dt_bias=ssm_dt_bias, ssm_a_log=ssm_a_log, ssm_d=ssm_d, ssm_norm_g=ssm_norm_g, ssm_w_out=ssm_w_out, ffn_w_up=ffn_w_up, ffn_w_dw=ffn_w_dw, ffn_b_dw=ffn_b_dw, ffn_w_down=ffn_w_down)
    mom_m = dict(norm_mix_g=m_norm_mix_g, norm_ffn_g=m_norm_ffn_g, norm_final_g=m_norm_final_g, cv_w_in=m_cv_w_in, cv_b_in=m_cv_b_in, cv_w_dw=m_cv_w_dw, cv_b_dw=m_cv_b_dw, cv_ln_g=m_cv_ln_g, cv_ln_b=m_cv_ln_b, cv_w_out=m_cv_w_out, cv_b_out=m_cv_b_out, ssm_w_in=m_ssm_w_in, ssm_w_conv=m_ssm_w_conv, ssm_b_conv=m_ssm_b_conv, ssm_dt_bias=m_ssm_dt_bias, ssm_a_log=m_ssm_a_log, ssm_d=m_ssm_d, ssm_norm_g=m_ssm_norm_g, ssm_w_out=m_ssm_w_out, ffn_w_up=m_ffn_w_up, ffn_w_dw=m_ffn_w_dw, ffn_b_dw=m_ffn_b_dw, ffn_w_down=m_ffn_w_down)
    mom_v = dict(norm_mix_g=v_norm_mix_g, norm_ffn_g=v_norm_ffn_g, norm_final_g=v_norm_final_g, cv_w_in=v_cv_w_in, cv_b_in=v_cv_b_in, cv_w_dw=v_cv_w_dw, cv_b_dw=v_cv_b_dw, cv_ln_g=v_cv_ln_g, cv_ln_b=v_cv_ln_b, cv_w_out=v_cv_w_out, cv_b_out=v_cv_b_out, ssm_w_in=v_ssm_w_in, ssm_w_conv=v_ssm_w_conv, ssm_b_conv=v_ssm_b_conv, ssm_dt_bias=v_ssm_dt_bias, ssm_a_log=v_ssm_a_log, ssm_d=v_ssm_d, ssm_norm_g=v_ssm_norm_g, ssm_w_out=v_ssm_w_out, ffn_w_up=v_ffn_w_up, ffn_w_dw=v_ffn_w_dw, ffn_b_dw=v_ffn_b_dw, ffn_w_down=v_ffn_w_down)
    names = list(weights)

    xt = x[0]
    tgt = loss_target[0]
    t, d = xt.shape
    depth = norm_mix_g.shape[0]
    n_cv, n_ssm = cv_w_in.shape[0], ssm_w_in.shape[0]
    di = ssm_w_out.shape[1] * N_CHIPS
    n_heads = di // HEAD_DIM
    gn = N_GROUPS * D_STATE
    ssm_in = ssm_w_in.shape[2] * N_CHIPS
    chip = 2 * lax.axis_index("x") + lax.axis_index("y")

    cq = ssm_w_in.shape[2]
    cqp = LANES * (-(-cq // LANES))
    by_col = ("cv_w_in", "ssm_w_in", "ffn_w_up")
    big_names = ("cv_w_in", "cv_w_out", "ssm_w_in", "ssm_w_out", "ffn_w_up", "ffn_w_down")

    def layer_tensors(i):
        mixer = [("cv_w_in", i // 2), ("cv_w_out", i // 2)] if i % 2 == 0 else [("ssm_w_in", i // 2), ("ssm_w_out", i // 2)]
        return mixer + [("ffn_w_up", i), ("ffn_w_down", i)]

    def halves(a):
        return a.reshape((2, a.shape[0] // 2) + a.shape[1:])

    order = [key for i in range(depth) for key in layer_tensors(i)]
    shards = []
    for nm, l in order:
        w = weights[nm][l]
        if nm == "ssm_w_in":
            w = jnp.pad(w, ((0, 0), (0, cqp - cq)))
        shards.append(halves(w.astype(BF16)))
    shards = dict(zip(order, lax.optimization_barrier(shards)))
    full = {}

    def arrived(key, g):
        g = lax.dynamic_update_index_in_dim(g, shards[key], chip, 0)
        g = g.reshape((N_CHIPS, 2 * g.shape[2], g.shape[3]))
        full[key] = g if key[0] in by_col else g.reshape(N_CHIPS * g.shape[1], g.shape[2])

    ahead = 2
    for key, g in zip(order[:ahead], _gather_chips([shards[k] for k in order[:ahead]], name="gather_weights")):
        arrived(key, g)

    def mm_fwd(key, a, **kw):
        pos = order.index(key)
        kw["b_chips"] = key[0] in by_col
        if pos + ahead >= len(order):
            return _matmul(a, full[key], **kw)
        nxt = order[pos + ahead]
        out, land = _matmul(a, full[key], carry=("gather", shards[nxt]), **kw)
        arrived(nxt, _pass_on([land], name="pass_on")[0])
        return out

    def ssm_cols(a, lo, hi):
        parts = []
        for jj in range(N_CHIPS):
            s0, s1 = max(lo, jj * cq), min(hi, (jj + 1) * cq)
            if s0 < s1:
                parts.append(a[:, jj * cqp + s0 - jj * cq:jj * cqp + s1 - jj * cq])
        return parts[0] if len(parts) == 1 else jnp.concatenate(parts, axis=1)

    def ssm_cols_back(a):
        return jnp.concatenate([jnp.pad(a[:, jj * cq:(jj + 1) * cq], ((0, 0), (0, cqp - cq))) for jj in range(N_CHIPS)],
                               axis=1)

    small_sharded = ["cv_w_dw", "ssm_w_conv", "ssm_b_conv", "ssm_norm_g", "ffn_w_dw"]
    spack = _flat_rows([weights[nm] for nm in small_sharded], LANES)
    sg = _gather_devices(spack, name="gather_small").reshape(8, -1)[::2]
    o = 0
    for nm in small_sharded:
        shp = weights[nm].shape
        n = weights[nm].size
        full[nm] = jnp.concatenate([sg[j, o:o + n].reshape(shp) for j in range(N_CHIPS)], axis=-1)
        o += n

    row = lambda v: v.reshape(1, -1)

    saved = []
    xc = xt
    for i in range(depth):
        j = i // 2
        s = {"x_in": xc}
        h = _rms_fwd(xc, row(norm_mix_g[i]), name="rms_mix_fwd")
        s["h"] = h
        if i % 2 == 0:
            u = mm_fwd(("cv_w_in", j), h, name="cv_in_fwd")
            v1 = _glu_fwd(u, row(cv_b_in[j]), name="cv_glu_fwd")
            v2 = _dwconv_fwd(v1, full["cv_w_dw"][j], row(cv_b_dw[j]), name="cv_dw_fwd")
            v4 = _ln_silu_fwd(v2, row(cv_ln_g[j]), row(cv_ln_b[j]), name="cv_ln_fwd")
            xc = mm_fwd(("cv_w_out", j), v4, bias=row(cv_b_out[j]), res=xc, name="cv_out_fwd")
            s.update(u=u, v1=v1, v2=v2, v4=v4)
        else:
            zx = mm_fwd(("ssm_w_in", j), h, name="ssm_in_fwd")
            z = ssm_cols(zx, 0, di)
            xbc_pre = ssm_cols(zx, di, 2 * di + 2 * gn)
            dtp_exp = jnp.repeat(ssm_cols(zx, 2 * di + 2 * gn, ssm_in), HEAD_DIM, axis=1)
            conv = _dwconv_fwd(xbc_pre, full["ssm_w_conv"][j], row(full["ssm_b_conv"][j]), name="ssm_dw_fwd")
            bias_exp = _expand(ssm_dt_bias[j], di)
            aneg_exp = _expand(-jnp.exp(ssm_a_log[j]), di)
            d_exp = _expand(ssm_d[j], di)
            xs, bm, cm, dt_exp, cs_exp = _ssm_act(conv, dtp_exp, bias_exp, aneg_exp, di=di, name="ssm_act_fwd")
            cs_rows = cs_exp[:, ::HEAD_DIM].T.reshape(N_GROUPS, n_heads // N_GROUPS, t)
            y_ssd, states = _ssd_fwd(xs, dt_exp, cs_exp, cs_rows, bm, cm, name="ssd_fwd")
            gnrm = _gated_norm_fwd(y_ssd, xs, z, d_exp, row(full["ssm_norm_g"][j]), name="ssm_norm_fwd")
            xc = mm_fwd(("ssm_w_out", j), gnrm, res=xc, name="ssm_out_fwd")
            s.update(z=z, xbc_pre=xbc_pre, dtp_exp=dtp_exp, conv=conv, bias_exp=bias_exp, aneg_exp=aneg_exp,
                     d_exp=d_exp, xs=xs, bm=bm, cm=cm, dt_exp=dt_exp, cs_exp=cs_exp, cs_rows=cs_rows, y_ssd=y_ssd,
                     states=states, gnrm=gnrm)
        s["x_mid"] = xc
        h2 = _rms_fwd(xc, row(norm_ffn_g[i]), name="rms_ffn_fwd")
        u2 = mm_fwd(("ffn_w_up", i), h2, out_chips=True, name="ffn_up_fwd")
        hm = _ffn_mid_fwd(u2, full["ffn_w_dw"][i], row(ffn_b_dw[i]), name="ffn_mid_fwd")
        xc = mm_fwd(("ffn_w_down", i), hm, res=xc, name="ffn_down_fwd")
        s.update(h2=h2, u2=u2, hm=hm)
        saved.append(s)

    dx, dxb, sq, dg_final = _loss_head(xc, row(norm_final_g), tgt, name="loss_head")
    loss_part = 0.5 / d * jnp.sum(sq)
    gr = {nm: [None] * weights[nm].shape[0] for nm in names if nm != "norm_final_g"}
    reduced = {}
    sel_c = jnp.reshape(lax.axis_index("c"), (1,)).astype(jnp.int32)
    sel_j = jnp.stack([chip, lax.axis_index("c")]).astype(jnp.int32)

    def backward_pair(key, act, dout, dw_name, dx_name, dout_chips=False):
        col = key[0] in by_col
        g = _matmul(act, dout, ta=True, b_chips=dout_chips, out_chips=col, out_dtype=BF16, name=dw_name)
        if not col:
            g = g.reshape(N_CHIPS, g.shape[0] // N_CHIPS, g.shape[1])
        g = g.reshape(N_CHIPS, 2, g.shape[1] // 2, g.shape[2])
        (got,) = _swap_halves([g], name="grads_swap")
        pair = _add_pair(sel_c, g, got, name="grads_add2")
        dact, got3 = _matmul(dout, full[key], tb=True, a_chips=dout_chips, b_chips=col, carry=("scatter", pair),
                             name=dx_name)
        half = _add_four(sel_j, pair, got3, name="grads_add4")
        (r,) = _join_halves([half], name="grads_join")
        reduced[key] = r.reshape(2 * r.shape[1], r.shape[2])[:, :weights[key[0]].shape[2]]
        return dact

    for i in reversed(range(depth)):
        j = i // 2
        s = saved[i]
        dhm = backward_pair(("ffn_w_down", i), s["hm"], dxb, "ffn_down_dw", "ffn_down_dx")
        du2b, dw_dw, db_dw = _ffn_mid_bwd(s["u2"], dhm, full["ffn_w_dw"][i], row(ffn_b_dw[i]), name="ffn_mid_bwd")
        gr["ffn_w_dw"][i], gr["ffn_b_dw"][i] = dw_dw, db_dw[0]
        dh2 = backward_pair(("ffn_w_up", i), s["h2"], du2b, "ffn_up_dw", "ffn_up_dx", dout_chips=True)
        dx, dxb, colsum, dg = _rms_bwd(s["x_mid"], row(norm_ffn_g[i]), dh2, dx, name="rms_ffn_bwd")
        gr["norm_ffn_g"][i] = dg[0]
        if i % 2 == 0:
            gr["cv_b_out"][j] = colsum[0]
            dv4 = backward_pair(("cv_w_out", j), s["v4"], dxb, "cv_out_dw", "cv_out_dx")
            dv2, dlg, dlb = _ln_silu_bwd(s["v2"], row(cv_ln_g[j]), row(cv_ln_b[j]), dv4, name="cv_ln_bwd")
            gr["cv_ln_g"][j], gr["cv_ln_b"][j] = dlg[0], dlb[0]
            dv1, dw_dw, db_dw = _dwconv_bwd(s["v1"], dv2, full["cv_w_dw"][j], name="cv_dw_bwd")
            gr["cv_w_dw"][j], gr["cv_b_dw"][j] = dw_dw, db_dw[0]
            du, db_in = _glu_bwd(s["u"], row(cv_b_in[j]), dv1, name="cv_glu_bwd")
            gr["cv_b_in"][j] = db_in[0]
            dh = backward_pair(("cv_w_in", j), s["h"], du, "cv_in_dw", "cv_in_dx")
        else:
            dgn = backward_pair(("ssm_w_out", j), s["gnrm"], dxb, "ssm_out_dw", "ssm_out_dx")
            dy, dz, dng, ddl = _gated_norm_bwd(s["y_ssd"], s["xs"], s["z"], s["d_exp"], row(full["ssm_norm_g"][j]),
                                               dgn, name="ssm_norm_bwd")
            gr["ssm_norm_g"][j] = dng[0]
            gr["ssm_d"][j] = ddl.reshape(n_heads, HEAD_DIM).sum(axis=1)
            dxp, dbm, dcm, ddt_exp, dan = _ssd_bwd(s["xs"], s["dt_exp"], s["cs_exp"], s["cs_rows"], s["bm"], s["cm"],
                                                   dy, s["states"], s["aneg_exp"], name="ssd_bwd")
            gr["ssm_a_log"][j] = dan[0, ::HEAD_DIM] * s["aneg_exp"][0, ::HEAD_DIM]
            dconv, ddtp, dbias = _ssm_act_bwd(s["conv"], s["dtp_exp"], dxp, dy, dbm, dcm, ddt_exp, s["dt_exp"],
                                              s["bias_exp"], s["d_exp"], di=di, name="ssm_act_bwd")
            gr["ssm_dt_bias"][j] = dbias[0, ::HEAD_DIM]
            dxbc, dw_c, db_c = _dwconv_bwd(s["xbc_pre"], dconv, full["ssm_w_conv"][j], dx_dtype=BF16, name="ssm_dw_bwd")
            gr["ssm_w_conv"][j], gr["ssm_b_conv"][j] = dw_c, db_c[0]
            dzx = ssm_cols_back(jnp.concatenate([dz, dxbc, ddtp[:, ::HEAD_DIM].astype(BF16)], axis=1))
            dh = backward_pair(("ssm_w_in", j), s["h"], dzx, "ssm_in_dw", "ssm_in_dx")
        dx, dxb, _, dg = _rms_bwd(s["x_in"], row(norm_mix_g[i]), dh, dx, name="rms_mix_bwd")
        gr["norm_mix_g"][i] = dg[0]

    grads = {nm: jnp.stack([reduced[nm, l] for l in range(weights[nm].shape[0])]) for nm in big_names}

    small = [nm for nm in names if nm not in grads]
    small_parts = []
    for nm in small:
        small_parts.append(dg_final[0] if nm == "norm_final_g" else jnp.stack(gr[nm]))
    gs_pack = _flat_rows(small_parts + [loss_part.reshape(1)], LANES)
    gs = _sum8(_gather_devices(gs_pack, name="gather_small_grads"), name="sum_small_grads").reshape(-1)
    o = 0
    for nm, p in zip(small, small_parts):
        gfull = gs[o:o + p.size].reshape(p.shape)
        o += p.size
        if nm in small_sharded:
            width = weights[nm].shape[-1]
            gfull = lax.dynamic_slice_in_dim(gfull, chip * width, width, axis=gfull.ndim - 1)
        grads[nm] = gfull
    loss = gs[o]

    delta, new_m, new_v = {}, {}, {}
    for nm in big_names:
        shp = weights[nm].shape
        as2d = lambda a: a.reshape(-1, shp[-1])
        dl, mn, vn = _adamw(as2d(weights[nm]), as2d(grads[nm]), as2d(mom_m[nm]), as2d(mom_v[nm]), name="adamw_" + nm)
        delta[nm], new_m[nm], new_v[nm] = dl.reshape(shp), mn.reshape(shp), vn.reshape(shp)
    pk = lambda dct: _flat_rows([dct[nm] for nm in small], LANES)
    dl, mn, vn = _adamw(pk(weights), pk(grads), pk(mom_m), pk(mom_v), name="adamw_small")
    dl, mn, vn = dl.reshape(-1), mn.reshape(-1), vn.reshape(-1)
    o = 0
    for nm in small:
        shp, n = weights[nm].shape, weights[nm].size
        delta[nm], new_m[nm], new_v[nm] = (a[o:o + n].reshape(shp) for a in (dl, mn, vn))
        o += n

    return (loss, dx[None], *[grads[nm] for nm in names], *[delta[nm] for nm in names],
            *[new_m[nm] for nm in names], *[new_v[nm] for nm in names])
```

```python
import functools

import jax
import jax.numpy as jnp
from jax import lax
from jax.experimental import pallas as pl
from jax.experimental.pallas import tpu as pltpu

F32, BF16 = jnp.float32, jnp.bfloat16
MESH = pl.DeviceIdType.MESH
ANY = pl.BlockSpec(memory_space=pl.ANY)

RMS_EPS = 1e-6
LN_EPS = 1e-5
HEAD_DIM = 64
N_GROUPS = 8
D_STATE = 128
ADAM_LR, ADAM_B1, ADAM_B2, ADAM_EPS, ADAM_WD, ADAM_STEP = 0.001, 0.9, 0.999, 1e-08, 0.01, 10

VMEM_LIMIT_BYTES = 56 * 1024 * 1024
LANES = 128
SSD_CHUNK = 128
ADAMW_BLOCK_ELEMS = 512 * 1024
N_CHIPS = 4


def _cparams(**kw):
    return pltpu.CompilerParams(vmem_limit_bytes=VMEM_LIMIT_BYTES, **kw)


def _pick(n, prefs):
    for p in prefs:
        if n % p == 0:
            return p
    return n


def _sigmoid(x):
    return 1.0 / (1.0 + jnp.exp(-x))


def _silu(x):
    return x * _sigmoid(x)


def _dsilu(x):
    s = _sigmoid(x)
    return s * (1.0 + x * (1.0 - s))


def _rowsum(x):
    return jnp.sum(x, axis=0, keepdims=True)


MM_TILES = (2816, 2688, 2048, 1408, 1024, 896, 512, 384, 256, 128)
MM_VMEM_BUDGET = 40 * 1024 * 1024


def _mm_tiles(m, n, k, n_unit, k_unit, out_bytes, has_res):
    best = None
    for tk in [t for t in MM_TILES if k_unit % t == 0]:
        for tm in [t for t in (1024, 512, 256, 128) if m % t == 0] or [m]:
            for tn in [t for t in MM_TILES if n_unit % t == 0]:
                vmem = 2 * 2 * (tm * tk + tk * tn) + 2 * tm * tn * out_bytes
                vmem += tm * tn * 4 if k // tk > 1 else 0
                vmem += 2 * tm * tn * 4 if has_res else 0
                if vmem > MM_VMEM_BUDGET:
                    continue
                traffic = m * k * (n // tn) + k * n * (m // tm)
                if best is None or traffic < best[0]:
                    best = (traffic, tm, tn, tk)
        if best is not None:
            return best[1:]
    raise ValueError((m, n, k))


def _matmul(a, b, *, name, ta=False, tb=False, a_chips=False, b_chips=False, out_chips=False, out_dtype=F32, bias=None,
            res=None, carry=None):
    if a_chips:
        assert tb and b_chips and not ta
        m, k = a.shape[1], N_CHIPS * a.shape[2]
    else:
        m, k = (a.shape[1], a.shape[0]) if ta else a.shape
    if b_chips:
        nq = b.shape[2]
        n = b.shape[1] if tb else N_CHIPS * nq
        assert k == (N_CHIPS * nq if tb else b.shape[1])
    else:
        n = b.shape[0] if tb else b.shape[1]
        assert k == (b.shape[1] if tb else b.shape[0])
        nq = n // N_CHIPS
    has_bias, has_res = bias is not None, res is not None
    tm, tn, tk = _mm_tiles(m, n, k, nq if ((b_chips and not tb) or out_chips) else n, nq if (b_chips and tb) else k,
                           jnp.dtype(out_dtype).itemsize, has_res)
    gm, gn, nk = m // tm, n // tn, k // tk
    nbq = nq // (tk if tb else tn) if (b_chips or out_chips) else 1
    dn = (((0 if ta else 1,), (1 if tb else 0,)), ((), ()))
    carry = carry or []
    nj = len(carry)

    def body(*refs):
        a_ref, b_ref = refs[0], refs[1]
        rest = list(refs[2:])
        bias_ref = rest.pop(0) if has_bias else None
        res_ref = rest.pop(0) if has_res else None
        srcs = [rest.pop(0) for _ in range(nj)]
        o_ref = rest.pop(0)
        dsts = [rest.pop(0) for _ in range(nj)]
        acc_ref = rest.pop(0) if nk > 1 else None
        i, j, kk = pl.program_id(0), pl.program_id(1), pl.program_id(2)

        if carry:
            send_sems, recv_sems = rest
            x, y, c, chips = _place()
            sibling = (x, y, 1 - c)
            cps, landing = [], []
            for jb, ((kind, _), src, dst) in enumerate(zip(carry, srcs, dsts)):
                for q, (cx, cy) in enumerate(chips):
                    sem = 3 * jb + q
                    if kind == "gather":
                        cps.append(_rcopy(src.at[c], dst.at[2 * x + y, c], send_sems, recv_sems, sem, (cx, cy, c)))
                        landing.append((dst.at[2 * cx + cy, c], sem, (cx, cy, c)))
                    elif kind == "pass":
                        cps.append(_rcopy(src.at[2 * cx + cy, c], dst.at[2 * cx + cy, c], send_sems, recv_sems, sem, sibling))
                        landing.append((dst.at[2 * cx + cy, 1 - c], sem, sibling))
                    else:
                        cps.append(_rcopy(src.at[2 * cx + cy], dst.at[q], send_sems, recv_sems, sem, (cx, cy, c)))
                        landing.append((dst.at[q], sem, (cx, cy, c)))

            @pl.when((i == 0) & (j == 0) & (kk == 0))
            def _():
                for cp in cps:
                    cp.start()

        def finish(r):
            if has_bias:
                r = r + bias_ref[...]
            if has_res:
                r = r + res_ref[...]
            o_ref[...] = r.astype(o_ref.dtype)

        part = lax.dot_general(a_ref[...].astype(BF16), b_ref[...].astype(BF16), dn, preferred_element_type=F32)
        if nk == 1:
            finish(part)
        else:
            @pl.when(kk == 0)
            def _():
                acc_ref[...] = part

            @pl.when(kk > 0)
            def _():
                acc_ref[...] += part

            @pl.when(kk == nk - 1)
            def _():
                finish(acc_ref[...])

        if carry:
            @pl.when((i == gm - 1) & (j == gn - 1) & (kk == nk - 1))
            def _():
                for blk, sem, frm in landing:
                    _rcopy(blk, blk, send_sems, recv_sems, sem, frm).wait_recv()
                for cp in cps:
                    cp.wait_send()

    if a_chips:
        a_spec = pl.BlockSpec((None, tm, tk), lambda i, j, kk: (kk // nbq, i, kk % nbq))
    elif ta:
        a_spec = pl.BlockSpec((tk, tm), lambda i, j, kk: (kk, i))
    else:
        a_spec = pl.BlockSpec((tm, tk), lambda i, j, kk: (i, kk))
    if b_chips and tb:
        b_spec = pl.BlockSpec((None, tn, tk), lambda i, j, kk: (kk // nbq, j, kk % nbq))
    elif b_chips:
        b_spec = pl.BlockSpec((None, tk, tn), lambda i, j, kk: (j // nbq, kk, j % nbq))
    elif tb:
        b_spec = pl.BlockSpec((tn, tk), lambda i, j, kk: (j, kk))
    else:
        b_spec = pl.BlockSpec((tk, tn), lambda i, j, kk: (kk, j))
    if out_chips:
        out_spec = pl.BlockSpec((None, tm, tn), lambda i, j, kk: (j // nbq, i, j % nbq))
        out_shape = jax.ShapeDtypeStruct((N_CHIPS, m, nq), out_dtype)
    else:
        out_spec = pl.BlockSpec((tm, tn), lambda i, j, kk: (i, j))
        out_shape = jax.ShapeDtypeStruct((m, n), out_dtype)
    in_specs, args = [a_spec, b_spec], [a, b]
    if has_bias:
        in_specs.append(pl.BlockSpec((1, tn), lambda i, j, kk: (0, j)))
        args.append(bias)
    if has_res:
        in_specs.append(pl.BlockSpec((tm, tn), lambda i, j, kk: (i, j)))
        args.append(res)
    scratch = [pltpu.VMEM((tm, tn), F32)] if nk > 1 else []
    if not carry:
        return pl.pallas_call(
            body, name=name, grid=(gm, gn, nk), in_specs=in_specs, out_specs=out_spec, out_shape=out_shape,
            scratch_shapes=scratch,
            compiler_params=_cparams(dimension_semantics=("parallel", "parallel", "arbitrary")),
        )(*args)
    lands, aliases = [], {}
    for jb, (kind, moved) in enumerate(carry):
        shape = {"gather": (N_CHIPS,) + moved.shape, "pass": moved.shape, "scatter": (3,) + moved.shape[1:]}[kind]
        lands.append(jax.ShapeDtypeStruct(shape, moved.dtype))
        if kind == "pass":
            aliases[len(args) + jb] = 1 + jb
    return pl.pallas_call(
        body, name=name, grid=(gm, gn, nk), in_specs=in_specs + [ANY] * nj, out_specs=[out_spec] + [ANY] * nj,
        out_shape=[out_shape] + lands, input_output_aliases=aliases,
        scratch_shapes=scratch + [pltpu.SemaphoreType.DMA((3 * nj,)), pltpu.SemaphoreType.DMA((3 * nj,))],
        compiler_params=_cparams(dimension_semantics=("arbitrary", "arbitrary", "arbitrary")),
    )(*args, *[moved for _, moved in carry])


def _rowwise(fn, rows, pars, outs, reds, *, tm, name):
    rows = [r if isinstance(r, tuple) else (r, r.shape[1], 0) for r in rows]
    t = rows[0][0].shape[0]
    assert t % tm == 0
    n_in, n_o = len(rows) + len(pars), len(outs)

    def body(*refs):
        i = pl.program_id(0)
        o, d = fn(*[r[...] for r in refs[:n_in]])
        for ref, val in zip(refs[n_in:n_in + n_o], o):
            ref[...] = val.astype(ref.dtype)
        d_refs = refs[n_in + n_o:]

        @pl.when(i == 0)
        def _():
            for ref in d_refs:
                ref[...] = jnp.zeros_like(ref)

        for ref, val in zip(d_refs, d):
            ref[...] += val

    in_specs = [pl.BlockSpec((tm, w), lambda i, b=blk: (i, b)) for _, w, blk in rows]
    in_specs += [pl.BlockSpec((1, p.shape[1]), lambda i: (0, 0)) for p in pars]
    out_specs = [pl.BlockSpec((tm, c), lambda i: (i, 0)) for c, _ in outs]
    out_specs += [pl.BlockSpec((1, c), lambda i: (0, 0)) for c in reds]
    out_shape = [jax.ShapeDtypeStruct((t, c), dt) for c, dt in outs] + [jax.ShapeDtypeStruct((1, c), F32) for c in reds]
    res = pl.pallas_call(
        body, name=name, grid=(t // tm,), in_specs=in_specs, out_specs=out_specs, out_shape=out_shape,
        compiler_params=_cparams(dimension_semantics=("arbitrary",)),
    )(*[r[0] for r in rows], *pars)
    return res[:n_o], res[n_o:]


def _rms_fwd(x, g, *, name):
    def fn(xb, gb):
        r = lax.rsqrt(jnp.mean(xb * xb, axis=-1, keepdims=True) + RMS_EPS)
        return [xb * r * gb], []
    (h,), _ = _rowwise(fn, [x], [g], [(x.shape[1], BF16)], [], tm=256, name=name)
    return h


def _rms_bwd(x, g, dh, dres, *, name):
    def fn(xb, dhb, drb, gb):
        r = lax.rsqrt(jnp.mean(xb * xb, axis=-1, keepdims=True) + RMS_EPS)
        xh = xb * r
        dxh = dhb * gb
        dx = r * (dxh - xh * jnp.mean(dxh * xh, axis=-1, keepdims=True))
        out = drb + dx
        return [out, out], [_rowsum(out), _rowsum(dhb * xh)]
    c = x.shape[1]
    (dx, dxb), (colsum, dg) = _rowwise(fn, [x, dh, dres], [g], [(c, F32), (c, BF16)], [c, c], tm=256, name=name)
    return dx, dxb, colsum, dg


def _loss_head(x, g, tgt, *, name):
    d_model = x.shape[1]

    def fn(xb, tb, gb):
        r = lax.rsqrt(jnp.mean(xb * xb, axis=-1, keepdims=True) + RMS_EPS)
        xh = xb * r
        e = xh * gb - tb
        dy = e * (1.0 / d_model)
        dxh = dy * gb
        dx = r * (dxh - xh * jnp.mean(dxh * xh, axis=-1, keepdims=True))
        return [dx, dx], [_rowsum(e * e), _rowsum(dy * xh)]
    (dx, dxb), (sq, dg) = _rowwise(fn, [x, tgt], [g], [(d_model, F32), (d_model, BF16)], [d_model, d_model], tm=256,
                                   name=name)
    return dx, dxb, sq, dg


def _halo_rows(k):
    return 8 * ((k - 1 + 7) // 8) if k > 1 else 8


def _pad_taps(w):
    k = w.shape[0]
    kp = 8 * ((k + 7) // 8)
    return jnp.pad(w, ((0, kp - k), (0, 0)))


def _conv_chunks(k, tc):
    rc, lw = (32, 256) if k > 9 else (16, 512)
    lanes, l0 = [], 0
    while l0 < tc:
        lanes.append((l0, min(lw, tc - l0)))
        l0 += lw
    return rc, lanes


def _fold8(v):
    acc = v[0:8]
    for q in range(1, v.shape[0] // 8):
        acc = acc + v[8 * q:8 * q + 8]
    return acc


def _taps(src_ref, lead, base, rc, lanes, w_ref, k, sign, acc):
    for s in range(k):
        rows = pl.ds(base + sign * s, rc)
        acc = acc + w_ref[k - 1 - s:k - s, lanes] * src_ref[lead + (rows, lanes)]
    return acc


def _dwconv_fwd(x, w, b, *, name):
    t, c = x.shape
    k = w.shape[0]
    h = _halo_rows(k)
    tm = _pick(t, (256, 128))
    tc = _pick(c, (512, 256, 128) if k > 9 else (1536, 1024, 512, 256, 128))
    wp = _pad_taps(w)
    kp = wp.shape[0]
    rb = tm // h
    rc, lane_chunks = _conv_chunks(k, tc)

    def body(x_ref, p_ref, w_ref, b_ref, o_ref, ext):
        i = pl.program_id(0)
        ext[pl.ds(h, tm), :] = x_ref[...]
        ext[pl.ds(0, h), :] = jnp.where(i > 0, p_ref[...], 0.0)
        for l0, lw in lane_chunks:
            lanes = pl.ds(l0, lw)
            for r0 in range(0, tm, rc):
                acc = jnp.broadcast_to(b_ref[:, lanes], (rc, lw))
                o_ref[pl.ds(r0, rc), lanes] = _taps(ext, (), h + r0, rc, lanes, w_ref, k, -1, acc)

    return pl.pallas_call(
        body, name=name, grid=(t // tm, c // tc),
        in_specs=[pl.BlockSpec((tm, tc), lambda i, j: (i, j)),
                  pl.BlockSpec((h, tc), lambda i, j: (jnp.maximum(i * rb - 1, 0), j)),
                  pl.BlockSpec((kp, tc), lambda i, j: (0, j)),
                  pl.BlockSpec((1, tc), lambda i, j: (0, j))],
        out_specs=pl.BlockSpec((tm, tc), lambda i, j: (i, j)),
        out_shape=jax.ShapeDtypeStruct((t, c), F32),
        scratch_shapes=[pltpu.VMEM((h + tm, tc), F32)],
        compiler_params=_cparams(dimension_semantics=("parallel", "parallel")),
    )(x, x, wp, b)


def _dwconv_bwd(x, dy, w, *, name, dx_dtype=F32):
    t, c = x.shape
    k = w.shape[0]
    h = _halo_rows(k)
    tm = _pick(t, (256, 128))
    tc = _pick(c, (512, 256, 128) if k > 9 else (1536, 1024, 512, 256, 128))
    wp = _pad_taps(w)
    kp = wp.shape[0]
    rb = tm // h
    nt = t // tm
    rc, lane_chunks = _conv_chunks(k, tc)

    def body(x_ref, p_ref, dy_ref, n_ref, w_ref, dx_ref, dw_ref, db_ref, xext, dext):
        i = pl.program_id(1)

        @pl.when(i == 0)
        def _():
            dw_ref[...] = jnp.zeros_like(dw_ref)
            db_ref[...] = jnp.zeros_like(db_ref)

        xext[pl.ds(h, tm), :] = x_ref[...]
        xext[pl.ds(0, h), :] = jnp.where(i > 0, p_ref[...], 0.0)
        dext[pl.ds(0, tm), :] = dy_ref[...]
        dext[pl.ds(tm, h), :] = jnp.where(i < nt - 1, n_ref[...], 0.0)
        for l0, lw in lane_chunks:
            lanes = pl.ds(l0, lw)
            for r0 in range(0, tm, rc):
                acc = _taps(dext, (), r0, rc, lanes, w_ref, k, +1, jnp.zeros((rc, lw), F32))
                dx_ref[pl.ds(r0, rc), lanes] = acc.astype(dx_ref.dtype)
            for s in range(k):
                a8 = jnp.zeros((8, lw), F32)
                for r0 in range(0, tm, rc):
                    a8 = a8 + _fold8(xext[pl.ds(h + r0 - s, rc), lanes] * dext[pl.ds(r0, rc), lanes])
                dw_ref[k - 1 - s:k - s, lanes] += _rowsum(a8)
            b8 = jnp.zeros((8, lw), F32)
            for r0 in range(0, tm, rc):
                b8 = b8 + _fold8(dext[pl.ds(r0, rc), lanes])
            db_ref[:, lanes] += _rowsum(b8)

    dx, dw, db = pl.pallas_call(
        body, name=name, grid=(c // tc, nt),
        in_specs=[pl.BlockSpec((tm, tc), lambda j, i: (i, j)),
                  pl.BlockSpec((h, tc), lambda j, i: (jnp.maximum(i * rb - 1, 0), j)),
                  pl.BlockSpec((tm, tc), lambda j, i: (i, j)),
                  pl.BlockSpec((h, tc), lambda j, i: (jnp.minimum((i + 1) * rb, nt * rb - 1), j)),
                  pl.BlockSpec((kp, tc), lambda j, i: (0, j))],
        out_specs=[pl.BlockSpec((tm, tc), lambda j, i: (i, j)),
                   pl.BlockSpec((kp, tc), lambda j, i: (0, j)),
                   pl.BlockSpec((1, tc), lambda j, i: (0, j))],
        out_shape=[jax.ShapeDtypeStruct((t, c), dx_dtype), jax.ShapeDtypeStruct((kp, c), F32),
                   jax.ShapeDtypeStruct((1, c), F32)],
        scratch_shapes=[pltpu.VMEM((h + tm, tc), F32), pltpu.VMEM((tm + h, tc), F32)],
        compiler_params=_cparams(dimension_semantics=("parallel", "arbitrary")),
    )(x, x, dy, dy, wp)
    return dx, dw[:k], db


def _glu_fwd(u, b_in, *, name):
    d = u.shape[1] // 2

    def fn(ua, ug, ba, bg):
        return [(ua + ba) * _sigmoid(ug + bg)], []
    (v,), _ = _rowwise(fn, [(u, d, 0), (u, d, 1)], [b_in[:, :d], b_in[:, d:]], [(d, F32)], [], tm=256, name=name)
    return v


def _glu_bwd(u, b_in, dv, *, name):
    d = u.shape[1] // 2

    def fn(ua, ug, dvb, ba, bg):
        a = ua + ba
        s = _sigmoid(ug + bg)
        du = jnp.concatenate([dvb * s, dvb * a * s * (1.0 - s)], axis=1)
        return [du], [_rowsum(du)]
    (du,), (db,) = _rowwise(fn, [(u, d, 0), (u, d, 1), dv], [b_in[:, :d], b_in[:, d:]], [(2 * d, BF16)], [2 * d],
                            tm=256, name=name)
    return du, db


def _ln_silu_fwd(v, g, b, *, name):
    def fn(vb, gb, bb):
        mu = jnp.mean(vb, axis=-1, keepdims=True)
        xc = vb - mu
        rstd = lax.rsqrt(jnp.mean(xc * xc, axis=-1, keepdims=True) + LN_EPS)
        return [_silu(xc * rstd * gb + bb)], []
    (o,), _ = _rowwise(fn, [v], [g, b], [(v.shape[1], BF16)], [], tm=256, name=name)
    return o


def _ln_silu_bwd(v, g, b, do, *, name):
    def fn(vb, dob, gb, bb):
        mu = jnp.mean(vb, axis=-1, keepdims=True)
        xc = vb - mu
        rstd = lax.rsqrt(jnp.mean(xc * xc, axis=-1, keepdims=True) + LN_EPS)
        xh = xc * rstd
        dy = dob * _dsilu(xh * gb + bb)
        dxh = dy * gb
        dv = rstd * (dxh - jnp.mean(dxh, axis=-1, keepdims=True) - xh * jnp.mean(dxh * xh, axis=-1, keepdims=True))
        return [dv], [_rowsum(dy * xh), _rowsum(dy)]
    c = v.shape[1]
    (dv,), (dg, db) = _rowwise(fn, [v, do], [g, b], [(c, F32)], [c, c], tm=256, name=name)
    return dv, dg, db


def _ffn_mid_setup(u2c, w, b):
    _, t, nq = u2c.shape
    f = 2 * nq
    k = w.shape[0]
    h = _halo_rows(k)
    tm = _pick(t, (256, 128))
    tc = _pick(nq, (1408, 1024, 512, 256, 128))
    rc, lane_chunks = _conv_chunks(k, tc)
    return dict(t=t, nq=nq, f=f, k=k, h=h, tm=tm, tc=tc, npq=nq // tc, rb=tm // h, nt=t // tm, rc=rc,
                lane_chunks=lane_chunks, u4=u2c.reshape(2, 2, t, nq), wg=_pad_taps(w[:, :f]), wv=_pad_taps(w[:, f:]),
                bg=b[:, :f], bv=b[:, f:])


def _ffn_mid_fwd(u2c, w, b, *, name):
    p = _ffn_mid_setup(u2c, w, b)
    t, f, k, h, tm, tc, npq, rb, rc = (p[n] for n in ("t", "f", "k", "h", "tm", "tc", "npq", "rb", "rc"))
    kp = p["wg"].shape[0]

    def body(u_ref, p_ref, wg_ref, wv_ref, bg_ref, bv_ref, o_ref, ext):
        i = pl.program_id(2)
        for kind in range(2):
            ext[kind, pl.ds(h, tm), :] = u_ref[kind]
            ext[kind, pl.ds(0, h), :] = jnp.where(i > 0, p_ref[kind], 0.0)
        for l0, lw in p["lane_chunks"]:
            lanes = pl.ds(l0, lw)
            for r0 in range(0, tm, rc):
                g = _taps(ext, (0,), h + r0, rc, lanes, wg_ref, k, -1, jnp.broadcast_to(bg_ref[:, lanes], (rc, lw)))
                v = _taps(ext, (1,), h + r0, rc, lanes, wv_ref, k, -1, jnp.broadcast_to(bv_ref[:, lanes], (rc, lw)))
                o_ref[pl.ds(r0, rc), lanes] = (_silu(g) * v).astype(BF16)

    col = lambda q, jj, i: (0, q * npq + jj)
    return pl.pallas_call(
        body, name=name, grid=(2, npq, t // tm),
        in_specs=[pl.BlockSpec((2, None, tm, tc), lambda q, jj, i: (0, q, i, jj)),
                  pl.BlockSpec((2, None, h, tc), lambda q, jj, i: (0, q, jnp.maximum(i * rb - 1, 0), jj)),
                  pl.BlockSpec((kp, tc), col), pl.BlockSpec((kp, tc), col),
                  pl.BlockSpec((1, tc), col), pl.BlockSpec((1, tc), col)],
        out_specs=pl.BlockSpec((tm, tc), lambda q, jj, i: (i, q * npq + jj)),
        out_shape=jax.ShapeDtypeStruct((t, f), BF16),
        scratch_shapes=[pltpu.VMEM((2, h + tm, tc), F32)],
        compiler_params=_cparams(dimension_semantics=("parallel", "parallel", "parallel")),
    )(p["u4"], p["u4"], p["wg"], p["wv"], p["bg"], p["bv"])


def _ffn_mid_bwd(u2c, dhm, w, b, *, name):
    p = _ffn_mid_setup(u2c, w, b)
    t, nq, f, k, h, tm, tc, npq, rb, nt, rc = (p[n] for n in ("t", "nq", "f", "k", "h", "tm", "tc", "npq", "rb", "nt", "rc"))
    kp = p["wg"].shape[0]
    chunks1 = [(r0, rc) for r0 in range(0, tm, rc)] + [(tm, h)]

    def body(u_ref, p_ref, n_ref, dh_ref, nd_ref, wg_ref, wv_ref, bg_ref, bv_ref,
             du_ref, dwg_ref, dwv_ref, dbg_ref, dbv_ref, uext, dsc):
        i = pl.program_id(2)
        w_refs, dw_refs, db_refs = (wg_ref, wv_ref), (dwg_ref, dwv_ref), (dbg_ref, dbv_ref)

        @pl.when(i == 0)
        def _():
            for ref in dw_refs + db_refs:
                ref[...] = jnp.zeros_like(ref)

        for kind in range(2):
            uext[kind, pl.ds(0, h), :] = jnp.where(i > 0, p_ref[kind], 0.0)
            uext[kind, pl.ds(h, tm), :] = u_ref[kind]
            uext[kind, pl.ds(h + tm, h), :] = jnp.where(i < nt - 1, n_ref[kind], 0.0)
        for l0, lw in p["lane_chunks"]:
            lanes = pl.ds(l0, lw)
            for r0, rr in chunks1:
                g = _taps(uext, (0,), h + r0, rr, lanes, wg_ref, k, -1, jnp.broadcast_to(bg_ref[:, lanes], (rr, lw)))
                v = _taps(uext, (1,), h + r0, rr, lanes, wv_ref, k, -1, jnp.broadcast_to(bv_ref[:, lanes], (rr, lw)))
                dh = dh_ref[pl.ds(r0, rr), lanes] if r0 < tm else jnp.where(i < nt - 1, nd_ref[:, lanes], 0.0)
                sg = _sigmoid(g)
                dsc[0, pl.ds(r0, rr), lanes] = dh * v * (sg * (1.0 + g * (1.0 - sg)))
                dsc[1, pl.ds(r0, rr), lanes] = dh * (g * sg)
            for kind in range(2):
                for r0 in range(0, tm, rc):
                    acc = _taps(dsc, (kind,), r0, rc, lanes, w_refs[kind], k, +1, jnp.zeros((rc, lw), F32))
                    du_ref[kind, pl.ds(r0, rc), lanes] = acc.astype(BF16)
                for s in range(k):
                    a8 = jnp.zeros((8, lw), F32)
                    for r0 in range(0, tm, rc):
                        a8 = a8 + _fold8(uext[kind, pl.ds(h + r0 - s, rc), lanes] * dsc[kind, pl.ds(r0, rc), lanes])
                    dw_refs[kind][k - 1 - s:k - s, lanes] += _rowsum(a8)
                b8 = jnp.zeros((8, lw), F32)
                for r0 in range(0, tm, rc):
                    b8 = b8 + _fold8(dsc[kind, pl.ds(r0, rc), lanes])
                db_refs[kind][:, lanes] += _rowsum(b8)

    col = lambda q, jj, i: (0, q * npq + jj)
    nxt = lambda i: jnp.minimum((i + 1) * rb, nt * rb - 1)
    du, dwg, dwv, dbg, dbv = pl.pallas_call(
        body, name=name, grid=(2, npq, nt),
        in_specs=[pl.BlockSpec((2, None, tm, tc), lambda q, jj, i: (0, q, i, jj)),
                  pl.BlockSpec((2, None, h, tc), lambda q, jj, i: (0, q, jnp.maximum(i * rb - 1, 0), jj)),
                  pl.BlockSpec((2, None, h, tc), lambda q, jj, i: (0, q, nxt(i), jj)),
                  pl.BlockSpec((tm, tc), lambda q, jj, i: (i, q * npq + jj)),
                  pl.BlockSpec((h, tc), lambda q, jj, i: (nxt(i), q * npq + jj)),
                  pl.BlockSpec((kp, tc), col), pl.BlockSpec((kp, tc), col),
                  pl.BlockSpec((1, tc), col), pl.BlockSpec((1, tc), col)],
        out_specs=[pl.BlockSpec((2, None, tm, tc), lambda q, jj, i: (0, q, i, jj)),
                   pl.BlockSpec((kp, tc), col), pl.BlockSpec((kp, tc), col),
                   pl.BlockSpec((1, tc), col), pl.BlockSpec((1, tc), col)],
        out_shape=[jax.ShapeDtypeStruct((2, 2, t, nq), BF16), jax.ShapeDtypeStruct((kp, f), F32),
                   jax.ShapeDtypeStruct((kp, f), F32), jax.ShapeDtypeStruct((1, f), F32), jax.ShapeDtypeStruct((1, f), F32)],
        scratch_shapes=[pltpu.VMEM((2, h + tm + h, tc), F32), pltpu.VMEM((2, tm + h, tc), F32)],
        compiler_params=_cparams(dimension_semantics=("parallel", "parallel", "arbitrary")),
    )(p["u4"], p["u4"], p["u4"], dhm, dhm, p["wg"], p["wv"], p["bg"], p["bv"])
    return (du.reshape(N_CHIPS, t, nq), jnp.concatenate([dwg[:k], dwv[:k]], axis=1), jnp.concatenate([dbg, dbv], axis=1))


def _ssm_act(conv, dtp_exp, bias_exp, aneg_exp, *, di, name):
    q = SSD_CHUNK
    gn = (conv.shape[1] - di) // 2

    def fn(cb, dtb, bb, ab):
        act = _silu(cb)
        dt = dtb + bb
        dt = jnp.maximum(dt, 0.0) + jnp.log(1.0 + jnp.exp(-jnp.abs(dt)))
        a = dt * ab
        tri = (lax.broadcasted_iota(jnp.int32, (q, q), 0) >= lax.broadcasted_iota(jnp.int32, (q, q), 1)).astype(F32)
        cs = _dot3(tri, a, (((1,), (0,)), ((), ())), 1)
        return [act[:, :di], act[:, di:di + gn], act[:, di + gn:], dt, cs], []
    outs, _ = _rowwise(fn, [conv, dtp_exp], [bias_exp, aneg_exp],
                       [(di, F32), (gn, F32), (gn, F32), (di, F32), (di, F32)], [], tm=q, name=name)
    return outs


def _head_masks(q):
    lane = lax.broadcasted_iota(jnp.int32, (q, LANES), 1)
    return lane < HEAD_DIM


def _pair_cols(cs, lo):
    sw = pltpu.roll(cs, HEAD_DIM, 1)
    return jnp.where(lo, cs, sw), jnp.where(lo, sw, cs)


def _ssd_fwd(xs, dt_exp, cs_exp, cs_rows, bm, cm, *, name):
    t, di = xs.shape
    q = SSD_CHUNK
    hg = di // N_GROUPS
    npair = hg // LANES
    nheads = hg // HEAD_DIM
    nc = t // q
    n = D_STATE
    cpb = _pick(nc, (4, 2, 1))
    qb = cpb * q

    def body(xs_ref, dt_ref, cs_ref, csr_ref, b_ref, c_ref, y_ref, st_ref, s_scr):
        ci = pl.program_id(1)

        @pl.when(ci == 0)
        def _():
            s_scr[...] = jnp.zeros_like(s_scr)

        tri = lax.broadcasted_iota(jnp.int32, (q, q), 0) >= lax.broadcasted_iota(jnp.int32, (q, q), 1)
        lo = _head_masks(q)
        for cc in range(cpb):
            rows = pl.ds(cc * q, q)
            bb = b_ref[rows, :].astype(BF16)
            cb_ = c_ref[rows, :].astype(BF16)
            cbm = lax.dot_general(cb_, bb, (((1,), (1,)), ((), ())), preferred_element_type=F32)
            csr = csr_ref[0, :, pl.ds(cc * q, q)]
            for p in range(npair):
                sl = pl.ds(p * LANES, LANES)
                x = xs_ref[rows, sl] * dt_ref[rows, sl]
                cs = cs_ref[rows, sl]
                col0, col1 = _pair_cols(cs, lo)
                l0 = jnp.where(tri, jnp.exp(jnp.minimum(col0 - csr[2 * p:2 * p + 1, :], 0.0)), 0.0)
                l1 = jnp.where(tri, jnp.exp(jnp.minimum(col1 - csr[2 * p + 1:2 * p + 2, :], 0.0)), 0.0)
                xb = x.astype(BF16)
                yd = jnp.where(lo, jnp.dot((cbm * l0).astype(BF16), xb, preferred_element_type=F32),
                               jnp.dot((cbm * l1).astype(BF16), xb, preferred_element_type=F32))
                s = s_scr[p]
                st_ref[0, cc, p] = s
                yo = jnp.exp(cs) * jnp.dot(cb_, s.astype(BF16), preferred_element_type=F32)
                y_ref[rows, sl] = yd + yo
                cs_end = cs[q - 1:q, :]
                xd = (x * jnp.exp(cs_end - cs)).astype(BF16)
                s_scr[p] = jnp.exp(cs_end) * s + lax.dot_general(bb, xd, (((0,), (0,)), ((), ())),
                                                                 preferred_element_type=F32)

    return pl.pallas_call(
        body, name=name, grid=(N_GROUPS, nc // cpb),
        in_specs=[pl.BlockSpec((qb, hg), lambda g, c: (c, g)),
                  pl.BlockSpec((qb, hg), lambda g, c: (c, g)),
                  pl.BlockSpec((qb, hg), lambda g, c: (c, g)),
                  pl.BlockSpec((1, nheads, qb), lambda g, c: (g, 0, c)),
                  pl.BlockSpec((qb, n), lambda g, c: (c, g)),
                  pl.BlockSpec((qb, n), lambda g, c: (c, g))],
        out_specs=[pl.BlockSpec((qb, hg), lambda g, c: (c, g)),
                   pl.BlockSpec((1, cpb, npair, n, LANES), lambda g, c: (g, c, 0, 0, 0))],
        out_shape=[jax.ShapeDtypeStruct((t, di), F32),
                   jax.ShapeDtypeStruct((N_GROUPS, nc, npair, n, LANES), F32)],
        scratch_shapes=[pltpu.VMEM((npair, n, LANES), F32)],
        compiler_params=_cparams(dimension_semantics=("parallel", "arbitrary")),
    )(xs, dt_exp, cs_exp, cs_rows, bm, cm)


def _dot3(a, b, dims, split):
    rest = (a, b)[split].astype(F32)
    other = (a, b)[1 - split].astype(BF16)
    acc = None
    for _ in range(3):
        part = rest.astype(BF16)
        rest = rest - part.astype(F32)
        d = (lax.dot_general(part, other, dims, preferred_element_type=F32) if split == 0
             else lax.dot_general(other, part, dims, preferred_element_type=F32))
        acc = d if acc is None else acc + d
    return acc


def _ssd_bwd(xs, dt_exp, cs_exp, cs_rows, bm, cm, dy, states, aneg_exp, *, name):
    t, di = xs.shape
    q = SSD_CHUNK
    hg = di // N_GROUPS
    npair = hg // LANES
    nheads = hg // HEAD_DIM
    nc = t // q
    n = D_STATE
    nt_dims = (((1,), (1,)), ((), ()))
    tn_dims = (((0,), (0,)), ((), ()))

    mm_dims = (((1,), (0,)), ((), ()))

    def body(xs_ref, dt_ref, cs_ref, csr_ref, b_ref, c_ref, dy_ref, st_ref, an_ref,
             dxp_ref, db_ref, dc_ref, ddt_ref, dan_ref, r_scr):
        ci = pl.program_id(1)

        @pl.when(ci == 0)
        def _():
            r_scr[...] = jnp.zeros_like(r_scr)
            dan_ref[...] = jnp.zeros_like(dan_ref)

        bb = b_ref[...].astype(BF16)
        cb_ = c_ref[...].astype(BF16)
        cbm = lax.dot_general(cb_, bb, nt_dims, preferred_element_type=F32)
        row = lax.broadcasted_iota(jnp.int32, (q, q), 0)
        col = lax.broadcasted_iota(jnp.int32, (q, q), 1)
        tri = row >= col
        triu = (row <= col).astype(F32)
        trisl = (row > col).astype(F32)
        ones2 = (lax.broadcasted_iota(jnp.int32, (LANES, LANES), 0) // HEAD_DIM
                 == lax.broadcasted_iota(jnp.int32, (LANES, LANES), 1) // HEAD_DIM).astype(BF16)
        onesq = jnp.ones((q, LANES), BF16)
        last = lax.broadcasted_iota(jnp.int32, (q, LANES), 0) == q - 1
        lo = _head_masks(q)
        csr = csr_ref[0]
        dcb = jnp.zeros((q, q), F32)
        dc_acc = jnp.zeros((q, n), F32)
        db_acc = jnp.zeros((q, n), F32)
        for p in range(npair):
            sl = pl.ds(p * LANES, LANES)
            xsv = xs_ref[:, sl]
            dtv = dt_ref[:, sl]
            x = xsv * dtv
            cs = cs_ref[:, sl]
            dyv = dy_ref[:, sl]
            col0, col1 = _pair_cols(cs, lo)
            l0 = jnp.where(tri, jnp.exp(jnp.minimum(col0 - csr[2 * p:2 * p + 1, :], 0.0)), 0.0)
            l1 = jnp.where(tri, jnp.exp(jnp.minimum(col1 - csr[2 * p + 1:2 * p + 2, :], 0.0)), 0.0)
            xb = x.astype(BF16)
            dyb = dyv.astype(BF16)
            g0 = lax.dot_general(jnp.where(lo, dyv, 0.0).astype(BF16), xb, nt_dims, preferred_element_type=F32)
            g1 = lax.dot_general(jnp.where(lo, 0.0, dyv).astype(BF16), xb, nt_dims, preferred_element_type=F32)
            gl0, gl1 = g0 * l0, g1 * l1
            dcb = dcb + gl0 + gl1
            w0, w1 = cbm * gl0, cbm * gl1
            dxd = jnp.where(lo,
                            lax.dot_general((cbm * l0).astype(BF16), dyb, tn_dims, preferred_element_type=F32),
                            lax.dot_general((cbm * l1).astype(BF16), dyb, tn_dims, preferred_element_type=F32))
            e = jnp.exp(cs)
            cs_end = cs[q - 1:q, :]
            dte = jnp.exp(cs_end - cs)
            dend = jnp.exp(cs_end)
            sf = st_ref[0, 0, p]
            sb = sf.astype(BF16)
            r = r_scr[p]
            rb = r.astype(BF16)
            dyeb = (dyv * e).astype(BF16)
            dc_acc = dc_acc + lax.dot_general(dyeb, sb, nt_dims, preferred_element_type=F32)
            dxo = dte * jnp.dot(bb, rb, preferred_element_type=F32)
            db_acc = db_acc + lax.dot_general((x * dte).astype(BF16), rb, nt_dims, preferred_element_type=F32)
            r_scr[p] = dend * r + lax.dot_general(cb_, dyeb, tn_dims, preferred_element_type=F32)
            dx = dxd + dxo
            dxp_ref[:, sl] = dx
            yoff = e * jnp.dot(cb_, sb, preferred_element_type=F32)
            w0b, w1b = w0.astype(BF16), w1.astype(BF16)
            rw = jnp.where(lo, jnp.sum(w0b.astype(F32), axis=1, keepdims=True), jnp.sum(w1b.astype(F32), axis=1, keepdims=True))
            cw = jnp.where(lo, lax.dot_general(w0b, onesq, tn_dims, preferred_element_type=F32),
                           lax.dot_general(w1b, onesq, tn_dims, preferred_element_type=F32))
            through = jnp.where(last, dend * _rowsum(r * sf), 0.0)
            suf = jnp.dot((dyv * yoff + through).astype(BF16), ones2, preferred_element_type=F32) + rw - cw
            pre = jnp.dot((dxo * x).astype(BF16), ones2, preferred_element_type=F32)
            da = _dot3(triu, suf, mm_dims, 1) + _dot3(trisl, pre, mm_dims, 1)
            qs = jnp.dot((dx * xsv).astype(BF16), ones2, preferred_element_type=F32)
            ddt_ref[:, sl] = da * an_ref[:, sl] + qs
            dan_ref[:, sl] += _rowsum(da * dtv)
        dcbb = dcb.astype(BF16)
        dc_ref[...] = dc_acc + jnp.dot(dcbb, bb, preferred_element_type=F32)
        db_ref[...] = db_acc + lax.dot_general(dcbb, cb_, tn_dims, preferred_element_type=F32)

    rev = lambda g, c: (nc - 1 - c, g)
    return pl.pallas_call(
        body, name=name, grid=(N_GROUPS, nc),
        in_specs=[pl.BlockSpec((q, hg), rev), pl.BlockSpec((q, hg), rev), pl.BlockSpec((q, hg), rev),
                  pl.BlockSpec((1, nheads, q), lambda g, c: (g, 0, nc - 1 - c)),
                  pl.BlockSpec((q, n), rev), pl.BlockSpec((q, n), rev),
                  pl.BlockSpec((q, hg), rev),
                  pl.BlockSpec((1, 1, npair, n, LANES), lambda g, c: (g, nc - 1 - c, 0, 0, 0)),
                  pl.BlockSpec((1, hg), lambda g, c: (0, g))],
        out_specs=[pl.BlockSpec((q, hg), rev), pl.BlockSpec((q, n), rev), pl.BlockSpec((q, n), rev),
                   pl.BlockSpec((q, hg), rev), pl.BlockSpec((1, hg), lambda g, c: (0, g))],
        out_shape=[jax.ShapeDtypeStruct((t, di), F32), jax.ShapeDtypeStruct((t, N_GROUPS * n), F32),
                   jax.ShapeDtypeStruct((t, N_GROUPS * n), F32), jax.ShapeDtypeStruct((t, di), F32),
                   jax.ShapeDtypeStruct((1, di), F32)],
        scratch_shapes=[pltpu.VMEM((npair, n, LANES), F32)],
        compiler_params=_cparams(dimension_semantics=("parallel", "arbitrary")),
    )(xs, dt_exp, cs_exp, cs_rows, bm, cm, dy, states, aneg_exp)


def _group_stats(w, gw):
    return [lax.rsqrt(jnp.mean(w[:, i * gw:(i + 1) * gw] ** 2, axis=-1, keepdims=True) + RMS_EPS)
            for i in range(N_GROUPS)]


def _gated_norm_fwd(y_ssd, xs, z, d_exp, g, *, name):
    di = xs.shape[1]
    gw = di // N_GROUPS

    def fn(yb, xb, zb, db, gb):
        w = (yb + db * xb) * _silu(zb)
        rs = _group_stats(w, gw)
        return [jnp.concatenate([w[:, i * gw:(i + 1) * gw] * rs[i] for i in range(N_GROUPS)], axis=1) * gb], []
    (o,), _ = _rowwise(fn, [y_ssd, xs, (z, di, 0)], [d_exp, g], [(di, BF16)], [], tm=128, name=name)
    return o


def _gated_norm_bwd(y_ssd, xs, z, d_exp, g, do, *, name):
    di = xs.shape[1]
    gw = di // N_GROUPS

    def fn(yb, xb, zb, dob, db, gb):
        yy = yb + db * xb
        sz = _silu(zb)
        w = yy * sz
        rs = _group_stats(w, gw)
        dwh = dob * gb
        wh_parts, dw_parts = [], []
        for i in range(N_GROUPS):
            sl = slice(i * gw, (i + 1) * gw)
            wh = w[:, sl] * rs[i]
            wh_parts.append(wh)
            dw_parts.append(rs[i] * (dwh[:, sl] - wh * jnp.mean(dwh[:, sl] * wh, axis=-1, keepdims=True)))
        wh = jnp.concatenate(wh_parts, axis=1)
        dw = jnp.concatenate(dw_parts, axis=1)
        dy = dw * sz
        dz = dw * yy * _dsilu(zb)
        return [dy, dz], [_rowsum(dob * wh), _rowsum(dy * xb)]
    (dy, dz), (dg, dd) = _rowwise(fn, [y_ssd, xs, (z, di, 0), do], [d_exp, g], [(di, F32), (di, BF16)], [di, di],
                                  tm=128, name=name)
    return dy, dz, dg, dd


def _ssm_act_bwd(conv, dtp_exp, dxp, dy, dbm, dcm, ddt_exp, dt_exp, bias_exp, d_exp, *, di, name):
    gn = dbm.shape[1]

    def fn(cb, dtb, dxpb, dyb, dbb, dcb, ddtb, dteb, bb, db):
        dxs = dxpb * dteb + dyb * db
        dact = jnp.concatenate([dxs, dbb, dcb], axis=1)
        dconv = dact * _dsilu(cb)
        ddtp = ddtb * _sigmoid(dtb + bb)
        return [dconv, ddtp], [_rowsum(ddtp)]
    (dconv, ddtp), (dbias,) = _rowwise(fn, [conv, dtp_exp, dxp, dy, dbm, dcm, ddt_exp, dt_exp], [bias_exp, d_exp],
                                       [(di + 2 * gn, F32), (di, F32)], [di], tm=64, name=name)
    return dconv, ddtp, dbias


def _adamw(w, g, m, v, *, name):
    r, c = w.shape
    c1 = 1.0 / (1.0 - ADAM_B1 ** ADAM_STEP)
    c2 = 1.0 / (1.0 - ADAM_B2 ** ADAM_STEP)

    def fn(wb, gb, mb, vb):
        mn = ADAM_B1 * mb + (1.0 - ADAM_B1) * gb
        vn = ADAM_B2 * vb + (1.0 - ADAM_B2) * (gb * gb)
        delta = -ADAM_LR * ((mn * c1) / (jnp.sqrt(vn * c2) + ADAM_EPS) + ADAM_WD * wb)
        return [delta, mn, vn], []
    cap = max(8, ADAMW_BLOCK_ELEMS // c)
    tm = _pick(r, [p for p in (512, 256, 128, 64, 32, 16, 8) if p <= cap])
    (d, mn, vn), _ = _rowwise(fn, [w, g, m, v], [], [(c, F32)] * 3, [], tm=tm, name=name)
    return d, mn, vn


def _add_pair(sel, g, r, *, name):
    _, _, rows, cols = g.shape
    tm = _pick(rows, (256, 128, 64, 32, 16))

    def body(s_ref, g_ref, r_ref, o_ref):
        o_ref[...] = (g_ref[...].astype(F32) + r_ref[...].astype(F32)).astype(BF16)

    return pl.pallas_call(
        body, name=name,
        grid_spec=pltpu.PrefetchScalarGridSpec(
            num_scalar_prefetch=1, grid=(N_CHIPS, rows // tm),
            in_specs=[pl.BlockSpec((None, None, tm, cols), lambda j, i, s: (j, s[0], i, 0)),
                      pl.BlockSpec((None, tm, cols), lambda j, i, s: (j, i, 0))],
            out_specs=pl.BlockSpec((None, tm, cols), lambda j, i, s: (j, i, 0))),
        out_shape=jax.ShapeDtypeStruct((N_CHIPS, rows, cols), BF16),
        compiler_params=_cparams(dimension_semantics=("parallel", "parallel")),
    )(sel, g, r)


def _add_four(sel, p, r, *, name):
    _, rows, cols = p.shape
    tm = _pick(rows, (256, 128, 64, 32, 16))

    def body(s_ref, p_ref, r0, r1, r2, o_ref):
        o_ref[...] = ((p_ref[...].astype(F32) + r0[...].astype(F32)) + r1[...].astype(F32)) + r2[...].astype(F32)

    rspec = lambda k: pl.BlockSpec((None, tm, cols), lambda i, s, k=k: (k, i, 0))
    return pl.pallas_call(
        body, name=name,
        grid_spec=pltpu.PrefetchScalarGridSpec(
            num_scalar_prefetch=1, grid=(rows // tm,),
            in_specs=[pl.BlockSpec((None, tm, cols), lambda i, s: (s[0], i, 0)), rspec(0), rspec(1), rspec(2)],
            out_specs=pl.BlockSpec((None, tm, cols), lambda i, s: (s[1], i, 0))),
        out_shape=jax.ShapeDtypeStruct((2, rows, cols), F32),
        compiler_params=_cparams(dimension_semantics=("parallel",)),
    )(sel, p, r, r, r)


def _sum8(g, *, name):
    _, rows, cols = g.shape
    tm = _pick(rows, (512, 256, 128, 64, 32, 16, 8))

    def body(g_ref, o_ref):
        acc = g_ref[0]
        for k in range(1, 8):
            acc = acc + g_ref[k]
        o_ref[...] = acc

    return pl.pallas_call(
        body, name=name, grid=(rows // tm,),
        in_specs=[pl.BlockSpec((8, tm, cols), lambda i: (0, i, 0))],
        out_specs=pl.BlockSpec((tm, cols), lambda i: (i, 0)),
        out_shape=jax.ShapeDtypeStruct((rows, cols), F32),
        compiler_params=_cparams(dimension_semantics=("parallel",)),
    )(g)


def _place():
    x, y, c = lax.axis_index("x"), lax.axis_index("y"), lax.axis_index("c")
    chips = [(1 - x, y), (x, 1 - y), (1 - x, 1 - y)]
    return x, y, c, chips


def _rcopy(src, dst, send_sems, recv_sems, k, to):
    return pltpu.make_async_remote_copy(src_ref=src, dst_ref=dst, send_sem=send_sems.at[k], recv_sem=recv_sems.at[k],
                                        device_id=to, device_id_type=MESH)


def _gather_chips(packs, *, name):
    n = len(packs)

    def body(*refs):
        srcs, outs, (send_sems, recv_sems) = refs[:n], refs[n:2 * n], refs[2 * n:]
        x, y, c, chips = _place()
        sibling = (x, y, 1 - c)
        me = 2 * x + y
        first, passed = [], []
        for t, (src, out) in enumerate(zip(srcs, outs)):
            for k, (cx, cy) in enumerate(chips):
                cp = _rcopy(src.at[c], out.at[me, c], send_sems, recv_sems, 6 * t + k, (cx, cy, c))
                cp.start()
                first.append(cp)
        for t, out in enumerate(outs):
            for k, (cx, cy) in enumerate(chips):
                blk = out.at[2 * cx + cy, c]
                _rcopy(blk, blk, send_sems, recv_sems, 6 * t + k, (cx, cy, c)).wait_recv()
                fw = _rcopy(blk, blk, send_sems, recv_sems, 6 * t + 3 + k, sibling)
                fw.start()
                passed.append(fw)
        for t, out in enumerate(outs):
            for k, (cx, cy) in enumerate(chips):
                blk = out.at[2 * cx + cy, 1 - c]
                _rcopy(blk, blk, send_sems, recv_sems, 6 * t + 3 + k, sibling).wait_recv()
        for cp in first + passed:
            cp.wait_send()

    return pl.pallas_call(
        body, name=name, in_specs=[ANY] * n, out_specs=[ANY] * n,
        out_shape=[jax.ShapeDtypeStruct((N_CHIPS,) + p.shape, p.dtype) for p in packs],
        scratch_shapes=[pltpu.SemaphoreType.DMA((6 * n,)), pltpu.SemaphoreType.DMA((6 * n,))],
    )(*packs)


def _gather_devices(pack, *, name):
    rows, cols = pack.shape

    def body(src, out, send_sems, recv_sems, local_sem):
        x, y, c, chips = _place()
        sibling = (x, y, 1 - c)

        def blk(px, py, pc):
            return out.at[4 * px + 2 * py + pc]

        mine = pltpu.make_async_copy(src, blk(x, y, c), local_sem)
        mine.start()
        first = [_rcopy(src, blk(x, y, c), send_sems, recv_sems, 0, sibling)]
        first += [_rcopy(src, blk(x, y, c), send_sems, recv_sems, 1 + k, (cx, cy, c)) for k, (cx, cy) in enumerate(chips)]
        for cp in first:
            cp.start()
        passed = []
        for k, (cx, cy) in enumerate(chips):
            b = blk(cx, cy, c)
            _rcopy(b, b, send_sems, recv_sems, 1 + k, (cx, cy, c)).wait_recv()
            fw = _rcopy(b, b, send_sems, recv_sems, 4 + k, sibling)
            fw.start()
            passed.append(fw)
        b = blk(x, y, 1 - c)
        _rcopy(b, b, send_sems, recv_sems, 0, sibling).wait_recv()
        for k, (cx, cy) in enumerate(chips):
            b = blk(cx, cy, 1 - c)
            _rcopy(b, b, send_sems, recv_sems, 4 + k, sibling).wait_recv()
        for cp in first + passed:
            cp.wait_send()
        mine.wait()

    return pl.pallas_call(
        body, name=name, in_specs=[ANY], out_specs=ANY,
        out_shape=jax.ShapeDtypeStruct((8, rows, cols), pack.dtype),
        scratch_shapes=[pltpu.SemaphoreType.DMA((7,)), pltpu.SemaphoreType.DMA((7,)), pltpu.SemaphoreType.DMA],
    )(pack)


def _swap_halves(gs, *, name):
    n = len(gs)

    def body(*refs):
        srcs, outs, (send_sems, recv_sems) = refs[:n], refs[n:2 * n], refs[2 * n:]
        x, y, c, _ = _place()
        cps = [_rcopy(src.at[j, 1 - c], out.at[j], send_sems, recv_sems, N_CHIPS * t + j, (x, y, 1 - c))
               for t, (src, out) in enumerate(zip(srcs, outs)) for j in range(N_CHIPS)]
        for cp in cps:
            cp.start()
        for cp in cps:
            cp.wait()

    return pl.pallas_call(
        body, name=name, in_specs=[ANY] * n, out_specs=[ANY] * n,
        out_shape=[jax.ShapeDtypeStruct((N_CHIPS,) + g.shape[2:], g.dtype) for g in gs],
        scratch_shapes=[pltpu.SemaphoreType.DMA((N_CHIPS * n,)), pltpu.SemaphoreType.DMA((N_CHIPS * n,))],
    )(*gs)


def _join_halves(rs, *, name):
    n = len(rs)

    def body(*refs):
        srcs, outs, (send_sems, recv_sems) = refs[:n], refs[n:2 * n], refs[2 * n:]
        x, y, c, _ = _place()
        cps = [_rcopy(src.at[c], out.at[c], send_sems, recv_sems, t, (x, y, 1 - c))
               for t, (src, out) in enumerate(zip(srcs, outs))]
        for cp in cps:
            cp.start()
        for t, out in enumerate(outs):
            b = out.at[1 - c]
            _rcopy(b, b, send_sems, recv_sems, t, (x, y, 1 - c)).wait_recv()
        for cp in cps:
            cp.wait_send()

    return pl.pallas_call(
        body, name=name, in_specs=[ANY] * n, out_specs=[ANY] * n,
        out_shape=[jax.ShapeDtypeStruct(r.shape, r.dtype) for r in rs], input_output_aliases={t: t for t in range(n)},
        scratch_shapes=[pltpu.SemaphoreType.DMA((n,)), pltpu.SemaphoreType.DMA((n,))],
    )(*rs)


def _flat_rows(parts, cols):
    flat = jnp.concatenate([p.reshape(-1) for p in parts])
    n = flat.shape[0]
    rows = -(-n // cols)
    unit = 256 if rows > 256 else 8
    rows = unit * (-(-rows // unit))
    return jnp.pad(flat, (0, rows * cols - n)).reshape(rows, cols)


def _expand(v, di):
    return jnp.repeat(v, HEAD_DIM).reshape(1, di)


def kernel(x, norm_mix_g, norm_ffn_g, norm_final_g, cv_w_in, cv_b_in, cv_w_dw, cv_b_dw, cv_ln_g, cv_ln_b, cv_w_out, cv_b_out, ssm_w_in, ssm_w_conv, ssm_b_conv, ssm_dt_bias, ssm_a_log, ssm_d, ssm_norm_g, ssm_w_out, ffn_w_up, ffn_w_dw, ffn_b_dw, ffn_w_down, loss_target, m_norm_mix_g, m_norm_ffn_g, m_norm_final_g, m_cv_w_in, m_cv_b_in, m_cv_w_dw, m_cv_b_dw, m_cv_ln_g, m_cv_ln_b, m_cv_w_out, m_cv_b_out, m_ssm_w_in, m_ssm_w_conv, m_ssm_b_conv, m_ssm_dt_bias, m_ssm_a_log, m_ssm_d, m_ssm_norm_g, m_ssm_w_out, m_ffn_w_up, m_ffn_w_dw, m_ffn_b_dw, m_ffn_w_down, v_norm_mix_g, v_norm_ffn_g, v_norm_final_g, v_cv_w_in, v_cv_b_in, v_cv_w_dw, v_cv_b_dw, v_cv_ln_g, v_cv_ln_b, v_cv_w_out, v_cv_b_out, v_ssm_w_in, v_ssm_w_conv, v_ssm_b_conv, v_ssm_dt_bias, v_ssm_a_log, v_ssm_d, v_ssm_norm_g, v_ssm_w_out, v_ffn_w_up, v_ffn_w_dw, v_ffn_b_dw, v_ffn_w_down):
    weights = dict(norm_mix_g=norm_mix_g, norm_ffn_g=norm_ffn_g, norm_final_g=norm_final_g, cv_w_in=cv_w_in, cv_b_in=cv_b_in, cv_w_dw=cv_w_dw, cv_b_dw=cv_b_dw, cv_ln_g=cv_ln_g, cv_ln_b=cv_ln_b, cv_w_out=cv_w_out, cv_b_out=cv_b_out, ssm_w_in=ssm_w_in, ssm_w_conv=ssm_w_conv, ssm_b_conv=ssm_b_conv, ssm_dt_bias=ssm_dt_bias, ssm_a_log=ssm_a_log, ssm_d=ssm_d, ssm_norm_g=ssm_norm_g, ssm_w_out=ssm_w_out, ffn_w_up=ffn_w_up, ffn_w_dw=ffn_w_dw, ffn_b_dw=ffn_b_dw, ffn_w_down=ffn_w_down)
    mom_m = dict(norm_mix_g=m_norm_mix_g, norm_ffn_g=m_norm_ffn_g, norm_final_g=m_norm_final_g, cv_w_in=m_cv_w_in, cv_b_in=m_cv_b_in, cv_w_dw=m_cv_w_dw, cv_b_dw=m_cv_b_dw, cv_ln_g=m_cv_ln_g, cv_ln_b=m_cv_ln_b, cv_w_out=m_cv_w_out, cv_b_out=m_cv_b_out, ssm_w_in=m_ssm_w_in, ssm_w_conv=m_ssm_w_conv, ssm_b_conv=m_ssm_b_conv, ssm_dt_bias=m_ssm_dt_bias, ssm_a_log=m_ssm_a_log, ssm_d=m_ssm_d, ssm_norm_g=m_ssm_norm_g, ssm_w_out=m_ssm_w_out, ffn_w_up=m_ffn_w_up, ffn_w_dw=m_ffn_w_dw, ffn_b_dw=m_ffn_b_dw, ffn_w_down=m_ffn_w_down)
    mom_v = dict(norm_mix_g=v_norm_mix_g, norm_ffn_g=v_norm_ffn_g, norm_final_g=v_norm_final_g, cv_w_in=v_cv_w_in, cv_b_in=v_cv_b_in, cv_w_dw=v_cv_w_dw, cv_b_dw=v_cv_b_dw, cv_ln_g=v_cv_ln_g, cv_ln_b=v_cv_ln_b, cv_w_out=v_cv_w_out, cv_b_out=v_cv_b_out, ssm_w_in=v_ssm_w_in, ssm_w_conv=v_ssm_w_conv, ssm_b_conv=v_ssm_b_conv, ssm_dt_bias=v_ssm_dt_bias, ssm_a_log=v_ssm_a_log, ssm_d=v_ssm_d, ssm_norm_g=v_ssm_norm_g, ssm_w_out=v_ssm_w_out, ffn_w_up=v_ffn_w_up, ffn_w_dw=v_ffn_w_dw, ffn_b_dw=v_ffn_b_dw, ffn_w_down=v_ffn_w_down)
    names = list(weights)

    xt = x[0]
    tgt = loss_target[0]
    t, d = xt.shape
    depth = norm_mix_g.shape[0]
    n_cv, n_ssm = cv_w_in.shape[0], ssm_w_in.shape[0]
    di = ssm_w_out.shape[1] * N_CHIPS
    n_heads = di // HEAD_DIM
    gn = N_GROUPS * D_STATE
    ssm_in = ssm_w_in.shape[2] * N_CHIPS
    chip = 2 * lax.axis_index("x") + lax.axis_index("y")

    cq = ssm_w_in.shape[2]
    cqp = LANES * (-(-cq // LANES))
    by_col = ("cv_w_in", "ssm_w_in", "ffn_w_up")
    big_names = ("cv_w_in", "cv_w_out", "ssm_w_in", "ssm_w_out", "ffn_w_up", "ffn_w_down")

    def layer_tensors(i):
        mixer = [("cv_w_in", i // 2), ("cv_w_out", i // 2)] if i % 2 == 0 else [("ssm_w_in", i // 2), ("ssm_w_out", i // 2)]
        return mixer + [("ffn_w_up", i), ("ffn_w_down", i)]

    def halves(a):
        return a.reshape((2, a.shape[0] // 2) + a.shape[1:])

    order = [key for i in range(depth) for key in layer_tensors(i)]
    shards = []
    for nm, l in order:
        w = weights[nm][l]
        if nm == "ssm_w_in":
            w = jnp.pad(w, ((0, 0), (0, cqp - cq)))
        shards.append(halves(w.astype(BF16)))
    shards = dict(zip(order, lax.optimization_barrier(shards)))
    full = {}

    def arrived(key, g):
        g = lax.dynamic_update_index_in_dim(g, shards[key], chip, 0)
        g = g.reshape((N_CHIPS, 2 * g.shape[2], g.shape[3]))
        full[key] = g if key[0] in by_col else g.reshape(N_CHIPS * g.shape[1], g.shape[2])

    for key, g in zip(order[:2], _gather_chips([shards[k] for k in order[:2]], name="gather_weights")):
        arrived(key, g)
    landed = {}

    def mm_fwd(key, a, **kw):
        pos = order.index(key)
        kw["b_chips"] = key[0] in by_col
        jobs = []
        if pos + 1 in landed:
            jobs.append(("pass", landed.pop(pos + 1)))
        if pos + 2 < len(order):
            jobs.append(("gather", shards[order[pos + 2]]))
        if not jobs:
            return _matmul(a, full[key], **kw)
        out, *got = _matmul(a, full[key], carry=jobs, **kw)
        for (kind, _), g in zip(jobs, got):
            if kind == "pass":
                arrived(order[pos + 1], g)
            else:
                landed[pos + 2] = g
        return out

    def ssm_cols(a, lo, hi):
        parts = []
        for jj in range(N_CHIPS):
            s0, s1 = max(lo, jj * cq), min(hi, (jj + 1) * cq)
            if s0 < s1:
                parts.append(a[:, jj * cqp + s0 - jj * cq:jj * cqp + s1 - jj * cq])
        return parts[0] if len(parts) == 1 else jnp.concatenate(parts, axis=1)

    def ssm_cols_back(a):
        return jnp.concatenate([jnp.pad(a[:, jj * cq:(jj + 1) * cq], ((0, 0), (0, cqp - cq))) for jj in range(N_CHIPS)],
                               axis=1)

    small_sharded = ["cv_w_dw", "ssm_w_conv", "ssm_b_conv", "ssm_norm_g", "ffn_w_dw"]
    spack = _flat_rows([weights[nm] for nm in small_sharded], LANES)
    sg = _gather_devices(spack, name="gather_small").reshape(8, -1)[::2]
    o = 0
    for nm in small_sharded:
        shp = weights[nm].shape
        n = weights[nm].size
        full[nm] = jnp.concatenate([sg[j, o:o + n].reshape(shp) for j in range(N_CHIPS)], axis=-1)
        o += n

    row = lambda v: v.reshape(1, -1)

    saved = []
    xc = xt
    for i in range(depth):
        j = i // 2
        s = {"x_in": xc}
        h = _rms_fwd(xc, row(norm_mix_g[i]), name="rms_mix_fwd")
        s["h"] = h
        if i % 2 == 0:
            u = mm_fwd(("cv_w_in", j), h, name="cv_in_fwd")
            v1 = _glu_fwd(u, row(cv_b_in[j]), name="cv_glu_fwd")
            v2 = _dwconv_fwd(v1, full["cv_w_dw"][j], row(cv_b_dw[j]), name="cv_dw_fwd")
            v4 = _ln_silu_fwd(v2, row(cv_ln_g[j]), row(cv_ln_b[j]), name="cv_ln_fwd")
            xc = mm_fwd(("cv_w_out", j), v4, bias=row(cv_b_out[j]), res=xc, name="cv_out_fwd")
            s.update(u=u, v1=v1, v2=v2, v4=v4)
        else:
            zx = mm_fwd(("ssm_w_in", j), h, name="ssm_in_fwd")
            z = ssm_cols(zx, 0, di)
            xbc_pre = ssm_cols(zx, di, 2 * di + 2 * gn)
            dtp_exp = jnp.repeat(ssm_cols(zx, 2 * di + 2 * gn, ssm_in), HEAD_DIM, axis=1)
            conv = _dwconv_fwd(xbc_pre, full["ssm_w_conv"][j], row(full["ssm_b_conv"][j]), name="ssm_dw_fwd")
            bias_exp = _expand(ssm_dt_bias[j], di)
            aneg_exp = _expand(-jnp.exp(ssm_a_log[j]), di)
            d_exp = _expand(ssm_d[j], di)
            xs, bm, cm, dt_exp, cs_exp = _ssm_act(conv, dtp_exp, bias_exp, aneg_exp, di=di, name="ssm_act_fwd")
            cs_rows = cs_exp[:, ::HEAD_DIM].T.reshape(N_GROUPS, n_heads // N_GROUPS, t)
            y_ssd, states = _ssd_fwd(xs, dt_exp, cs_exp, cs_rows, bm, cm, name="ssd_fwd")
            gnrm = _gated_norm_fwd(y_ssd, xs, z, d_exp, row(full["ssm_norm_g"][j]), name="ssm_norm_fwd")
            xc = mm_fwd(("ssm_w_out", j), gnrm, res=xc, name="ssm_out_fwd")
            s.update(z=z, xbc_pre=xbc_pre, dtp_exp=dtp_exp, conv=conv, bias_exp=bias_exp, aneg_exp=aneg_exp,
                     d_exp=d_exp, xs=xs, bm=bm, cm=cm, dt_exp=dt_exp, cs_exp=cs_exp, cs_rows=cs_rows, y_ssd=y_ssd,
                     states=states, gnrm=gnrm)
        s["x_mid"] = xc
        h2 = _rms_fwd(xc, row(norm_ffn_g[i]), name="rms_ffn_fwd")
        u2 = mm_fwd(("ffn_w_up", i), h2, out_chips=True, name="ffn_up_fwd")
        hm = _ffn_mid_fwd(u2, full["ffn_w_dw"][i], row(ffn_b_dw[i]), name="ffn_mid_fwd")
        xc = mm_fwd(("ffn_w_down", i), hm, res=xc, name="ffn_down_fwd")
        s.update(h2=h2, u2=u2, hm=hm)
        saved.append(s)

    dx, dxb, sq, dg_final = _loss_head(xc, row(norm_final_g), tgt, name="loss_head")
    loss_part = 0.5 / d * jnp.sum(sq)
    gr = {nm: [None] * weights[nm].shape[0] for nm in names if nm != "norm_final_g"}
    reduced = {}
    sel_c = jnp.reshape(lax.axis_index("c"), (1,)).astype(jnp.int32)
    sel_j = jnp.stack([chip, lax.axis_index("c")]).astype(jnp.int32)

    def backward_pair(key, act, dout, dw_name, dx_name, dout_chips=False):
        col = key[0] in by_col
        g = _matmul(act, dout, ta=True, b_chips=dout_chips, out_chips=col, out_dtype=BF16, name=dw_name)
        if not col:
            g = g.reshape(N_CHIPS, g.shape[0] // N_CHIPS, g.shape[1])
        g = g.reshape(N_CHIPS, 2, g.shape[1] // 2, g.shape[2])
        (got,) = _swap_halves([g], name="grads_swap")
        pair = _add_pair(sel_c, g, got, name="grads_add2")
        dact, got3 = _matmul(dout, full[key], tb=True, a_chips=dout_chips, b_chips=col, carry=[("scatter", pair)],
                             name=dx_name)
        half = _add_four(sel_j, pair, got3, name="grads_add4")
        (r,) = _join_halves([half], name="grads_join")
        reduced[key] = r.reshape(2 * r.shape[1], r.shape[2])[:, :weights[key[0]].shape[2]]
        return dact

    for i in reversed(range(depth)):
        j = i // 2
        s = saved[i]
        dhm = backward_pair(("ffn_w_down", i), s["hm"], dxb, "ffn_down_dw", "ffn_down_dx")
        du2b, dw_dw, db_dw = _ffn_mid_bwd(s["u2"], dhm, full["ffn_w_dw"][i], row(ffn_b_dw[i]), name="ffn_mid_bwd")
        gr["ffn_w_dw"][i], gr["ffn_b_dw"][i] = dw_dw, db_dw[0]
        dh2 = backward_pair(("ffn_w_up", i), s["h2"], du2b, "ffn_up_dw", "ffn_up_dx", dout_chips=True)
        dx, dxb, colsum, dg = _rms_bwd(s["x_mid"], row(norm_ffn_g[i]), dh2, dx, name="rms_ffn_bwd")
        gr["norm_ffn_g"][i] = dg[0]
        if i % 2 == 0:
            gr["cv_b_out"][j] = colsum[0]
            dv4 = backward_pair(("cv_w_out", j), s["v4"], dxb, "cv_out_dw", "cv_out_dx")
            dv2, dlg, dlb = _ln_silu_bwd(s["v2"], row(cv_ln_g[j]), row(cv_ln_b[j]), dv4, name="cv_ln_bwd")
            gr["cv_ln_g"][j], gr["cv_ln_b"][j] = dlg[0], dlb[0]
            dv1, dw_dw, db_dw = _dwconv_bwd(s["v1"], dv2, full["cv_w_dw"][j], name="cv_dw_bwd")
            gr["cv_w_dw"][j], gr["cv_b_dw"][j] = dw_dw, db_dw[0]
            du, db_in = _glu_bwd(s["u"], row(cv_b_in[j]), dv1, name="cv_glu_bwd")
            gr["cv_b_in"][j] = db_in[0]
            dh = backward_pair(("cv_w_in", j), s["h"], du, "cv_in_dw", "cv_in_dx")
        else:
            dgn = backward_pair(("ssm_w_out", j), s["gnrm"], dxb, "ssm_out_dw", "ssm_out_dx")
            dy, dz, dng, ddl = _gated_norm_bwd(s["y_ssd"], s["xs"], s["z"], s["d_exp"], row(full["ssm_norm_g"][j]),
                                               dgn, name="ssm_norm_bwd")
            gr["ssm_norm_g"][j] = dng[0]
            gr["ssm_d"][j] = ddl.reshape(n_heads, HEAD_DIM).sum(axis=1)
            dxp, dbm, dcm, ddt_exp, dan = _ssd_bwd(s["xs"], s["dt_exp"], s["cs_exp"], s["cs_rows"], s["bm"], s["cm"],
                                                   dy, s["states"], s["aneg_exp"], name="ssd_bwd")
            gr["ssm_a_log"][j] = dan[0, ::HEAD_DIM] * s["aneg_exp"][0, ::HEAD_DIM]
            dconv, ddtp, dbias = _ssm_act_bwd(s["conv"], s["dtp_exp"], dxp, dy, dbm, dcm, ddt_exp, s["dt_exp"],
                                              s["bias_exp"], s["d_exp"], di=di, name="ssm_act_bwd")
            gr["ssm_dt_bias"][j] = dbias[0, ::HEAD_DIM]
            dxbc, dw_c, db_c = _dwconv_bwd(s["xbc_pre"], dconv, full["ssm_w_conv"][j], dx_dtype=BF16, name="ssm_dw_bwd")
            gr["ssm_w_conv"][j], gr["ssm_b_conv"][j] = dw_c, db_c[0]
            dzx = ssm_cols_back(jnp.concatenate([dz, dxbc, ddtp[:, ::HEAD_DIM].astype(BF16)], axis=1))
            dh = backward_pair(("ssm_w_in", j), s["h"], dzx, "ssm_in_dw", "ssm_in_dx")
        dx, dxb, _, dg = _rms_bwd(s["x_in"], row(norm_mix_g[i]), dh, dx, name="rms_mix_bwd")
        gr["norm_mix_g"][i] = dg[0]

    grads = {nm: jnp.stack([reduced[nm, l] for l in range(weights[nm].shape[0])]) for nm in big_names}

    small = [nm for nm in names if nm not in grads]
    small_parts = []
    for nm in small:
        small_parts.append(dg_final[0] if nm == "norm_final_g" else jnp.stack(gr[nm]))
    gs_pack = _flat_rows(small_parts + [loss_part.reshape(1)], LANES)
    gs = _sum8(_gather_devices(gs_pack, name="gather_small_grads"), name="sum_small_grads").reshape(-1)
    o = 0
    for nm, p in zip(small, small_parts):
        gfull = gs[o:o + p.size].reshape(p.shape)
        o += p.size
        if nm in small_sharded:
            width = weights[nm].shape[-1]
            gfull = lax.dynamic_slice_in_dim(gfull, chip * width, width, axis=gfull.ndim - 1)
        grads[nm] = gfull
    loss = gs[o]

    delta, new_m, new_v = {}, {}, {}
    for nm in big_names:
        shp = weights[nm].shape
        as2d = lambda a: a.reshape(-1, shp[-1])
        dl, mn, vn = _adamw(as2d(weights[nm]), as2d(grads[nm]), as2d(mom_m[nm]), as2d(mom_v[nm]), name="adamw_" + nm)
        delta[nm], new_m[nm], new_v[nm] = dl.reshape(shp), mn.reshape(shp), vn.reshape(shp)
    pk = lambda dct: _flat_rows([dct[nm] for nm in small], LANES)
    dl, mn, vn = _adamw(pk(weights), pk(grads), pk(mom_m), pk(mom_v), name="adamw_small")
    dl, mn, vn = dl.reshape(-1), mn.reshape(-1), vn.reshape(-1)
    o = 0
    for nm in small:
        shp, n = weights[nm].shape, weights[nm].size
        delta[nm], new_m[nm], new_v[nm] = (a[o:o + n].reshape(shp) for a in (dl, mn, vn))
        o += n

    return (loss, dx[None], *[grads[nm] for nm in names], *[delta[nm] for nm in names],
            *[new_m[nm] for nm in names], *[new_v[nm] for nm in names])
```

```python
import math

import jax
import jax.numpy as jnp
from jax import lax
from jax.experimental import pallas as pl
from jax.experimental.pallas import tpu as pltpu

F32, BF16 = jnp.float32, jnp.bfloat16
MESH = pl.DeviceIdType.MESH
ANY = pl.BlockSpec(memory_space=pl.ANY)

RMS_EPS = 1e-6
LN_EPS = 1e-5
HEAD_DIM = 64
N_GROUPS = 8
D_STATE = 128
ADAM_LR, ADAM_B1, ADAM_B2, ADAM_EPS, ADAM_WD, ADAM_STEP = 0.001, 0.9, 0.999, 1e-08, 0.01, 10

VMEM_LIMIT_BYTES = 56 * 1024 * 1024
LANES = 128
SSD_CHUNK = 128
ADAMW_BLOCK_ELEMS = 512 * 1024
N_CHIPS = 4


def _cparams(**kw):
    return pltpu.CompilerParams(vmem_limit_bytes=VMEM_LIMIT_BYTES, **kw)


def _pick(n, prefs):
    for p in prefs:
        if n % p == 0:
            return p
    return n


def _sigmoid(x):
    return 1.0 / (1.0 + jnp.exp(-x))


def _silu(x):
    return x * _sigmoid(x)


def _dsilu(x):
    s = _sigmoid(x)
    return s * (1.0 + x * (1.0 - s))


def _rowsum(x):
    return jnp.sum(x, axis=0, keepdims=True)


MM_TILES = (2816, 2688, 2048, 1408, 1024, 896, 512, 384, 256, 128)
MM_VMEM_BUDGET = 40 * 1024 * 1024


def _mm_tiles(m, n, k, n_unit, k_unit, out_bytes, has_res):
    best = None
    for tk in [t for t in MM_TILES if k_unit % t == 0]:
        for tm in [t for t in (1024, 512, 256, 128) if m % t == 0] or [m]:
            for tn in [t for t in MM_TILES if n_unit % t == 0]:
                vmem = 2 * 2 * (tm * tk + tk * tn) + 2 * tm * tn * out_bytes
                vmem += tm * tn * 4 if k // tk > 1 else 0
                vmem += 2 * tm * tn * 4 if has_res else 0
                if vmem > MM_VMEM_BUDGET:
                    continue
                traffic = m * k * (n // tn) + k * n * (m // tm)
                if best is None or traffic < best[0]:
                    best = (traffic, tm, tn, tk)
        if best is not None:
            return best[1:]
    raise ValueError((m, n, k))


def _matmul(a, b, *, name, ta=False, tb=False, a_chips=False, b_chips=False, out_chips=False, out_dtype=F32, bias=None,
            res=None, carry=None):
    if a_chips:
        assert tb and b_chips and not ta
        m, k = a.shape[1], N_CHIPS * a.shape[2]
    else:
        m, k = (a.shape[1], a.shape[0]) if ta else a.shape
    if b_chips:
        nq = b.shape[2]
        n = b.shape[1] if tb else N_CHIPS * nq
        assert k == (N_CHIPS * nq if tb else b.shape[1])
    else:
        n = b.shape[0] if tb else b.shape[1]
        assert k == (b.shape[1] if tb else b.shape[0])
        nq = n // N_CHIPS
    has_bias, has_res = bias is not None, res is not None
    tm, tn, tk = _mm_tiles(m, n, k, nq if ((b_chips and not tb) or out_chips) else n, nq if (b_chips and tb) else k,
                           jnp.dtype(out_dtype).itemsize, has_res)
    gm, gn, nk = m // tm, n // tn, k // tk
    nbq = nq // (tk if tb else tn) if (b_chips or out_chips) else 1
    dn = (((0 if ta else 1,), (1 if tb else 0,)), ((), ()))
    carry = carry or []
    nj = len(carry)

    def body(*refs):
        a_ref, b_ref = refs[0], refs[1]
        rest = list(refs[2:])
        bias_ref = rest.pop(0) if has_bias else None
        res_ref = rest.pop(0) if has_res else None
        srcs = [rest.pop(0) for _ in range(nj)]
        o_ref = rest.pop(0)
        dsts = [rest.pop(0) for _ in range(nj)]
        acc_ref = rest.pop(0) if nk > 1 else None
        i, j, kk = pl.program_id(0), pl.program_id(1), pl.program_id(2)

        if carry:
            send_sems, recv_sems = rest
            x, y, c, chips = _place()
            sibling = (x, y, 1 - c)
            cps, landing = [], []
            for jb, ((kind, _), src, dst) in enumerate(zip(carry, srcs, dsts)):
                for q, (cx, cy) in enumerate(chips):
                    sem = 3 * jb + q
                    if kind == "gather":
                        cps.append(_rcopy(src.at[c], dst.at[2 * x + y, c], send_sems, recv_sems, sem, (cx, cy, c)))
                        landing.append((dst.at[2 * cx + cy, c], sem, (cx, cy, c)))
                    elif kind == "pass":
                        cps.append(_rcopy(src.at[2 * cx + cy, c], dst.at[2 * cx + cy, c], send_sems, recv_sems, sem, sibling))
                        landing.append((dst.at[2 * cx + cy, 1 - c], sem, sibling))
                    else:
                        cps.append(_rcopy(src.at[2 * cx + cy], dst.at[q], send_sems, recv_sems, sem, (cx, cy, c)))
                        landing.append((dst.at[q], sem, (cx, cy, c)))

            @pl.when((i == 0) & (j == 0) & (kk == 0))
            def _():
                for cp in cps:
                    cp.start()

        def finish(r):
            if has_bias:
                r = r + bias_ref[...]
            if has_res:
                r = r + res_ref[...]
            o_ref[...] = r.astype(o_ref.dtype)

        part = lax.dot_general(a_ref[...].astype(BF16), b_ref[...].astype(BF16), dn, preferred_element_type=F32)
        if nk == 1:
            finish(part)
        else:
            @pl.when(kk == 0)
            def _():
                acc_ref[...] = part

            @pl.when(kk > 0)
            def _():
                acc_ref[...] += part

            @pl.when(kk == nk - 1)
            def _():
                finish(acc_ref[...])

        if carry:
            @pl.when((i == gm - 1) & (j == gn - 1) & (kk == nk - 1))
            def _():
                for blk, sem, frm in landing:
                    _rcopy(blk, blk, send_sems, recv_sems, sem, frm).wait_recv()
                for cp in cps:
                    cp.wait_send()

    if a_chips:
        a_spec = pl.BlockSpec((None, tm, tk), lambda i, j, kk: (kk // nbq, i, kk % nbq))
    elif ta:
        a_spec = pl.BlockSpec((tk, tm), lambda i, j, kk: (kk, i))
    else:
        a_spec = pl.BlockSpec((tm, tk), lambda i, j, kk: (i, kk))
    if b_chips and tb:
        b_spec = pl.BlockSpec((None, tn, tk), lambda i, j, kk: (kk // nbq, j, kk % nbq))
    elif b_chips:
        b_spec = pl.BlockSpec((None, tk, tn), lambda i, j, kk: (j // nbq, kk, j % nbq))
    elif tb:
        b_spec = pl.BlockSpec((tn, tk), lambda i, j, kk: (j, kk))
    else:
        b_spec = pl.BlockSpec((tk, tn), lambda i, j, kk: (kk, j))
    if out_chips:
        out_spec = pl.BlockSpec((None, tm, tn), lambda i, j, kk: (j // nbq, i, j % nbq))
        out_shape = jax.ShapeDtypeStruct((N_CHIPS, m, nq), out_dtype)
    else:
        out_spec = pl.BlockSpec((tm, tn), lambda i, j, kk: (i, j))
        out_shape = jax.ShapeDtypeStruct((m, n), out_dtype)
    in_specs, args = [a_spec, b_spec], [a, b]
    if has_bias:
        in_specs.append(pl.BlockSpec((1, tn), lambda i, j, kk: (0, j)))
        args.append(bias)
    if has_res:
        in_specs.append(pl.BlockSpec((tm, tn), lambda i, j, kk: (i, j)))
        args.append(res)
    scratch = [pltpu.VMEM((tm, tn), F32)] if nk > 1 else []
    if not carry:
        return pl.pallas_call(
            body, name=name, grid=(gm, gn, nk), in_specs=in_specs, out_specs=out_spec, out_shape=out_shape,
            scratch_shapes=scratch,
            compiler_params=_cparams(dimension_semantics=("parallel", "parallel", "arbitrary")),
        )(*args)
    lands, aliases = [], {}
    for jb, (kind, moved) in enumerate(carry):
        shape = {"gather": (N_CHIPS,) + moved.shape, "pass": moved.shape, "scatter": (3,) + moved.shape[1:]}[kind]
        lands.append(jax.ShapeDtypeStruct(shape, moved.dtype))
        if kind == "pass":
            aliases[len(args) + jb] = 1 + jb
    return pl.pallas_call(
        body, name=name, grid=(gm, gn, nk), in_specs=in_specs + [ANY] * nj, out_specs=[out_spec] + [ANY] * nj,
        out_shape=[out_shape] + lands, input_output_aliases=aliases,
        scratch_shapes=scratch + [pltpu.SemaphoreType.DMA((3 * nj,)), pltpu.SemaphoreType.DMA((3 * nj,))],
        compiler_params=_cparams(dimension_semantics=("arbitrary", "arbitrary", "arbitrary")),
    )(*args, *[moved for _, moved in carry])


def _rowwise(fn, rows, pars, outs, reds, *, tm, name):
    rows = [r if isinstance(r, tuple) else (r, r.shape[1], 0) for r in rows]
    t = rows[0][0].shape[0]
    assert t % tm == 0
    n_in, n_o = len(rows) + len(pars), len(outs)

    def body(*refs):
        i = pl.program_id(0)
        o, d = fn(*[r[...] for r in refs[:n_in]])
        for ref, val in zip(refs[n_in:n_in + n_o], o):
            ref[...] = val.astype(ref.dtype)
        d_refs = refs[n_in + n_o:]

        @pl.when(i == 0)
        def _():
            for ref in d_refs:
                ref[...] = jnp.zeros_like(ref)

        for ref, val in zip(d_refs, d):
            ref[...] += val

    in_specs = [pl.BlockSpec((tm, w), lambda i, b=blk: (i, b)) for _, w, blk in rows]
    in_specs += [pl.BlockSpec(p.shape, lambda i: (0, 0)) for p in pars]
    out_specs = [pl.BlockSpec((tm, c), lambda i: (i, 0)) for c, _ in outs]
    out_specs += [pl.BlockSpec((1, c), lambda i: (0, 0)) for c in reds]
    out_shape = [jax.ShapeDtypeStruct((t, c), dt) for c, dt in outs] + [jax.ShapeDtypeStruct((1, c), F32) for c in reds]
    res = pl.pallas_call(
        body, name=name, grid=(t // tm,), in_specs=in_specs, out_specs=out_specs, out_shape=out_shape,
        compiler_params=_cparams(dimension_semantics=("arbitrary",)),
    )(*[r[0] for r in rows], *pars)
    return res[:n_o], res[n_o:]


def _rms_fwd(x, g, *, name):
    def fn(xb, gb):
        r = lax.rsqrt(jnp.mean(xb * xb, axis=-1, keepdims=True) + RMS_EPS)
        return [xb * r * gb], []
    (h,), _ = _rowwise(fn, [x], [g], [(x.shape[1], BF16)], [], tm=256, name=name)
    return h


def _rms_bwd(x, g, dh, dres, *, name):
    def fn(xb, dhb, drb, gb):
        r = lax.rsqrt(jnp.mean(xb * xb, axis=-1, keepdims=True) + RMS_EPS)
        xh = xb * r
        dxh = dhb * gb
        dx = r * (dxh - xh * jnp.mean(dxh * xh, axis=-1, keepdims=True))
        out = drb + dx
        return [out, out], [_rowsum(out), _rowsum(dhb * xh)]
    c = x.shape[1]
    (dx, dxb), (colsum, dg) = _rowwise(fn, [x, dh, dres], [g], [(c, F32), (c, BF16)], [c, c], tm=256, name=name)
    return dx, dxb, colsum, dg


def _loss_head(x, g, tgt, *, name):
    d_model = x.shape[1]

    def fn(xb, tb, gb):
        r = lax.rsqrt(jnp.mean(xb * xb, axis=-1, keepdims=True) + RMS_EPS)
        xh = xb * r
        e = xh * gb - tb
        dy = e * (1.0 / d_model)
        dxh = dy * gb
        dx = r * (dxh - xh * jnp.mean(dxh * xh, axis=-1, keepdims=True))
        return [dx, dx], [_rowsum(e * e), _rowsum(dy * xh)]
    (dx, dxb), (sq, dg) = _rowwise(fn, [x, tgt], [g], [(d_model, F32), (d_model, BF16)], [d_model, d_model], tm=256,
                                   name=name)
    return dx, dxb, sq, dg


def _halo_rows(k):
    return 8 * ((k - 1 + 7) // 8) if k > 1 else 8


def _pad_taps(w):
    k = w.shape[0]
    kp = 8 * ((k + 7) // 8)
    return jnp.pad(w, ((0, kp - k), (0, 0)))


def _conv_chunks(k, tc):
    rc, lw = (32, 256) if k > 9 else (16, 512)
    lanes, l0 = [], 0
    while l0 < tc:
        lanes.append((l0, min(lw, tc - l0)))
        l0 += lw
    return rc, lanes


def _fold8(v):
    acc = v[0:8]
    for q in range(1, v.shape[0] // 8):
        acc = acc + v[8 * q:8 * q + 8]
    return acc


def _taps(src_ref, lead, base, rc, lanes, w_ref, k, sign, acc):
    for s in range(k):
        rows = pl.ds(base + sign * s, rc)
        acc = acc + w_ref[k - 1 - s:k - s, lanes] * src_ref[lead + (rows, lanes)]
    return acc


def _conv_cols(x, k, cols):
    c0, c = cols if cols else (0, x.shape[1])
    tc = _pick(math.gcd(c0, c), (512, 256, 128) if k > 9 else (1536, 1024, 512, 256, 128))
    return c0, c, tc


def _dwconv_fwd(x, w, b, *, name, cols=None):
    t = x.shape[0]
    k = w.shape[0]
    h = _halo_rows(k)
    tm = _pick(t, (256, 128))
    c0, c, tc = _conv_cols(x, k, cols)
    cb0 = c0 // tc
    wp = _pad_taps(w)
    kp = wp.shape[0]
    rb = tm // h
    rc, lane_chunks = _conv_chunks(k, tc)

    def body(x_ref, p_ref, w_ref, b_ref, o_ref, ext):
        i = pl.program_id(0)
        ext[pl.ds(h, tm), :] = x_ref[...]
        ext[pl.ds(0, h), :] = jnp.where(i > 0, p_ref[...], 0.0)
        for l0, lw in lane_chunks:
            lanes = pl.ds(l0, lw)
            for r0 in range(0, tm, rc):
                acc = jnp.broadcast_to(b_ref[:, lanes], (rc, lw))
                o_ref[pl.ds(r0, rc), lanes] = _taps(ext, (), h + r0, rc, lanes, w_ref, k, -1, acc)

    return pl.pallas_call(
        body, name=name, grid=(t // tm, c // tc),
        in_specs=[pl.BlockSpec((tm, tc), lambda i, j: (i, cb0 + j)),
                  pl.BlockSpec((h, tc), lambda i, j: (jnp.maximum(i * rb - 1, 0), cb0 + j)),
                  pl.BlockSpec((kp, tc), lambda i, j: (0, j)),
                  pl.BlockSpec((1, tc), lambda i, j: (0, j))],
        out_specs=pl.BlockSpec((tm, tc), lambda i, j: (i, j)),
        out_shape=jax.ShapeDtypeStruct((t, c), F32),
        scratch_shapes=[pltpu.VMEM((h + tm, tc), F32)],
        compiler_params=_cparams(dimension_semantics=("parallel", "parallel")),
    )(x, x, wp, b)


def _dwconv_bwd(x, dy, w, *, name, dx_dtype=F32, cols=None):
    t = x.shape[0]
    k = w.shape[0]
    h = _halo_rows(k)
    tm = _pick(t, (256, 128))
    c0, c, tc = _conv_cols(x, k, cols)
    cb0 = c0 // tc
    wp = _pad_taps(w)
    kp = wp.shape[0]
    rb = tm // h
    nt = t // tm
    rc, lane_chunks = _conv_chunks(k, tc)

    def body(x_ref, p_ref, dy_ref, n_ref, w_ref, dx_ref, dw_ref, db_ref, xext, dext):
        i = pl.program_id(1)

        @pl.when(i == 0)
        def _():
            dw_ref[...] = jnp.zeros_like(dw_ref)
            db_ref[...] = jnp.zeros_like(db_ref)

        xext[pl.ds(h, tm), :] = x_ref[...]
        xext[pl.ds(0, h), :] = jnp.where(i > 0, p_ref[...], 0.0)
        dext[pl.ds(0, tm), :] = dy_ref[...]
        dext[pl.ds(tm, h), :] = jnp.where(i < nt - 1, n_ref[...], 0.0)
        for l0, lw in lane_chunks:
            lanes = pl.ds(l0, lw)
            for r0 in range(0, tm, rc):
                acc = _taps(dext, (), r0, rc, lanes, w_ref, k, +1, jnp.zeros((rc, lw), F32))
                dx_ref[pl.ds(r0, rc), lanes] = acc.astype(dx_ref.dtype)
            for s in range(k):
                a8 = jnp.zeros((8, lw), F32)
                for r0 in range(0, tm, rc):
                    a8 = a8 + _fold8(xext[pl.ds(h + r0 - s, rc), lanes] * dext[pl.ds(r0, rc), lanes])
                dw_ref[k - 1 - s:k - s, lanes] += _rowsum(a8)
            b8 = jnp.zeros((8, lw), F32)
            for r0 in range(0, tm, rc):
                b8 = b8 + _fold8(dext[pl.ds(r0, rc), lanes])
            db_ref[:, lanes] += _rowsum(b8)

    dx, dw, db = pl.pallas_call(
        body, name=name, grid=(c // tc, nt),
        in_specs=[pl.BlockSpec((tm, tc), lambda j, i: (i, cb0 + j)),
                  pl.BlockSpec((h, tc), lambda j, i: (jnp.maximum(i * rb - 1, 0), cb0 + j)),
                  pl.BlockSpec((tm, tc), lambda j, i: (i, j)),
                  pl.BlockSpec((h, tc), lambda j, i: (jnp.minimum((i + 1) * rb, nt * rb - 1), j)),
                  pl.BlockSpec((kp, tc), lambda j, i: (0, j))],
        out_specs=[pl.BlockSpec((tm, tc), lambda j, i: (i, j)),
                   pl.BlockSpec((kp, tc), lambda j, i: (0, j)),
                   pl.BlockSpec((1, tc), lambda j, i: (0, j))],
        out_shape=[jax.ShapeDtypeStruct((t, c), dx_dtype), jax.ShapeDtypeStruct((kp, c), F32),
                   jax.ShapeDtypeStruct((1, c), F32)],
        scratch_shapes=[pltpu.VMEM((h + tm, tc), F32), pltpu.VMEM((tm + h, tc), F32)],
        compiler_params=_cparams(dimension_semantics=("parallel", "arbitrary")),
    )(x, x, dy, dy, wp)
    return dx, dw[:k], db


def _glu_fwd(u, b_in, *, name):
    d = u.shape[1] // 2

    def fn(ua, ug, ba, bg):
        return [(ua + ba) * _sigmoid(ug + bg)], []
    (v,), _ = _rowwise(fn, [(u, d, 0), (u, d, 1)], [b_in[:, :d], b_in[:, d:]], [(d, F32)], [], tm=256, name=name)
    return v


def _glu_bwd(u, b_in, dv, *, name):
    d = u.shape[1] // 2

    def fn(ua, ug, dvb, ba, bg):
        a = ua + ba
        s = _sigmoid(ug + bg)
        du = jnp.concatenate([dvb * s, dvb * a * s * (1.0 - s)], axis=1)
        return [du], [_rowsum(du)]
    (du,), (db,) = _rowwise(fn, [(u, d, 0), (u, d, 1), dv], [b_in[:, :d], b_in[:, d:]], [(2 * d, BF16)], [2 * d],
                            tm=256, name=name)
    return du, db


def _ln_silu_fwd(v, g, b, *, name):
    def fn(vb, gb, bb):
        mu = jnp.mean(vb, axis=-1, keepdims=True)
        xc = vb - mu
        rstd = lax.rsqrt(jnp.mean(xc * xc, axis=-1, keepdims=True) + LN_EPS)
        return [_silu(xc * rstd * gb + bb)], []
    (o,), _ = _rowwise(fn, [v], [g, b], [(v.shape[1], BF16)], [], tm=256, name=name)
    return o


def _ln_silu_bwd(v, g, b, do, *, name):
    def fn(vb, dob, gb, bb):
        mu = jnp.mean(vb, axis=-1, keepdims=True)
        xc = vb - mu
        rstd = lax.rsqrt(jnp.mean(xc * xc, axis=-1, keepdims=True) + LN_EPS)
        xh = xc * rstd
        dy = dob * _dsilu(xh * gb + bb)
        dxh = dy * gb
        dv = rstd * (dxh - jnp.mean(dxh, axis=-1, keepdims=True) - xh * jnp.mean(dxh * xh, axis=-1, keepdims=True))
        return [dv], [_rowsum(dy * xh), _rowsum(dy)]
    c = v.shape[1]
    (dv,), (dg, db) = _rowwise(fn, [v, do], [g, b], [(c, F32)], [c, c], tm=256, name=name)
    return dv, dg, db


def _ffn_mid_setup(u2c, w, b):
    _, t, nq = u2c.shape
    f = 2 * nq
    k = w.shape[0]
    h = _halo_rows(k)
    tm = _pick(t, (256, 128))
    tc = _pick(nq, (1408, 1024, 512, 256, 128))
    rc, lane_chunks = _conv_chunks(k, tc)
    return dict(t=t, nq=nq, f=f, k=k, h=h, tm=tm, tc=tc, npq=nq // tc, rb=tm // h, nt=t // tm, rc=rc,
                lane_chunks=lane_chunks, u4=u2c.reshape(2, 2, t, nq), wg=_pad_taps(w[:, :f]), wv=_pad_taps(w[:, f:]),
                bg=b[:, :f], bv=b[:, f:])


def _ffn_mid_fwd(u2c, w, b, *, name):
    p = _ffn_mid_setup(u2c, w, b)
    t, f, k, h, tm, tc, npq, rb, rc = (p[n] for n in ("t", "f", "k", "h", "tm", "tc", "npq", "rb", "rc"))
    kp = p["wg"].shape[0]

    def body(u_ref, p_ref, wg_ref, wv_ref, bg_ref, bv_ref, o_ref, ext):
        i = pl.program_id(2)
        for kind in range(2):
            ext[kind, pl.ds(h, tm), :] = u_ref[kind]
            ext[kind, pl.ds(0, h), :] = jnp.where(i > 0, p_ref[kind], 0.0)
        for l0, lw in p["lane_chunks"]:
            lanes = pl.ds(l0, lw)
            for r0 in range(0, tm, rc):
                g = _taps(ext, (0,), h + r0, rc, lanes, wg_ref, k, -1, jnp.broadcast_to(bg_ref[:, lanes], (rc, lw)))
                v = _taps(ext, (1,), h + r0, rc, lanes, wv_ref, k, -1, jnp.broadcast_to(bv_ref[:, lanes], (rc, lw)))
                o_ref[pl.ds(r0, rc), lanes] = (_silu(g) * v).astype(BF16)

    col = lambda q, jj, i: (0, q * npq + jj)
    return pl.pallas_call(
        body, name=name, grid=(2, npq, t // tm),
        in_specs=[pl.BlockSpec((2, None, tm, tc), lambda q, jj, i: (0, q, i, jj)),
                  pl.BlockSpec((2, None, h, tc), lambda q, jj, i: (0, q, jnp.maximum(i * rb - 1, 0), jj)),
                  pl.BlockSpec((kp, tc), col), pl.BlockSpec((kp, tc), col),
                  pl.BlockSpec((1, tc), col), pl.BlockSpec((1, tc), col)],
        out_specs=pl.BlockSpec((tm, tc), lambda q, jj, i: (i, q * npq + jj)),
        out_shape=jax.ShapeDtypeStruct((t, f), BF16),
        scratch_shapes=[pltpu.VMEM((2, h + tm, tc), F32)],
        compiler_params=_cparams(dimension_semantics=("parallel", "parallel", "parallel")),
    )(p["u4"], p["u4"], p["wg"], p["wv"], p["bg"], p["bv"])


def _ffn_mid_bwd(u2c, dhm, w, b, *, name):
    p = _ffn_mid_setup(u2c, w, b)
    t, nq, f, k, h, tm, tc, npq, rb, nt, rc = (p[n] for n in ("t", "nq", "f", "k", "h", "tm", "tc", "npq", "rb", "nt", "rc"))
    kp = p["wg"].shape[0]
    chunks1 = [(r0, rc) for r0 in range(0, tm, rc)] + [(tm, h)]

    def body(u_ref, p_ref, n_ref, dh_ref, nd_ref, wg_ref, wv_ref, bg_ref, bv_ref,
             du_ref, dwg_ref, dwv_ref, dbg_ref, dbv_ref, uext, dsc):
        i = pl.program_id(2)
        w_refs, dw_refs, db_refs = (wg_ref, wv_ref), (dwg_ref, dwv_ref), (dbg_ref, dbv_ref)

        @pl.when(i == 0)
        def _():
            for ref in dw_refs + db_refs:
                ref[...] = jnp.zeros_like(ref)

        for kind in range(2):
            uext[kind, pl.ds(0, h), :] = jnp.where(i > 0, p_ref[kind], 0.0)
            uext[kind, pl.ds(h, tm), :] = u_ref[kind]
            uext[kind, pl.ds(h + tm, h), :] = jnp.where(i < nt - 1, n_ref[kind], 0.0)
        for l0, lw in p["lane_chunks"]:
            lanes = pl.ds(l0, lw)
            for r0, rr in chunks1:
                g = _taps(uext, (0,), h + r0, rr, lanes, wg_ref, k, -1, jnp.broadcast_to(bg_ref[:, lanes], (rr, lw)))
                v = _taps(uext, (1,), h + r0, rr, lanes, wv_ref, k, -1, jnp.broadcast_to(bv_ref[:, lanes], (rr, lw)))
                dh = dh_ref[pl.ds(r0, rr), lanes] if r0 < tm else jnp.where(i < nt - 1, nd_ref[:, lanes], 0.0)
                sg = _sigmoid(g)
                dsc[0, pl.ds(r0, rr), lanes] = dh * v * (sg * (1.0 + g * (1.0 - sg)))
                dsc[1, pl.ds(r0, rr), lanes] = dh * (g * sg)
            for kind in range(2):
                for r0 in range(0, tm, rc):
                    acc = _taps(dsc, (kind,), r0, rc, lanes, w_refs[kind], k, +1, jnp.zeros((rc, lw), F32))
                    du_ref[kind, pl.ds(r0, rc), lanes] = acc.astype(BF16)
                for s in range(k):
                    a8 = jnp.zeros((8, lw), F32)
                    for r0 in range(0, tm, rc):
                        a8 = a8 + _fold8(uext[kind, pl.ds(h + r0 - s, rc), lanes] * dsc[kind, pl.ds(r0, rc), lanes])
                    dw_refs[kind][k - 1 - s:k - s, lanes] += _rowsum(a8)
                b8 = jnp.zeros((8, lw), F32)
                for r0 in range(0, tm, rc):
                    b8 = b8 + _fold8(dsc[kind, pl.ds(r0, rc), lanes])
                db_refs[kind][:, lanes] += _rowsum(b8)

    col = lambda q, jj, i: (0, q * npq + jj)
    nxt = lambda i: jnp.minimum((i + 1) * rb, nt * rb - 1)
    du, dwg, dwv, dbg, dbv = pl.pallas_call(
        body, name=name, grid=(2, npq, nt),
        in_specs=[pl.BlockSpec((2, None, tm, tc), lambda q, jj, i: (0, q, i, jj)),
                  pl.BlockSpec((2, None, h, tc), lambda q, jj, i: (0, q, jnp.maximum(i * rb - 1, 0), jj)),
                  pl.BlockSpec((2, None, h, tc), lambda q, jj, i: (0, q, nxt(i), jj)),
                  pl.BlockSpec((tm, tc), lambda q, jj, i: (i, q * npq + jj)),
                  pl.BlockSpec((h, tc), lambda q, jj, i: (nxt(i), q * npq + jj)),
                  pl.BlockSpec((kp, tc), col), pl.BlockSpec((kp, tc), col),
                  pl.BlockSpec((1, tc), col), pl.BlockSpec((1, tc), col)],
        out_specs=[pl.BlockSpec((2, None, tm, tc), lambda q, jj, i: (0, q, i, jj)),
                   pl.BlockSpec((kp, tc), col), pl.BlockSpec((kp, tc), col),
                   pl.BlockSpec((1, tc), col), pl.BlockSpec((1, tc), col)],
        out_shape=[jax.ShapeDtypeStruct((2, 2, t, nq), BF16), jax.ShapeDtypeStruct((kp, f), F32),
                   jax.ShapeDtypeStruct((kp, f), F32), jax.ShapeDtypeStruct((1, f), F32), jax.ShapeDtypeStruct((1, f), F32)],
        scratch_shapes=[pltpu.VMEM((2, h + tm + h, tc), F32), pltpu.VMEM((2, tm + h, tc), F32)],
        compiler_params=_cparams(dimension_semantics=("parallel", "parallel", "arbitrary")),
    )(p["u4"], p["u4"], p["u4"], dhm, dhm, p["wg"], p["wv"], p["bg"], p["bv"])
    return (du.reshape(N_CHIPS, t, nq), jnp.concatenate([dwg[:k], dwv[:k]], axis=1), jnp.concatenate([dbg, dbv], axis=1))


def _head_expander(n_heads):
    return (jnp.arange(n_heads * HEAD_DIM)[None, :] // HEAD_DIM == jnp.arange(n_heads)[:, None]).astype(BF16)


def _ssm_act(conv, dtp, bias_exp, aneg_exp, expand, *, di, name):
    q = SSD_CHUNK
    gn = (conv.shape[1] - di) // 2
    mm_dims = (((1,), (0,)), ((), ()))

    def fn(cb, dtb, bb, ab, eb):
        act = _silu(cb)
        dt = _dot3(dtb, eb, mm_dims, 0) + bb
        dt = jnp.maximum(dt, 0.0) + jnp.log(1.0 + jnp.exp(-jnp.abs(dt)))
        a = dt * ab
        tri = (lax.broadcasted_iota(jnp.int32, (q, q), 0) >= lax.broadcasted_iota(jnp.int32, (q, q), 1)).astype(F32)
        cs = _dot3(tri, a, mm_dims, 1)
        return [act[:, :di], act[:, di:di + gn], act[:, di + gn:], dt, cs], []
    outs, _ = _rowwise(fn, [conv, dtp], [bias_exp, aneg_exp, expand],
                       [(di, F32), (gn, F32), (gn, F32), (di, F32), (di, F32)], [], tm=q, name=name)
    return outs


def _head_masks(q):
    lane = lax.broadcasted_iota(jnp.int32, (q, LANES), 1)
    return lane < HEAD_DIM


def _pair_cols(cs, lo):
    sw = pltpu.roll(cs, HEAD_DIM, 1)
    return jnp.where(lo, cs, sw), jnp.where(lo, sw, cs)


def _ssd_fwd(xs, dt_exp, cs_exp, cs_rows, bm, cm, *, name):
    t, di = xs.shape
    q = SSD_CHUNK
    hg = di // N_GROUPS
    npair = hg // LANES
    nheads = hg // HEAD_DIM
    nc = t // q
    n = D_STATE
    cpb = _pick(nc, (4, 2, 1))
    qb = cpb * q

    def body(xs_ref, dt_ref, cs_ref, csr_ref, b_ref, c_ref, y_ref, st_ref, s_scr):
        ci = pl.program_id(1)

        @pl.when(ci == 0)
        def _():
            s_scr[...] = jnp.zeros_like(s_scr)

        tri = lax.broadcasted_iota(jnp.int32, (q, q), 0) >= lax.broadcasted_iota(jnp.int32, (q, q), 1)
        lo = _head_masks(q)
        for cc in range(cpb):
            rows = pl.ds(cc * q, q)
            bb = b_ref[rows, :].astype(BF16)
            cb_ = c_ref[rows, :].astype(BF16)
            cbm = lax.dot_general(cb_, bb, (((1,), (1,)), ((), ())), preferred_element_type=F32)
            csr = csr_ref[0, :, pl.ds(cc * q, q)]
            for p in range(npair):
                sl = pl.ds(p * LANES, LANES)
                x = xs_ref[rows, sl] * dt_ref[rows, sl]
                cs = cs_ref[rows, sl]
                col0, col1 = _pair_cols(cs, lo)
                l0 = jnp.where(tri, jnp.exp(jnp.minimum(col0 - csr[2 * p:2 * p + 1, :], 0.0)), 0.0)
                l1 = jnp.where(tri, jnp.exp(jnp.minimum(col1 - csr[2 * p + 1:2 * p + 2, :], 0.0)), 0.0)
                xb = x.astype(BF16)
                yd = jnp.where(lo, jnp.dot((cbm * l0).astype(BF16), xb, preferred_element_type=F32),
                               jnp.dot((cbm * l1).astype(BF16), xb, preferred_element_type=F32))
                s = s_scr[p]
                st_ref[0, cc, p] = s
                yo = jnp.exp(cs) * jnp.dot(cb_, s.astype(BF16), preferred_element_type=F32)
                y_ref[rows, sl] = yd + yo
                cs_end = cs[q - 1:q, :]
                xd = (x * jnp.exp(cs_end - cs)).astype(BF16)
                s_scr[p] = jnp.exp(cs_end) * s + lax.dot_general(bb, xd, (((0,), (0,)), ((), ())),
                                                                 preferred_element_type=F32)

    return pl.pallas_call(
        body, name=name, grid=(N_GROUPS, nc // cpb),
        in_specs=[pl.BlockSpec((qb, hg), lambda g, c: (c, g)),
                  pl.BlockSpec((qb, hg), lambda g, c: (c, g)),
                  pl.BlockSpec((qb, hg), lambda g, c: (c, g)),
                  pl.BlockSpec((1, nheads, qb), lambda g, c: (g, 0, c)),
                  pl.BlockSpec((qb, n), lambda g, c: (c, g)),
                  pl.BlockSpec((qb, n), lambda g, c: (c, g))],
        out_specs=[pl.BlockSpec((qb, hg), lambda g, c: (c, g)),
                   pl.BlockSpec((1, cpb, npair, n, LANES), lambda g, c: (g, c, 0, 0, 0))],
        out_shape=[jax.ShapeDtypeStruct((t, di), F32),
                   jax.ShapeDtypeStruct((N_GROUPS, nc, npair, n, LANES), F32)],
        scratch_shapes=[pltpu.VMEM((npair, n, LANES), F32)],
        compiler_params=_cparams(dimension_semantics=("parallel", "arbitrary")),
    )(xs, dt_exp, cs_exp, cs_rows, bm, cm)


def _dot3(a, b, dims, split):
    rest = (a, b)[split].astype(F32)
    other = (a, b)[1 - split].astype(BF16)
    acc = None
    for _ in range(3):
        part = rest.astype(BF16)
        rest = rest - part.astype(F32)
        d = (lax.dot_general(part, other, dims, preferred_element_type=F32) if split == 0
             else lax.dot_general(other, part, dims, preferred_element_type=F32))
        acc = d if acc is None else acc + d
    return acc


def _ssd_bwd(xs, dt_exp, cs_exp, cs_rows, bm, cm, dy, states, aneg_exp, *, name):
    t, di = xs.shape
    q = SSD_CHUNK
    hg = di // N_GROUPS
    npair = hg // LANES
    nheads = hg // HEAD_DIM
    nc = t // q
    n = D_STATE
    nt_dims = (((1,), (1,)), ((), ()))
    tn_dims = (((0,), (0,)), ((), ()))

    mm_dims = (((1,), (0,)), ((), ()))

    def body(xs_ref, dt_ref, cs_ref, csr_ref, b_ref, c_ref, dy_ref, st_ref, an_ref,
             dxp_ref, db_ref, dc_ref, ddt_ref, dan_ref, r_scr):
        ci = pl.program_id(1)

        @pl.when(ci == 0)
        def _():
            r_scr[...] = jnp.zeros_like(r_scr)
            dan_ref[...] = jnp.zeros_like(dan_ref)

        bb = b_ref[...].astype(BF16)
        cb_ = c_ref[...].astype(BF16)
        cbm = lax.dot_general(cb_, bb, nt_dims, preferred_element_type=F32)
        row = lax.broadcasted_iota(jnp.int32, (q, q), 0)
        col = lax.broadcasted_iota(jnp.int32, (q, q), 1)
        tri = row >= col
        triu = (row <= col).astype(F32)
        trisl = (row > col).astype(F32)
        ones2 = (lax.broadcasted_iota(jnp.int32, (LANES, LANES), 0) // HEAD_DIM
                 == lax.broadcasted_iota(jnp.int32, (LANES, LANES), 1) // HEAD_DIM).astype(BF16)
        onesq = jnp.ones((q, LANES), BF16)
        last = lax.broadcasted_iota(jnp.int32, (q, LANES), 0) == q - 1
        lo = _head_masks(q)
        csr = csr_ref[0]
        dcb = jnp.zeros((q, q), F32)
        dc_acc = jnp.zeros((q, n), F32)
        db_acc = jnp.zeros((q, n), F32)
        for p in range(npair):
            sl = pl.ds(p * LANES, LANES)
            xsv = xs_ref[:, sl]
            dtv = dt_ref[:, sl]
            x = xsv * dtv
            cs = cs_ref[:, sl]
            dyv = dy_ref[:, sl]
            col0, col1 = _pair_cols(cs, lo)
            l0 = jnp.where(tri, jnp.exp(jnp.minimum(col0 - csr[2 * p:2 * p + 1, :], 0.0)), 0.0)
            l1 = jnp.where(tri, jnp.exp(jnp.minimum(col1 - csr[2 * p + 1:2 * p + 2, :], 0.0)), 0.0)
            xb = x.astype(BF16)
            dyb = dyv.astype(BF16)
            g0 = lax.dot_general(jnp.where(lo, dyv, 0.0).astype(BF16), xb, nt_dims, preferred_element_type=F32)
            g1 = lax.dot_general(jnp.where(lo, 0.0, dyv).astype(BF16), xb, nt_dims, preferred_element_type=F32)
            gl0, gl1 = g0 * l0, g1 * l1
            dcb = dcb + gl0 + gl1
            w0, w1 = cbm * gl0, cbm * gl1
            dxd = jnp.where(lo,
                            lax.dot_general((cbm * l0).astype(BF16), dyb, tn_dims, preferred_element_type=F32),
                            lax.dot_general((cbm * l1).astype(BF16), dyb, tn_dims, preferred_element_type=F32))
            e = jnp.exp(cs)
            cs_end = cs[q - 1:q, :]
            dte = jnp.exp(cs_end - cs)
            dend = jnp.exp(cs_end)
            sf = st_ref[0, 0, p]
            sb = sf.astype(BF16)
            r = r_scr[p]
            rb = r.astype(BF16)
            dyeb = (dyv * e).astype(BF16)
            dc_acc = dc_acc + lax.dot_general(dyeb, sb, nt_dims, preferred_element_type=F32)
            dxo = dte * jnp.dot(bb, rb, preferred_element_type=F32)
            db_acc = db_acc + lax.dot_general((x * dte).astype(BF16), rb, nt_dims, preferred_element_type=F32)
            r_scr[p] = dend * r + lax.dot_general(cb_, dyeb, tn_dims, preferred_element_type=F32)
            dx = dxd + dxo
            dxp_ref[:, sl] = dx
            yoff = e * jnp.dot(cb_, sb, preferred_element_type=F32)
            w0b, w1b = w0.astype(BF16), w1.astype(BF16)
            rw = jnp.where(lo, jnp.sum(w0b.astype(F32), axis=1, keepdims=True), jnp.sum(w1b.astype(F32), axis=1, keepdims=True))
            cw = jnp.where(lo, lax.dot_general(w0b, onesq, tn_dims, preferred_element_type=F32),
                           lax.dot_general(w1b, onesq, tn_dims, preferred_element_type=F32))
            through = jnp.where(last, dend * _rowsum(r * sf), 0.0)
            suf = jnp.dot((dyv * yoff + through).astype(BF16), ones2, preferred_element_type=F32) + rw - cw
            pre = jnp.dot((dxo * x).astype(BF16), ones2, preferred_element_type=F32)
            da = _dot3(triu, suf, mm_dims, 1) + _dot3(trisl, pre, mm_dims, 1)
            qs = jnp.dot((dx * xsv).astype(BF16), ones2, preferred_element_type=F32)
            ddt_ref[:, sl] = da * an_ref[:, sl] + qs
            dan_ref[:, sl] += _rowsum(da * dtv)
        dcbb = dcb.astype(BF16)
        dc_ref[...] = dc_acc + jnp.dot(dcbb, bb, preferred_element_type=F32)
        db_ref[...] = db_acc + lax.dot_general(dcbb, cb_, tn_dims, preferred_element_type=F32)

    rev = lambda g, c: (nc - 1 - c, g)
    return pl.pallas_call(
        body, name=name, grid=(N_GROUPS, nc),
        in_specs=[pl.BlockSpec((q, hg), rev), pl.BlockSpec((q, hg), rev), pl.BlockSpec((q, hg), rev),
                  pl.BlockSpec((1, nheads, q), lambda g, c: (g, 0, nc - 1 - c)),
                  pl.BlockSpec((q, n), rev), pl.BlockSpec((q, n), rev),
                  pl.BlockSpec((q, hg), rev),
                  pl.BlockSpec((1, 1, npair, n, LANES), lambda g, c: (g, nc - 1 - c, 0, 0, 0)),
                  pl.BlockSpec((1, hg), lambda g, c: (0, g))],
        out_specs=[pl.BlockSpec((q, hg), rev), pl.BlockSpec((q, n), rev), pl.BlockSpec((q, n), rev),
                   pl.BlockSpec((q, hg), rev), pl.BlockSpec((1, hg), lambda g, c: (0, g))],
        out_shape=[jax.ShapeDtypeStruct((t, di), F32), jax.ShapeDtypeStruct((t, N_GROUPS * n), F32),
                   jax.ShapeDtypeStruct((t, N_GROUPS * n), F32), jax.ShapeDtypeStruct((t, di), F32),
                   jax.ShapeDtypeStruct((1, di), F32)],
        scratch_shapes=[pltpu.VMEM((npair, n, LANES), F32)],
        compiler_params=_cparams(dimension_semantics=("parallel", "arbitrary")),
    )(xs, dt_exp, cs_exp, cs_rows, bm, cm, dy, states, aneg_exp)


def _group_stats(w, gw):
    return [lax.rsqrt(jnp.mean(w[:, i * gw:(i + 1) * gw] ** 2, axis=-1, keepdims=True) + RMS_EPS)
            for i in range(N_GROUPS)]


def _gated_norm_fwd(y_ssd, xs, z, d_exp, g, *, name):
    di = xs.shape[1]
    gw = di // N_GROUPS

    def fn(yb, xb, zb, db, gb):
        w = (yb + db * xb) * _silu(zb)
        rs = _group_stats(w, gw)
        return [jnp.concatenate([w[:, i * gw:(i + 1) * gw] * rs[i] for i in range(N_GROUPS)], axis=1) * gb], []
    (o,), _ = _rowwise(fn, [y_ssd, xs, (z, di, 0)], [d_exp, g], [(di, BF16)], [], tm=128, name=name)
    return o


def _gated_norm_bwd(y_ssd, xs, z, d_exp, g, do, *, name):
    di = xs.shape[1]
    gw = di // N_GROUPS

    def fn(yb, xb, zb, dob, db, gb):
        yy = yb + db * xb
        sz = _silu(zb)
        w = yy * sz
        rs = _group_stats(w, gw)
        dwh = dob * gb
        wh_parts, dw_parts = [], []
        for i in range(N_GROUPS):
            sl = slice(i * gw, (i + 1) * gw)
            wh = w[:, sl] * rs[i]
            wh_parts.append(wh)
            dw_parts.append(rs[i] * (dwh[:, sl] - wh * jnp.mean(dwh[:, sl] * wh, axis=-1, keepdims=True)))
        wh = jnp.concatenate(wh_parts, axis=1)
        dw = jnp.concatenate(dw_parts, axis=1)
        dy = dw * sz
        dz = dw * yy * _dsilu(zb)
        return [dy, dz], [_rowsum(dob * wh), _rowsum(dy * xb)]
    (dy, dz), (dg, dd) = _rowwise(fn, [y_ssd, xs, (z, di, 0), do], [d_exp, g], [(di, F32), (di, BF16)], [di, di],
                                  tm=128, name=name)
    return dy, dz, dg, dd


def _ssm_act_bwd(conv, dtp, dxp, dy, dbm, dcm, ddt_exp, dt_exp, bias_exp, d_exp, expand, *, di, name):
    gn = dbm.shape[1]

    def fn(cb, dtb, dxpb, dyb, dbb, dcb, ddtb, dteb, bb, db, eb):
        dxs = dxpb * dteb + dyb * db
        dact = jnp.concatenate([dxs, dbb, dcb], axis=1)
        dconv = dact * _dsilu(cb)
        ddtp = ddtb * _sigmoid(_dot3(dtb, eb, (((1,), (0,)), ((), ())), 0) + bb)
        return [dconv, ddtp], [_rowsum(ddtp)]
    (dconv, ddtp), (dbias,) = _rowwise(fn, [conv, dtp, dxp, dy, dbm, dcm, ddt_exp, dt_exp], [bias_exp, d_exp, expand],
                                       [(di + 2 * gn, F32), (di, F32)], [di], tm=64, name=name)
    return dconv, ddtp, dbias


def _adamw(w, g, m, v, *, name):
    r, c = w.shape
    c1 = 1.0 / (1.0 - ADAM_B1 ** ADAM_STEP)
    c2 = 1.0 / (1.0 - ADAM_B2 ** ADAM_STEP)

    def fn(wb, gb, mb, vb):
        mn = ADAM_B1 * mb + (1.0 - ADAM_B1) * gb
        vn = ADAM_B2 * vb + (1.0 - ADAM_B2) * (gb * gb)
        delta = -ADAM_LR * ((mn * c1) / (jnp.sqrt(vn * c2) + ADAM_EPS) + ADAM_WD * wb)
        return [delta, mn, vn], []
    cap = max(8, ADAMW_BLOCK_ELEMS // c)
    tm = _pick(r, [p for p in (512, 256, 128, 64, 32, 16, 8) if p <= cap])
    (d, mn, vn), _ = _rowwise(fn, [w, g, m, v], [], [(c, F32)] * 3, [], tm=tm, name=name)
    return d, mn, vn


def _add_pair(sel, g, r, *, name):
    _, _, rows, cols = g.shape
    tm = _pick(rows, (256, 128, 64, 32, 16))

    def body(s_ref, g_ref, r_ref, o_ref):
        o_ref[...] = (g_ref[...].astype(F32) + r_ref[...].astype(F32)).astype(BF16)

    return pl.pallas_call(
        body, name=name,
        grid_spec=pltpu.PrefetchScalarGridSpec(
            num_scalar_prefetch=1, grid=(N_CHIPS, rows // tm),
            in_specs=[pl.BlockSpec((None, None, tm, cols), lambda j, i, s: (j, s[0], i, 0)),
                      pl.BlockSpec((None, tm, cols), lambda j, i, s: (j, i, 0))],
            out_specs=pl.BlockSpec((None, tm, cols), lambda j, i, s: (j, i, 0))),
        out_shape=jax.ShapeDtypeStruct((N_CHIPS, rows, cols), BF16),
        compiler_params=_cparams(dimension_semantics=("parallel", "parallel")),
    )(sel, g, r)


def _add_four(sel, p, r, *, name):
    _, rows, cols = p.shape
    tm = _pick(rows, (256, 128, 64, 32, 16))

    def body(s_ref, p_ref, r0, r1, r2, o_ref):
        o_ref[...] = ((p_ref[...].astype(F32) + r0[...].astype(F32)) + r1[...].astype(F32)) + r2[...].astype(F32)

    rspec = lambda k: pl.BlockSpec((None, tm, cols), lambda i, s, k=k: (k, i, 0))
    return pl.pallas_call(
        body, name=name,
        grid_spec=pltpu.PrefetchScalarGridSpec(
            num_scalar_prefetch=1, grid=(rows // tm,),
            in_specs=[pl.BlockSpec((None, tm, cols), lambda i, s: (s[0], i, 0)), rspec(0), rspec(1), rspec(2)],
            out_specs=pl.BlockSpec((None, tm, cols), lambda i, s: (s[1], i, 0))),
        out_shape=jax.ShapeDtypeStruct((2, rows, cols), F32),
        compiler_params=_cparams(dimension_semantics=("parallel",)),
    )(sel, p, r, r, r)


def _sum8(g, *, name):
    _, rows, cols = g.shape
    tm = _pick(rows, (512, 256, 128, 64, 32, 16, 8))

    def body(g_ref, o_ref):
        acc = g_ref[0]
        for k in range(1, 8):
            acc = acc + g_ref[k]
        o_ref[...] = acc

    return pl.pallas_call(
        body, name=name, grid=(rows // tm,),
        in_specs=[pl.BlockSpec((8, tm, cols), lambda i: (0, i, 0))],
        out_specs=pl.BlockSpec((tm, cols), lambda i: (i, 0)),
        out_shape=jax.ShapeDtypeStruct((rows, cols), F32),
        compiler_params=_cparams(dimension_semantics=("parallel",)),
    )(g)


def _place():
    x, y, c = lax.axis_index("x"), lax.axis_index("y"), lax.axis_index("c")
    chips = [(1 - x, y), (x, 1 - y), (1 - x, 1 - y)]
    return x, y, c, chips


def _rcopy(src, dst, send_sems, recv_sems, k, to):
    return pltpu.make_async_remote_copy(src_ref=src, dst_ref=dst, send_sem=send_sems.at[k], recv_sem=recv_sems.at[k],
                                        device_id=to, device_id_type=MESH)


def _gather_chips(packs, *, name):
    n = len(packs)

    def body(*refs):
        srcs, outs, (send_sems, recv_sems) = refs[:n], refs[n:2 * n], refs[2 * n:]
        x, y, c, chips = _place()
        sibling = (x, y, 1 - c)
        me = 2 * x + y
        first, passed = [], []
        for t, (src, out) in enumerate(zip(srcs, outs)):
            for k, (cx, cy) in enumerate(chips):
                cp = _rcopy(src.at[c], out.at[me, c], send_sems, recv_sems, 6 * t + k, (cx, cy, c))
                cp.start()
                first.append(cp)
        for t, out in enumerate(outs):
            for k, (cx, cy) in enumerate(chips):
                blk = out.at[2 * cx + cy, c]
                _rcopy(blk, blk, send_sems, recv_sems, 6 * t + k, (cx, cy, c)).wait_recv()
                fw = _rcopy(blk, blk, send_sems, recv_sems, 6 * t + 3 + k, sibling)
                fw.start()
                passed.append(fw)
        for t, out in enumerate(outs):
            for k, (cx, cy) in enumerate(chips):
                blk = out.at[2 * cx + cy, 1 - c]
                _rcopy(blk, blk, send_sems, recv_sems, 6 * t + 3 + k, sibling).wait_recv()
        for cp in first + passed:
            cp.wait_send()

    return pl.pallas_call(
        body, name=name, in_specs=[ANY] * n, out_specs=[ANY] * n,
        out_shape=[jax.ShapeDtypeStruct((N_CHIPS,) + p.shape, p.dtype) for p in packs],
        scratch_shapes=[pltpu.SemaphoreType.DMA((6 * n,)), pltpu.SemaphoreType.DMA((6 * n,))],
    )(*packs)


def _gather_devices(pack, *, name):
    rows, cols = pack.shape

    def body(src, out, send_sems, recv_sems, local_sem):
        x, y, c, chips = _place()
        sibling = (x, y, 1 - c)

        def blk(px, py, pc):
            return out.at[4 * px + 2 * py + pc]

        mine = pltpu.make_async_copy(src, blk(x, y, c), local_sem)
        mine.start()
        first = [_rcopy(src, blk(x, y, c), send_sems, recv_sems, 0, sibling)]
        first += [_rcopy(src, blk(x, y, c), send_sems, recv_sems, 1 + k, (cx, cy, c)) for k, (cx, cy) in enumerate(chips)]
        for cp in first:
            cp.start()
        passed = []
        for k, (cx, cy) in enumerate(chips):
            b = blk(cx, cy, c)
            _rcopy(b, b, send_sems, recv_sems, 1 + k, (cx, cy, c)).wait_recv()
            fw = _rcopy(b, b, send_sems, recv_sems, 4 + k, sibling)
            fw.start()
            passed.append(fw)
        b = blk(x, y, 1 - c)
        _rcopy(b, b, send_sems, recv_sems, 0, sibling).wait_recv()
        for k, (cx, cy) in enumerate(chips):
            b = blk(cx, cy, 1 - c)
            _rcopy(b, b, send_sems, recv_sems, 4 + k, sibling).wait_recv()
        for cp in first + passed:
            cp.wait_send()
        mine.wait()

    return pl.pallas_call(
        body, name=name, in_specs=[ANY], out_specs=ANY,
        out_shape=jax.ShapeDtypeStruct((8, rows, cols), pack.dtype),
        scratch_shapes=[pltpu.SemaphoreType.DMA((7,)), pltpu.SemaphoreType.DMA((7,)), pltpu.SemaphoreType.DMA],
    )(pack)


def _swap_halves(gs, *, name):
    n = len(gs)

    def body(*refs):
        srcs, outs, (send_sems, recv_sems) = refs[:n], refs[n:2 * n], refs[2 * n:]
        x, y, c, _ = _place()
        cps = [_rcopy(src.at[j, 1 - c], out.at[j], send_sems, recv_sems, N_CHIPS * t + j, (x, y, 1 - c))
               for t, (src, out) in enumerate(zip(srcs, outs)) for j in range(N_CHIPS)]
        for cp in cps:
            cp.start()
        for cp in cps:
            cp.wait()

    return pl.pallas_call(
        body, name=name, in_specs=[ANY] * n, out_specs=[ANY] * n,
        out_shape=[jax.ShapeDtypeStruct((N_CHIPS,) + g.shape[2:], g.dtype) for g in gs],
        scratch_shapes=[pltpu.SemaphoreType.DMA((N_CHIPS * n,)), pltpu.SemaphoreType.DMA((N_CHIPS * n,))],
    )(*gs)


def _join_halves(rs, *, name):
    n = len(rs)

    def body(*refs):
        srcs, outs, (send_sems, recv_sems) = refs[:n], refs[n:2 * n], refs[2 * n:]
        x, y, c, _ = _place()
        cps = [_rcopy(src.at[c], out.at[c], send_sems, recv_sems, t, (x, y, 1 - c))
               for t, (src, out) in enumerate(zip(srcs, outs))]
        for cp in cps:
            cp.start()
        for t, out in enumerate(outs):
            b = out.at[1 - c]
            _rcopy(b, b, send_sems, recv_sems, t, (x, y, 1 - c)).wait_recv()
        for cp in cps:
            cp.wait_send()

    return pl.pallas_call(
        body, name=name, in_specs=[ANY] * n, out_specs=[ANY] * n,
        out_shape=[jax.ShapeDtypeStruct(r.shape, r.dtype) for r in rs], input_output_aliases={t: t for t in range(n)},
        scratch_shapes=[pltpu.SemaphoreType.DMA((n,)), pltpu.SemaphoreType.DMA((n,))],
    )(*rs)


def _flat_rows(parts, cols):
    flat = jnp.concatenate([p.reshape(-1) for p in parts])
    n = flat.shape[0]
    rows = -(-n // cols)
    unit = 256 if rows > 256 else 8
    rows = unit * (-(-rows // unit))
    return jnp.pad(flat, (0, rows * cols - n)).reshape(rows, cols)


def _expand(v, di):
    return jnp.repeat(v, HEAD_DIM).reshape(1, di)


def kernel(x, norm_mix_g, norm_ffn_g, norm_final_g, cv_w_in, cv_b_in, cv_w_dw, cv_b_dw, cv_ln_g, cv_ln_b, cv_w_out, cv_b_out, ssm_w_in, ssm_w_conv, ssm_b_conv, ssm_dt_bias, ssm_a_log, ssm_d, ssm_norm_g, ssm_w_out, ffn_w_up, ffn_w_dw, ffn_b_dw, ffn_w_down, loss_target, m_norm_mix_g, m_norm_ffn_g, m_norm_final_g, m_cv_w_in, m_cv_b_in, m_cv_w_dw, m_cv_b_dw, m_cv_ln_g, m_cv_ln_b, m_cv_w_out, m_cv_b_out, m_ssm_w_in, m_ssm_w_conv, m_ssm_b_conv, m_ssm_dt_bias, m_ssm_a_log, m_ssm_d, m_ssm_norm_g, m_ssm_w_out, m_ffn_w_up, m_ffn_w_dw, m_ffn_b_dw, m_ffn_w_down, v_norm_mix_g, v_norm_ffn_g, v_norm_final_g, v_cv_w_in, v_cv_b_in, v_cv_w_dw, v_cv_b_dw, v_cv_ln_g, v_cv_ln_b, v_cv_w_out, v_cv_b_out, v_ssm_w_in, v_ssm_w_conv, v_ssm_b_conv, v_ssm_dt_bias, v_ssm_a_log, v_ssm_d, v_ssm_norm_g, v_ssm_w_out, v_ffn_w_up, v_ffn_w_dw, v_ffn_b_dw, v_ffn_w_down):
    weights = dict(norm_mix_g=norm_mix_g, norm_ffn_g=norm_ffn_g, norm_final_g=norm_final_g, cv_w_in=cv_w_in, cv_b_in=cv_b_in, cv_w_dw=cv_w_dw, cv_b_dw=cv_b_dw, cv_ln_g=cv_ln_g, cv_ln_b=cv_ln_b, cv_w_out=cv_w_out, cv_b_out=cv_b_out, ssm_w_in=ssm_w_in, ssm_w_conv=ssm_w_conv, ssm_b_conv=ssm_b_conv, ssm_dt_bias=ssm_dt_bias, ssm_a_log=ssm_a_log, ssm_d=ssm_d, ssm_norm_g=ssm_norm_g, ssm_w_out=ssm_w_out, ffn_w_up=ffn_w_up, ffn_w_dw=ffn_w_dw, ffn_b_dw=ffn_b_dw, ffn_w_down=ffn_w_down)
    mom_m = dict(norm_mix_g=m_norm_mix_g, norm_ffn_g=m_norm_ffn_g, norm_final_g=m_norm_final_g, cv_w_in=m_cv_w_in, cv_b_in=m_cv_b_in, cv_w_dw=m_cv_w_dw, cv_b_dw=m_cv_b_dw, cv_ln_g=m_cv_ln_g, cv_ln_b=m_cv_ln_b, cv_w_out=m_cv_w_out, cv_b_out=m_cv_b_out, ssm_w_in=m_ssm_w_in, ssm_w_conv=m_ssm_w_conv, ssm_b_conv=m_ssm_b_conv, ssm_dt_bias=m_ssm_dt_bias, ssm_a_log=m_ssm_a_log, ssm_d=m_ssm_d, ssm_norm_g=m_ssm_norm_g, ssm_w_out=m_ssm_w_out, ffn_w_up=m_ffn_w_up, ffn_w_dw=m_ffn_w_dw, ffn_b_dw=m_ffn_b_dw, ffn_w_down=m_ffn_w_down)
    mom_v = dict(norm_mix_g=v_norm_mix_g, norm_ffn_g=v_norm_ffn_g, norm_final_g=v_norm_final_g, cv_w_in=v_cv_w_in, cv_b_in=v_cv_b_in, cv_w_dw=v_cv_w_dw, cv_b_dw=v_cv_b_dw, cv_ln_g=v_cv_ln_g, cv_ln_b=v_cv_ln_b, cv_w_out=v_cv_w_out, cv_b_out=v_cv_b_out, ssm_w_in=v_ssm_w_in, ssm_w_conv=v_ssm_w_conv, ssm_b_conv=v_ssm_b_conv, ssm_dt_bias=v_ssm_dt_bias, ssm_a_log=v_ssm_a_log, ssm_d=v_ssm_d, ssm_norm_g=v_ssm_norm_g, ssm_w_out=v_ssm_w_out, ffn_w_up=v_ffn_w_up, ffn_w_dw=v_ffn_w_dw, ffn_b_dw=v_ffn_b_dw, ffn_w_down=v_ffn_w_down)
    names = list(weights)

    xt = x[0]
    tgt = loss_target[0]
    t, d = xt.shape
    depth = norm_mix_g.shape[0]
    n_cv, n_ssm = cv_w_in.shape[0], ssm_w_in.shape[0]
    di = ssm_w_out.shape[1] * N_CHIPS
    n_heads = di // HEAD_DIM
    gn = N_GROUPS * D_STATE
    ssm_in = ssm_w_in.shape[2] * N_CHIPS
    chip = 2 * lax.axis_index("x") + lax.axis_index("y")

    cq = ssm_w_in.shape[2]
    cqp = LANES * (-(-cq // LANES))
    ssm_inp = N_CHIPS * cqp
    by_col = ("cv_w_in", "ffn_w_up")
    big_names = ("cv_w_in", "cv_w_out", "ssm_w_in", "ssm_w_out", "ffn_w_up", "ffn_w_down")

    def layer_tensors(i):
        mixer = [("cv_w_in", i // 2), ("cv_w_out", i // 2)] if i % 2 == 0 else [("ssm_w_in", i // 2), ("ssm_w_out", i // 2)]
        return mixer + [("ffn_w_up", i), ("ffn_w_down", i)]

    def halves(a):
        return a.reshape((2, a.shape[0] // 2) + a.shape[1:])

    order = [key for i in range(depth) for key in layer_tensors(i)]
    shards = []
    for nm, l in order:
        w = weights[nm][l]
        if nm == "ssm_w_in":
            w = jnp.pad(w, ((0, 0), (0, cqp - cq)))
        shards.append(halves(w.astype(BF16)))
    shards = dict(zip(order, lax.optimization_barrier(shards)))
    full = {}

    def arrived(key, g):
        g = lax.dynamic_update_index_in_dim(g, shards[key], chip, 0)
        g = g.reshape((N_CHIPS, 2 * g.shape[2], g.shape[3]))
        if key[0] == "ssm_w_in":
            full[key] = jnp.concatenate([g[jj, :, :cq] for jj in range(N_CHIPS)] + [jnp.zeros((d, ssm_inp - ssm_in), BF16)],
                                        axis=1)
        else:
            full[key] = g if key[0] in by_col else g.reshape(N_CHIPS * g.shape[1], g.shape[2])

    for key, g in zip(order[:2], _gather_chips([shards[k] for k in order[:2]], name="gather_weights")):
        arrived(key, g)
    landed = {}

    def mm_fwd(key, a, **kw):
        pos = order.index(key)
        kw["b_chips"] = key[0] in by_col
        jobs = []
        if pos + 1 in landed:
            jobs.append(("pass", landed.pop(pos + 1)))
        if pos + 2 < len(order):
            jobs.append(("gather", shards[order[pos + 2]]))
        if not jobs:
            return _matmul(a, full[key], **kw)
        out, *got = _matmul(a, full[key], carry=jobs, **kw)
        for (kind, _), g in zip(jobs, got):
            if kind == "pass":
                arrived(order[pos + 1], g)
            else:
                landed[pos + 2] = g
        return out

    expand = _head_expander(n_heads)

    small_sharded = ["cv_w_dw", "ssm_w_conv", "ssm_b_conv", "ssm_norm_g", "ffn_w_dw"]
    spack = _flat_rows([weights[nm] for nm in small_sharded], LANES)
    sg = _gather_devices(spack, name="gather_small").reshape(8, -1)[::2]
    o = 0
    for nm in small_sharded:
        shp = weights[nm].shape
        n = weights[nm].size
        full[nm] = jnp.concatenate([sg[j, o:o + n].reshape(shp) for j in range(N_CHIPS)], axis=-1)
        o += n

    row = lambda v: v.reshape(1, -1)

    saved = []
    xc = xt
    for i in range(depth):
        j = i // 2
        s = {"x_in": xc}
        h = _rms_fwd(xc, row(norm_mix_g[i]), name="rms_mix_fwd")
        s["h"] = h
        if i % 2 == 0:
            u = mm_fwd(("cv_w_in", j), h, name="cv_in_fwd")
            v1 = _glu_fwd(u, row(cv_b_in[j]), name="cv_glu_fwd")
            v2 = _dwconv_fwd(v1, full["cv_w_dw"][j], row(cv_b_dw[j]), name="cv_dw_fwd")
            v4 = _ln_silu_fwd(v2, row(cv_ln_g[j]), row(cv_ln_b[j]), name="cv_ln_fwd")
            xc = mm_fwd(("cv_w_out", j), v4, bias=row(cv_b_out[j]), res=xc, name="cv_out_fwd")
            s.update(u=u, v1=v1, v2=v2, v4=v4)
        else:
            zx = mm_fwd(("ssm_w_in", j), h, name="ssm_in_fwd")
            xbc_cols = (di, di + 2 * gn)
            dtp = zx[:, 2 * di + 2 * gn:ssm_in]
            conv = _dwconv_fwd(zx, full["ssm_w_conv"][j], row(full["ssm_b_conv"][j]), cols=xbc_cols, name="ssm_dw_fwd")
            bias_exp = _expand(ssm_dt_bias[j], di)
            aneg_exp = _expand(-jnp.exp(ssm_a_log[j]), di)
            d_exp = _expand(ssm_d[j], di)
            xs, bm, cm, dt_exp, cs_exp = _ssm_act(conv, dtp, bias_exp, aneg_exp, expand, di=di, name="ssm_act_fwd")
            cs_rows = cs_exp[:, ::HEAD_DIM].T.reshape(N_GROUPS, n_heads // N_GROUPS, t)
            y_ssd, states = _ssd_fwd(xs, dt_exp, cs_exp, cs_rows, bm, cm, name="ssd_fwd")
            gnrm = _gated_norm_fwd(y_ssd, xs, zx, d_exp, row(full["ssm_norm_g"][j]), name="ssm_norm_fwd")
            xc = mm_fwd(("ssm_w_out", j), gnrm, res=xc, name="ssm_out_fwd")
            s.update(zx=zx, xbc_cols=xbc_cols, dtp=dtp, conv=conv, bias_exp=bias_exp, aneg_exp=aneg_exp,
                     d_exp=d_exp, xs=xs, bm=bm, cm=cm, dt_exp=dt_exp, cs_exp=cs_exp, cs_rows=cs_rows, y_ssd=y_ssd,
                     states=states, gnrm=gnrm)
        s["x_mid"] = xc
        h2 = _rms_fwd(xc, row(norm_ffn_g[i]), name="rms_ffn_fwd")
        u2 = mm_fwd(("ffn_w_up", i), h2, out_chips=True, name="ffn_up_fwd")
        hm = _ffn_mid_fwd(u2, full["ffn_w_dw"][i], row(ffn_b_dw[i]), name="ffn_mid_fwd")
        xc = mm_fwd(("ffn_w_down", i), hm, res=xc, name="ffn_down_fwd")
        s.update(h2=h2, u2=u2, hm=hm)
        saved.append(s)

    dx, dxb, sq, dg_final = _loss_head(xc, row(norm_final_g), tgt, name="loss_head")
    loss_part = 0.5 / d * jnp.sum(sq)
    gr = {nm: [None] * weights[nm].shape[0] for nm in names if nm != "norm_final_g"}
    reduced = {}
    sel_c = jnp.reshape(lax.axis_index("c"), (1,)).astype(jnp.int32)
    sel_j = jnp.stack([chip, lax.axis_index("c")]).astype(jnp.int32)

    def backward_pair(key, act, dout, dw_name, dx_name, dout_chips=False):
        col = key[0] in by_col
        g = _matmul(act, dout, ta=True, b_chips=dout_chips, out_chips=col, out_dtype=BF16, name=dw_name)
        if key[0] == "ssm_w_in":
            g = jnp.stack([jnp.pad(g[:, jj * cq:(jj + 1) * cq], ((0, 0), (0, cqp - cq))) for jj in range(N_CHIPS)])
        elif not col:
            g = g.reshape(N_CHIPS, g.shape[0] // N_CHIPS, g.shape[1])
        g = g.reshape(N_CHIPS, 2, g.shape[1] // 2, g.shape[2])
        (got,) = _swap_halves([g], name="grads_swap")
        pair = _add_pair(sel_c, g, got, name="grads_add2")
        dact, got3 = _matmul(dout, full[key], tb=True, a_chips=dout_chips, b_chips=col, carry=[("scatter", pair)],
                             name=dx_name)
        half = _add_four(sel_j, pair, got3, name="grads_add4")
        (r,) = _join_halves([half], name="grads_join")
        reduced[key] = r.reshape(2 * r.shape[1], r.shape[2])[:, :weights[key[0]].shape[2]]
        return dact

    for i in reversed(range(depth)):
        j = i // 2
        s = saved[i]
        dhm = backward_pair(("ffn_w_down", i), s["hm"], dxb, "ffn_down_dw", "ffn_down_dx")
        du2b, dw_dw, db_dw = _ffn_mid_bwd(s["u2"], dhm, full["ffn_w_dw"][i], row(ffn_b_dw[i]), name="ffn_mid_bwd")
        gr["ffn_w_dw"][i], gr["ffn_b_dw"][i] = dw_dw, db_dw[0]
        dh2 = backward_pair(("ffn_w_up", i), s["h2"], du2b, "ffn_up_dw", "ffn_up_dx", dout_chips=True)
        dx, dxb, colsum, dg = _rms_bwd(s["x_mid"], row(norm_ffn_g[i]), dh2, dx, name="rms_ffn_bwd")
        gr["norm_ffn_g"][i] = dg[0]
        if i % 2 == 0:
            gr["cv_b_out"][j] = colsum[0]
            dv4 = backward_pair(("cv_w_out", j), s["v4"], dxb, "cv_out_dw", "cv_out_dx")
            dv2, dlg, dlb = _ln_silu_bwd(s["v2"], row(cv_ln_g[j]), row(cv_ln_b[j]), dv4, name="cv_ln_bwd")
            gr["cv_ln_g"][j], gr["cv_ln_b"][j] = dlg[0], dlb[0]
            dv1, dw_dw, db_dw = _dwconv_bwd(s["v1"], dv2, full["cv_w_dw"][j], name="cv_dw_bwd")
            gr["cv_w_dw"][j], gr["cv_b_dw"][j] = dw_dw, db_dw[0]
            du, db_in = _glu_bwd(s["u"], row(cv_b_in[j]), dv1, name="cv_glu_bwd")
            gr["cv_b_in"][j] = db_in[0]
            dh = backward_pair(("cv_w_in", j), s["h"], du, "cv_in_dw", "cv_in_dx")
        else:
            dgn = backward_pair(("ssm_w_out", j), s["gnrm"], dxb, "ssm_out_dw", "ssm_out_dx")
            dy, dz, dng, ddl = _gated_norm_bwd(s["y_ssd"], s["xs"], s["zx"], s["d_exp"], row(full["ssm_norm_g"][j]),
                                               dgn, name="ssm_norm_bwd")
            gr["ssm_norm_g"][j] = dng[0]
            gr["ssm_d"][j] = ddl.reshape(n_heads, HEAD_DIM).sum(axis=1)
            dxp, dbm, dcm, ddt_exp, dan = _ssd_bwd(s["xs"], s["dt_exp"], s["cs_exp"], s["cs_rows"], s["bm"], s["cm"],
                                                   dy, s["states"], s["aneg_exp"], name="ssd_bwd")
            gr["ssm_a_log"][j] = dan[0, ::HEAD_DIM] * s["aneg_exp"][0, ::HEAD_DIM]
            dconv, ddtp, dbias = _ssm_act_bwd(s["conv"], s["dtp"], dxp, dy, dbm, dcm, ddt_exp, s["dt_exp"],
                                              s["bias_exp"], s["d_exp"], expand, di=di, name="ssm_act_bwd")
            gr["ssm_dt_bias"][j] = dbias[0, ::HEAD_DIM]
            dxbc, dw_c, db_c = _dwconv_bwd(s["zx"], dconv, full["ssm_w_conv"][j], dx_dtype=BF16, cols=s["xbc_cols"],
                                           name="ssm_dw_bwd")
            gr["ssm_w_conv"][j], gr["ssm_b_conv"][j] = dw_c, db_c[0]
            dzx = jnp.concatenate([dz, dxbc, ddtp[:, ::HEAD_DIM].astype(BF16), jnp.zeros((t, ssm_inp - ssm_in), BF16)],
                                  axis=1)
            dh = backward_pair(("ssm_w_in", j), s["h"], dzx, "ssm_in_dw", "ssm_in_dx")
        dx, dxb, _, dg = _rms_bwd(s["x_in"], row(norm_mix_g[i]), dh, dx, name="rms_mix_bwd")
        gr["norm_mix_g"][i] = dg[0]

    grads = {nm: jnp.stack([reduced[nm, l] for l in range(weights[nm].shape[0])]) for nm in big_names}

    small = [nm for nm in names if nm not in grads]
    small_parts = []
    for nm in small:
        small_parts.append(dg_final[0] if nm == "norm_final_g" else jnp.stack(gr[nm]))
    gs_pack = _flat_rows(small_parts + [loss_part.reshape(1)], LANES)
    gs = _sum8(_gather_devices(gs_pack, name="gather_small_grads"), name="sum_small_grads").reshape(-1)
    o = 0
    for nm, p in zip(small, small_parts):
        gfull = gs[o:o + p.size].reshape(p.shape)
        o += p.size
        if nm in small_sharded:
            width = weights[nm].shape[-1]
            gfull = lax.dynamic_slice_in_dim(gfull, chip * width, width, axis=gfull.ndim - 1)
        grads[nm] = gfull
    loss = gs[o]

    delta, new_m, new_v = {}, {}, {}
    for nm in big_names:
        shp = weights[nm].shape
        as2d = lambda a: a.reshape(-1, shp[-1])
        dl, mn, vn = _adamw(as2d(weights[nm]), as2d(grads[nm]), as2d(mom_m[nm]), as2d(mom_v[nm]), name="adamw_" + nm)
        delta[nm], new_m[nm], new_v[nm] = dl.reshape(shp), mn.reshape(shp), vn.reshape(shp)
    pk = lambda dct: _flat_rows([dct[nm] for nm in small], LANES)
    dl, mn, vn = _adamw(pk(weights), pk(grads), pk(mom_m), pk(mom_v), name="adamw_small")
    dl, mn, vn = dl.reshape(-1), mn.reshape(-1), vn.reshape(-1)
    o = 0
    for nm in small:
        shp, n = weights[nm].shape, weights[nm].size
        delta[nm], new_m[nm], new_v[nm] = (a[o:o + n].reshape(shp) for a in (dl, mn, vn))
        o += n

    return (loss, dx[None], *[grads[nm] for nm in names], *[delta[nm] for nm in names],
            *[new_m[nm] for nm in names], *[new_v[nm] for nm in names])
```

```python
import math

import jax
import jax.numpy as jnp
from jax import lax
from jax.experimental import pallas as pl
from jax.experimental.pallas import tpu as pltpu

F32, BF16 = jnp.float32, jnp.bfloat16
MESH = pl.DeviceIdType.MESH
ANY = pl.BlockSpec(memory_space=pl.ANY)

RMS_EPS = 1e-6
LN_EPS = 1e-5
HEAD_DIM = 64
N_GROUPS = 8
D_STATE = 128
ADAM_LR, ADAM_B1, ADAM_B2, ADAM_EPS, ADAM_WD, ADAM_STEP = 0.001, 0.9, 0.999, 1e-08, 0.01, 10

VMEM_LIMIT_BYTES = 56 * 1024 * 1024
LANES = 128
SSD_CHUNK = 128
ADAMW_BLOCK_ELEMS = 512 * 1024
N_CHIPS = 4


def _cparams(**kw):
    return pltpu.CompilerParams(vmem_limit_bytes=VMEM_LIMIT_BYTES, **kw)


def _pick(n, prefs):
    for p in prefs:
        if n % p == 0:
            return p
    return n


def _sigmoid(x):
    return 1.0 / (1.0 + jnp.exp(-x))


def _silu(x):
    return x * _sigmoid(x)


def _dsilu(x):
    s = _sigmoid(x)
    return s * (1.0 + x * (1.0 - s))


def _rowsum(x):
    return jnp.sum(x, axis=0, keepdims=True)


MM_TILES = (2816, 2688, 2048, 1408, 1024, 896, 512, 384, 256, 128)
MM_VMEM_BUDGET = 40 * 1024 * 1024


def _mm_tiles(m, n, k, n_unit, k_unit, out_bytes, has_res):
    best = None
    for tk in [t for t in MM_TILES if k_unit % t == 0]:
        for tm in [t for t in (1024, 512, 256, 128) if m % t == 0] or [m]:
            for tn in [t for t in MM_TILES if n_unit % t == 0]:
                vmem = 2 * 2 * (tm * tk + tk * tn) + 2 * tm * tn * out_bytes
                vmem += tm * tn * 4 if k // tk > 1 else 0
                vmem += 2 * tm * tn * 4 if has_res else 0
                if vmem > MM_VMEM_BUDGET:
                    continue
                traffic = m * k * (n // tn) + k * n * (m // tm)
                if best is None or traffic < best[0]:
                    best = (traffic, tm, tn, tk)
        if best is not None:
            return best[1:]
    raise ValueError((m, n, k))


def _matmul(a, b, *, name, ta=False, tb=False, a_chips=False, b_chips=False, out_chips=False, out_dtype=F32, bias=None,
            res=None, carry=None):
    if a_chips:
        assert tb and b_chips and not ta
        m, k = a.shape[1], N_CHIPS * a.shape[2]
    else:
        m, k = (a.shape[1], a.shape[0]) if ta else a.shape
    if b_chips:
        nq = b.shape[2]
        n = b.shape[1] if tb else N_CHIPS * nq
        assert k == (N_CHIPS * nq if tb else b.shape[1])
    else:
        n = b.shape[0] if tb else b.shape[1]
        assert k == (b.shape[1] if tb else b.shape[0])
        nq = n // N_CHIPS
    has_bias, has_res = bias is not None, res is not None
    tm, tn, tk = _mm_tiles(m, n, k, nq if ((b_chips and not tb) or out_chips) else n, nq if (b_chips and tb) else k,
                           jnp.dtype(out_dtype).itemsize, has_res)
    gm, gn, nk = m // tm, n // tn, k // tk
    nbq = nq // (tk if tb else tn) if (b_chips or out_chips) else 1
    dn = (((0 if ta else 1,), (1 if tb else 0,)), ((), ()))
    carry = carry or []
    nj = len(carry)

    def body(*refs):
        a_ref, b_ref = refs[0], refs[1]
        rest = list(refs[2:])
        bias_ref = rest.pop(0) if has_bias else None
        res_ref = rest.pop(0) if has_res else None
        srcs = [rest.pop(0) for _ in range(nj)]
        o_ref = rest.pop(0)
        dsts = [rest.pop(0) for _ in range(nj)]
        acc_ref = rest.pop(0) if nk > 1 else None
        i, j, kk = pl.program_id(0), pl.program_id(1), pl.program_id(2)

        if carry:
            send_sems, recv_sems = rest
            x, y, c, chips = _place()
            sibling = (x, y, 1 - c)
            cps, landing = [], []
            sem = 0
            for (kind, _), src, dst in zip(carry, srcs, dsts):
                if kind == "swap":
                    for jj in range(N_CHIPS):
                        cps.append(_rcopy(src.at[jj, 1 - c], dst.at[jj], send_sems, recv_sems, sem, sibling))
                        landing.append((dst.at[jj], sem, sibling))
                        sem += 1
                    continue
                if kind == "join":
                    cps.append(_rcopy(src.at[c], dst.at[c], send_sems, recv_sems, sem, sibling))
                    landing.append((dst.at[1 - c], sem, sibling))
                    sem += 1
                    continue
                for q, (cx, cy) in enumerate(chips):
                    if kind == "gather":
                        cps.append(_rcopy(src.at[c], dst.at[2 * x + y, c], send_sems, recv_sems, sem, (cx, cy, c)))
                        landing.append((dst.at[2 * cx + cy, c], sem, (cx, cy, c)))
                    elif kind == "pass":
                        cps.append(_rcopy(src.at[2 * cx + cy, c], dst.at[2 * cx + cy, c], send_sems, recv_sems, sem, sibling))
                        landing.append((dst.at[2 * cx + cy, 1 - c], sem, sibling))
                    else:
                        cps.append(_rcopy(src.at[2 * cx + cy], dst.at[q], send_sems, recv_sems, sem, (cx, cy, c)))
                        landing.append((dst.at[q], sem, (cx, cy, c)))
                    sem += 1

            @pl.when((i == 0) & (j == 0) & (kk == 0))
            def _():
                for cp in cps:
                    cp.start()

        def finish(r):
            if has_bias:
                r = r + bias_ref[...]
            if has_res:
                r = r + res_ref[...]
            o_ref[...] = r.astype(o_ref.dtype)

        part = lax.dot_general(a_ref[...].astype(BF16), b_ref[...].astype(BF16), dn, preferred_element_type=F32)
        if nk == 1:
            finish(part)
        else:
            @pl.when(kk == 0)
            def _():
                acc_ref[...] = part

            @pl.when(kk > 0)
            def _():
                acc_ref[...] += part

            @pl.when(kk == nk - 1)
            def _():
                finish(acc_ref[...])

        if carry:
            @pl.when((i == gm - 1) & (j == gn - 1) & (kk == nk - 1))
            def _():
                for blk, sem, frm in landing:
                    _rcopy(blk, blk, send_sems, recv_sems, sem, frm).wait_recv()
                for cp in cps:
                    cp.wait_send()

    if a_chips:
        a_spec = pl.BlockSpec((None, tm, tk), lambda i, j, kk: (kk // nbq, i, kk % nbq))
    elif ta:
        a_spec = pl.BlockSpec((tk, tm), lambda i, j, kk: (kk, i))
    else:
        a_spec = pl.BlockSpec((tm, tk), lambda i, j, kk: (i, kk))
    if b_chips and tb:
        b_spec = pl.BlockSpec((None, tn, tk), lambda i, j, kk: (kk // nbq, j, kk % nbq))
    elif b_chips:
        b_spec = pl.BlockSpec((None, tk, tn), lambda i, j, kk: (j // nbq, kk, j % nbq))
    elif tb:
        b_spec = pl.BlockSpec((tn, tk), lambda i, j, kk: (j, kk))
    else:
        b_spec = pl.BlockSpec((tk, tn), lambda i, j, kk: (kk, j))
    if out_chips:
        out_spec = pl.BlockSpec((None, tm, tn), lambda i, j, kk: (j // nbq, i, j % nbq))
        out_shape = jax.ShapeDtypeStruct((N_CHIPS, m, nq), out_dtype)
    else:
        out_spec = pl.BlockSpec((tm, tn), lambda i, j, kk: (i, j))
        out_shape = jax.ShapeDtypeStruct((m, n), out_dtype)
    in_specs, args = [a_spec, b_spec], [a, b]
    if has_bias:
        in_specs.append(pl.BlockSpec((1, tn), lambda i, j, kk: (0, j)))
        args.append(bias)
    if has_res:
        in_specs.append(pl.BlockSpec((tm, tn), lambda i, j, kk: (i, j)))
        args.append(res)
    scratch = [pltpu.VMEM((tm, tn), F32)] if nk > 1 else []
    if not carry:
        return pl.pallas_call(
            body, name=name, grid=(gm, gn, nk), in_specs=in_specs, out_specs=out_spec, out_shape=out_shape,
            scratch_shapes=scratch,
            compiler_params=_cparams(dimension_semantics=("parallel", "parallel", "arbitrary")),
        )(*args)
    lands, aliases, n_sems = [], {}, 0
    for jb, (kind, moved) in enumerate(carry):
        shape = {"gather": (N_CHIPS,) + moved.shape, "pass": moved.shape, "scatter": (3,) + moved.shape[1:],
                 "swap": (N_CHIPS,) + moved.shape[2:], "join": moved.shape}[kind]
        lands.append(jax.ShapeDtypeStruct(shape, moved.dtype))
        n_sems += {"swap": N_CHIPS, "join": 1}.get(kind, 3)
        if kind in ("pass", "join"):
            aliases[len(args) + jb] = 1 + jb
    return pl.pallas_call(
        body, name=name, grid=(gm, gn, nk), in_specs=in_specs + [ANY] * nj, out_specs=[out_spec] + [ANY] * nj,
        out_shape=[out_shape] + lands, input_output_aliases=aliases,
        scratch_shapes=scratch + [pltpu.SemaphoreType.DMA((n_sems,)), pltpu.SemaphoreType.DMA((n_sems,))],
        compiler_params=_cparams(dimension_semantics=("arbitrary", "arbitrary", "arbitrary")),
    )(*args, *[moved for _, moved in carry])


def _rowwise(fn, rows, pars, outs, reds, *, tm, name):
    rows = [r if isinstance(r, tuple) else (r, r.shape[1], 0) for r in rows]
    t = rows[0][0].shape[0]
    assert t % tm == 0
    n_in, n_o = len(rows) + len(pars), len(outs)

    def body(*refs):
        i = pl.program_id(0)
        o, d = fn(*[r[...] for r in refs[:n_in]])
        for ref, val in zip(refs[n_in:n_in + n_o], o):
            ref[...] = val.astype(ref.dtype)
        d_refs = refs[n_in + n_o:]

        @pl.when(i == 0)
        def _():
            for ref in d_refs:
                ref[...] = jnp.zeros_like(ref)

        for ref, val in zip(d_refs, d):
            ref[...] += val

    in_specs = [pl.BlockSpec((tm, w), lambda i, b=blk: (i, b)) for _, w, blk in rows]
    in_specs += [pl.BlockSpec(p.shape, lambda i: (0, 0)) for p in pars]
    out_specs = [pl.BlockSpec((tm, c), lambda i: (i, 0)) for c, _ in outs]
    out_specs += [pl.BlockSpec((1, c), lambda i: (0, 0)) for c in reds]
    out_shape = [jax.ShapeDtypeStruct((t, c), dt) for c, dt in outs] + [jax.ShapeDtypeStruct((1, c), F32) for c in reds]
    res = pl.pallas_call(
        body, name=name, grid=(t // tm,), in_specs=in_specs, out_specs=out_specs, out_shape=out_shape,
        compiler_params=_cparams(dimension_semantics=("arbitrary",)),
    )(*[r[0] for r in rows], *pars)
    return res[:n_o], res[n_o:]


def _rms_fwd(x, g, *, name):
    def fn(xb, gb):
        r = lax.rsqrt(jnp.mean(xb * xb, axis=-1, keepdims=True) + RMS_EPS)
        return [xb * r * gb], []
    (h,), _ = _rowwise(fn, [x], [g], [(x.shape[1], BF16)], [], tm=256, name=name)
    return h


def _rms_bwd(x, g, dh, dres, *, name):
    def fn(xb, dhb, drb, gb):
        r = lax.rsqrt(jnp.mean(xb * xb, axis=-1, keepdims=True) + RMS_EPS)
        xh = xb * r
        dxh = dhb * gb
        dx = r * (dxh - xh * jnp.mean(dxh * xh, axis=-1, keepdims=True))
        out = drb + dx
        return [out, out], [_rowsum(out), _rowsum(dhb * xh)]
    c = x.shape[1]
    (dx, dxb), (colsum, dg) = _rowwise(fn, [x, dh, dres], [g], [(c, F32), (c, BF16)], [c, c], tm=256, name=name)
    return dx, dxb, colsum, dg


def _loss_head(x, g, tgt, *, name):
    d_model = x.shape[1]

    def fn(xb, tb, gb):
        r = lax.rsqrt(jnp.mean(xb * xb, axis=-1, keepdims=True) + RMS_EPS)
        xh = xb * r
        e = xh * gb - tb
        dy = e * (1.0 / d_model)
        dxh = dy * gb
        dx = r * (dxh - xh * jnp.mean(dxh * xh, axis=-1, keepdims=True))
        return [dx, dx], [_rowsum(e * e), _rowsum(dy * xh)]
    (dx, dxb), (sq, dg) = _rowwise(fn, [x, tgt], [g], [(d_model, F32), (d_model, BF16)], [d_model, d_model], tm=256,
                                   name=name)
    return dx, dxb, sq, dg


def _halo_rows(k):
    return 8 * ((k - 1 + 7) // 8) if k > 1 else 8


def _pad_taps(w):
    k = w.shape[0]
    kp = 8 * ((k + 7) // 8)
    return jnp.pad(w, ((0, kp - k), (0, 0)))


def _conv_chunks(k, tc):
    rc, lw = (32, 256) if k > 9 else (16, 512)
    lanes, l0 = [], 0
    while l0 < tc:
        lanes.append((l0, min(lw, tc - l0)))
        l0 += lw
    return rc, lanes


def _fold8(v):
    acc = v[0:8]
    for q in range(1, v.shape[0] // 8):
        acc = acc + v[8 * q:8 * q + 8]
    return acc


def _taps(src_ref, lead, base, rc, lanes, w_ref, k, sign, acc):
    for s in range(k):
        rows = pl.ds(base + sign * s, rc)
        acc = acc + w_ref[k - 1 - s:k - s, lanes] * src_ref[lead + (rows, lanes)]
    return acc


def _conv_cols(x, k, cols):
    c0, c = cols if cols else (0, x.shape[1])
    tc = _pick(math.gcd(c0, c), (512, 256, 128) if k > 9 else (2048, 1536, 1024, 512, 256, 128))
    return c0, c, tc


def _dwconv_fwd(x, w, b, *, name, cols=None):
    t = x.shape[0]
    k = w.shape[0]
    h = _halo_rows(k)
    tm = _pick(t, (256, 128))
    c0, c, tc = _conv_cols(x, k, cols)
    cb0 = c0 // tc
    wp = _pad_taps(w)
    kp = wp.shape[0]
    rb = tm // h
    rc, lane_chunks = _conv_chunks(k, tc)

    def body(x_ref, p_ref, w_ref, b_ref, o_ref, ext):
        i = pl.program_id(0)
        ext[pl.ds(h, tm), :] = x_ref[...]
        ext[pl.ds(0, h), :] = jnp.where(i > 0, p_ref[...], 0.0)
        for l0, lw in lane_chunks:
            lanes = pl.ds(l0, lw)
            for r0 in range(0, tm, rc):
                acc = jnp.broadcast_to(b_ref[:, lanes], (rc, lw))
                o_ref[pl.ds(r0, rc), lanes] = _taps(ext, (), h + r0, rc, lanes, w_ref, k, -1, acc)

    return pl.pallas_call(
        body, name=name, grid=(t // tm, c // tc),
        in_specs=[pl.BlockSpec((tm, tc), lambda i, j: (i, cb0 + j)),
                  pl.BlockSpec((h, tc), lambda i, j: (jnp.maximum(i * rb - 1, 0), cb0 + j)),
                  pl.BlockSpec((kp, tc), lambda i, j: (0, j)),
                  pl.BlockSpec((1, tc), lambda i, j: (0, j))],
        out_specs=pl.BlockSpec((tm, tc), lambda i, j: (i, j)),
        out_shape=jax.ShapeDtypeStruct((t, c), F32),
        scratch_shapes=[pltpu.VMEM((h + tm, tc), F32)],
        compiler_params=_cparams(dimension_semantics=("parallel", "parallel")),
    )(x, x, wp, b)


def _dwconv_bwd(x, dy, w, *, name, dx_dtype=F32, cols=None):
    t = x.shape[0]
    k = w.shape[0]
    h = _halo_rows(k)
    tm = _pick(t, (256, 128))
    c0, c, tc = _conv_cols(x, k, cols)
    cb0 = c0 // tc
    wp = _pad_taps(w)
    kp = wp.shape[0]
    rb = tm // h
    nt = t // tm
    rc, lane_chunks = _conv_chunks(k, tc)

    def body(x_ref, p_ref, dy_ref, n_ref, w_ref, dx_ref, dw_ref, db_ref, xext, dext):
        i = pl.program_id(1)

        @pl.when(i == 0)
        def _():
            dw_ref[...] = jnp.zeros_like(dw_ref)
            db_ref[...] = jnp.zeros_like(db_ref)

        xext[pl.ds(h, tm), :] = x_ref[...]
        xext[pl.ds(0, h), :] = jnp.where(i > 0, p_ref[...], 0.0)
        dext[pl.ds(0, tm), :] = dy_ref[...]
        dext[pl.ds(tm, h), :] = jnp.where(i < nt - 1, n_ref[...], 0.0)
        for l0, lw in lane_chunks:
            lanes = pl.ds(l0, lw)
            for r0 in range(0, tm, rc):
                acc = _taps(dext, (), r0, rc, lanes, w_ref, k, +1, jnp.zeros((rc, lw), F32))
                dx_ref[pl.ds(r0, rc), lanes] = acc.astype(dx_ref.dtype)
            for s in range(k):
                a8 = jnp.zeros((8, lw), F32)
                for r0 in range(0, tm, rc):
                    a8 = a8 + _fold8(xext[pl.ds(h + r0 - s, rc), lanes] * dext[pl.ds(r0, rc), lanes])
                dw_ref[k - 1 - s:k - s, lanes] += _rowsum(a8)
            b8 = jnp.zeros((8, lw), F32)
            for r0 in range(0, tm, rc):
                b8 = b8 + _fold8(dext[pl.ds(r0, rc), lanes])
            db_ref[:, lanes] += _rowsum(b8)

    dx, dw, db = pl.pallas_call(
        body, name=name, grid=(c // tc, nt),
        in_specs=[pl.BlockSpec((tm, tc), lambda j, i: (i, cb0 + j)),
                  pl.BlockSpec((h, tc), lambda j, i: (jnp.maximum(i * rb - 1, 0), cb0 + j)),
                  pl.BlockSpec((tm, tc), lambda j, i: (i, j)),
                  pl.BlockSpec((h, tc), lambda j, i: (jnp.minimum((i + 1) * rb, nt * rb - 1), j)),
                  pl.BlockSpec((kp, tc), lambda j, i: (0, j))],
        out_specs=[pl.BlockSpec((tm, tc), lambda j, i: (i, j)),
                   pl.BlockSpec((kp, tc), lambda j, i: (0, j)),
                   pl.BlockSpec((1, tc), lambda j, i: (0, j))],
        out_shape=[jax.ShapeDtypeStruct((t, c), dx_dtype), jax.ShapeDtypeStruct((kp, c), F32),
                   jax.ShapeDtypeStruct((1, c), F32)],
        scratch_shapes=[pltpu.VMEM((h + tm, tc), F32), pltpu.VMEM((tm + h, tc), F32)],
        compiler_params=_cparams(dimension_semantics=("parallel", "arbitrary")),
    )(x, x, dy, dy, wp)
    return dx, dw[:k], db


def _glu_fwd(u, b_in, *, name):
    d = u.shape[1] // 2

    def fn(ua, ug, ba, bg):
        return [(ua + ba) * _sigmoid(ug + bg)], []
    (v,), _ = _rowwise(fn, [(u, d, 0), (u, d, 1)], [b_in[:, :d], b_in[:, d:]], [(d, F32)], [], tm=256, name=name)
    return v


def _glu_bwd(u, b_in, dv, *, name):
    d = u.shape[1] // 2

    def fn(ua, ug, dvb, ba, bg):
        a = ua + ba
        s = _sigmoid(ug + bg)
        du = jnp.concatenate([dvb * s, dvb * a * s * (1.0 - s)], axis=1)
        return [du], [_rowsum(du)]
    (du,), (db,) = _rowwise(fn, [(u, d, 0), (u, d, 1), dv], [b_in[:, :d], b_in[:, d:]], [(2 * d, BF16)], [2 * d],
                            tm=256, name=name)
    return du, db


def _ln_silu_fwd(v, g, b, *, name):
    def fn(vb, gb, bb):
        mu = jnp.mean(vb, axis=-1, keepdims=True)
        xc = vb - mu
        rstd = lax.rsqrt(jnp.mean(xc * xc, axis=-1, keepdims=True) + LN_EPS)
        return [_silu(xc * rstd * gb + bb)], []
    (o,), _ = _rowwise(fn, [v], [g, b], [(v.shape[1], BF16)], [], tm=256, name=name)
    return o


def _ln_silu_bwd(v, g, b, do, *, name):
    def fn(vb, dob, gb, bb):
        mu = jnp.mean(vb, axis=-1, keepdims=True)
        xc = vb - mu
        rstd = lax.rsqrt(jnp.mean(xc * xc, axis=-1, keepdims=True) + LN_EPS)
        xh = xc * rstd
        dy = dob * _dsilu(xh * gb + bb)
        dxh = dy * gb
        dv = rstd * (dxh - jnp.mean(dxh, axis=-1, keepdims=True) - xh * jnp.mean(dxh * xh, axis=-1, keepdims=True))
        return [dv], [_rowsum(dy * xh), _rowsum(dy)]
    c = v.shape[1]
    (dv,), (dg, db) = _rowwise(fn, [v, do], [g, b], [(c, F32)], [c, c], tm=256, name=name)
    return dv, dg, db


def _ffn_mid_setup(u2c, w, b):
    _, t, nq = u2c.shape
    f = 2 * nq
    k = w.shape[0]
    h = _halo_rows(k)
    tm = _pick(t, (256, 128))
    tc = _pick(nq, (1408, 1024, 512, 256, 128))
    rc, lane_chunks = _conv_chunks(k, tc)
    return dict(t=t, nq=nq, f=f, k=k, h=h, tm=tm, tc=tc, npq=nq // tc, rb=tm // h, nt=t // tm, rc=rc,
                lane_chunks=lane_chunks, u4=u2c.reshape(2, 2, t, nq), wg=_pad_taps(w[:, :f]), wv=_pad_taps(w[:, f:]),
                bg=b[:, :f], bv=b[:, f:])


def _ffn_mid_fwd(u2c, w, b, *, name):
    p = _ffn_mid_setup(u2c, w, b)
    t, f, k, h, tm, tc, npq, rb, rc = (p[n] for n in ("t", "f", "k", "h", "tm", "tc", "npq", "rb", "rc"))
    kp = p["wg"].shape[0]

    def body(u_ref, p_ref, wg_ref, wv_ref, bg_ref, bv_ref, o_ref, ext):
        i = pl.program_id(2)
        for kind in range(2):
            ext[kind, pl.ds(h, tm), :] = u_ref[kind]
            ext[kind, pl.ds(0, h), :] = jnp.where(i > 0, p_ref[kind], 0.0)
        for l0, lw in p["lane_chunks"]:
            lanes = pl.ds(l0, lw)
            for r0 in range(0, tm, rc):
                g = _taps(ext, (0,), h + r0, rc, lanes, wg_ref, k, -1, jnp.broadcast_to(bg_ref[:, lanes], (rc, lw)))
                v = _taps(ext, (1,), h + r0, rc, lanes, wv_ref, k, -1, jnp.broadcast_to(bv_ref[:, lanes], (rc, lw)))
                o_ref[pl.ds(r0, rc), lanes] = (_silu(g) * v).astype(BF16)

    col = lambda q, jj, i: (0, q * npq + jj)
    return pl.pallas_call(
        body, name=name, grid=(2, npq, t // tm),
        in_specs=[pl.BlockSpec((2, None, tm, tc), lambda q, jj, i: (0, q, i, jj)),
                  pl.BlockSpec((2, None, h, tc), lambda q, jj, i: (0, q, jnp.maximum(i * rb - 1, 0), jj)),
                  pl.BlockSpec((kp, tc), col), pl.BlockSpec((kp, tc), col),
                  pl.BlockSpec((1, tc), col), pl.BlockSpec((1, tc), col)],
        out_specs=pl.BlockSpec((tm, tc), lambda q, jj, i: (i, q * npq + jj)),
        out_shape=jax.ShapeDtypeStruct((t, f), BF16),
        scratch_shapes=[pltpu.VMEM((2, h + tm, tc), F32)],
        compiler_params=_cparams(dimension_semantics=("parallel", "parallel", "parallel")),
    )(p["u4"], p["u4"], p["wg"], p["wv"], p["bg"], p["bv"])


def _ffn_mid_bwd(u2c, dhm, w, b, *, name):
    p = _ffn_mid_setup(u2c, w, b)
    t, nq, f, k, h, tm, tc, npq, rb, nt, rc = (p[n] for n in ("t", "nq", "f", "k", "h", "tm", "tc", "npq", "rb", "nt", "rc"))
    kp = p["wg"].shape[0]
    chunks1 = [(r0, rc) for r0 in range(0, tm, rc)] + [(tm, h)]

    def body(u_ref, p_ref, n_ref, dh_ref, nd_ref, wg_ref, wv_ref, bg_ref, bv_ref,
             du_ref, dwg_ref, dwv_ref, dbg_ref, dbv_ref, uext, dsc):
        i = pl.program_id(2)
        w_refs, dw_refs, db_refs = (wg_ref, wv_ref), (dwg_ref, dwv_ref), (dbg_ref, dbv_ref)

        @pl.when(i == 0)
        def _():
            for ref in dw_refs + db_refs:
                ref[...] = jnp.zeros_like(ref)

        for kind in range(2):
            uext[kind, pl.ds(0, h), :] = jnp.where(i > 0, p_ref[kind], 0.0)
            uext[kind, pl.ds(h, tm), :] = u_ref[kind]
            uext[kind, pl.ds(h + tm, h), :] = jnp.where(i < nt - 1, n_ref[kind], 0.0)
        for l0, lw in p["lane_chunks"]:
            lanes = pl.ds(l0, lw)
            for r0, rr in chunks1:
                g = _taps(uext, (0,), h + r0, rr, lanes, wg_ref, k, -1, jnp.broadcast_to(bg_ref[:, lanes], (rr, lw)))
                v = _taps(uext, (1,), h + r0, rr, lanes, wv_ref, k, -1, jnp.broadcast_to(bv_ref[:, lanes], (rr, lw)))
                dh = dh_ref[pl.ds(r0, rr), lanes] if r0 < tm else jnp.where(i < nt - 1, nd_ref[:, lanes], 0.0)
                sg = _sigmoid(g)
                dsc[0, pl.ds(r0, rr), lanes] = dh * v * (sg * (1.0 + g * (1.0 - sg)))
                dsc[1, pl.ds(r0, rr), lanes] = dh * (g * sg)
            for kind in range(2):
                for r0 in range(0, tm, rc):
                    acc = _taps(dsc, (kind,), r0, rc, lanes, w_refs[kind], k, +1, jnp.zeros((rc, lw), F32))
                    du_ref[kind, pl.ds(r0, rc), lanes] = acc.astype(BF16)
                for s in range(k):
                    a8 = jnp.zeros((8, lw), F32)
                    for r0 in range(0, tm, rc):
                        a8 = a8 + _fold8(uext[kind, pl.ds(h + r0 - s, rc), lanes] * dsc[kind, pl.ds(r0, rc), lanes])
                    dw_refs[kind][k - 1 - s:k - s, lanes] += _rowsum(a8)
                b8 = jnp.zeros((8, lw), F32)
                for r0 in range(0, tm, rc):
                    b8 = b8 + _fold8(dsc[kind, pl.ds(r0, rc), lanes])
                db_refs[kind][:, lanes] += _rowsum(b8)

    col = lambda q, jj, i: (0, q * npq + jj)
    nxt = lambda i: jnp.minimum((i + 1) * rb, nt * rb - 1)
    du, dwg, dwv, dbg, dbv = pl.pallas_call(
        body, name=name, grid=(2, npq, nt),
        in_specs=[pl.BlockSpec((2, None, tm, tc), lambda q, jj, i: (0, q, i, jj)),
                  pl.BlockSpec((2, None, h, tc), lambda q, jj, i: (0, q, jnp.maximum(i * rb - 1, 0), jj)),
                  pl.BlockSpec((2, None, h, tc), lambda q, jj, i: (0, q, nxt(i), jj)),
                  pl.BlockSpec((tm, tc), lambda q, jj, i: (i, q * npq + jj)),
                  pl.BlockSpec((h, tc), lambda q, jj, i: (nxt(i), q * npq + jj)),
                  pl.BlockSpec((kp, tc), col), pl.BlockSpec((kp, tc), col),
                  pl.BlockSpec((1, tc), col), pl.BlockSpec((1, tc), col)],
        out_specs=[pl.BlockSpec((2, None, tm, tc), lambda q, jj, i: (0, q, i, jj)),
                   pl.BlockSpec((kp, tc), col), pl.BlockSpec((kp, tc), col),
                   pl.BlockSpec((1, tc), col), pl.BlockSpec((1, tc), col)],
        out_shape=[jax.ShapeDtypeStruct((2, 2, t, nq), BF16), jax.ShapeDtypeStruct((kp, f), F32),
                   jax.ShapeDtypeStruct((kp, f), F32), jax.ShapeDtypeStruct((1, f), F32), jax.ShapeDtypeStruct((1, f), F32)],
        scratch_shapes=[pltpu.VMEM((2, h + tm + h, tc), F32), pltpu.VMEM((2, tm + h, tc), F32)],
        compiler_params=_cparams(dimension_semantics=("parallel", "parallel", "arbitrary")),
    )(p["u4"], p["u4"], p["u4"], dhm, dhm, p["wg"], p["wv"], p["bg"], p["bv"])
    return (du.reshape(N_CHIPS, t, nq), jnp.concatenate([dwg[:k], dwv[:k]], axis=1), jnp.concatenate([dbg, dbv], axis=1))


def _head_expander(n_heads):
    return (jnp.arange(n_heads * HEAD_DIM)[None, :] // HEAD_DIM == jnp.arange(n_heads)[:, None]).astype(BF16)


def _ssm_act(conv, dtp, bias_exp, aneg_exp, expand, *, di, name):
    q = SSD_CHUNK
    gn = (conv.shape[1] - di) // 2
    mm_dims = (((1,), (0,)), ((), ()))

    def fn(cb, dtb, bb, ab, eb):
        act = _silu(cb)
        dt = _dot3(dtb, eb, mm_dims, 0) + bb
        dt = jnp.maximum(dt, 0.0) + jnp.log(1.0 + jnp.exp(-jnp.abs(dt)))
        a = dt * ab
        tri = (lax.broadcasted_iota(jnp.int32, (q, q), 0) >= lax.broadcasted_iota(jnp.int32, (q, q), 1)).astype(F32)
        cs = _dot3(tri, a, mm_dims, 1)
        return [act[:, :di], act[:, di:di + gn], act[:, di + gn:], dt, cs], []
    outs, _ = _rowwise(fn, [conv, dtp], [bias_exp, aneg_exp, expand],
                       [(di, F32), (gn, F32), (gn, F32), (di, F32), (di, F32)], [], tm=q, name=name)
    return outs


def _head_masks(q):
    lane = lax.broadcasted_iota(jnp.int32, (q, LANES), 1)
    return lane < HEAD_DIM


def _pair_cols(cs, lo):
    sw = pltpu.roll(cs, HEAD_DIM, 1)
    return jnp.where(lo, cs, sw), jnp.where(lo, sw, cs)


def _ssd_fwd(xs, dt_exp, cs_exp, cs_rows, bm, cm, *, name):
    t, di = xs.shape
    q = SSD_CHUNK
    hg = di // N_GROUPS
    npair = hg // LANES
    nheads = hg // HEAD_DIM
    nc = t // q
    n = D_STATE
    cpb = _pick(nc, (4, 2, 1))
    qb = cpb * q

    def body(xs_ref, dt_ref, cs_ref, csr_ref, b_ref, c_ref, y_ref, st_ref, s_scr):
        ci = pl.program_id(1)

        @pl.when(ci == 0)
        def _():
            s_scr[...] = jnp.zeros_like(s_scr)

        tri = lax.broadcasted_iota(jnp.int32, (q, q), 0) >= lax.broadcasted_iota(jnp.int32, (q, q), 1)
        lo = _head_masks(q)
        for cc in range(cpb):
            rows = pl.ds(cc * q, q)
            bb = b_ref[rows, :].astype(BF16)
            cb_ = c_ref[rows, :].astype(BF16)
            cbm = lax.dot_general(cb_, bb, (((1,), (1,)), ((), ())), preferred_element_type=F32)
            csr = csr_ref[0, :, pl.ds(cc * q, q)]
            for p in range(npair):
                sl = pl.ds(p * LANES, LANES)
                x = xs_ref[rows, sl] * dt_ref[rows, sl]
                cs = cs_ref[rows, sl]
                col0, col1 = _pair_cols(cs, lo)
                l0 = jnp.where(tri, jnp.exp(jnp.minimum(col0 - csr[2 * p:2 * p + 1, :], 0.0)), 0.0)
                l1 = jnp.where(tri, jnp.exp(jnp.minimum(col1 - csr[2 * p + 1:2 * p + 2, :], 0.0)), 0.0)
                xb = x.astype(BF16)
                yd = jnp.where(lo, jnp.dot((cbm * l0).astype(BF16), xb, preferred_element_type=F32),
                               jnp.dot((cbm * l1).astype(BF16), xb, preferred_element_type=F32))
                s = s_scr[p]
                st_ref[0, cc, p] = s
                yo = jnp.exp(cs) * jnp.dot(cb_, s.astype(BF16), preferred_element_type=F32)
                y_ref[rows, sl] = yd + yo
                cs_end = cs[q - 1:q, :]
                xd = (x * jnp.exp(cs_end - cs)).astype(BF16)
                s_scr[p] = jnp.exp(cs_end) * s + lax.dot_general(bb, xd, (((0,), (0,)), ((), ())),
                                                                 preferred_element_type=F32)

    return pl.pallas_call(
        body, name=name, grid=(N_GROUPS, nc // cpb),
        in_specs=[pl.BlockSpec((qb, hg), lambda g, c: (c, g)),
                  pl.BlockSpec((qb, hg), lambda g, c: (c, g)),
                  pl.BlockSpec((qb, hg), lambda g, c: (c, g)),
                  pl.BlockSpec((1, nheads, qb), lambda g, c: (g, 0, c)),
                  pl.BlockSpec((qb, n), lambda g, c: (c, g)),
                  pl.BlockSpec((qb, n), lambda g, c: (c, g))],
        out_specs=[pl.BlockSpec((qb, hg), lambda g, c: (c, g)),
                   pl.BlockSpec((1, cpb, npair, n, LANES), lambda g, c: (g, c, 0, 0, 0))],
        out_shape=[jax.ShapeDtypeStruct((t, di), F32),
                   jax.ShapeDtypeStruct((N_GROUPS, nc, npair, n, LANES), F32)],
        scratch_shapes=[pltpu.VMEM((npair, n, LANES), F32)],
        compiler_params=_cparams(dimension_semantics=("parallel", "arbitrary")),
    )(xs, dt_exp, cs_exp, cs_rows, bm, cm)


def _dot3(a, b, dims, split):
    rest = (a, b)[split].astype(F32)
    other = (a, b)[1 - split].astype(BF16)
    acc = None
    for _ in range(3):
        part = rest.astype(BF16)
        rest = rest - part.astype(F32)
        d = (lax.dot_general(part, other, dims, preferred_element_type=F32) if split == 0
             else lax.dot_general(other, part, dims, preferred_element_type=F32))
        acc = d if acc is None else acc + d
    return acc


def _ssd_bwd(xs, dt_exp, cs_exp, cs_rows, bm, cm, dy, states, aneg_exp, *, name):
    t, di = xs.shape
    q = SSD_CHUNK
    hg = di // N_GROUPS
    npair = hg // LANES
    nheads = hg // HEAD_DIM
    nc = t // q
    n = D_STATE
    nt_dims = (((1,), (1,)), ((), ()))
    tn_dims = (((0,), (0,)), ((), ()))

    mm_dims = (((1,), (0,)), ((), ()))

    def body(xs_ref, dt_ref, cs_ref, csr_ref, b_ref, c_ref, dy_ref, st_ref, an_ref,
             dxp_ref, db_ref, dc_ref, ddt_ref, dan_ref, r_scr):
        ci = pl.program_id(1)

        @pl.when(ci == 0)
        def _():
            r_scr[...] = jnp.zeros_like(r_scr)
            dan_ref[...] = jnp.zeros_like(dan_ref)

        bb = b_ref[...].astype(BF16)
        cb_ = c_ref[...].astype(BF16)
        cbm = lax.dot_general(cb_, bb, nt_dims, preferred_element_type=F32)
        row = lax.broadcasted_iota(jnp.int32, (q, q), 0)
        col = lax.broadcasted_iota(jnp.int32, (q, q), 1)
        tri = row >= col
        triu = (row <= col).astype(F32)
        trisl = (row > col).astype(F32)
        ones2 = (lax.broadcasted_iota(jnp.int32, (LANES, LANES), 0) // HEAD_DIM
                 == lax.broadcasted_iota(jnp.int32, (LANES, LANES), 1) // HEAD_DIM).astype(BF16)
        onesq = jnp.ones((q, LANES), BF16)
        last = lax.broadcasted_iota(jnp.int32, (q, LANES), 0) == q - 1
        lo = _head_masks(q)
        csr = csr_ref[0]
        dcb = jnp.zeros((q, q), F32)
        dc_acc = jnp.zeros((q, n), F32)
        db_acc = jnp.zeros((q, n), F32)
        for p in range(npair):
            sl = pl.ds(p * LANES, LANES)
            xsv = xs_ref[:, sl]
            dtv = dt_ref[:, sl]
            x = xsv * dtv
            cs = cs_ref[:, sl]
            dyv = dy_ref[:, sl]
            col0, col1 = _pair_cols(cs, lo)
            l0 = jnp.where(tri, jnp.exp(jnp.minimum(col0 - csr[2 * p:2 * p + 1, :], 0.0)), 0.0)
            l1 = jnp.where(tri, jnp.exp(jnp.minimum(col1 - csr[2 * p + 1:2 * p + 2, :], 0.0)), 0.0)
            xb = x.astype(BF16)
            dyb = dyv.astype(BF16)
            g0 = lax.dot_general(jnp.where(lo, dyv, 0.0).astype(BF16), xb, nt_dims, preferred_element_type=F32)
            g1 = lax.dot_general(jnp.where(lo, 0.0, dyv).astype(BF16), xb, nt_dims, preferred_element_type=F32)
            gl0, gl1 = g0 * l0, g1 * l1
            dcb = dcb + gl0 + gl1
            w0, w1 = cbm * gl0, cbm * gl1
            dxd = jnp.where(lo,
                            lax.dot_general((cbm * l0).astype(BF16), dyb, tn_dims, preferred_element_type=F32),
                            lax.dot_general((cbm * l1).astype(BF16), dyb, tn_dims, preferred_element_type=F32))
            e = jnp.exp(cs)
            cs_end = cs[q - 1:q, :]
            dte = jnp.exp(cs_end - cs)
            dend = jnp.exp(cs_end)
            sf = st_ref[0, 0, p]
            sb = sf.astype(BF16)
            r = r_scr[p]
            rb = r.astype(BF16)
            dyeb = (dyv * e).astype(BF16)
            dc_acc = dc_acc + lax.dot_general(dyeb, sb, nt_dims, preferred_element_type=F32)
            dxo = dte * jnp.dot(bb, rb, preferred_element_type=F32)
            db_acc = db_acc + lax.dot_general((x * dte).astype(BF16), rb, nt_dims, preferred_element_type=F32)
            r_scr[p] = dend * r + lax.dot_general(cb_, dyeb, tn_dims, preferred_element_type=F32)
            dx = dxd + dxo
            dxp_ref[:, sl] = dx
            yoff = e * jnp.dot(cb_, sb, preferred_element_type=F32)
            w0b, w1b = w0.astype(BF16), w1.astype(BF16)
            rw = jnp.where(lo, jnp.sum(w0b.astype(F32), axis=1, keepdims=True), jnp.sum(w1b.astype(F32), axis=1, keepdims=True))
            cw = jnp.where(lo, lax.dot_general(w0b, onesq, tn_dims, preferred_element_type=F32),
                           lax.dot_general(w1b, onesq, tn_dims, preferred_element_type=F32))
            through = jnp.where(last, dend * _rowsum(r * sf), 0.0)
            suf = jnp.dot((dyv * yoff + through).astype(BF16), ones2, preferred_element_type=F32) + rw - cw
            pre = jnp.dot((dxo * x).astype(BF16), ones2, preferred_element_type=F32)
            da = _dot3(triu, suf, mm_dims, 1) + _dot3(trisl, pre, mm_dims, 1)
            qs = jnp.dot((dx * xsv).astype(BF16), ones2, preferred_element_type=F32)
            ddt_ref[:, sl] = da * an_ref[:, sl] + qs
            dan_ref[:, sl] += _rowsum(da * dtv)
        dcbb = dcb.astype(BF16)
        dc_ref[...] = dc_acc + jnp.dot(dcbb, bb, preferred_element_type=F32)
        db_ref[...] = db_acc + lax.dot_general(dcbb, cb_, tn_dims, preferred_element_type=F32)

    rev = lambda g, c: (nc - 1 - c, g)
    return pl.pallas_call(
        body, name=name, grid=(N_GROUPS, nc),
        in_specs=[pl.BlockSpec((q, hg), rev), pl.BlockSpec((q, hg), rev), pl.BlockSpec((q, hg), rev),
                  pl.BlockSpec((1, nheads, q), lambda g, c: (g, 0, nc - 1 - c)),
                  pl.BlockSpec((q, n), rev), pl.BlockSpec((q, n), rev),
                  pl.BlockSpec((q, hg), rev),
                  pl.BlockSpec((1, 1, npair, n, LANES), lambda g, c: (g, nc - 1 - c, 0, 0, 0)),
                  pl.BlockSpec((1, hg), lambda g, c: (0, g))],
        out_specs=[pl.BlockSpec((q, hg), rev), pl.BlockSpec((q, n), rev), pl.BlockSpec((q, n), rev),
                   pl.BlockSpec((q, hg), rev), pl.BlockSpec((1, hg), lambda g, c: (0, g))],
        out_shape=[jax.ShapeDtypeStruct((t, di), F32), jax.ShapeDtypeStruct((t, N_GROUPS * n), F32),
                   jax.ShapeDtypeStruct((t, N_GROUPS * n), F32), jax.ShapeDtypeStruct((t, di), F32),
                   jax.ShapeDtypeStruct((1, di), F32)],
        scratch_shapes=[pltpu.VMEM((npair, n, LANES), F32)],
        compiler_params=_cparams(dimension_semantics=("parallel", "arbitrary")),
    )(xs, dt_exp, cs_exp, cs_rows, bm, cm, dy, states, aneg_exp)


def _group_stats(w, gw):
    return [lax.rsqrt(jnp.mean(w[:, i * gw:(i + 1) * gw] ** 2, axis=-1, keepdims=True) + RMS_EPS)
            for i in range(N_GROUPS)]


def _gated_norm_fwd(y_ssd, xs, z, d_exp, g, *, name):
    di = xs.shape[1]
    gw = di // N_GROUPS

    def fn(yb, xb, zb, db, gb):
        w = (yb + db * xb) * _silu(zb)
        rs = _group_stats(w, gw)
        return [jnp.concatenate([w[:, i * gw:(i + 1) * gw] * rs[i] for i in range(N_GROUPS)], axis=1) * gb], []
    (o,), _ = _rowwise(fn, [y_ssd, xs, (z, di, 0)], [d_exp, g], [(di, BF16)], [], tm=128, name=name)
    return o


def _gated_norm_bwd(y_ssd, xs, z, d_exp, g, do, *, name):
    di = xs.shape[1]
    gw = di // N_GROUPS

    def fn(yb, xb, zb, dob, db, gb):
        yy = yb + db * xb
        sz = _silu(zb)
        w = yy * sz
        rs = _group_stats(w, gw)
        dwh = dob * gb
        wh_parts, dw_parts = [], []
        for i in range(N_GROUPS):
            sl = slice(i * gw, (i + 1) * gw)
            wh = w[:, sl] * rs[i]
            wh_parts.append(wh)
            dw_parts.append(rs[i] * (dwh[:, sl] - wh * jnp.mean(dwh[:, sl] * wh, axis=-1, keepdims=True)))
        wh = jnp.concatenate(wh_parts, axis=1)
        dw = jnp.concatenate(dw_parts, axis=1)
        dy = dw * sz
        dz = dw * yy * _dsilu(zb)
        return [dy, dz], [_rowsum(dob * wh), _rowsum(dy * xb)]
    (dy, dz), (dg, dd) = _rowwise(fn, [y_ssd, xs, (z, di, 0), do], [d_exp, g], [(di, F32), (di, BF16)], [di, di],
                                  tm=128, name=name)
    return dy, dz, dg, dd


def _ssm_act_bwd(conv, dtp, dxp, dy, dbm, dcm, ddt_exp, dt_exp, bias_exp, d_exp, expand, *, di, name):
    gn = dbm.shape[1]

    def fn(cb, dtb, dxpb, dyb, dbb, dcb, ddtb, dteb, bb, db, eb):
        dxs = dxpb * dteb + dyb * db
        dact = jnp.concatenate([dxs, dbb, dcb], axis=1)
        dconv = dact * _dsilu(cb)
        ddtp = ddtb * _sigmoid(_dot3(dtb, eb, (((1,), (0,)), ((), ())), 0) + bb)
        return [dconv, ddtp], [_rowsum(ddtp)]
    (dconv, ddtp), (dbias,) = _rowwise(fn, [conv, dtp, dxp, dy, dbm, dcm, ddt_exp, dt_exp], [bias_exp, d_exp, expand],
                                       [(di + 2 * gn, F32), (di, F32)], [di], tm=64, name=name)
    return dconv, ddtp, dbias


def _adamw(w, g, m, v, *, name):
    r, c = w.shape
    c1 = 1.0 / (1.0 - ADAM_B1 ** ADAM_STEP)
    c2 = 1.0 / (1.0 - ADAM_B2 ** ADAM_STEP)

    def fn(wb, gb, mb, vb):
        mn = ADAM_B1 * mb + (1.0 - ADAM_B1) * gb
        vn = ADAM_B2 * vb + (1.0 - ADAM_B2) * (gb * gb)
        delta = -ADAM_LR * ((mn * c1) / (jnp.sqrt(vn * c2) + ADAM_EPS) + ADAM_WD * wb)
        return [delta, mn, vn], []
    cap = max(8, ADAMW_BLOCK_ELEMS // c)
    tm = _pick(r, [p for p in (512, 256, 128, 64, 32, 16, 8) if p <= cap])
    (d, mn, vn), _ = _rowwise(fn, [w, g, m, v], [], [(c, F32)] * 3, [], tm=tm, name=name)
    return d, mn, vn


def _add_pair(sel, g, r, *, name):
    _, _, rows, cols = g.shape
    tm = _pick(rows, (256, 128, 64, 32, 16))

    def body(s_ref, g_ref, r_ref, o_ref):
        o_ref[...] = (g_ref[...].astype(F32) + r_ref[...].astype(F32)).astype(BF16)

    return pl.pallas_call(
        body, name=name,
        grid_spec=pltpu.PrefetchScalarGridSpec(
            num_scalar_prefetch=1, grid=(N_CHIPS, rows // tm),
            in_specs=[pl.BlockSpec((None, None, tm, cols), lambda j, i, s: (j, s[0], i, 0)),
                      pl.BlockSpec((None, tm, cols), lambda j, i, s: (j, i, 0))],
            out_specs=pl.BlockSpec((None, tm, cols), lambda j, i, s: (j, i, 0))),
        out_shape=jax.ShapeDtypeStruct((N_CHIPS, rows, cols), BF16),
        compiler_params=_cparams(dimension_semantics=("parallel", "parallel")),
    )(sel, g, r)


def _add_four(sel, p, r, *, name):
    _, rows, cols = p.shape
    tm = _pick(rows, (256, 128, 64, 32, 16))

    def body(s_ref, p_ref, r0, r1, r2, o_ref):
        o_ref[...] = ((p_ref[...].astype(F32) + r0[...].astype(F32)) + r1[...].astype(F32)) + r2[...].astype(F32)

    rspec = lambda k: pl.BlockSpec((None, tm, cols), lambda i, s, k=k: (k, i, 0))
    return pl.pallas_call(
        body, name=name,
        grid_spec=pltpu.PrefetchScalarGridSpec(
            num_scalar_prefetch=1, grid=(rows // tm,),
            in_specs=[pl.BlockSpec((None, tm, cols), lambda i, s: (s[0], i, 0)), rspec(0), rspec(1), rspec(2)],
            out_specs=pl.BlockSpec((None, tm, cols), lambda i, s: (s[1], i, 0))),
        out_shape=jax.ShapeDtypeStruct((2, rows, cols), F32),
        compiler_params=_cparams(dimension_semantics=("parallel",)),
    )(sel, p, r, r, r)


def _sum8(g, *, name):
    _, rows, cols = g.shape
    tm = _pick(rows, (512, 256, 128, 64, 32, 16, 8))

    def body(g_ref, o_ref):
        acc = g_ref[0]
        for k in range(1, 8):
            acc = acc + g_ref[k]
        o_ref[...] = acc

    return pl.pallas_call(
        body, name=name, grid=(rows // tm,),
        in_specs=[pl.BlockSpec((8, tm, cols), lambda i: (0, i, 0))],
        out_specs=pl.BlockSpec((tm, cols), lambda i: (i, 0)),
        out_shape=jax.ShapeDtypeStruct((rows, cols), F32),
        compiler_params=_cparams(dimension_semantics=("parallel",)),
    )(g)


def _place():
    x, y, c = lax.axis_index("x"), lax.axis_index("y"), lax.axis_index("c")
    chips = [(1 - x, y), (x, 1 - y), (1 - x, 1 - y)]
    return x, y, c, chips


def _rcopy(src, dst, send_sems, recv_sems, k, to):
    return pltpu.make_async_remote_copy(src_ref=src, dst_ref=dst, send_sem=send_sems.at[k], recv_sem=recv_sems.at[k],
                                        device_id=to, device_id_type=MESH)


def _gather_chips(packs, *, name):
    n = len(packs)

    def body(*refs):
        srcs, outs, (send_sems, recv_sems) = refs[:n], refs[n:2 * n], refs[2 * n:]
        x, y, c, chips = _place()
        sibling = (x, y, 1 - c)
        me = 2 * x + y
        first, passed = [], []
        for t, (src, out) in enumerate(zip(srcs, outs)):
            for k, (cx, cy) in enumerate(chips):
                cp = _rcopy(src.at[c], out.at[me, c], send_sems, recv_sems, 6 * t + k, (cx, cy, c))
                cp.start()
                first.append(cp)
        for t, out in enumerate(outs):
            for k, (cx, cy) in enumerate(chips):
                blk = out.at[2 * cx + cy, c]
                _rcopy(blk, blk, send_sems, recv_sems, 6 * t + k, (cx, cy, c)).wait_recv()
                fw = _rcopy(blk, blk, send_sems, recv_sems, 6 * t + 3 + k, sibling)
                fw.start()
                passed.append(fw)
        for t, out in enumerate(outs):
            for k, (cx, cy) in enumerate(chips):
                blk = out.at[2 * cx + cy, 1 - c]
                _rcopy(blk, blk, send_sems, recv_sems, 6 * t + 3 + k, sibling).wait_recv()
        for cp in first + passed:
            cp.wait_send()

    return pl.pallas_call(
        body, name=name, in_specs=[ANY] * n, out_specs=[ANY] * n,
        out_shape=[jax.ShapeDtypeStruct((N_CHIPS,) + p.shape, p.dtype) for p in packs],
        scratch_shapes=[pltpu.SemaphoreType.DMA((6 * n,)), pltpu.SemaphoreType.DMA((6 * n,))],
    )(*packs)


def _gather_devices(pack, *, name):
    rows, cols = pack.shape

    def body(src, out, send_sems, recv_sems, local_sem):
        x, y, c, chips = _place()
        sibling = (x, y, 1 - c)

        def blk(px, py, pc):
            return out.at[4 * px + 2 * py + pc]

        mine = pltpu.make_async_copy(src, blk(x, y, c), local_sem)
        mine.start()
        first = [_rcopy(src, blk(x, y, c), send_sems, recv_sems, 0, sibling)]
        first += [_rcopy(src, blk(x, y, c), send_sems, recv_sems, 1 + k, (cx, cy, c)) for k, (cx, cy) in enumerate(chips)]
        for cp in first:
            cp.start()
        passed = []
        for k, (cx, cy) in enumerate(chips):
            b = blk(cx, cy, c)
            _rcopy(b, b, send_sems, recv_sems, 1 + k, (cx, cy, c)).wait_recv()
            fw = _rcopy(b, b, send_sems, recv_sems, 4 + k, sibling)
            fw.start()
            passed.append(fw)
        b = blk(x, y, 1 - c)
        _rcopy(b, b, send_sems, recv_sems, 0, sibling).wait_recv()
        for k, (cx, cy) in enumerate(chips):
            b = blk(cx, cy, 1 - c)
            _rcopy(b, b, send_sems, recv_sems, 4 + k, sibling).wait_recv()
        for cp in first + passed:
            cp.wait_send()
        mine.wait()

    return pl.pallas_call(
        body, name=name, in_specs=[ANY], out_specs=ANY,
        out_shape=jax.ShapeDtypeStruct((8, rows, cols), pack.dtype),
        scratch_shapes=[pltpu.SemaphoreType.DMA((7,)), pltpu.SemaphoreType.DMA((7,)), pltpu.SemaphoreType.DMA],
    )(pack)


def _scatter_chips(ps, *, name):
    n = len(ps)

    def body(*refs):
        srcs, outs, (send_sems, recv_sems) = refs[:n], refs[n:2 * n], refs[2 * n:]
        x, y, c, chips = _place()
        cps = [_rcopy(src.at[2 * cx + cy], out.at[k], send_sems, recv_sems, 3 * t + k, (cx, cy, c))
               for t, (src, out) in enumerate(zip(srcs, outs)) for k, (cx, cy) in enumerate(chips)]
        for cp in cps:
            cp.start()
        for cp in cps:
            cp.wait()

    return pl.pallas_call(
        body, name=name, in_specs=[ANY] * n, out_specs=[ANY] * n,
        out_shape=[jax.ShapeDtypeStruct((3,) + p.shape[1:], p.dtype) for p in ps],
        scratch_shapes=[pltpu.SemaphoreType.DMA((3 * n,)), pltpu.SemaphoreType.DMA((3 * n,))],
    )(*ps)


def _join_halves(rs, *, name):
    n = len(rs)

    def body(*refs):
        srcs, outs, (send_sems, recv_sems) = refs[:n], refs[n:2 * n], refs[2 * n:]
        x, y, c, _ = _place()
        cps = [_rcopy(src.at[c], out.at[c], send_sems, recv_sems, t, (x, y, 1 - c))
               for t, (src, out) in enumerate(zip(srcs, outs))]
        for cp in cps:
            cp.start()
        for t, out in enumerate(outs):
            b = out.at[1 - c]
            _rcopy(b, b, send_sems, recv_sems, t, (x, y, 1 - c)).wait_recv()
        for cp in cps:
            cp.wait_send()

    return pl.pallas_call(
        body, name=name, in_specs=[ANY] * n, out_specs=[ANY] * n,
        out_shape=[jax.ShapeDtypeStruct(r.shape, r.dtype) for r in rs], input_output_aliases={t: t for t in range(n)},
        scratch_shapes=[pltpu.SemaphoreType.DMA((n,)), pltpu.SemaphoreType.DMA((n,))],
    )(*rs)


def _flat_rows(parts, cols):
    flat = jnp.concatenate([p.reshape(-1) for p in parts])
    n = flat.shape[0]
    rows = -(-n // cols)
    unit = 256 if rows > 256 else 8
    rows = unit * (-(-rows // unit))
    return jnp.pad(flat, (0, rows * cols - n)).reshape(rows, cols)


def _expand(v, di):
    return jnp.repeat(v, HEAD_DIM).reshape(1, di)


def kernel(x, norm_mix_g, norm_ffn_g, norm_final_g, cv_w_in, cv_b_in, cv_w_dw, cv_b_dw, cv_ln_g, cv_ln_b, cv_w_out, cv_b_out, ssm_w_in, ssm_w_conv, ssm_b_conv, ssm_dt_bias, ssm_a_log, ssm_d, ssm_norm_g, ssm_w_out, ffn_w_up, ffn_w_dw, ffn_b_dw, ffn_w_down, loss_target, m_norm_mix_g, m_norm_ffn_g, m_norm_final_g, m_cv_w_in, m_cv_b_in, m_cv_w_dw, m_cv_b_dw, m_cv_ln_g, m_cv_ln_b, m_cv_w_out, m_cv_b_out, m_ssm_w_in, m_ssm_w_conv, m_ssm_b_conv, m_ssm_dt_bias, m_ssm_a_log, m_ssm_d, m_ssm_norm_g, m_ssm_w_out, m_ffn_w_up, m_ffn_w_dw, m_ffn_b_dw, m_ffn_w_down, v_norm_mix_g, v_norm_ffn_g, v_norm_final_g, v_cv_w_in, v_cv_b_in, v_cv_w_dw, v_cv_b_dw, v_cv_ln_g, v_cv_ln_b, v_cv_w_out, v_cv_b_out, v_ssm_w_in, v_ssm_w_conv, v_ssm_b_conv, v_ssm_dt_bias, v_ssm_a_log, v_ssm_d, v_ssm_norm_g, v_ssm_w_out, v_ffn_w_up, v_ffn_w_dw, v_ffn_b_dw, v_ffn_w_down):
    weights = dict(norm_mix_g=norm_mix_g, norm_ffn_g=norm_ffn_g, norm_final_g=norm_final_g, cv_w_in=cv_w_in, cv_b_in=cv_b_in, cv_w_dw=cv_w_dw, cv_b_dw=cv_b_dw, cv_ln_g=cv_ln_g, cv_ln_b=cv_ln_b, cv_w_out=cv_w_out, cv_b_out=cv_b_out, ssm_w_in=ssm_w_in, ssm_w_conv=ssm_w_conv, ssm_b_conv=ssm_b_conv, ssm_dt_bias=ssm_dt_bias, ssm_a_log=ssm_a_log, ssm_d=ssm_d, ssm_norm_g=ssm_norm_g, ssm_w_out=ssm_w_out, ffn_w_up=ffn_w_up, ffn_w_dw=ffn_w_dw, ffn_b_dw=ffn_b_dw, ffn_w_down=ffn_w_down)
    mom_m = dict(norm_mix_g=m_norm_mix_g, norm_ffn_g=m_norm_ffn_g, norm_final_g=m_norm_final_g, cv_w_in=m_cv_w_in, cv_b_in=m_cv_b_in, cv_w_dw=m_cv_w_dw, cv_b_dw=m_cv_b_dw, cv_ln_g=m_cv_ln_g, cv_ln_b=m_cv_ln_b, cv_w_out=m_cv_w_out, cv_b_out=m_cv_b_out, ssm_w_in=m_ssm_w_in, ssm_w_conv=m_ssm_w_conv, ssm_b_conv=m_ssm_b_conv, ssm_dt_bias=m_ssm_dt_bias, ssm_a_log=m_ssm_a_log, ssm_d=m_ssm_d, ssm_norm_g=m_ssm_norm_g, ssm_w_out=m_ssm_w_out, ffn_w_up=m_ffn_w_up, ffn_w_dw=m_ffn_w_dw, ffn_b_dw=m_ffn_b_dw, ffn_w_down=m_ffn_w_down)
    mom_v = dict(norm_mix_g=v_norm_mix_g, norm_ffn_g=v_norm_ffn_g, norm_final_g=v_norm_final_g, cv_w_in=v_cv_w_in, cv_b_in=v_cv_b_in, cv_w_dw=v_cv_w_dw, cv_b_dw=v_cv_b_dw, cv_ln_g=v_cv_ln_g, cv_ln_b=v_cv_ln_b, cv_w_out=v_cv_w_out, cv_b_out=v_cv_b_out, ssm_w_in=v_ssm_w_in, ssm_w_conv=v_ssm_w_conv, ssm_b_conv=v_ssm_b_conv, ssm_dt_bias=v_ssm_dt_bias, ssm_a_log=v_ssm_a_log, ssm_d=v_ssm_d, ssm_norm_g=v_ssm_norm_g, ssm_w_out=v_ssm_w_out, ffn_w_up=v_ffn_w_up, ffn_w_dw=v_ffn_w_dw, ffn_b_dw=v_ffn_b_dw, ffn_w_down=v_ffn_w_down)
    names = list(weights)

    xt = x[0]
    tgt = loss_target[0]
    t, d = xt.shape
    depth = norm_mix_g.shape[0]
    n_cv, n_ssm = cv_w_in.shape[0], ssm_w_in.shape[0]
    di = ssm_w_out.shape[1] * N_CHIPS
    n_heads = di // HEAD_DIM
    gn = N_GROUPS * D_STATE
    ssm_in = ssm_w_in.shape[2] * N_CHIPS
    chip = 2 * lax.axis_index("x") + lax.axis_index("y")

    cq = ssm_w_in.shape[2]
    cqp = LANES * (-(-cq // LANES))
    ssm_inp = N_CHIPS * cqp
    by_col = ("cv_w_in", "ffn_w_up")
    big_names = ("cv_w_in", "cv_w_out", "ssm_w_in", "ssm_w_out", "ffn_w_up", "ffn_w_down")

    def layer_tensors(i):
        mixer = [("cv_w_in", i // 2), ("cv_w_out", i // 2)] if i % 2 == 0 else [("ssm_w_in", i // 2), ("ssm_w_out", i // 2)]
        return mixer + [("ffn_w_up", i), ("ffn_w_down", i)]

    def halves(a):
        return a.reshape((2, a.shape[0] // 2) + a.shape[1:])

    order = [key for i in range(depth) for key in layer_tensors(i)]
    shards = []
    for nm, l in order:
        w = weights[nm][l]
        if nm == "ssm_w_in":
            w = jnp.pad(w, ((0, 0), (0, cqp - cq)))
        shards.append(halves(w.astype(BF16)))
    shards = dict(zip(order, lax.optimization_barrier(shards)))
    full = {}

    def arrived(key, g):
        g = lax.dynamic_update_index_in_dim(g, shards[key], chip, 0)
        g = g.reshape((N_CHIPS, 2 * g.shape[2], g.shape[3]))
        if key[0] == "ssm_w_in":
            full[key] = jnp.concatenate([g[jj, :, :cq] for jj in range(N_CHIPS)] + [jnp.zeros((d, ssm_inp - ssm_in), BF16)],
                                        axis=1)
        else:
            full[key] = g if key[0] in by_col else g.reshape(N_CHIPS * g.shape[1], g.shape[2])

    for key, g in zip(order[:2], _gather_chips([shards[k] for k in order[:2]], name="gather_weights")):
        arrived(key, g)
    landed = {}

    def mm_fwd(key, a, **kw):
        pos = order.index(key)
        kw["b_chips"] = key[0] in by_col
        jobs = []
        if pos + 1 in landed:
            jobs.append(("pass", landed.pop(pos + 1)))
        if pos + 2 < len(order):
            jobs.append(("gather", shards[order[pos + 2]]))
        if not jobs:
            return _matmul(a, full[key], **kw)
        out, *got = _matmul(a, full[key], carry=jobs, **kw)
        for (kind, _), g in zip(jobs, got):
            if kind == "pass":
                arrived(order[pos + 1], g)
            else:
                landed[pos + 2] = g
        return out

    expand = _head_expander(n_heads)

    small_sharded = ["cv_w_dw", "ssm_w_conv", "ssm_b_conv", "ssm_norm_g", "ffn_w_dw"]
    spack = _flat_rows([weights[nm] for nm in small_sharded], LANES)
    sg = _gather_devices(spack, name="gather_small").reshape(8, -1)[::2]
    o = 0
    for nm in small_sharded:
        shp = weights[nm].shape
        n = weights[nm].size
        full[nm] = jnp.concatenate([sg[j, o:o + n].reshape(shp) for j in range(N_CHIPS)], axis=-1)
        o += n

    row = lambda v: v.reshape(1, -1)

    saved = []
    xc = xt
    for i in range(depth):
        j = i // 2
        s = {"x_in": xc}
        h = _rms_fwd(xc, row(norm_mix_g[i]), name="rms_mix_fwd")
        s["h"] = h
        if i % 2 == 0:
            u = mm_fwd(("cv_w_in", j), h, name="cv_in_fwd")
            v1 = _glu_fwd(u, row(cv_b_in[j]), name="cv_glu_fwd")
            v2 = _dwconv_fwd(v1, full["cv_w_dw"][j], row(cv_b_dw[j]), name="cv_dw_fwd")
            v4 = _ln_silu_fwd(v2, row(cv_ln_g[j]), row(cv_ln_b[j]), name="cv_ln_fwd")
            xc = mm_fwd(("cv_w_out", j), v4, bias=row(cv_b_out[j]), res=xc, name="cv_out_fwd")
            s.update(u=u, v1=v1, v2=v2, v4=v4)
        else:
            zx = mm_fwd(("ssm_w_in", j), h, name="ssm_in_fwd")
            xbc_cols = (di, di + 2 * gn)
            dtp = zx[:, 2 * di + 2 * gn:ssm_in]
            conv = _dwconv_fwd(zx, full["ssm_w_conv"][j], row(full["ssm_b_conv"][j]), cols=xbc_cols, name="ssm_dw_fwd")
            bias_exp = _expand(ssm_dt_bias[j], di)
            aneg_exp = _expand(-jnp.exp(ssm_a_log[j]), di)
            d_exp = _expand(ssm_d[j], di)
            xs, bm, cm, dt_exp, cs_exp = _ssm_act(conv, dtp, bias_exp, aneg_exp, expand, di=di, name="ssm_act_fwd")
            cs_rows = cs_exp[:, ::HEAD_DIM].T.reshape(N_GROUPS, n_heads // N_GROUPS, t)
            y_ssd, states = _ssd_fwd(xs, dt_exp, cs_exp, cs_rows, bm, cm, name="ssd_fwd")
            gnrm = _gated_norm_fwd(y_ssd, xs, zx, d_exp, row(full["ssm_norm_g"][j]), name="ssm_norm_fwd")
            xc = mm_fwd(("ssm_w_out", j), gnrm, res=xc, name="ssm_out_fwd")
            s.update(zx=zx, xbc_cols=xbc_cols, dtp=dtp, conv=conv, bias_exp=bias_exp, aneg_exp=aneg_exp,
                     d_exp=d_exp, xs=xs, bm=bm, cm=cm, dt_exp=dt_exp, cs_exp=cs_exp, cs_rows=cs_rows, y_ssd=y_ssd,
                     states=states, gnrm=gnrm)
        s["x_mid"] = xc
        h2 = _rms_fwd(xc, row(norm_ffn_g[i]), name="rms_ffn_fwd")
        u2 = mm_fwd(("ffn_w_up", i), h2, out_chips=True, name="ffn_up_fwd")
        hm = _ffn_mid_fwd(u2, full["ffn_w_dw"][i], row(ffn_b_dw[i]), name="ffn_mid_fwd")
        xc = mm_fwd(("ffn_w_down", i), hm, res=xc, name="ffn_down_fwd")
        s.update(h2=h2, u2=u2, hm=hm)
        saved.append(s)

    dx, dxb, sq, dg_final = _loss_head(xc, row(norm_final_g), tgt, name="loss_head")
    loss_part = 0.5 / d * jnp.sum(sq)
    gr = {nm: [None] * weights[nm].shape[0] for nm in names if nm != "norm_final_g"}
    reduced = {}
    sel_c = jnp.reshape(lax.axis_index("c"), (1,)).astype(jnp.int32)
    sel_j = jnp.stack([chip, lax.axis_index("c")]).astype(jnp.int32)

    waiting = {}

    def scattered(got3):
        key, pair = waiting.pop("scatter")
        waiting["join"] = (key, _add_four(sel_j, pair, got3, name="grads_add4"))

    def joined(r):
        key, _ = waiting.pop("join")
        reduced[key] = r.reshape(2 * r.shape[1], r.shape[2])[:, :weights[key[0]].shape[2]]

    def backward_pair(key, act, dout, dw_name, dx_name, dout_chips=False):
        col = key[0] in by_col
        jobs = [("scatter", waiting["scatter"][1])] if "scatter" in waiting else []
        g = _matmul(act, dout, ta=True, b_chips=dout_chips, out_chips=col, out_dtype=BF16, carry=jobs, name=dw_name)
        if jobs:
            g, got3 = g
            scattered(got3)
        if key[0] == "ssm_w_in":
            g = jnp.stack([jnp.pad(g[:, jj * cq:(jj + 1) * cq], ((0, 0), (0, cqp - cq))) for jj in range(N_CHIPS)])
        elif not col:
            g = g.reshape(N_CHIPS, g.shape[0] // N_CHIPS, g.shape[1])
        g = g.reshape(N_CHIPS, 2, g.shape[1] // 2, g.shape[2])
        jobs = [("swap", g)] + ([("join", waiting["join"][1])] if "join" in waiting else [])
        dact, got, *rest = _matmul(dout, full[key], tb=True, a_chips=dout_chips, b_chips=col, carry=jobs, name=dx_name)
        if rest:
            joined(rest[0])
        waiting["scatter"] = (key, _add_pair(sel_c, g, got, name="grads_add2"))
        return dact

    for i in reversed(range(depth)):
        j = i // 2
        s = saved[i]
        dhm = backward_pair(("ffn_w_down", i), s["hm"], dxb, "ffn_down_dw", "ffn_down_dx")
        du2b, dw_dw, db_dw = _ffn_mid_bwd(s["u2"], dhm, full["ffn_w_dw"][i], row(ffn_b_dw[i]), name="ffn_mid_bwd")
        gr["ffn_w_dw"][i], gr["ffn_b_dw"][i] = dw_dw, db_dw[0]
        dh2 = backward_pair(("ffn_w_up", i), s["h2"], du2b, "ffn_up_dw", "ffn_up_dx", dout_chips=True)
        dx, dxb, colsum, dg = _rms_bwd(s["x_mid"], row(norm_ffn_g[i]), dh2, dx, name="rms_ffn_bwd")
        gr["norm_ffn_g"][i] = dg[0]
        if i % 2 == 0:
            gr["cv_b_out"][j] = colsum[0]
            dv4 = backward_pair(("cv_w_out", j), s["v4"], dxb, "cv_out_dw", "cv_out_dx")
            dv2, dlg, dlb = _ln_silu_bwd(s["v2"], row(cv_ln_g[j]), row(cv_ln_b[j]), dv4, name="cv_ln_bwd")
            gr["cv_ln_g"][j], gr["cv_ln_b"][j] = dlg[0], dlb[0]
            dv1, dw_dw, db_dw = _dwconv_bwd(s["v1"], dv2, full["cv_w_dw"][j], name="cv_dw_bwd")
            gr["cv_w_dw"][j], gr["cv_b_dw"][j] = dw_dw, db_dw[0]
            du, db_in = _glu_bwd(s["u"], row(cv_b_in[j]), dv1, name="cv_glu_bwd")
            gr["cv_b_in"][j] = db_in[0]
            dh = backward_pair(("cv_w_in", j), s["h"], du, "cv_in_dw", "cv_in_dx")
        else:
            dgn = backward_pair(("ssm_w_out", j), s["gnrm"], dxb, "ssm_out_dw", "ssm_out_dx")
            dy, dz, dng, ddl = _gated_norm_bwd(s["y_ssd"], s["xs"], s["zx"], s["d_exp"], row(full["ssm_norm_g"][j]),
                                               dgn, name="ssm_norm_bwd")
            gr["ssm_norm_g"][j] = dng[0]
            gr["ssm_d"][j] = ddl.reshape(n_heads, HEAD_DIM).sum(axis=1)
            dxp, dbm, dcm, ddt_exp, dan = _ssd_bwd(s["xs"], s["dt_exp"], s["cs_exp"], s["cs_rows"], s["bm"], s["cm"],
                                                   dy, s["states"], s["aneg_exp"], name="ssd_bwd")
            gr["ssm_a_log"][j] = dan[0, ::HEAD_DIM] * s["aneg_exp"][0, ::HEAD_DIM]
            dconv, ddtp, dbias = _ssm_act_bwd(s["conv"], s["dtp"], dxp, dy, dbm, dcm, ddt_exp, s["dt_exp"],
                                              s["bias_exp"], s["d_exp"], expand, di=di, name="ssm_act_bwd")
            gr["ssm_dt_bias"][j] = dbias[0, ::HEAD_DIM]
            dxbc, dw_c, db_c = _dwconv_bwd(s["zx"], dconv, full["ssm_w_conv"][j], dx_dtype=BF16, cols=s["xbc_cols"],
                                           name="ssm_dw_bwd")
            gr["ssm_w_conv"][j], gr["ssm_b_conv"][j] = dw_c, db_c[0]
            dtail = jnp.pad(ddtp[:, ::HEAD_DIM].astype(BF16), ((0, 0), (0, ssm_inp - ssm_in)))
            dzx = jnp.concatenate([dz, dxbc, dtail], axis=1)
            dh = backward_pair(("ssm_w_in", j), s["h"], dzx, "ssm_in_dw", "ssm_in_dx")
        dx, dxb, _, dg = _rms_bwd(s["x_in"], row(norm_mix_g[i]), dh, dx, name="rms_mix_bwd")
        gr["norm_mix_g"][i] = dg[0]

    (got3,) = _scatter_chips([waiting["scatter"][1]], name="grads_scatter")
    scattered(got3)
    joined(_join_halves([waiting["join"][1]], name="grads_join")[0])
    grads = {nm: jnp.stack([reduced[nm, l] for l in range(weights[nm].shape[0])]) for nm in big_names}

    small = [nm for nm in names if nm not in grads]
    small_parts = []
    for nm in small:
        small_parts.append(dg_final[0] if nm == "norm_final_g" else jnp.stack(gr[nm]))
    gs_pack = _flat_rows(small_parts + [loss_part.reshape(1)], LANES)
    gs = _sum8(_gather_devices(gs_pack, name="gather_small_grads"), name="sum_small_grads").reshape(-1)
    o = 0
    for nm, p in zip(small, small_parts):
        gfull = gs[o:o + p.size].reshape(p.shape)
        o += p.size
        if nm in small_sharded:
            width = weights[nm].shape[-1]
            gfull = lax.dynamic_slice_in_dim(gfull, chip * width, width, axis=gfull.ndim - 1)
        grads[nm] = gfull
    loss = gs[o]

    delta, new_m, new_v = {}, {}, {}
    for nm in big_names:
        shp = weights[nm].shape
        as2d = lambda a: a.reshape(-1, shp[-1])
        dl, mn, vn = _adamw(as2d(weights[nm]), as2d(grads[nm]), as2d(mom_m[nm]), as2d(mom_v[nm]), name="adamw_" + nm)
        delta[nm], new_m[nm], new_v[nm] = dl.reshape(shp), mn.reshape(shp), vn.reshape(shp)
    pk = lambda dct: _flat_rows([dct[nm] for nm in small], LANES)
    dl, mn, vn = _adamw(pk(weights), pk(grads), pk(mom_m), pk(mom_v), name="adamw_small")
    dl, mn, vn = dl.reshape(-1), mn.reshape(-1), vn.reshape(-1)
    o = 0
    for nm in small:
        shp, n = weights[nm].shape, weights[nm].size
        delta[nm], new_m[nm], new_v[nm] = (a[o:o + n].reshape(shp) for a in (dl, mn, vn))
        o += n

    return (loss, dx[None], *[grads[nm] for nm in names], *[delta[nm] for nm in names],
            *[new_m[nm] for nm in names], *[new_v[nm] for nm in names])
```

```python
import math

import jax
import jax.numpy as jnp
from jax import lax
from jax.experimental import pallas as pl
from jax.experimental.pallas import tpu as pltpu

F32, BF16 = jnp.float32, jnp.bfloat16
MESH = pl.DeviceIdType.MESH
ANY = pl.BlockSpec(memory_space=pl.ANY)

RMS_EPS = 1e-6
LN_EPS = 1e-5
HEAD_DIM = 64
N_GROUPS = 8
D_STATE = 128
ADAM_LR, ADAM_B1, ADAM_B2, ADAM_EPS, ADAM_WD, ADAM_STEP = 0.001, 0.9, 0.999, 1e-08, 0.01, 10

VMEM_LIMIT_BYTES = 56 * 1024 * 1024
LANES = 128
SSD_CHUNK = 128
ADAMW_BLOCK_ELEMS = 512 * 1024
N_CHIPS = 4


def _cparams(**kw):
    return pltpu.CompilerParams(vmem_limit_bytes=VMEM_LIMIT_BYTES, **kw)


def _pick(n, prefs):
    for p in prefs:
        if n % p == 0:
            return p
    return n


def _sigmoid(x):
    return 1.0 / (1.0 + jnp.exp(-x))


def _silu(x):
    return x * _sigmoid(x)


def _dsilu(x):
    s = _sigmoid(x)
    return s * (1.0 + x * (1.0 - s))


def _rowsum(x):
    return jnp.sum(x, axis=0, keepdims=True)


MM_TILES = (2816, 2688, 2048, 1408, 1024, 896, 512, 384, 256, 128)
MM_VMEM_BUDGET = 40 * 1024 * 1024


def _mm_tiles(m, n, k, n_unit, k_unit, out_bytes, has_res):
    best = None
    for tk in [t for t in MM_TILES if k_unit % t == 0]:
        for tm in [t for t in (1024, 512, 256, 128) if m % t == 0] or [m]:
            for tn in [t for t in MM_TILES if n_unit % t == 0]:
                vmem = 2 * 2 * (tm * tk + tk * tn) + 2 * tm * tn * out_bytes
                vmem += tm * tn * 4 if k // tk > 1 else 0
                vmem += 2 * tm * tn * 4 if has_res else 0
                if vmem > MM_VMEM_BUDGET:
                    continue
                traffic = m * k * (n // tn) + k * n * (m // tm)
                if best is None or traffic < best[0]:
                    best = (traffic, tm, tn, tk)
        if best is not None:
            return best[1:]
    raise ValueError((m, n, k))


def _matmul(a, b, *, name, ta=False, tb=False, a_chips=False, b_chips=False, out_chips=False, out_dtype=F32, bias=None,
            res=None, carry=None):
    if a_chips:
        assert tb and b_chips and not ta
        m, k = a.shape[1], N_CHIPS * a.shape[2]
    else:
        m, k = (a.shape[1], a.shape[0]) if ta else a.shape
    if b_chips:
        nq = b.shape[2]
        n = b.shape[1] if tb else N_CHIPS * nq
        assert k == (N_CHIPS * nq if tb else b.shape[1])
    else:
        n = b.shape[0] if tb else b.shape[1]
        assert k == (b.shape[1] if tb else b.shape[0])
        nq = n // N_CHIPS
    has_bias, has_res = bias is not None, res is not None
    tm, tn, tk = _mm_tiles(m, n, k, nq if ((b_chips and not tb) or out_chips) else n, nq if (b_chips and tb) else k,
                           jnp.dtype(out_dtype).itemsize, has_res)
    gm, gn, nk = m // tm, n // tn, k // tk
    nbq = nq // (tk if tb else tn) if (b_chips or out_chips) else 1
    dn = (((0 if ta else 1,), (1 if tb else 0,)), ((), ()))
    carry = carry or []
    nj = len(carry)

    def body(*refs):
        a_ref, b_ref = refs[0], refs[1]
        rest = list(refs[2:])
        bias_ref = rest.pop(0) if has_bias else None
        res_ref = rest.pop(0) if has_res else None
        srcs = [rest.pop(0) for _ in range(nj)]
        o_ref = rest.pop(0)
        dsts = [rest.pop(0) for _ in range(nj)]
        acc_ref = rest.pop(0) if nk > 1 else None
        i, j, kk = pl.program_id(0), pl.program_id(1), pl.program_id(2)

        if carry:
            send_sems, recv_sems = rest
            x, y, c, chips = _place()
            sibling = (x, y, 1 - c)
            cps, landing = [], []
            sem = 0
            for (kind, _), src, dst in zip(carry, srcs, dsts):
                if kind == "join":
                    cps.append(_rcopy(src.at[c], dst.at[c], send_sems, recv_sems, sem, sibling))
                    landing.append((dst.at[1 - c], sem, sibling))
                    sem += 1
                    continue
                for q, (cx, cy) in enumerate(chips):
                    if kind == "gather":
                        cps.append(_rcopy(src.at[c], dst.at[2 * x + y, c], send_sems, recv_sems, sem, (cx, cy, c)))
                        landing.append((dst.at[2 * cx + cy, c], sem, (cx, cy, c)))
                    elif kind == "pass":
                        cps.append(_rcopy(src.at[2 * cx + cy, c], dst.at[2 * cx + cy, c], send_sems, recv_sems, sem, sibling))
                        landing.append((dst.at[2 * cx + cy, 1 - c], sem, sibling))
                    else:
                        cps.append(_rcopy(src.at[2 * cx + cy], dst.at[q], send_sems, recv_sems, sem, (cx, cy, c)))
                        landing.append((dst.at[q], sem, (cx, cy, c)))
                    sem += 1

            @pl.when((i == 0) & (j == 0) & (kk == 0))
            def _():
                for cp in cps:
                    cp.start()

        def finish(r):
            if has_bias:
                r = r + bias_ref[...]
            if has_res:
                r = r + res_ref[...]
            o_ref[...] = r.astype(o_ref.dtype)

        part = lax.dot_general(a_ref[...].astype(BF16), b_ref[...].astype(BF16), dn, preferred_element_type=F32)
        if nk == 1:
            finish(part)
        else:
            @pl.when(kk == 0)
            def _():
                acc_ref[...] = part

            @pl.when(kk > 0)
            def _():
                acc_ref[...] += part

            @pl.when(kk == nk - 1)
            def _():
                finish(acc_ref[...])

        if carry:
            @pl.when((i == gm - 1) & (j == gn - 1) & (kk == nk - 1))
            def _():
                for blk, sem, frm in landing:
                    _rcopy(blk, blk, send_sems, recv_sems, sem, frm).wait_recv()
                for cp in cps:
                    cp.wait_send()

    if a_chips:
        a_spec = pl.BlockSpec((None, tm, tk), lambda i, j, kk: (kk // nbq, i, kk % nbq))
    elif ta:
        a_spec = pl.BlockSpec((tk, tm), lambda i, j, kk: (kk, i))
    else:
        a_spec = pl.BlockSpec((tm, tk), lambda i, j, kk: (i, kk))
    if b_chips and tb:
        b_spec = pl.BlockSpec((None, tn, tk), lambda i, j, kk: (kk // nbq, j, kk % nbq))
    elif b_chips:
        b_spec = pl.BlockSpec((None, tk, tn), lambda i, j, kk: (j // nbq, kk, j % nbq))
    elif tb:
        b_spec = pl.BlockSpec((tn, tk), lambda i, j, kk: (j, kk))
    else:
        b_spec = pl.BlockSpec((tk, tn), lambda i, j, kk: (kk, j))
    if out_chips:
        out_spec = pl.BlockSpec((None, tm, tn), lambda i, j, kk: (j // nbq, i, j % nbq))
        out_shape = jax.ShapeDtypeStruct((N_CHIPS, m, nq), out_dtype)
    else:
        out_spec = pl.BlockSpec((tm, tn), lambda i, j, kk: (i, j))
        out_shape = jax.ShapeDtypeStruct((m, n), out_dtype)
    in_specs, args = [a_spec, b_spec], [a, b]
    if has_bias:
        in_specs.append(pl.BlockSpec((1, tn), lambda i, j, kk: (0, j)))
        args.append(bias)
    if has_res:
        in_specs.append(pl.BlockSpec((tm, tn), lambda i, j, kk: (i, j)))
        args.append(res)
    scratch = [pltpu.VMEM((tm, tn), F32)] if nk > 1 else []
    if not carry:
        return pl.pallas_call(
            body, name=name, grid=(gm, gn, nk), in_specs=in_specs, out_specs=out_spec, out_shape=out_shape,
            scratch_shapes=scratch,
            compiler_params=_cparams(dimension_semantics=("parallel", "parallel", "arbitrary")),
        )(*args)
    lands, aliases, n_sems = [], {}, 0
    for jb, (kind, moved) in enumerate(carry):
        shape = {"gather": (N_CHIPS,) + moved.shape, "pass": moved.shape, "scatter": (3,) + moved.shape[1:],
                 "join": moved.shape}[kind]
        lands.append(jax.ShapeDtypeStruct(shape, moved.dtype))
        n_sems += 1 if kind == "join" else 3
        if kind in ("pass", "join"):
            aliases[len(args) + jb] = 1 + jb
    return pl.pallas_call(
        body, name=name, grid=(gm, gn, nk), in_specs=in_specs + [ANY] * nj, out_specs=[out_spec] + [ANY] * nj,
        out_shape=[out_shape] + lands, input_output_aliases=aliases,
        scratch_shapes=scratch + [pltpu.SemaphoreType.DMA((n_sems,)), pltpu.SemaphoreType.DMA((n_sems,))],
        compiler_params=_cparams(dimension_semantics=("arbitrary", "arbitrary", "arbitrary")),
    )(*args, *[moved for _, moved in carry])


def _rowwise(fn, rows, pars, outs, reds, *, tm, name):
    rows = [r if isinstance(r, tuple) else (r, r.shape[1], 0) for r in rows]
    t = rows[0][0].shape[0]
    assert t % tm == 0
    n_in, n_o = len(rows) + len(pars), len(outs)

    def body(*refs):
        i = pl.program_id(0)
        o, d = fn(*[r[...] for r in refs[:n_in]])
        for ref, val in zip(refs[n_in:n_in + n_o], o):
            ref[...] = val.astype(ref.dtype)
        d_refs = refs[n_in + n_o:]

        @pl.when(i == 0)
        def _():
            for ref in d_refs:
                ref[...] = jnp.zeros_like(ref)

        for ref, val in zip(d_refs, d):
            ref[...] += val

    in_specs = [pl.BlockSpec((tm, w), lambda i, b=blk: (i, b)) for _, w, blk in rows]
    in_specs += [pl.BlockSpec(p.shape, lambda i: (0, 0)) for p in pars]
    out_specs = [pl.BlockSpec((tm, c), lambda i: (i, 0)) for c, _ in outs]
    out_specs += [pl.BlockSpec((1, c), lambda i: (0, 0)) for c in reds]
    out_shape = [jax.ShapeDtypeStruct((t, c), dt) for c, dt in outs] + [jax.ShapeDtypeStruct((1, c), F32) for c in reds]
    res = pl.pallas_call(
        body, name=name, grid=(t // tm,), in_specs=in_specs, out_specs=out_specs, out_shape=out_shape,
        compiler_params=_cparams(dimension_semantics=("arbitrary",)),
    )(*[r[0] for r in rows], *pars)
    return res[:n_o], res[n_o:]


def _rms_fwd(x, g, *, name):
    def fn(xb, gb):
        r = lax.rsqrt(jnp.mean(xb * xb, axis=-1, keepdims=True) + RMS_EPS)
        return [xb * r * gb], []
    (h,), _ = _rowwise(fn, [x], [g], [(x.shape[1], BF16)], [], tm=256, name=name)
    return h


def _rms_bwd(x, g, dh, dres, *, name):
    def fn(xb, dhb, drb, gb):
        r = lax.rsqrt(jnp.mean(xb * xb, axis=-1, keepdims=True) + RMS_EPS)
        xh = xb * r
        dxh = dhb * gb
        dx = r * (dxh - xh * jnp.mean(dxh * xh, axis=-1, keepdims=True))
        out = drb + dx
        return [out, out], [_rowsum(out), _rowsum(dhb * xh)]
    c = x.shape[1]
    (dx, dxb), (colsum, dg) = _rowwise(fn, [x, dh, dres], [g], [(c, F32), (c, BF16)], [c, c], tm=256, name=name)
    return dx, dxb, colsum, dg


def _loss_head(x, g, tgt, *, name):
    d_model = x.shape[1]

    def fn(xb, tb, gb):
        r = lax.rsqrt(jnp.mean(xb * xb, axis=-1, keepdims=True) + RMS_EPS)
        xh = xb * r
        e = xh * gb - tb
        dy = e * (1.0 / d_model)
        dxh = dy * gb
        dx = r * (dxh - xh * jnp.mean(dxh * xh, axis=-1, keepdims=True))
        return [dx, dx], [_rowsum(e * e), _rowsum(dy * xh)]
    (dx, dxb), (sq, dg) = _rowwise(fn, [x, tgt], [g], [(d_model, F32), (d_model, BF16)], [d_model, d_model], tm=256,
                                   name=name)
    return dx, dxb, sq, dg


def _halo_rows(k):
    return 8 * ((k - 1 + 7) // 8) if k > 1 else 8


def _pad_taps(w):
    k = w.shape[0]
    kp = 8 * ((k + 7) // 8)
    return jnp.pad(w, ((0, kp - k), (0, 0)))


def _conv_chunks(k, tc):
    rc, lw = (32, 256) if k > 9 else (16, 512)
    lanes, l0 = [], 0
    while l0 < tc:
        lanes.append((l0, min(lw, tc - l0)))
        l0 += lw
    return rc, lanes


def _fold8(v):
    acc = v[0:8]
    for q in range(1, v.shape[0] // 8):
        acc = acc + v[8 * q:8 * q + 8]
    return acc


def _taps(src_ref, lead, base, rc, lanes, w_ref, k, sign, acc):
    for s in range(k):
        rows = pl.ds(base + sign * s, rc)
        acc = acc + w_ref[k - 1 - s:k - s, lanes] * src_ref[lead + (rows, lanes)]
    return acc


def _conv_cols(x, k, cols):
    c0, c = cols if cols else (0, x.shape[1])
    tc = _pick(math.gcd(c0, c), (512, 256, 128) if k > 9 else (2048, 1536, 1024, 512, 256, 128))
    return c0, c, tc


def _dwconv_fwd(x, w, b, *, name, cols=None):
    t = x.shape[0]
    k = w.shape[0]
    h = _halo_rows(k)
    tm = _pick(t, (256, 128))
    c0, c, tc = _conv_cols(x, k, cols)
    cb0 = c0 // tc
    wp = _pad_taps(w)
    kp = wp.shape[0]
    rb = tm // h
    rc, lane_chunks = _conv_chunks(k, tc)

    def body(x_ref, p_ref, w_ref, b_ref, o_ref, ext):
        i = pl.program_id(0)
        ext[pl.ds(h, tm), :] = x_ref[...]
        ext[pl.ds(0, h), :] = jnp.where(i > 0, p_ref[...], 0.0)
        for l0, lw in lane_chunks:
            lanes = pl.ds(l0, lw)
            for r0 in range(0, tm, rc):
                acc = jnp.broadcast_to(b_ref[:, lanes], (rc, lw))
                o_ref[pl.ds(r0, rc), lanes] = _taps(ext, (), h + r0, rc, lanes, w_ref, k, -1, acc)

    return pl.pallas_call(
        body, name=name, grid=(t // tm, c // tc),
        in_specs=[pl.BlockSpec((tm, tc), lambda i, j: (i, cb0 + j)),
                  pl.BlockSpec((h, tc), lambda i, j: (jnp.maximum(i * rb - 1, 0), cb0 + j)),
                  pl.BlockSpec((kp, tc), lambda i, j: (0, j)),
                  pl.BlockSpec((1, tc), lambda i, j: (0, j))],
        out_specs=pl.BlockSpec((tm, tc), lambda i, j: (i, j)),
        out_shape=jax.ShapeDtypeStruct((t, c), F32),
        scratch_shapes=[pltpu.VMEM((h + tm, tc), F32)],
        compiler_params=_cparams(dimension_semantics=("parallel", "parallel")),
    )(x, x, wp, b)


def _dwconv_bwd(x, dy, w, *, name, dx_dtype=F32, cols=None):
    t = x.shape[0]
    k = w.shape[0]
    h = _halo_rows(k)
    tm = _pick(t, (256, 128))
    c0, c, tc = _conv_cols(x, k, cols)
    cb0 = c0 // tc
    wp = _pad_taps(w)
    kp = wp.shape[0]
    rb = tm // h
    nt = t // tm
    rc, lane_chunks = _conv_chunks(k, tc)

    def body(x_ref, p_ref, dy_ref, n_ref, w_ref, dx_ref, dw_ref, db_ref, xext, dext):
        i = pl.program_id(1)

        @pl.when(i == 0)
        def _():
            dw_ref[...] = jnp.zeros_like(dw_ref)
            db_ref[...] = jnp.zeros_like(db_ref)

        xext[pl.ds(h, tm), :] = x_ref[...]
        xext[pl.ds(0, h), :] = jnp.where(i > 0, p_ref[...], 0.0)
        dext[pl.ds(0, tm), :] = dy_ref[...]
        dext[pl.ds(tm, h), :] = jnp.where(i < nt - 1, n_ref[...], 0.0)
        for l0, lw in lane_chunks:
            lanes = pl.ds(l0, lw)
            for r0 in range(0, tm, rc):
                acc = _taps(dext, (), r0, rc, lanes, w_ref, k, +1, jnp.zeros((rc, lw), F32))
                dx_ref[pl.ds(r0, rc), lanes] = acc.astype(dx_ref.dtype)
            for s in range(k):
                a8 = jnp.zeros((8, lw), F32)
                for r0 in range(0, tm, rc):
                    a8 = a8 + _fold8(xext[pl.ds(h + r0 - s, rc), lanes] * dext[pl.ds(r0, rc), lanes])
                dw_ref[k - 1 - s:k - s, lanes] += _rowsum(a8)
            b8 = jnp.zeros((8, lw), F32)
            for r0 in range(0, tm, rc):
                b8 = b8 + _fold8(dext[pl.ds(r0, rc), lanes])
            db_ref[:, lanes] += _rowsum(b8)

    dx, dw, db = pl.pallas_call(
        body, name=name, grid=(c // tc, nt),
        in_specs=[pl.BlockSpec((tm, tc), lambda j, i: (i, cb0 + j)),
                  pl.BlockSpec((h, tc), lambda j, i: (jnp.maximum(i * rb - 1, 0), cb0 + j)),
                  pl.BlockSpec((tm, tc), lambda j, i: (i, j)),
                  pl.BlockSpec((h, tc), lambda j, i: (jnp.minimum((i + 1) * rb, nt * rb - 1), j)),
                  pl.BlockSpec((kp, tc), lambda j, i: (0, j))],
        out_specs=[pl.BlockSpec((tm, tc), lambda j, i: (i, j)),
                   pl.BlockSpec((kp, tc), lambda j, i: (0, j)),
                   pl.BlockSpec((1, tc), lambda j, i: (0, j))],
        out_shape=[jax.ShapeDtypeStruct((t, c), dx_dtype), jax.ShapeDtypeStruct((kp, c), F32),
                   jax.ShapeDtypeStruct((1, c), F32)],
        scratch_shapes=[pltpu.VMEM((h + tm, tc), F32), pltpu.VMEM((tm + h, tc), F32)],
        compiler_params=_cparams(dimension_semantics=("parallel", "arbitrary")),
    )(x, x, dy, dy, wp)
    return dx, dw[:k], db


def _glu_fwd(u, b_in, *, name):
    d = u.shape[1] // 2

    def fn(ua, ug, ba, bg):
        return [(ua + ba) * _sigmoid(ug + bg)], []
    (v,), _ = _rowwise(fn, [(u, d, 0), (u, d, 1)], [b_in[:, :d], b_in[:, d:]], [(d, F32)], [], tm=256, name=name)
    return v


def _glu_bwd(u, b_in, dv, *, name):
    d = u.shape[1] // 2

    def fn(ua, ug, dvb, ba, bg):
        a = ua + ba
        s = _sigmoid(ug + bg)
        du = jnp.concatenate([dvb * s, dvb * a * s * (1.0 - s)], axis=1)
        return [du], [_rowsum(du)]
    (du,), (db,) = _rowwise(fn, [(u, d, 0), (u, d, 1), dv], [b_in[:, :d], b_in[:, d:]], [(2 * d, BF16)], [2 * d],
                            tm=256, name=name)
    return du, db


def _ln_silu_fwd(v, g, b, *, name):
    def fn(vb, gb, bb):
        mu = jnp.mean(vb, axis=-1, keepdims=True)
        xc = vb - mu
        rstd = lax.rsqrt(jnp.mean(xc * xc, axis=-1, keepdims=True) + LN_EPS)
        return [_silu(xc * rstd * gb + bb)], []
    (o,), _ = _rowwise(fn, [v], [g, b], [(v.shape[1], BF16)], [], tm=256, name=name)
    return o


def _ln_silu_bwd(v, g, b, do, *, name):
    def fn(vb, dob, gb, bb):
        mu = jnp.mean(vb, axis=-1, keepdims=True)
        xc = vb - mu
        rstd = lax.rsqrt(jnp.mean(xc * xc, axis=-1, keepdims=True) + LN_EPS)
        xh = xc * rstd
        dy = dob * _dsilu(xh * gb + bb)
        dxh = dy * gb
        dv = rstd * (dxh - jnp.mean(dxh, axis=-1, keepdims=True) - xh * jnp.mean(dxh * xh, axis=-1, keepdims=True))
        return [dv], [_rowsum(dy * xh), _rowsum(dy)]
    c = v.shape[1]
    (dv,), (dg, db) = _rowwise(fn, [v, do], [g, b], [(c, F32)], [c, c], tm=256, name=name)
    return dv, dg, db


def _ffn_mid_setup(u2c, w, b):
    _, t, nq = u2c.shape
    f = 2 * nq
    k = w.shape[0]
    h = _halo_rows(k)
    tm = _pick(t, (256, 128))
    tc = _pick(nq, (1408, 1024, 512, 256, 128))
    rc, lane_chunks = _conv_chunks(k, tc)
    return dict(t=t, nq=nq, f=f, k=k, h=h, tm=tm, tc=tc, npq=nq // tc, rb=tm // h, nt=t // tm, rc=rc,
                lane_chunks=lane_chunks, u4=u2c.reshape(2, 2, t, nq), wg=_pad_taps(w[:, :f]), wv=_pad_taps(w[:, f:]),
                bg=b[:, :f], bv=b[:, f:])


def _ffn_mid_fwd(u2c, w, b, *, name):
    p = _ffn_mid_setup(u2c, w, b)
    t, f, k, h, tm, tc, npq, rb, rc = (p[n] for n in ("t", "f", "k", "h", "tm", "tc", "npq", "rb", "rc"))
    kp = p["wg"].shape[0]

    def body(u_ref, p_ref, wg_ref, wv_ref, bg_ref, bv_ref, o_ref, ext):
        i = pl.program_id(2)
        for kind in range(2):
            ext[kind, pl.ds(h, tm), :] = u_ref[kind]
            ext[kind, pl.ds(0, h), :] = jnp.where(i > 0, p_ref[kind], 0.0)
        for l0, lw in p["lane_chunks"]:
            lanes = pl.ds(l0, lw)
            for r0 in range(0, tm, rc):
                g = _taps(ext, (0,), h + r0, rc, lanes, wg_ref, k, -1, jnp.broadcast_to(bg_ref[:, lanes], (rc, lw)))
                v = _taps(ext, (1,), h + r0, rc, lanes, wv_ref, k, -1, jnp.broadcast_to(bv_ref[:, lanes], (rc, lw)))
                o_ref[pl.ds(r0, rc), lanes] = (_silu(g) * v).astype(BF16)

    col = lambda q, jj, i: (0, q * npq + jj)
    return pl.pallas_call(
        body, name=name, grid=(2, npq, t // tm),
        in_specs=[pl.BlockSpec((2, None, tm, tc), lambda q, jj, i: (0, q, i, jj)),
                  pl.BlockSpec((2, None, h, tc), lambda q, jj, i: (0, q, jnp.maximum(i * rb - 1, 0), jj)),
                  pl.BlockSpec((kp, tc), col), pl.BlockSpec((kp, tc), col),
                  pl.BlockSpec((1, tc), col), pl.BlockSpec((1, tc), col)],
        out_specs=pl.BlockSpec((tm, tc), lambda q, jj, i: (i, q * npq + jj)),
        out_shape=jax.ShapeDtypeStruct((t, f), BF16),
        scratch_shapes=[pltpu.VMEM((2, h + tm, tc), F32)],
        compiler_params=_cparams(dimension_semantics=("parallel", "parallel", "parallel")),
    )(p["u4"], p["u4"], p["wg"], p["wv"], p["bg"], p["bv"])


def _ffn_mid_bwd(u2c, dhm, w, b, *, name):
    p = _ffn_mid_setup(u2c, w, b)
    t, nq, f, k, h, tm, tc, npq, rb, nt, rc = (p[n] for n in ("t", "nq", "f", "k", "h", "tm", "tc", "npq", "rb", "nt", "rc"))
    kp = p["wg"].shape[0]
    chunks1 = [(r0, rc) for r0 in range(0, tm, rc)] + [(tm, h)]

    def body(u_ref, p_ref, n_ref, dh_ref, nd_ref, wg_ref, wv_ref, bg_ref, bv_ref,
             du_ref, dwg_ref, dwv_ref, dbg_ref, dbv_ref, uext, dsc):
        i = pl.program_id(2)
        w_refs, dw_refs, db_refs = (wg_ref, wv_ref), (dwg_ref, dwv_ref), (dbg_ref, dbv_ref)

        @pl.when(i == 0)
        def _():
            for ref in dw_refs + db_refs:
                ref[...] = jnp.zeros_like(ref)

        for kind in range(2):
            uext[kind, pl.ds(0, h), :] = jnp.where(i > 0, p_ref[kind], 0.0)
            uext[kind, pl.ds(h, tm), :] = u_ref[kind]
            uext[kind, pl.ds(h + tm, h), :] = jnp.where(i < nt - 1, n_ref[kind], 0.0)
        for l0, lw in p["lane_chunks"]:
            lanes = pl.ds(l0, lw)
            for r0, rr in chunks1:
                g = _taps(uext, (0,), h + r0, rr, lanes, wg_ref, k, -1, jnp.broadcast_to(bg_ref[:, lanes], (rr, lw)))
                v = _taps(uext, (1,), h + r0, rr, lanes, wv_ref, k, -1, jnp.broadcast_to(bv_ref[:, lanes], (rr, lw)))
                dh = dh_ref[pl.ds(r0, rr), lanes] if r0 < tm else jnp.where(i < nt - 1, nd_ref[:, lanes], 0.0)
                sg = _sigmoid(g)
                dsc[0, pl.ds(r0, rr), lanes] = dh * v * (sg * (1.0 + g * (1.0 - sg)))
                dsc[1, pl.ds(r0, rr), lanes] = dh * (g * sg)
            for kind in range(2):
                for r0 in range(0, tm, rc):
                    acc = _taps(dsc, (kind,), r0, rc, lanes, w_refs[kind], k, +1, jnp.zeros((rc, lw), F32))
                    du_ref[kind, pl.ds(r0, rc), lanes] = acc.astype(BF16)
                for s in range(k):
                    a8 = jnp.zeros((8, lw), F32)
                    for r0 in range(0, tm, rc):
                        a8 = a8 + _fold8(uext[kind, pl.ds(h + r0 - s, rc), lanes] * dsc[kind, pl.ds(r0, rc), lanes])
                    dw_refs[kind][k - 1 - s:k - s, lanes] += _rowsum(a8)
                b8 = jnp.zeros((8, lw), F32)
                for r0 in range(0, tm, rc):
                    b8 = b8 + _fold8(dsc[kind, pl.ds(r0, rc), lanes])
                db_refs[kind][:, lanes] += _rowsum(b8)

    col = lambda q, jj, i: (0, q * npq + jj)
    nxt = lambda i: jnp.minimum((i + 1) * rb, nt * rb - 1)
    du, dwg, dwv, dbg, dbv = pl.pallas_call(
        body, name=name, grid=(2, npq, nt),
        in_specs=[pl.BlockSpec((2, None, tm, tc), lambda q, jj, i: (0, q, i, jj)),
                  pl.BlockSpec((2, None, h, tc), lambda q, jj, i: (0, q, jnp.maximum(i * rb - 1, 0), jj)),
                  pl.BlockSpec((2, None, h, tc), lambda q, jj, i: (0, q, nxt(i), jj)),
                  pl.BlockSpec((tm, tc), lambda q, jj, i: (i, q * npq + jj)),
                  pl.BlockSpec((h, tc), lambda q, jj, i: (nxt(i), q * npq + jj)),
                  pl.BlockSpec((kp, tc), col), pl.BlockSpec((kp, tc), col),
                  pl.BlockSpec((1, tc), col), pl.BlockSpec((1, tc), col)],
        out_specs=[pl.BlockSpec((2, None, tm, tc), lambda q, jj, i: (0, q, i, jj)),
                   pl.BlockSpec((kp, tc), col), pl.BlockSpec((kp, tc), col),
                   pl.BlockSpec((1, tc), col), pl.BlockSpec((1, tc), col)],
        out_shape=[jax.ShapeDtypeStruct((2, 2, t, nq), BF16), jax.ShapeDtypeStruct((kp, f), F32),
                   jax.ShapeDtypeStruct((kp, f), F32), jax.ShapeDtypeStruct((1, f), F32), jax.ShapeDtypeStruct((1, f), F32)],
        scratch_shapes=[pltpu.VMEM((2, h + tm + h, tc), F32), pltpu.VMEM((2, tm + h, tc), F32)],
        compiler_params=_cparams(dimension_semantics=("parallel", "parallel", "arbitrary")),
    )(p["u4"], p["u4"], p["u4"], dhm, dhm, p["wg"], p["wv"], p["bg"], p["bv"])
    return (du.reshape(N_CHIPS, t, nq), jnp.concatenate([dwg[:k], dwv[:k]], axis=1), jnp.concatenate([dbg, dbv], axis=1))


def _head_expander(n_heads):
    return (jnp.arange(n_heads * HEAD_DIM)[None, :] // HEAD_DIM == jnp.arange(n_heads)[:, None]).astype(BF16)


def _ssm_act(conv, dtp, bias_exp, aneg_exp, expand, *, di, name):
    q = SSD_CHUNK
    gn = (conv.shape[1] - di) // 2
    mm_dims = (((1,), (0,)), ((), ()))

    def fn(cb, dtb, bb, ab, eb):
        act = _silu(cb)
        dt = _dot3(dtb, eb, mm_dims, 0) + bb
        dt = jnp.maximum(dt, 0.0) + jnp.log(1.0 + jnp.exp(-jnp.abs(dt)))
        a = dt * ab
        tri = (lax.broadcasted_iota(jnp.int32, (q, q), 0) >= lax.broadcasted_iota(jnp.int32, (q, q), 1)).astype(F32)
        cs = _dot3(tri, a, mm_dims, 1)
        return [act[:, :di], act[:, di:di + gn], act[:, di + gn:], dt, cs], []
    outs, _ = _rowwise(fn, [conv, dtp], [bias_exp, aneg_exp, expand],
                       [(di, F32), (gn, F32), (gn, F32), (di, F32), (di, F32)], [], tm=q, name=name)
    return outs


def _head_masks(q):
    lane = lax.broadcasted_iota(jnp.int32, (q, LANES), 1)
    return lane < HEAD_DIM


def _pair_cols(cs, lo):
    sw = pltpu.roll(cs, HEAD_DIM, 1)
    return jnp.where(lo, cs, sw), jnp.where(lo, sw, cs)


def _ssd_fwd(xs, dt_exp, cs_exp, cs_rows, bm, cm, *, name):
    t, di = xs.shape
    q = SSD_CHUNK
    hg = di // N_GROUPS
    npair = hg // LANES
    nheads = hg // HEAD_DIM
    nc = t // q
    n = D_STATE
    cpb = _pick(nc, (4, 2, 1))
    qb = cpb * q

    def body(xs_ref, dt_ref, cs_ref, csr_ref, b_ref, c_ref, y_ref, st_ref, s_scr):
        ci = pl.program_id(1)

        @pl.when(ci == 0)
        def _():
            s_scr[...] = jnp.zeros_like(s_scr)

        tri = lax.broadcasted_iota(jnp.int32, (q, q), 0) >= lax.broadcasted_iota(jnp.int32, (q, q), 1)
        lo = _head_masks(q)
        for cc in range(cpb):
            rows = pl.ds(cc * q, q)
            bb = b_ref[rows, :].astype(BF16)
            cb_ = c_ref[rows, :].astype(BF16)
            cbm = lax.dot_general(cb_, bb, (((1,), (1,)), ((), ())), preferred_element_type=F32)
            csr = csr_ref[0, :, pl.ds(cc * q, q)]
            for p in range(npair):
                sl = pl.ds(p * LANES, LANES)
                x = xs_ref[rows, sl] * dt_ref[rows, sl]
                cs = cs_ref[rows, sl]
                col0, col1 = _pair_cols(cs, lo)
                l0 = jnp.where(tri, jnp.exp(jnp.minimum(col0 - csr[2 * p:2 * p + 1, :], 0.0)), 0.0)
                l1 = jnp.where(tri, jnp.exp(jnp.minimum(col1 - csr[2 * p + 1:2 * p + 2, :], 0.0)), 0.0)
                xb = x.astype(BF16)
                yd = jnp.where(lo, jnp.dot((cbm * l0).astype(BF16), xb, preferred_element_type=F32),
                               jnp.dot((cbm * l1).astype(BF16), xb, preferred_element_type=F32))
                s = s_scr[p]
                st_ref[0, cc, p] = s
                yo = jnp.exp(cs) * jnp.dot(cb_, s.astype(BF16), preferred_element_type=F32)
                y_ref[rows, sl] = yd + yo
                cs_end = cs[q - 1:q, :]
                xd = (x * jnp.exp(cs_end - cs)).astype(BF16)
                s_scr[p] = jnp.exp(cs_end) * s + lax.dot_general(bb, xd, (((0,), (0,)), ((), ())),
                                                                 preferred_element_type=F32)

    return pl.pallas_call(
        body, name=name, grid=(N_GROUPS, nc // cpb),
        in_specs=[pl.BlockSpec((qb, hg), lambda g, c: (c, g)),
                  pl.BlockSpec((qb, hg), lambda g, c: (c, g)),
                  pl.BlockSpec((qb, hg), lambda g, c: (c, g)),
                  pl.BlockSpec((1, nheads, qb), lambda g, c: (g, 0, c)),
                  pl.BlockSpec((qb, n), lambda g, c: (c, g)),
                  pl.BlockSpec((qb, n), lambda g, c: (c, g))],
        out_specs=[pl.BlockSpec((qb, hg), lambda g, c: (c, g)),
                   pl.BlockSpec((1, cpb, npair, n, LANES), lambda g, c: (g, c, 0, 0, 0))],
        out_shape=[jax.ShapeDtypeStruct((t, di), F32),
                   jax.ShapeDtypeStruct((N_GROUPS, nc, npair, n, LANES), F32)],
        scratch_shapes=[pltpu.VMEM((npair, n, LANES), F32)],
        compiler_params=_cparams(dimension_semantics=("parallel", "arbitrary")),
    )(xs, dt_exp, cs_exp, cs_rows, bm, cm)


def _dot3(a, b, dims, split):
    rest = (a, b)[split].astype(F32)
    other = (a, b)[1 - split].astype(BF16)
    acc = None
    for _ in range(3):
        part = rest.astype(BF16)
        rest = rest - part.astype(F32)
        d = (lax.dot_general(part, other, dims, preferred_element_type=F32) if split == 0
             else lax.dot_general(other, part, dims, preferred_element_type=F32))
        acc = d if acc is None else acc + d
    return acc


def _ssd_bwd(xs, dt_exp, cs_exp, cs_rows, bm, cm, dy, states, aneg_exp, *, name):
    t, di = xs.shape
    q = SSD_CHUNK
    hg = di // N_GROUPS
    npair = hg // LANES
    nheads = hg // HEAD_DIM
    nc = t // q
    n = D_STATE
    nt_dims = (((1,), (1,)), ((), ()))
    tn_dims = (((0,), (0,)), ((), ()))

    mm_dims = (((1,), (0,)), ((), ()))

    def body(xs_ref, dt_ref, cs_ref, csr_ref, b_ref, c_ref, dy_ref, st_ref, an_ref,
             dxp_ref, db_ref, dc_ref, ddt_ref, dan_ref, r_scr):
        ci = pl.program_id(1)

        @pl.when(ci == 0)
        def _():
            r_scr[...] = jnp.zeros_like(r_scr)
            dan_ref[...] = jnp.zeros_like(dan_ref)

        bb = b_ref[...].astype(BF16)
        cb_ = c_ref[...].astype(BF16)
        cbm = lax.dot_general(cb_, bb, nt_dims, preferred_element_type=F32)
        row = lax.broadcasted_iota(jnp.int32, (q, q), 0)
        col = lax.broadcasted_iota(jnp.int32, (q, q), 1)
        tri = row >= col
        triu = (row <= col).astype(F32)
        trisl = (row > col).astype(F32)
        ones2 = (lax.broadcasted_iota(jnp.int32, (LANES, LANES), 0) // HEAD_DIM
                 == lax.broadcasted_iota(jnp.int32, (LANES, LANES), 1) // HEAD_DIM).astype(BF16)
        onesq = jnp.ones((q, LANES), BF16)
        last = lax.broadcasted_iota(jnp.int32, (q, LANES), 0) == q - 1
        lo = _head_masks(q)
        csr = csr_ref[0]
        dcb = jnp.zeros((q, q), F32)
        dc_acc = jnp.zeros((q, n), F32)
        db_acc = jnp.zeros((q, n), F32)
        for p in range(npair):
            sl = pl.ds(p * LANES, LANES)
            xsv = xs_ref[:, sl]
            dtv = dt_ref[:, sl]
            x = xsv * dtv
            cs = cs_ref[:, sl]
            dyv = dy_ref[:, sl]
            col0, col1 = _pair_cols(cs, lo)
            l0 = jnp.where(tri, jnp.exp(jnp.minimum(col0 - csr[2 * p:2 * p + 1, :], 0.0)), 0.0)
            l1 = jnp.where(tri, jnp.exp(jnp.minimum(col1 - csr[2 * p + 1:2 * p + 2, :], 0.0)), 0.0)
            xb = x.astype(BF16)
            dyb = dyv.astype(BF16)
            g0 = lax.dot_general(jnp.where(lo, dyv, 0.0).astype(BF16), xb, nt_dims, preferred_element_type=F32)
            g1 = lax.dot_general(jnp.where(lo, 0.0, dyv).astype(BF16), xb, nt_dims, preferred_element_type=F32)
            gl0, gl1 = g0 * l0, g1 * l1
            dcb = dcb + gl0 + gl1
            w0, w1 = cbm * gl0, cbm * gl1
            dxd = jnp.where(lo,
                            lax.dot_general((cbm * l0).astype(BF16), dyb, tn_dims, preferred_element_type=F32),
                            lax.dot_general((cbm * l1).astype(BF16), dyb, tn_dims, preferred_element_type=F32))
            e = jnp.exp(cs)
            cs_end = cs[q - 1:q, :]
            dte = jnp.exp(cs_end - cs)
            dend = jnp.exp(cs_end)
            sf = st_ref[0, 0, p]
            sb = sf.astype(BF16)
            r = r_scr[p]
            rb = r.astype(BF16)
            dyeb = (dyv * e).astype(BF16)
            dc_acc = dc_acc + lax.dot_general(dyeb, sb, nt_dims, preferred_element_type=F32)
            dxo = dte * jnp.dot(bb, rb, preferred_element_type=F32)
            db_acc = db_acc + lax.dot_general((x * dte).astype(BF16), rb, nt_dims, preferred_element_type=F32)
            r_scr[p] = dend * r + lax.dot_general(cb_, dyeb, tn_dims, preferred_element_type=F32)
            dx = dxd + dxo
            dxp_ref[:, sl] = dx
            yoff = e * jnp.dot(cb_, sb, preferred_element_type=F32)
            w0b, w1b = w0.astype(BF16), w1.astype(BF16)
            rw = jnp.where(lo, jnp.sum(w0b.astype(F32), axis=1, keepdims=True), jnp.sum(w1b.astype(F32), axis=1, keepdims=True))
            cw = jnp.where(lo, lax.dot_general(w0b, onesq, tn_dims, preferred_element_type=F32),
                           lax.dot_general(w1b, onesq, tn_dims, preferred_element_type=F32))
            through = jnp.where(last, dend * _rowsum(r * sf), 0.0)
            suf = jnp.dot((dyv * yoff + through).astype(BF16), ones2, preferred_element_type=F32) + rw - cw
            pre = jnp.dot((dxo * x).astype(BF16), ones2, preferred_element_type=F32)
            da = _dot3(triu, suf, mm_dims, 1) + _dot3(trisl, pre, mm_dims, 1)
            qs = jnp.dot((dx * xsv).astype(BF16), ones2, preferred_element_type=F32)
            ddt_ref[:, sl] = da * an_ref[:, sl] + qs
            dan_ref[:, sl] += _rowsum(da * dtv)
        dcbb = dcb.astype(BF16)
        dc_ref[...] = dc_acc + jnp.dot(dcbb, bb, preferred_element_type=F32)
        db_ref[...] = db_acc + lax.dot_general(dcbb, cb_, tn_dims, preferred_element_type=F32)

    rev = lambda g, c: (nc - 1 - c, g)
    return pl.pallas_call(
        body, name=name, grid=(N_GROUPS, nc),
        in_specs=[pl.BlockSpec((q, hg), rev), pl.BlockSpec((q, hg), rev), pl.BlockSpec((q, hg), rev),
                  pl.BlockSpec((1, nheads, q), lambda g, c: (g, 0, nc - 1 - c)),
                  pl.BlockSpec((q, n), rev), pl.BlockSpec((q, n), rev),
                  pl.BlockSpec((q, hg), rev),
                  pl.BlockSpec((1, 1, npair, n, LANES), lambda g, c: (g, nc - 1 - c, 0, 0, 0)),
                  pl.BlockSpec((1, hg), lambda g, c: (0, g))],
        out_specs=[pl.BlockSpec((q, hg), rev), pl.BlockSpec((q, n), rev), pl.BlockSpec((q, n), rev),
                   pl.BlockSpec((q, hg), rev), pl.BlockSpec((1, hg), lambda g, c: (0, g))],
        out_shape=[jax.ShapeDtypeStruct((t, di), F32), jax.ShapeDtypeStruct((t, N_GROUPS * n), F32),
                   jax.ShapeDtypeStruct((t, N_GROUPS * n), F32), jax.ShapeDtypeStruct((t, di), F32),
                   jax.ShapeDtypeStruct((1, di), F32)],
        scratch_shapes=[pltpu.VMEM((npair, n, LANES), F32)],
        compiler_params=_cparams(dimension_semantics=("parallel", "arbitrary")),
    )(xs, dt_exp, cs_exp, cs_rows, bm, cm, dy, states, aneg_exp)


def _group_stats(w, gw):
    return [lax.rsqrt(jnp.mean(w[:, i * gw:(i + 1) * gw] ** 2, axis=-1, keepdims=True) + RMS_EPS)
            for i in range(N_GROUPS)]


def _gated_norm_fwd(y_ssd, xs, z, d_exp, g, *, name):
    di = xs.shape[1]
    gw = di // N_GROUPS

    def fn(yb, xb, zb, db, gb):
        w = (yb + db * xb) * _silu(zb)
        rs = _group_stats(w, gw)
        return [jnp.concatenate([w[:, i * gw:(i + 1) * gw] * rs[i] for i in range(N_GROUPS)], axis=1) * gb], []
    (o,), _ = _rowwise(fn, [y_ssd, xs, (z, di, 0)], [d_exp, g], [(di, BF16)], [], tm=128, name=name)
    return o


def _gated_norm_bwd(y_ssd, xs, z, d_exp, g, do, *, name):
    di = xs.shape[1]
    gw = di // N_GROUPS

    def fn(yb, xb, zb, dob, db, gb):
        yy = yb + db * xb
        sz = _silu(zb)
        w = yy * sz
        rs = _group_stats(w, gw)
        dwh = dob * gb
        wh_parts, dw_parts = [], []
        for i in range(N_GROUPS):
            sl = slice(i * gw, (i + 1) * gw)
            wh = w[:, sl] * rs[i]
            wh_parts.append(wh)
            dw_parts.append(rs[i] * (dwh[:, sl] - wh * jnp.mean(dwh[:, sl] * wh, axis=-1, keepdims=True)))
        wh = jnp.concatenate(wh_parts, axis=1)
        dw = jnp.concatenate(dw_parts, axis=1)
        dy = dw * sz
        dz = dw * yy * _dsilu(zb)
        return [dy, dz], [_rowsum(dob * wh), _rowsum(dy * xb)]
    (dy, dz), (dg, dd) = _rowwise(fn, [y_ssd, xs, (z, di, 0), do], [d_exp, g], [(di, F32), (di, BF16)], [di, di],
                                  tm=128, name=name)
    return dy, dz, dg, dd


def _ssm_act_bwd(conv, dtp, dxp, dy, dbm, dcm, ddt_exp, dt_exp, bias_exp, d_exp, expand, *, di, name):
    gn = dbm.shape[1]

    def fn(cb, dtb, dxpb, dyb, dbb, dcb, ddtb, dteb, bb, db, eb):
        dxs = dxpb * dteb + dyb * db
        dact = jnp.concatenate([dxs, dbb, dcb], axis=1)
        dconv = dact * _dsilu(cb)
        ddtp = ddtb * _sigmoid(_dot3(dtb, eb, (((1,), (0,)), ((), ())), 0) + bb)
        return [dconv, ddtp], [_rowsum(ddtp)]
    (dconv, ddtp), (dbias,) = _rowwise(fn, [conv, dtp, dxp, dy, dbm, dcm, ddt_exp, dt_exp], [bias_exp, d_exp, expand],
                                       [(di + 2 * gn, F32), (di, F32)], [di], tm=64, name=name)
    return dconv, ddtp, dbias


def _adamw(w, g, m, v, *, name):
    r, c = w.shape
    c1 = 1.0 / (1.0 - ADAM_B1 ** ADAM_STEP)
    c2 = 1.0 / (1.0 - ADAM_B2 ** ADAM_STEP)

    def fn(wb, gb, mb, vb):
        mn = ADAM_B1 * mb + (1.0 - ADAM_B1) * gb
        vn = ADAM_B2 * vb + (1.0 - ADAM_B2) * (gb * gb)
        delta = -ADAM_LR * ((mn * c1) / (jnp.sqrt(vn * c2) + ADAM_EPS) + ADAM_WD * wb)
        return [delta, mn, vn], []
    cap = max(8, ADAMW_BLOCK_ELEMS // c)
    tm = _pick(r, [p for p in (512, 256, 128, 64, 32, 16, 8) if p <= cap])
    (d, mn, vn), _ = _rowwise(fn, [w, g, m, v], [], [(c, F32)] * 3, [], tm=tm, name=name)
    return d, mn, vn


def _add_pair(sel, g, r, *, name):
    _, _, rows, cols = g.shape
    tm = _pick(rows, (256, 128, 64, 32, 16))

    def body(s_ref, g_ref, r_ref, o_ref):
        o_ref[...] = (g_ref[...].astype(F32) + r_ref[...].astype(F32)).astype(BF16)

    return pl.pallas_call(
        body, name=name,
        grid_spec=pltpu.PrefetchScalarGridSpec(
            num_scalar_prefetch=1, grid=(N_CHIPS, rows // tm),
            in_specs=[pl.BlockSpec((None, None, tm, cols), lambda j, i, s: (j, s[0], i, 0)),
                      pl.BlockSpec((None, tm, cols), lambda j, i, s: (j, i, 0))],
            out_specs=pl.BlockSpec((None, tm, cols), lambda j, i, s: (j, i, 0))),
        out_shape=jax.ShapeDtypeStruct((N_CHIPS, rows, cols), BF16),
        compiler_params=_cparams(dimension_semantics=("parallel", "parallel")),
    )(sel, g, r)


def _add_four(sel, p, r, *, name):
    _, rows, cols = p.shape
    tm = _pick(rows, (256, 128, 64, 32, 16))

    def body(s_ref, p_ref, r0, r1, r2, o_ref):
        o_ref[...] = ((p_ref[...].astype(F32) + r0[...].astype(F32)) + r1[...].astype(F32)) + r2[...].astype(F32)

    rspec = lambda k: pl.BlockSpec((None, tm, cols), lambda i, s, k=k: (k, i, 0))
    return pl.pallas_call(
        body, name=name,
        grid_spec=pltpu.PrefetchScalarGridSpec(
            num_scalar_prefetch=1, grid=(rows // tm,),
            in_specs=[pl.BlockSpec((None, tm, cols), lambda i, s: (s[0], i, 0)), rspec(0), rspec(1), rspec(2)],
            out_specs=pl.BlockSpec((None, tm, cols), lambda i, s: (s[1], i, 0))),
        out_shape=jax.ShapeDtypeStruct((2, rows, cols), F32),
        compiler_params=_cparams(dimension_semantics=("parallel",)),
    )(sel, p, r, r, r)


def _sum8(g, *, name):
    _, rows, cols = g.shape
    tm = _pick(rows, (512, 256, 128, 64, 32, 16, 8))

    def body(g_ref, o_ref):
        acc = g_ref[0]
        for k in range(1, 8):
            acc = acc + g_ref[k]
        o_ref[...] = acc

    return pl.pallas_call(
        body, name=name, grid=(rows // tm,),
        in_specs=[pl.BlockSpec((8, tm, cols), lambda i: (0, i, 0))],
        out_specs=pl.BlockSpec((tm, cols), lambda i: (i, 0)),
        out_shape=jax.ShapeDtypeStruct((rows, cols), F32),
        compiler_params=_cparams(dimension_semantics=("parallel",)),
    )(g)


def _place():
    x, y, c = lax.axis_index("x"), lax.axis_index("y"), lax.axis_index("c")
    chips = [(1 - x, y), (x, 1 - y), (1 - x, 1 - y)]
    return x, y, c, chips


def _rcopy(src, dst, send_sems, recv_sems, k, to):
    return pltpu.make_async_remote_copy(src_ref=src, dst_ref=dst, send_sem=send_sems.at[k], recv_sem=recv_sems.at[k],
                                        device_id=to, device_id_type=MESH)


def _gather_chips(packs, *, name):
    n = len(packs)

    def body(*refs):
        srcs, outs, (send_sems, recv_sems) = refs[:n], refs[n:2 * n], refs[2 * n:]
        x, y, c, chips = _place()
        sibling = (x, y, 1 - c)
        me = 2 * x + y
        first, passed = [], []
        for t, (src, out) in enumerate(zip(srcs, outs)):
            for k, (cx, cy) in enumerate(chips):
                cp = _rcopy(src.at[c], out.at[me, c], send_sems, recv_sems, 6 * t + k, (cx, cy, c))
                cp.start()
                first.append(cp)
        for t, out in enumerate(outs):
            for k, (cx, cy) in enumerate(chips):
                blk = out.at[2 * cx + cy, c]
                _rcopy(blk, blk, send_sems, recv_sems, 6 * t + k, (cx, cy, c)).wait_recv()
                fw = _rcopy(blk, blk, send_sems, recv_sems, 6 * t + 3 + k, sibling)
                fw.start()
                passed.append(fw)
        for t, out in enumerate(outs):
            for k, (cx, cy) in enumerate(chips):
                blk = out.at[2 * cx + cy, 1 - c]
                _rcopy(blk, blk, send_sems, recv_sems, 6 * t + 3 + k, sibling).wait_recv()
        for cp in first + passed:
            cp.wait_send()

    return pl.pallas_call(
        body, name=name, in_specs=[ANY] * n, out_specs=[ANY] * n,
        out_shape=[jax.ShapeDtypeStruct((N_CHIPS,) + p.shape, p.dtype) for p in packs],
        scratch_shapes=[pltpu.SemaphoreType.DMA((6 * n,)), pltpu.SemaphoreType.DMA((6 * n,))],
    )(*packs)


def _gather_devices(pack, *, name):
    rows, cols = pack.shape

    def body(src, out, send_sems, recv_sems, local_sem):
        x, y, c, chips = _place()
        sibling = (x, y, 1 - c)

        def blk(px, py, pc):
            return out.at[4 * px + 2 * py + pc]

        mine = pltpu.make_async_copy(src, blk(x, y, c), local_sem)
        mine.start()
        first = [_rcopy(src, blk(x, y, c), send_sems, recv_sems, 0, sibling)]
        first += [_rcopy(src, blk(x, y, c), send_sems, recv_sems, 1 + k, (cx, cy, c)) for k, (cx, cy) in enumerate(chips)]
        for cp in first:
            cp.start()
        passed = []
        for k, (cx, cy) in enumerate(chips):
            b = blk(cx, cy, c)
            _rcopy(b, b, send_sems, recv_sems, 1 + k, (cx, cy, c)).wait_recv()
            fw = _rcopy(b, b, send_sems, recv_sems, 4 + k, sibling)
            fw.start()
            passed.append(fw)
        b = blk(x, y, 1 - c)
        _rcopy(b, b, send_sems, recv_sems, 0, sibling).wait_recv()
        for k, (cx, cy) in enumerate(chips):
            b = blk(cx, cy, 1 - c)
            _rcopy(b, b, send_sems, recv_sems, 4 + k, sibling).wait_recv()
        for cp in first + passed:
            cp.wait_send()
        mine.wait()

    return pl.pallas_call(
        body, name=name, in_specs=[ANY], out_specs=ANY,
        out_shape=jax.ShapeDtypeStruct((8, rows, cols), pack.dtype),
        scratch_shapes=[pltpu.SemaphoreType.DMA((7,)), pltpu.SemaphoreType.DMA((7,)), pltpu.SemaphoreType.DMA],
    )(pack)


def _swap_halves(gs, *, name):
    n = len(gs)

    def body(*refs):
        srcs, outs, (send_sems, recv_sems) = refs[:n], refs[n:2 * n], refs[2 * n:]
        x, y, c, _ = _place()
        cps = [_rcopy(src.at[j, 1 - c], out.at[j], send_sems, recv_sems, N_CHIPS * t + j, (x, y, 1 - c))
               for t, (src, out) in enumerate(zip(srcs, outs)) for j in range(N_CHIPS)]
        for cp in cps:
            cp.start()
        for cp in cps:
            cp.wait()

    return pl.pallas_call(
        body, name=name, in_specs=[ANY] * n, out_specs=[ANY] * n,
        out_shape=[jax.ShapeDtypeStruct((N_CHIPS,) + g.shape[2:], g.dtype) for g in gs],
        scratch_shapes=[pltpu.SemaphoreType.DMA((N_CHIPS * n,)), pltpu.SemaphoreType.DMA((N_CHIPS * n,))],
    )(*gs)


def _join_halves(rs, *, name):
    n = len(rs)

    def body(*refs):
        srcs, outs, (send_sems, recv_sems) = refs[:n], refs[n:2 * n], refs[2 * n:]
        x, y, c, _ = _place()
        cps = [_rcopy(src.at[c], out.at[c], send_sems, recv_sems, t, (x, y, 1 - c))
               for t, (src, out) in enumerate(zip(srcs, outs))]
        for cp in cps:
            cp.start()
        for t, out in enumerate(outs):
            b = out.at[1 - c]
            _rcopy(b, b, send_sems, recv_sems, t, (x, y, 1 - c)).wait_recv()
        for cp in cps:
            cp.wait_send()

    return pl.pallas_call(
        body, name=name, in_specs=[ANY] * n, out_specs=[ANY] * n,
        out_shape=[jax.ShapeDtypeStruct(r.shape, r.dtype) for r in rs], input_output_aliases={t: t for t in range(n)},
        scratch_shapes=[pltpu.SemaphoreType.DMA((n,)), pltpu.SemaphoreType.DMA((n,))],
    )(*rs)


def _flat_rows(parts, cols):
    flat = jnp.concatenate([p.reshape(-1) for p in parts])
    n = flat.shape[0]
    rows = -(-n // cols)
    unit = 256 if rows > 256 else 8
    rows = unit * (-(-rows // unit))
    return jnp.pad(flat, (0, rows * cols - n)).reshape(rows, cols)


def _expand(v, di):
    return jnp.repeat(v, HEAD_DIM).reshape(1, di)


def kernel(x, norm_mix_g, norm_ffn_g, norm_final_g, cv_w_in, cv_b_in, cv_w_dw, cv_b_dw, cv_ln_g, cv_ln_b, cv_w_out, cv_b_out, ssm_w_in, ssm_w_conv, ssm_b_conv, ssm_dt_bias, ssm_a_log, ssm_d, ssm_norm_g, ssm_w_out, ffn_w_up, ffn_w_dw, ffn_b_dw, ffn_w_down, loss_target, m_norm_mix_g, m_norm_ffn_g, m_norm_final_g, m_cv_w_in, m_cv_b_in, m_cv_w_dw, m_cv_b_dw, m_cv_ln_g, m_cv_ln_b, m_cv_w_out, m_cv_b_out, m_ssm_w_in, m_ssm_w_conv, m_ssm_b_conv, m_ssm_dt_bias, m_ssm_a_log, m_ssm_d, m_ssm_norm_g, m_ssm_w_out, m_ffn_w_up, m_ffn_w_dw, m_ffn_b_dw, m_ffn_w_down, v_norm_mix_g, v_norm_ffn_g, v_norm_final_g, v_cv_w_in, v_cv_b_in, v_cv_w_dw, v_cv_b_dw, v_cv_ln_g, v_cv_ln_b, v_cv_w_out, v_cv_b_out, v_ssm_w_in, v_ssm_w_conv, v_ssm_b_conv, v_ssm_dt_bias, v_ssm_a_log, v_ssm_d, v_ssm_norm_g, v_ssm_w_out, v_ffn_w_up, v_ffn_w_dw, v_ffn_b_dw, v_ffn_w_down):
    weights = dict(norm_mix_g=norm_mix_g, norm_ffn_g=norm_ffn_g, norm_final_g=norm_final_g, cv_w_in=cv_w_in, cv_b_in=cv_b_in, cv_w_dw=cv_w_dw, cv_b_dw=cv_b_dw, cv_ln_g=cv_ln_g, cv_ln_b=cv_ln_b, cv_w_out=cv_w_out, cv_b_out=cv_b_out, ssm_w_in=ssm_w_in, ssm_w_conv=ssm_w_conv, ssm_b_conv=ssm_b_conv, ssm_dt_bias=ssm_dt_bias, ssm_a_log=ssm_a_log, ssm_d=ssm_d, ssm_norm_g=ssm_norm_g, ssm_w_out=ssm_w_out, ffn_w_up=ffn_w_up, ffn_w_dw=ffn_w_dw, ffn_b_dw=ffn_b_dw, ffn_w_down=ffn_w_down)
    mom_m = dict(norm_mix_g=m_norm_mix_g, norm_ffn_g=m_norm_ffn_g, norm_final_g=m_norm_final_g, cv_w_in=m_cv_w_in, cv_b_in=m_cv_b_in, cv_w_dw=m_cv_w_dw, cv_b_dw=m_cv_b_dw, cv_ln_g=m_cv_ln_g, cv_ln_b=m_cv_ln_b, cv_w_out=m_cv_w_out, cv_b_out=m_cv_b_out, ssm_w_in=m_ssm_w_in, ssm_w_conv=m_ssm_w_conv, ssm_b_conv=m_ssm_b_conv, ssm_dt_bias=m_ssm_dt_bias, ssm_a_log=m_ssm_a_log, ssm_d=m_ssm_d, ssm_norm_g=m_ssm_norm_g, ssm_w_out=m_ssm_w_out, ffn_w_up=m_ffn_w_up, ffn_w_dw=m_ffn_w_dw, ffn_b_dw=m_ffn_b_dw, ffn_w_down=m_ffn_w_down)
    mom_v = dict(norm_mix_g=v_norm_mix_g, norm_ffn_g=v_norm_ffn_g, norm_final_g=v_norm_final_g, cv_w_in=v_cv_w_in, cv_b_in=v_cv_b_in, cv_w_dw=v_cv_w_dw, cv_b_dw=v_cv_b_dw, cv_ln_g=v_cv_ln_g, cv_ln_b=v_cv_ln_b, cv_w_out=v_cv_w_out, cv_b_out=v_cv_b_out, ssm_w_in=v_ssm_w_in, ssm_w_conv=v_ssm_w_conv, ssm_b_conv=v_ssm_b_conv, ssm_dt_bias=v_ssm_dt_bias, ssm_a_log=v_ssm_a_log, ssm_d=v_ssm_d, ssm_norm_g=v_ssm_norm_g, ssm_w_out=v_ssm_w_out, ffn_w_up=v_ffn_w_up, ffn_w_dw=v_ffn_w_dw, ffn_b_dw=v_ffn_b_dw, ffn_w_down=v_ffn_w_down)
    names = list(weights)

    xt = x[0]
    tgt = loss_target[0]
    t, d = xt.shape
    depth = norm_mix_g.shape[0]
    n_cv, n_ssm = cv_w_in.shape[0], ssm_w_in.shape[0]
    di = ssm_w_out.shape[1] * N_CHIPS
    n_heads = di // HEAD_DIM
    gn = N_GROUPS * D_STATE
    ssm_in = ssm_w_in.shape[2] * N_CHIPS
    chip = 2 * lax.axis_index("x") + lax.axis_index("y")

    cq = ssm_w_in.shape[2]
    cqp = LANES * (-(-cq // LANES))
    ssm_inp = N_CHIPS * cqp
    by_col = ("cv_w_in", "ffn_w_up")
    big_names = ("cv_w_in", "cv_w_out", "ssm_w_in", "ssm_w_out", "ffn_w_up", "ffn_w_down")

    def layer_tensors(i):
        mixer = [("cv_w_in", i // 2), ("cv_w_out", i // 2)] if i % 2 == 0 else [("ssm_w_in", i // 2), ("ssm_w_out", i // 2)]
        return mixer + [("ffn_w_up", i), ("ffn_w_down", i)]

    def halves(a):
        return a.reshape((2, a.shape[0] // 2) + a.shape[1:])

    order = [key for i in range(depth) for key in layer_tensors(i)]
    shards = []
    for nm, l in order:
        w = weights[nm][l]
        if nm == "ssm_w_in":
            w = jnp.pad(w, ((0, 0), (0, cqp - cq)))
        shards.append(halves(w.astype(BF16)))
    shards = dict(zip(order, lax.optimization_barrier(shards)))
    full = {}

    def arrived(key, g):
        g = lax.dynamic_update_index_in_dim(g, shards[key], chip, 0)
        g = g.reshape((N_CHIPS, 2 * g.shape[2], g.shape[3]))
        if key[0] == "ssm_w_in":
            full[key] = jnp.concatenate([g[jj, :, :cq] for jj in range(N_CHIPS)] + [jnp.zeros((d, ssm_inp - ssm_in), BF16)],
                                        axis=1)
        else:
            full[key] = g if key[0] in by_col else g.reshape(N_CHIPS * g.shape[1], g.shape[2])

    for key, g in zip(order[:2], _gather_chips([shards[k] for k in order[:2]], name="gather_weights")):
        arrived(key, g)
    landed = {}

    def mm_fwd(key, a, **kw):
        pos = order.index(key)
        kw["b_chips"] = key[0] in by_col
        jobs = []
        if pos + 1 in landed:
            jobs.append(("pass", landed.pop(pos + 1)))
        if pos + 2 < len(order):
            jobs.append(("gather", shards[order[pos + 2]]))
        if not jobs:
            return _matmul(a, full[key], **kw)
        out, *got = _matmul(a, full[key], carry=jobs, **kw)
        for (kind, _), g in zip(jobs, got):
            if kind == "pass":
                arrived(order[pos + 1], g)
            else:
                landed[pos + 2] = g
        return out

    expand = _head_expander(n_heads)

    small_sharded = ["cv_w_dw", "ssm_w_conv", "ssm_b_conv", "ssm_norm_g", "ffn_w_dw"]
    spack = _flat_rows([weights[nm] for nm in small_sharded], LANES)
    sg = _gather_devices(spack, name="gather_small").reshape(8, -1)[::2]
    o = 0
    for nm in small_sharded:
        shp = weights[nm].shape
        n = weights[nm].size
        full[nm] = jnp.concatenate([sg[j, o:o + n].reshape(shp) for j in range(N_CHIPS)], axis=-1)
        o += n

    row = lambda v: v.reshape(1, -1)

    saved = []
    xc = xt
    for i in range(depth):
        j = i // 2
        s = {"x_in": xc}
        h = _rms_fwd(xc, row(norm_mix_g[i]), name="rms_mix_fwd")
        s["h"] = h
        if i % 2 == 0:
            u = mm_fwd(("cv_w_in", j), h, name="cv_in_fwd")
            v1 = _glu_fwd(u, row(cv_b_in[j]), name="cv_glu_fwd")
            v2 = _dwconv_fwd(v1, full["cv_w_dw"][j], row(cv_b_dw[j]), name="cv_dw_fwd")
            v4 = _ln_silu_fwd(v2, row(cv_ln_g[j]), row(cv_ln_b[j]), name="cv_ln_fwd")
            xc = mm_fwd(("cv_w_out", j), v4, bias=row(cv_b_out[j]), res=xc, name="cv_out_fwd")
            s.update(u=u, v1=v1, v2=v2, v4=v4)
        else:
            zx = mm_fwd(("ssm_w_in", j), h, name="ssm_in_fwd")
            xbc_cols = (di, di + 2 * gn)
            dtp = zx[:, 2 * di + 2 * gn:ssm_in]
            conv = _dwconv_fwd(zx, full["ssm_w_conv"][j], row(full["ssm_b_conv"][j]), cols=xbc_cols, name="ssm_dw_fwd")
            bias_exp = _expand(ssm_dt_bias[j], di)
            aneg_exp = _expand(-jnp.exp(ssm_a_log[j]), di)
            d_exp = _expand(ssm_d[j], di)
            xs, bm, cm, dt_exp, cs_exp = _ssm_act(conv, dtp, bias_exp, aneg_exp, expand, di=di, name="ssm_act_fwd")
            cs_rows = cs_exp[:, ::HEAD_DIM].T.reshape(N_GROUPS, n_heads // N_GROUPS, t)
            y_ssd, states = _ssd_fwd(xs, dt_exp, cs_exp, cs_rows, bm, cm, name="ssd_fwd")
            gnrm = _gated_norm_fwd(y_ssd, xs, zx, d_exp, row(full["ssm_norm_g"][j]), name="ssm_norm_fwd")
            xc = mm_fwd(("ssm_w_out", j), gnrm, res=xc, name="ssm_out_fwd")
            s.update(zx=zx, xbc_cols=xbc_cols, dtp=dtp, conv=conv, bias_exp=bias_exp, aneg_exp=aneg_exp,
                     d_exp=d_exp, xs=xs, bm=bm, cm=cm, dt_exp=dt_exp, cs_exp=cs_exp, cs_rows=cs_rows, y_ssd=y_ssd,
                     states=states, gnrm=gnrm)
        s["x_mid"] = xc
        h2 = _rms_fwd(xc, row(norm_ffn_g[i]), name="rms_ffn_fwd")
        u2 = mm_fwd(("ffn_w_up", i), h2, out_chips=True, name="ffn_up_fwd")
        hm = _ffn_mid_fwd(u2, full["ffn_w_dw"][i], row(ffn_b_dw[i]), name="ffn_mid_fwd")
        xc = mm_fwd(("ffn_w_down", i), hm, res=xc, name="ffn_down_fwd")
        s.update(h2=h2, u2=u2, hm=hm)
        saved.append(s)

    dx, dxb, sq, dg_final = _loss_head(xc, row(norm_final_g), tgt, name="loss_head")
    loss_part = 0.5 / d * jnp.sum(sq)
    gr = {nm: [None] * weights[nm].shape[0] for nm in names if nm != "norm_final_g"}
    reduced = {}
    sel_c = jnp.reshape(lax.axis_index("c"), (1,)).astype(jnp.int32)
    sel_j = jnp.stack([chip, lax.axis_index("c")]).astype(jnp.int32)

    waiting = {}

    def joined(r):
        key, _ = waiting.pop("join")
        reduced[key] = r.reshape(2 * r.shape[1], r.shape[2])[:, :weights[key[0]].shape[2]]

    def backward_pair(key, act, dout, dw_name, dx_name, dout_chips=False):
        col = key[0] in by_col
        jobs = [("join", waiting["join"][1])] if "join" in waiting else []
        g = _matmul(act, dout, ta=True, b_chips=dout_chips, out_chips=col, out_dtype=BF16, carry=jobs, name=dw_name)
        if jobs:
            g, r = g
            joined(r)
        if key[0] == "ssm_w_in":
            g = jnp.stack([jnp.pad(g[:, jj * cq:(jj + 1) * cq], ((0, 0), (0, cqp - cq))) for jj in range(N_CHIPS)])
        elif not col:
            g = g.reshape(N_CHIPS, g.shape[0] // N_CHIPS, g.shape[1])
        g = g.reshape(N_CHIPS, 2, g.shape[1] // 2, g.shape[2])
        (got,) = _swap_halves([g], name="grads_swap")
        pair = _add_pair(sel_c, g, got, name="grads_add2")
        dact, got3 = _matmul(dout, full[key], tb=True, a_chips=dout_chips, b_chips=col, carry=[("scatter", pair)],
                             name=dx_name)
        waiting["join"] = (key, _add_four(sel_j, pair, got3, name="grads_add4"))
        return dact

    for i in reversed(range(depth)):
        j = i // 2
        s = saved[i]
        dhm = backward_pair(("ffn_w_down", i), s["hm"], dxb, "ffn_down_dw", "ffn_down_dx")
        du2b, dw_dw, db_dw = _ffn_mid_bwd(s["u2"], dhm, full["ffn_w_dw"][i], row(ffn_b_dw[i]), name="ffn_mid_bwd")
        gr["ffn_w_dw"][i], gr["ffn_b_dw"][i] = dw_dw, db_dw[0]
        dh2 = backward_pair(("ffn_w_up", i), s["h2"], du2b, "ffn_up_dw", "ffn_up_dx", dout_chips=True)
        dx, dxb, colsum, dg = _rms_bwd(s["x_mid"], row(norm_ffn_g[i]), dh2, dx, name="rms_ffn_bwd")
        gr["norm_ffn_g"][i] = dg[0]
        if i % 2 == 0:
            gr["cv_b_out"][j] = colsum[0]
            dv4 = backward_pair(("cv_w_out", j), s["v4"], dxb, "cv_out_dw", "cv_out_dx")
            dv2, dlg, dlb = _ln_silu_bwd(s["v2"], row(cv_ln_g[j]), row(cv_ln_b[j]), dv4, name="cv_ln_bwd")
            gr["cv_ln_g"][j], gr["cv_ln_b"][j] = dlg[0], dlb[0]
            dv1, dw_dw, db_dw = _dwconv_bwd(s["v1"], dv2, full["cv_w_dw"][j], name="cv_dw_bwd")
            gr["cv_w_dw"][j], gr["cv_b_dw"][j] = dw_dw, db_dw[0]
            du, db_in = _glu_bwd(s["u"], row(cv_b_in[j]), dv1, name="cv_glu_bwd")
            gr["cv_b_in"][j] = db_in[0]
            dh = backward_pair(("cv_w_in", j), s["h"], du, "cv_in_dw", "cv_in_dx")
        else:
            dgn = backward_pair(("ssm_w_out", j), s["gnrm"], dxb, "ssm_out_dw", "ssm_out_dx")
            dy, dz, dng, ddl = _gated_norm_bwd(s["y_ssd"], s["xs"], s["zx"], s["d_exp"], row(full["ssm_norm_g"][j]),
                                               dgn, name="ssm_norm_bwd")
            gr["ssm_norm_g"][j] = dng[0]
            gr["ssm_d"][j] = ddl.reshape(n_heads, HEAD_DIM).sum(axis=1)
            dxp, dbm, dcm, ddt_exp, dan = _ssd_bwd(s["xs"], s["dt_exp"], s["cs_exp"], s["cs_rows"], s["bm"], s["cm"],
                                                   dy, s["states"], s["aneg_exp"], name="ssd_bwd")
            gr["ssm_a_log"][j] = dan[0, ::HEAD_DIM] * s["aneg_exp"][0, ::HEAD_DIM]
            dconv, ddtp, dbias = _ssm_act_bwd(s["conv"], s["dtp"], dxp, dy, dbm, dcm, ddt_exp, s["dt_exp"],
                                              s["bias_exp"], s["d_exp"], expand, di=di, name="ssm_act_bwd")
            gr["ssm_dt_bias"][j] = dbias[0, ::HEAD_DIM]
            dxbc, dw_c, db_c = _dwconv_bwd(s["zx"], dconv, full["ssm_w_conv"][j], dx_dtype=BF16, cols=s["xbc_cols"],
                                           name="ssm_dw_bwd")
            gr["ssm_w_conv"][j], gr["ssm_b_conv"][j] = dw_c, db_c[0]
            dtail = jnp.pad(ddtp[:, ::HEAD_DIM].astype(BF16), ((0, 0), (0, ssm_inp - ssm_in)))
            dzx = jnp.concatenate([dz, dxbc, dtail], axis=1)
            dh = backward_pair(("ssm_w_in", j), s["h"], dzx, "ssm_in_dw", "ssm_in_dx")
        dx, dxb, _, dg = _rms_bwd(s["x_in"], row(norm_mix_g[i]), dh, dx, name="rms_mix_bwd")
        gr["norm_mix_g"][i] = dg[0]

    joined(_join_halves([waiting["join"][1]], name="grads_join")[0])
    grads = {nm: jnp.stack([reduced[nm, l] for l in range(weights[nm].shape[0])]) for nm in big_names}

    small = [nm for nm in names if nm not in grads]
    small_parts = []
    for nm in small:
        small_parts.append(dg_final[0] if nm == "norm_final_g" else jnp.stack(gr[nm]))
    gs_pack = _flat_rows(small_parts + [loss_part.reshape(1)], LANES)
    gs = _sum8(_gather_devices(gs_pack, name="gather_small_grads"), name="sum_small_grads").reshape(-1)
    o = 0
    for nm, p in zip(small, small_parts):
        gfull = gs[o:o + p.size].reshape(p.shape)
        o += p.size
        if nm in small_sharded:
            width = weights[nm].shape[-1]
            gfull = lax.dynamic_slice_in_dim(gfull, chip * width, width, axis=gfull.ndim - 1)
        grads[nm] = gfull
    loss = gs[o]

    delta, new_m, new_v = {}, {}, {}
    for nm in big_names:
        shp = weights[nm].shape
        as2d = lambda a: a.reshape(-1, shp[-1])
        dl, mn, vn = _adamw(as2d(weights[nm]), as2d(grads[nm]), as2d(mom_m[nm]), as2d(mom_v[nm]), name="adamw_" + nm)
        delta[nm], new_m[nm], new_v[nm] = dl.reshape(shp), mn.reshape(shp), vn.reshape(shp)
    pk = lambda dct: _flat_rows([dct[nm] for nm in small], LANES)
    dl, mn, vn = _adamw(pk(weights), pk(grads), pk(mom_m), pk(mom_v), name="adamw_small")
    dl, mn, vn = dl.reshape(-1), mn.reshape(-1), vn.reshape(-1)
    o = 0
    for nm in small:
        shp, n = weights[nm].shape, weights[nm].size
        delta[nm], new_m[nm], new_v[nm] = (a[o:o + n].reshape(shp) for a in (dl, mn, vn))
        o += n

    return (loss, dx[None], *[grads[nm] for nm in names], *[delta[nm] for nm in names],
            *[new_m[nm] for nm in names], *[new_v[nm] for nm in names])
```

```python
import math

import jax
import jax.numpy as jnp
from jax import lax
from jax.experimental import pallas as pl
from jax.experimental.pallas import tpu as pltpu

F32, BF16 = jnp.float32, jnp.bfloat16
MESH = pl.DeviceIdType.MESH
ANY = pl.BlockSpec(memory_space=pl.ANY)

RMS_EPS = 1e-6
LN_EPS = 1e-5
HEAD_DIM = 64
N_GROUPS = 8
D_STATE = 128
ADAM_LR, ADAM_B1, ADAM_B2, ADAM_EPS, ADAM_WD, ADAM_STEP = 0.001, 0.9, 0.999, 1e-08, 0.01, 10

VMEM_LIMIT_BYTES = 56 * 1024 * 1024
LANES = 128
SSD_CHUNK = 128
ADAMW_BLOCK_ELEMS = 512 * 1024
N_CHIPS = 4


def _cparams(**kw):
    return pltpu.CompilerParams(vmem_limit_bytes=VMEM_LIMIT_BYTES, **kw)


def _pick(n, prefs):
    for p in prefs:
        if n % p == 0:
            return p
    return n


def _sigmoid(x):
    return 1.0 / (1.0 + jnp.exp(-x))


def _silu(x):
    return x * _sigmoid(x)


def _dsilu(x):
    s = _sigmoid(x)
    return s * (1.0 + x * (1.0 - s))


def _rowsum(x):
    return jnp.sum(x, axis=0, keepdims=True)


MM_TILES = (2816, 2688, 2048, 1408, 1024, 896, 512, 384, 256, 128)
MM_VMEM_BUDGET = 40 * 1024 * 1024


def _mm_tiles(m, n, k, n_unit, k_unit, out_bytes, has_res):
    best = None
    for tk in [t for t in MM_TILES if k_unit % t == 0]:
        for tm in [t for t in (1024, 512, 256, 128) if m % t == 0] or [m]:
            for tn in [t for t in MM_TILES if n_unit % t == 0]:
                vmem = 2 * 2 * (tm * tk + tk * tn) + 2 * tm * tn * out_bytes
                vmem += tm * tn * 4 if k // tk > 1 else 0
                vmem += 2 * tm * tn * 4 if has_res else 0
                if vmem > MM_VMEM_BUDGET:
                    continue
                traffic = m * k * (n // tn) + k * n * (m // tm)
                if best is None or traffic < best[0]:
                    best = (traffic, tm, tn, tk)
        if best is not None:
            return best[1:]
    raise ValueError((m, n, k))


def _matmul(a, b, *, name, ta=False, tb=False, a_chips=False, b_chips=False, out_chips=False, out_dtype=F32, bias=None,
            res=None, carry=None):
    if a_chips:
        assert tb and b_chips and not ta
        m, k = a.shape[1], N_CHIPS * a.shape[2]
    else:
        m, k = (a.shape[1], a.shape[0]) if ta else a.shape
    if b_chips:
        nq = b.shape[2]
        n = b.shape[1] if tb else N_CHIPS * nq
        assert k == (N_CHIPS * nq if tb else b.shape[1])
    else:
        n = b.shape[0] if tb else b.shape[1]
        assert k == (b.shape[1] if tb else b.shape[0])
        nq = n // N_CHIPS
    has_bias, has_res = bias is not None, res is not None
    tm, tn, tk = _mm_tiles(m, n, k, nq if ((b_chips and not tb) or out_chips) else n, nq if (b_chips and tb) else k,
                           jnp.dtype(out_dtype).itemsize, has_res)
    gm, gn, nk = m // tm, n // tn, k // tk
    nbq = nq // (tk if tb else tn) if (b_chips or out_chips) else 1
    dn = (((0 if ta else 1,), (1 if tb else 0,)), ((), ()))
    carry = carry or []
    nj = len(carry)

    def body(*refs):
        a_ref, b_ref = refs[0], refs[1]
        rest = list(refs[2:])
        bias_ref = rest.pop(0) if has_bias else None
        res_ref = rest.pop(0) if has_res else None
        srcs = [rest.pop(0) for _ in range(nj)]
        o_ref = rest.pop(0)
        dsts = [rest.pop(0) for _ in range(nj)]
        acc_ref = rest.pop(0) if nk > 1 else None
        i, j, kk = pl.program_id(0), pl.program_id(1), pl.program_id(2)

        if carry:
            send_sems, recv_sems = rest
            x, y, c, chips = _place()
            sibling = (x, y, 1 - c)
            cps, landing = [], []
            sem = 0
            for (kind, _), src, dst in zip(carry, srcs, dsts):
                if kind == "join":
                    cps.append(_rcopy(src.at[c], dst.at[c], send_sems, recv_sems, sem, sibling))
                    landing.append((dst.at[1 - c], sem, sibling))
                    sem += 1
                    continue
                for q, (cx, cy) in enumerate(chips):
                    if kind == "gather":
                        cps.append(_rcopy(src.at[c], dst.at[2 * x + y, c], send_sems, recv_sems, sem, (cx, cy, c)))
                        landing.append((dst.at[2 * cx + cy, c], sem, (cx, cy, c)))
                    elif kind == "pass":
                        cps.append(_rcopy(src.at[2 * cx + cy, c], dst.at[2 * cx + cy, c], send_sems, recv_sems, sem, sibling))
                        landing.append((dst.at[2 * cx + cy, 1 - c], sem, sibling))
                    else:
                        cps.append(_rcopy(src.at[2 * cx + cy], dst.at[q], send_sems, recv_sems, sem, (cx, cy, c)))
                        landing.append((dst.at[q], sem, (cx, cy, c)))
                    sem += 1

            @pl.when((i == 0) & (j == 0) & (kk == 0))
            def _():
                for cp in cps:
                    cp.start()

        def finish(r):
            if has_bias:
                r = r + bias_ref[...]
            if has_res:
                r = r + res_ref[...]
            o_ref[...] = r.astype(o_ref.dtype)

        part = lax.dot_general(a_ref[...].astype(BF16), b_ref[...].astype(BF16), dn, preferred_element_type=F32)
        if nk == 1:
            finish(part)
        else:
            @pl.when(kk == 0)
            def _():
                acc_ref[...] = part

            @pl.when(kk > 0)
            def _():
                acc_ref[...] += part

            @pl.when(kk == nk - 1)
            def _():
                finish(acc_ref[...])

        if carry:
            @pl.when((i == gm - 1) & (j == gn - 1) & (kk == nk - 1))
            def _():
                for blk, sem, frm in landing:
                    _rcopy(blk, blk, send_sems, recv_sems, sem, frm).wait_recv()
                for cp in cps:
                    cp.wait_send()

    if a_chips:
        a_spec = pl.BlockSpec((None, tm, tk), lambda i, j, kk: (kk // nbq, i, kk % nbq))
    elif ta:
        a_spec = pl.BlockSpec((tk, tm), lambda i, j, kk: (kk, i))
    else:
        a_spec = pl.BlockSpec((tm, tk), lambda i, j, kk: (i, kk))
    if b_chips and tb:
        b_spec = pl.BlockSpec((None, tn, tk), lambda i, j, kk: (kk // nbq, j, kk % nbq))
    elif b_chips:
        b_spec = pl.BlockSpec((None, tk, tn), lambda i, j, kk: (j // nbq, kk, j % nbq))
    elif tb:
        b_spec = pl.BlockSpec((tn, tk), lambda i, j, kk: (j, kk))
    else:
        b_spec = pl.BlockSpec((tk, tn), lambda i, j, kk: (kk, j))
    if out_chips:
        out_spec = pl.BlockSpec((None, tm, tn), lambda i, j, kk: (j // nbq, i, j % nbq))
        out_shape = jax.ShapeDtypeStruct((N_CHIPS, m, nq), out_dtype)
    else:
        out_spec = pl.BlockSpec((tm, tn), lambda i, j, kk: (i, j))
        out_shape = jax.ShapeDtypeStruct((m, n), out_dtype)
    in_specs, args = [a_spec, b_spec], [a, b]
    if has_bias:
        in_specs.append(pl.BlockSpec((1, tn), lambda i, j, kk: (0, j)))
        args.append(bias)
    if has_res:
        in_specs.append(pl.BlockSpec((tm, tn), lambda i, j, kk: (i, j)))
        args.append(res)
    scratch = [pltpu.VMEM((tm, tn), F32)] if nk > 1 else []
    if not carry:
        return pl.pallas_call(
            body, name=name, grid=(gm, gn, nk), in_specs=in_specs, out_specs=out_spec, out_shape=out_shape,
            scratch_shapes=scratch,
            compiler_params=_cparams(dimension_semantics=("parallel", "parallel", "arbitrary")),
        )(*args)
    lands, aliases, n_sems = [], {}, 0
    for jb, (kind, moved) in enumerate(carry):
        shape = {"gather": (N_CHIPS,) + moved.shape, "pass": moved.shape, "scatter": (3,) + moved.shape[1:],
                 "join": moved.shape}[kind]
        lands.append(jax.ShapeDtypeStruct(shape, moved.dtype))
        n_sems += 1 if kind == "join" else 3
        if kind in ("pass", "join"):
            aliases[len(args) + jb] = 1 + jb
    return pl.pallas_call(
        body, name=name, grid=(gm, gn, nk), in_specs=in_specs + [ANY] * nj, out_specs=[out_spec] + [ANY] * nj,
        out_shape=[out_shape] + lands, input_output_aliases=aliases,
        scratch_shapes=scratch + [pltpu.SemaphoreType.DMA((n_sems,)), pltpu.SemaphoreType.DMA((n_sems,))],
        compiler_params=_cparams(dimension_semantics=("arbitrary", "arbitrary", "arbitrary")),
    )(*args, *[moved for _, moved in carry])


def _rowwise(fn, rows, pars, outs, reds, *, tm, name):
    rows = [r if isinstance(r, tuple) else (r, r.shape[1], 0) for r in rows]
    t = rows[0][0].shape[0]
    assert t % tm == 0
    n_in, n_o = len(rows) + len(pars), len(outs)

    def body(*refs):
        i = pl.program_id(0)
        o, d = fn(*[r[...] for r in refs[:n_in]])
        for ref, val in zip(refs[n_in:n_in + n_o], o):
            ref[...] = val.astype(ref.dtype)
        d_refs = refs[n_in + n_o:]

        @pl.when(i == 0)
        def _():
            for ref in d_refs:
                ref[...] = jnp.zeros_like(ref)

        for ref, val in zip(d_refs, d):
            ref[...] += val

    in_specs = [pl.BlockSpec((tm, w), lambda i, b=blk: (i, b)) for _, w, blk in rows]
    in_specs += [pl.BlockSpec(p.shape, lambda i: (0, 0)) for p in pars]
    out_specs = [pl.BlockSpec((tm, c), lambda i: (i, 0)) for c, _ in outs]
    out_specs += [pl.BlockSpec((1, c), lambda i: (0, 0)) for c in reds]
    out_shape = [jax.ShapeDtypeStruct((t, c), dt) for c, dt in outs] + [jax.ShapeDtypeStruct((1, c), F32) for c in reds]
    res = pl.pallas_call(
        body, name=name, grid=(t // tm,), in_specs=in_specs, out_specs=out_specs, out_shape=out_shape,
        compiler_params=_cparams(dimension_semantics=("arbitrary",)),
    )(*[r[0] for r in rows], *pars)
    return res[:n_o], res[n_o:]


def _rms_fwd(x, g, *, name):
    def fn(xb, gb):
        r = lax.rsqrt(jnp.mean(xb * xb, axis=-1, keepdims=True) + RMS_EPS)
        return [xb * r * gb], []
    (h,), _ = _rowwise(fn, [x], [g], [(x.shape[1], BF16)], [], tm=256, name=name)
    return h


def _rms_bwd(x, g, dh, dres, *, name):
    def fn(xb, dhb, drb, gb):
        r = lax.rsqrt(jnp.mean(xb * xb, axis=-1, keepdims=True) + RMS_EPS)
        xh = xb * r
        dxh = dhb * gb
        dx = r * (dxh - xh * jnp.mean(dxh * xh, axis=-1, keepdims=True))
        out = drb + dx
        return [out, out], [_rowsum(out), _rowsum(dhb * xh)]
    c = x.shape[1]
    (dx, dxb), (colsum, dg) = _rowwise(fn, [x, dh, dres], [g], [(c, F32), (c, BF16)], [c, c], tm=256, name=name)
    return dx, dxb, colsum, dg


def _loss_head(x, g, tgt, *, name):
    d_model = x.shape[1]

    def fn(xb, tb, gb):
        r = lax.rsqrt(jnp.mean(xb * xb, axis=-1, keepdims=True) + RMS_EPS)
        xh = xb * r
        e = xh * gb - tb
        dy = e * (1.0 / d_model)
        dxh = dy * gb
        dx = r * (dxh - xh * jnp.mean(dxh * xh, axis=-1, keepdims=True))
        return [dx, dx], [_rowsum(e * e), _rowsum(dy * xh)]
    (dx, dxb), (sq, dg) = _rowwise(fn, [x, tgt], [g], [(d_model, F32), (d_model, BF16)], [d_model, d_model], tm=256,
                                   name=name)
    return dx, dxb, sq, dg


def _halo_rows(k):
    return 8 * ((k - 1 + 7) // 8) if k > 1 else 8


def _pad_taps(w):
    k = w.shape[0]
    kp = 8 * ((k + 7) // 8)
    return jnp.pad(w, ((0, kp - k), (0, 0)))


def _conv_chunks(k, tc):
    rc, lw = (32, 256) if k > 9 else (16, 512)
    lanes, l0 = [], 0
    while l0 < tc:
        lanes.append((l0, min(lw, tc - l0)))
        l0 += lw
    return rc, lanes


def _fold8(v):
    acc = v[0:8]
    for q in range(1, v.shape[0] // 8):
        acc = acc + v[8 * q:8 * q + 8]
    return acc


def _taps(src_ref, lead, base, rc, lanes, w_ref, k, sign, acc):
    for s in range(k):
        rows = pl.ds(base + sign * s, rc)
        acc = acc + w_ref[k - 1 - s:k - s, lanes] * src_ref[lead + (rows, lanes)]
    return acc


def _conv_cols(x, k, cols):
    c0, c = cols if cols else (0, x.shape[1])
    tc = _pick(math.gcd(c0, c), (512, 256, 128) if k > 9 else (2048, 1536, 1024, 512, 256, 128))
    return c0, c, tc


def _dwconv_fwd(x, w, b, *, name, cols=None):
    t = x.shape[0]
    k = w.shape[0]
    h = _halo_rows(k)
    tm = _pick(t, (256, 128))
    c0, c, tc = _conv_cols(x, k, cols)
    cb0 = c0 // tc
    wp = _pad_taps(w)
    kp = wp.shape[0]
    rb = tm // h
    rc, lane_chunks = _conv_chunks(k, tc)

    def body(x_ref, p_ref, w_ref, b_ref, o_ref, ext):
        i = pl.program_id(0)
        ext[pl.ds(h, tm), :] = x_ref[...]
        ext[pl.ds(0, h), :] = jnp.where(i > 0, p_ref[...], 0.0)
        for l0, lw in lane_chunks:
            lanes = pl.ds(l0, lw)
            for r0 in range(0, tm, rc):
                acc = jnp.broadcast_to(b_ref[:, lanes], (rc, lw))
                o_ref[pl.ds(r0, rc), lanes] = _taps(ext, (), h + r0, rc, lanes, w_ref, k, -1, acc)

    return pl.pallas_call(
        body, name=name, grid=(t // tm, c // tc),
        in_specs=[pl.BlockSpec((tm, tc), lambda i, j: (i, cb0 + j)),
                  pl.BlockSpec((h, tc), lambda i, j: (jnp.maximum(i * rb - 1, 0), cb0 + j)),
                  pl.BlockSpec((kp, tc), lambda i, j: (0, j)),
                  pl.BlockSpec((1, tc), lambda i, j: (0, j))],
        out_specs=pl.BlockSpec((tm, tc), lambda i, j: (i, j)),
        out_shape=jax.ShapeDtypeStruct((t, c), F32),
        scratch_shapes=[pltpu.VMEM((h + tm, tc), F32)],
        compiler_params=_cparams(dimension_semantics=("parallel", "parallel")),
    )(x, x, wp, b)


def _dwconv_bwd(x, dy, w, *, name, dx_dtype=F32, cols=None):
    t = x.shape[0]
    k = w.shape[0]
    h = _halo_rows(k)
    tm = _pick(t, (256, 128))
    c0, c, tc = _conv_cols(x, k, cols)
    cb0 = c0 // tc
    wp = _pad_taps(w)
    kp = wp.shape[0]
    rb = tm // h
    nt = t // tm
    rc, lane_chunks = _conv_chunks(k, tc)

    def body(x_ref, p_ref, dy_ref, n_ref, w_ref, dx_ref, dw_ref, db_ref, xext, dext):
        i = pl.program_id(1)

        @pl.when(i == 0)
        def _():
            dw_ref[...] = jnp.zeros_like(dw_ref)
            db_ref[...] = jnp.zeros_like(db_ref)

        xext[pl.ds(h, tm), :] = x_ref[...]
        xext[pl.ds(0, h), :] = jnp.where(i > 0, p_ref[...], 0.0)
        dext[pl.ds(0, tm), :] = dy_ref[...]
        dext[pl.ds(tm, h), :] = jnp.where(i < nt - 1, n_ref[...], 0.0)
        for l0, lw in lane_chunks:
            lanes = pl.ds(l0, lw)
            for r0 in range(0, tm, rc):
                acc = _taps(dext, (), r0, rc, lanes, w_ref, k, +1, jnp.zeros((rc, lw), F32))
                dx_ref[pl.ds(r0, rc), lanes] = acc.astype(dx_ref.dtype)
            for s in range(k):
                a8 = jnp.zeros((8, lw), F32)
                for r0 in range(0, tm, rc):
                    a8 = a8 + _fold8(xext[pl.ds(h + r0 - s, rc), lanes] * dext[pl.ds(r0, rc), lanes])
                dw_ref[k - 1 - s:k - s, lanes] += _rowsum(a8)
            b8 = jnp.zeros((8, lw), F32)
            for r0 in range(0, tm, rc):
                b8 = b8 + _fold8(dext[pl.ds(r0, rc), lanes])
            db_ref[:, lanes] += _rowsum(b8)

    dx, dw, db = pl.pallas_call(
        body, name=name, grid=(c // tc, nt),
        in_specs=[pl.BlockSpec((tm, tc), lambda j, i: (i, cb0 + j)),
                  pl.BlockSpec((h, tc), lambda j, i: (jnp.maximum(i * rb - 1, 0), cb0 + j)),
                  pl.BlockSpec((tm, tc), lambda j, i: (i, j)),
                  pl.BlockSpec((h, tc), lambda j, i: (jnp.minimum((i + 1) * rb, nt * rb - 1), j)),
                  pl.BlockSpec((kp, tc), lambda j, i: (0, j))],
        out_specs=[pl.BlockSpec((tm, tc), lambda j, i: (i, j)),
                   pl.BlockSpec((kp, tc), lambda j, i: (0, j)),
                   pl.BlockSpec((1, tc), lambda j, i: (0, j))],
        out_shape=[jax.ShapeDtypeStruct((t, c), dx_dtype), jax.ShapeDtypeStruct((kp, c), F32),
                   jax.ShapeDtypeStruct((1, c), F32)],
        scratch_shapes=[pltpu.VMEM((h + tm, tc), F32), pltpu.VMEM((tm + h, tc), F32)],
        compiler_params=_cparams(dimension_semantics=("parallel", "arbitrary")),
    )(x, x, dy, dy, wp)
    return dx, dw[:k], db


def _glu_fwd(u, b_in, *, name):
    d = u.shape[1] // 2

    def fn(ua, ug, ba, bg):
        return [(ua + ba) * _sigmoid(ug + bg)], []
    (v,), _ = _rowwise(fn, [(u, d, 0), (u, d, 1)], [b_in[:, :d], b_in[:, d:]], [(d, F32)], [], tm=256, name=name)
    return v


def _glu_bwd(u, b_in, dv, *, name):
    d = u.shape[1] // 2

    def fn(ua, ug, dvb, ba, bg):
        a = ua + ba
        s = _sigmoid(ug + bg)
        du = jnp.concatenate([dvb * s, dvb * a * s * (1.0 - s)], axis=1)
        return [du], [_rowsum(du)]
    (du,), (db,) = _rowwise(fn, [(u, d, 0), (u, d, 1), dv], [b_in[:, :d], b_in[:, d:]], [(2 * d, BF16)], [2 * d],
                            tm=256, name=name)
    return du, db


def _ln_silu_fwd(v, g, b, *, name):
    def fn(vb, gb, bb):
        mu = jnp.mean(vb, axis=-1, keepdims=True)
        xc = vb - mu
        rstd = lax.rsqrt(jnp.mean(xc * xc, axis=-1, keepdims=True) + LN_EPS)
        return [_silu(xc * rstd * gb + bb)], []
    (o,), _ = _rowwise(fn, [v], [g, b], [(v.shape[1], BF16)], [], tm=256, name=name)
    return o


def _ln_silu_bwd(v, g, b, do, *, name):
    def fn(vb, dob, gb, bb):
        mu = jnp.mean(vb, axis=-1, keepdims=True)
        xc = vb - mu
        rstd = lax.rsqrt(jnp.mean(xc * xc, axis=-1, keepdims=True) + LN_EPS)
        xh = xc * rstd
        dy = dob * _dsilu(xh * gb + bb)
        dxh = dy * gb
        dv = rstd * (dxh - jnp.mean(dxh, axis=-1, keepdims=True) - xh * jnp.mean(dxh * xh, axis=-1, keepdims=True))
        return [dv], [_rowsum(dy * xh), _rowsum(dy)]
    c = v.shape[1]
    (dv,), (dg, db) = _rowwise(fn, [v, do], [g, b], [(c, F32)], [c, c], tm=256, name=name)
    return dv, dg, db


def _ffn_mid_setup(u2c, w, b):
    _, t, nq = u2c.shape
    f = 2 * nq
    k = w.shape[0]
    h = _halo_rows(k)
    tm = _pick(t, (256, 128))
    tc = _pick(nq, (1408, 1024, 512, 256, 128))
    rc, lane_chunks = _conv_chunks(k, tc)
    return dict(t=t, nq=nq, f=f, k=k, h=h, tm=tm, tc=tc, npq=nq // tc, rb=tm // h, nt=t // tm, rc=rc,
                lane_chunks=lane_chunks, u4=u2c.reshape(2, 2, t, nq), wg=_pad_taps(w[:, :f]), wv=_pad_taps(w[:, f:]),
                bg=b[:, :f], bv=b[:, f:])


def _ffn_mid_fwd(u2c, w, b, *, name):
    p = _ffn_mid_setup(u2c, w, b)
    t, f, k, h, tm, tc, npq, rb, rc = (p[n] for n in ("t", "f", "k", "h", "tm", "tc", "npq", "rb", "rc"))
    kp = p["wg"].shape[0]

    def body(u_ref, p_ref, wg_ref, wv_ref, bg_ref, bv_ref, o_ref, ext):
        i = pl.program_id(2)
        for kind in range(2):
            ext[kind, pl.ds(h, tm), :] = u_ref[kind]
            ext[kind, pl.ds(0, h), :] = jnp.where(i > 0, p_ref[kind], 0.0)
        for l0, lw in p["lane_chunks"]:
            lanes = pl.ds(l0, lw)
            for r0 in range(0, tm, rc):
                g = _taps(ext, (0,), h + r0, rc, lanes, wg_ref, k, -1, jnp.broadcast_to(bg_ref[:, lanes], (rc, lw)))
                v = _taps(ext, (1,), h + r0, rc, lanes, wv_ref, k, -1, jnp.broadcast_to(bv_ref[:, lanes], (rc, lw)))
                o_ref[pl.ds(r0, rc), lanes] = (_silu(g) * v).astype(BF16)

    col = lambda q, jj, i: (0, q * npq + jj)
    return pl.pallas_call(
        body, name=name, grid=(2, npq, t // tm),
        in_specs=[pl.BlockSpec((2, None, tm, tc), lambda q, jj, i: (0, q, i, jj)),
                  pl.BlockSpec((2, None, h, tc), lambda q, jj, i: (0, q, jnp.maximum(i * rb - 1, 0), jj)),
                  pl.BlockSpec((kp, tc), col), pl.BlockSpec((kp, tc), col),
                  pl.BlockSpec((1, tc), col), pl.BlockSpec((1, tc), col)],
        out_specs=pl.BlockSpec((tm, tc), lambda q, jj, i: (i, q * npq + jj)),
        out_shape=jax.ShapeDtypeStruct((t, f), BF16),
        scratch_shapes=[pltpu.VMEM((2, h + tm, tc), F32)],
        compiler_params=_cparams(dimension_semantics=("parallel", "parallel", "parallel")),
    )(p["u4"], p["u4"], p["wg"], p["wv"], p["bg"], p["bv"])


def _ffn_mid_bwd(u2c, dhm, w, b, *, name):
    p = _ffn_mid_setup(u2c, w, b)
    t, nq, f, k, h, tm, tc, npq, rb, nt, rc = (p[n] for n in ("t", "nq", "f", "k", "h", "tm", "tc", "npq", "rb", "nt", "rc"))
    kp = p["wg"].shape[0]
    chunks1 = [(r0, rc) for r0 in range(0, tm, rc)] + [(tm, h)]

    def body(u_ref, p_ref, n_ref, dh_ref, nd_ref, wg_ref, wv_ref, bg_ref, bv_ref,
             du_ref, dwg_ref, dwv_ref, dbg_ref, dbv_ref, uext, dsc):
        i = pl.program_id(2)
        w_refs, dw_refs, db_refs = (wg_ref, wv_ref), (dwg_ref, dwv_ref), (dbg_ref, dbv_ref)

        @pl.when(i == 0)
        def _():
            for ref in dw_refs + db_refs:
                ref[...] = jnp.zeros_like(ref)

        for kind in range(2):
            uext[kind, pl.ds(0, h), :] = jnp.where(i > 0, p_ref[kind], 0.0)
            uext[kind, pl.ds(h, tm), :] = u_ref[kind]
            uext[kind, pl.ds(h + tm, h), :] = jnp.where(i < nt - 1, n_ref[kind], 0.0)
        for l0, lw in p["lane_chunks"]:
            lanes = pl.ds(l0, lw)
            for r0, rr in chunks1:
                g = _taps(uext, (0,), h + r0, rr, lanes, wg_ref, k, -1, jnp.broadcast_to(bg_ref[:, lanes], (rr, lw)))
                v = _taps(uext, (1,), h + r0, rr, lanes, wv_ref, k, -1, jnp.broadcast_to(bv_ref[:, lanes], (rr, lw)))
                dh = dh_ref[pl.ds(r0, rr), lanes] if r0 < tm else jnp.where(i < nt - 1, nd_ref[:, lanes], 0.0)
                sg = _sigmoid(g)
                dsc[0, pl.ds(r0, rr), lanes] = dh * v * (sg * (1.0 + g * (1.0 - sg)))
                dsc[1, pl.ds(r0, rr), lanes] = dh * (g * sg)
            for kind in range(2):
                for r0 in range(0, tm, rc):
                    acc = _taps(dsc, (kind,), r0, rc, lanes, w_refs[kind], k, +1, jnp.zeros((rc, lw), F32))
                    du_ref[kind, pl.ds(r0, rc), lanes] = acc.astype(BF16)
                for s in range(k):
                    a8 = jnp.zeros((8, lw), F32)
                    for r0 in range(0, tm, rc):
                        a8 = a8 + _fold8(uext[kind, pl.ds(h + r0 - s, rc), lanes] * dsc[kind, pl.ds(r0, rc), lanes])
                    dw_refs[kind][k - 1 - s:k - s, lanes] += _rowsum(a8)
                b8 = jnp.zeros((8, lw), F32)
                for r0 in range(0, tm, rc):
                    b8 = b8 + _fold8(dsc[kind, pl.ds(r0, rc), lanes])
                db_refs[kind][:, lanes] += _rowsum(b8)

    col = lambda q, jj, i: (0, q * npq + jj)
    nxt = lambda i: jnp.minimum((i + 1) * rb, nt * rb - 1)
    du, dwg, dwv, dbg, dbv = pl.pallas_call(
        body, name=name, grid=(2, npq, nt),
        in_specs=[pl.BlockSpec((2, None, tm, tc), lambda q, jj, i: (0, q, i, jj)),
                  pl.BlockSpec((2, None, h, tc), lambda q, jj, i: (0, q, jnp.maximum(i * rb - 1, 0), jj)),
                  pl.BlockSpec((2, None, h, tc), lambda q, jj, i: (0, q, nxt(i), jj)),
                  pl.BlockSpec((tm, tc), lambda q, jj, i: (i, q * npq + jj)),
                  pl.BlockSpec((h, tc), lambda q, jj, i: (nxt(i), q * npq + jj)),
                  pl.BlockSpec((kp, tc), col), pl.BlockSpec((kp, tc), col),
                  pl.BlockSpec((1, tc), col), pl.BlockSpec((1, tc), col)],
        out_specs=[pl.BlockSpec((2, None, tm, tc), lambda q, jj, i: (0, q, i, jj)),
                   pl.BlockSpec((kp, tc), col), pl.BlockSpec((kp, tc), col),
                   pl.BlockSpec((1, tc), col), pl.BlockSpec((1, tc), col)],
        out_shape=[jax.ShapeDtypeStruct((2, 2, t, nq), BF16), jax.ShapeDtypeStruct((kp, f), F32),
                   jax.ShapeDtypeStruct((kp, f), F32), jax.ShapeDtypeStruct((1, f), F32), jax.ShapeDtypeStruct((1, f), F32)],
        scratch_shapes=[pltpu.VMEM((2, h + tm + h, tc), F32), pltpu.VMEM((2, tm + h, tc), F32)],
        compiler_params=_cparams(dimension_semantics=("parallel", "parallel", "arbitrary")),
    )(p["u4"], p["u4"], p["u4"], dhm, dhm, p["wg"], p["wv"], p["bg"], p["bv"])
    return (du.reshape(N_CHIPS, t, nq), jnp.concatenate([dwg[:k], dwv[:k]], axis=1), jnp.concatenate([dbg, dbv], axis=1))


def _head_expander(n_heads):
    return (jnp.arange(n_heads * HEAD_DIM)[None, :] // HEAD_DIM == jnp.arange(n_heads)[:, None]).astype(BF16)


def _ssm_act(conv, dtp, bias_exp, aneg_exp, expand, *, di, name):
    q = SSD_CHUNK
    gn = (conv.shape[1] - di) // 2
    mm_dims = (((1,), (0,)), ((), ()))

    def fn(cb, dtb, bb, ab, eb):
        act = _silu(cb)
        dt = _dot3(dtb, eb, mm_dims, 0) + bb
        dt = jnp.maximum(dt, 0.0) + jnp.log(1.0 + jnp.exp(-jnp.abs(dt)))
        a = dt * ab
        tri = (lax.broadcasted_iota(jnp.int32, (q, q), 0) >= lax.broadcasted_iota(jnp.int32, (q, q), 1)).astype(F32)
        cs = _dot3(tri, a, mm_dims, 1)
        return [act[:, :di], act[:, di:di + gn], act[:, di + gn:], dt, cs], []
    outs, _ = _rowwise(fn, [conv, dtp], [bias_exp, aneg_exp, expand],
                       [(di, F32), (gn, F32), (gn, F32), (di, F32), (di, F32)], [], tm=q, name=name)
    return outs


def _head_masks(q):
    lane = lax.broadcasted_iota(jnp.int32, (q, LANES), 1)
    return lane < HEAD_DIM


def _pair_cols(cs, lo):
    sw = pltpu.roll(cs, HEAD_DIM, 1)
    return jnp.where(lo, cs, sw), jnp.where(lo, sw, cs)


def _ssd_fwd(xs, dt_exp, cs_exp, cs_rows, bm, cm, *, name):
    t, di = xs.shape
    q = SSD_CHUNK
    hg = di // N_GROUPS
    npair = hg // LANES
    nheads = hg // HEAD_DIM
    nc = t // q
    n = D_STATE
    cpb = _pick(nc, (4, 2, 1))
    qb = cpb * q

    def body(xs_ref, dt_ref, cs_ref, csr_ref, b_ref, c_ref, y_ref, st_ref, s_scr):
        ci = pl.program_id(1)

        @pl.when(ci == 0)
        def _():
            s_scr[...] = jnp.zeros_like(s_scr)

        tri = lax.broadcasted_iota(jnp.int32, (q, q), 0) >= lax.broadcasted_iota(jnp.int32, (q, q), 1)
        lo = _head_masks(q)
        for cc in range(cpb):
            rows = pl.ds(cc * q, q)
            bb = b_ref[rows, :].astype(BF16)
            cb_ = c_ref[rows, :].astype(BF16)
            cbm = lax.dot_general(cb_, bb, (((1,), (1,)), ((), ())), preferred_element_type=F32)
            csr = csr_ref[0, :, pl.ds(cc * q, q)]
            for p in range(npair):
                sl = pl.ds(p * LANES, LANES)
                x = xs_ref[rows, sl] * dt_ref[rows, sl]
                cs = cs_ref[rows, sl]
                col0, col1 = _pair_cols(cs, lo)
                l0 = jnp.where(tri, jnp.exp(jnp.minimum(col0 - csr[2 * p:2 * p + 1, :], 0.0)), 0.0)
                l1 = jnp.where(tri, jnp.exp(jnp.minimum(col1 - csr[2 * p + 1:2 * p + 2, :], 0.0)), 0.0)
                xb = x.astype(BF16)
                yd = jnp.where(lo, jnp.dot((cbm * l0).astype(BF16), xb, preferred_element_type=F32),
                               jnp.dot((cbm * l1).astype(BF16), xb, preferred_element_type=F32))
                s = s_scr[p]
                st_ref[0, cc, p] = s
                yo = jnp.exp(cs) * jnp.dot(cb_, s.astype(BF16), preferred_element_type=F32)
                y_ref[rows, sl] = yd + yo
                cs_end = cs[q - 1:q, :]
                xd = (x * jnp.exp(cs_end - cs)).astype(BF16)
                s_scr[p] = jnp.exp(cs_end) * s + lax.dot_general(bb, xd, (((0,), (0,)), ((), ())),
                                                                 preferred_element_type=F32)

    return pl.pallas_call(
        body, name=name, grid=(N_GROUPS, nc // cpb),
        in_specs=[pl.BlockSpec((qb, hg), lambda g, c: (c, g)),
                  pl.BlockSpec((qb, hg), lambda g, c: (c, g)),
                  pl.BlockSpec((qb, hg), lambda g, c: (c, g)),
                  pl.BlockSpec((1, nheads, qb), lambda g, c: (g, 0, c)),
                  pl.BlockSpec((qb, n), lambda g, c: (c, g)),
                  pl.BlockSpec((qb, n), lambda g, c: (c, g))],
        out_specs=[pl.BlockSpec((qb, hg), lambda g, c: (c, g)),
                   pl.BlockSpec((1, cpb, npair, n, LANES), lambda g, c: (g, c, 0, 0, 0))],
        out_shape=[jax.ShapeDtypeStruct((t, di), F32),
                   jax.ShapeDtypeStruct((N_GROUPS, nc, npair, n, LANES), F32)],
        scratch_shapes=[pltpu.VMEM((npair, n, LANES), F32)],
        compiler_params=_cparams(dimension_semantics=("parallel", "arbitrary")),
    )(xs, dt_exp, cs_exp, cs_rows, bm, cm)


def _dot3(a, b, dims, split):
    rest = (a, b)[split].astype(F32)
    other = (a, b)[1 - split].astype(BF16)
    acc = None
    for _ in range(3):
        part = rest.astype(BF16)
        rest = rest - part.astype(F32)
        d = (lax.dot_general(part, other, dims, preferred_element_type=F32) if split == 0
             else lax.dot_general(other, part, dims, preferred_element_type=F32))
        acc = d if acc is None else acc + d
    return acc


def _ssd_bwd(xs, dt_exp, cs_exp, cs_rows, bm, cm, dy, states, aneg_exp, *, name):
    t, di = xs.shape
    q = SSD_CHUNK
    hg = di // N_GROUPS
    npair = hg // LANES
    nheads = hg // HEAD_DIM
    nc = t // q
    n = D_STATE
    nt_dims = (((1,), (1,)), ((), ()))
    tn_dims = (((0,), (0,)), ((), ()))

    mm_dims = (((1,), (0,)), ((), ()))

    def body(xs_ref, dt_ref, cs_ref, csr_ref, b_ref, c_ref, dy_ref, st_ref, an_ref,
             dxp_ref, db_ref, dc_ref, ddt_ref, dan_ref, r_scr):
        ci = pl.program_id(1)

        @pl.when(ci == 0)
        def _():
            r_scr[...] = jnp.zeros_like(r_scr)
            dan_ref[...] = jnp.zeros_like(dan_ref)

        bb = b_ref[...].astype(BF16)
        cb_ = c_ref[...].astype(BF16)
        cbm = lax.dot_general(cb_, bb, nt_dims, preferred_element_type=F32)
        row = lax.broadcasted_iota(jnp.int32, (q, q), 0)
        col = lax.broadcasted_iota(jnp.int32, (q, q), 1)
        tri = row >= col
        triu = (row <= col).astype(F32)
        trisl = (row > col).astype(F32)
        ones2 = (lax.broadcasted_iota(jnp.int32, (LANES, LANES), 0) // HEAD_DIM
                 == lax.broadcasted_iota(jnp.int32, (LANES, LANES), 1) // HEAD_DIM).astype(BF16)
        onesq = jnp.ones((q, LANES), BF16)
        last = lax.broadcasted_iota(jnp.int32, (q, LANES), 0) == q - 1
        lo = _head_masks(q)
        csr = csr_ref[0]
        dcb = jnp.zeros((q, q), F32)
        dc_acc = jnp.zeros((q, n), F32)
        db_acc = jnp.zeros((q, n), F32)
        for p in range(npair):
            sl = pl.ds(p * LANES, LANES)
            xsv = xs_ref[:, sl]
            dtv = dt_ref[:, sl]
            x = xsv * dtv
            cs = cs_ref[:, sl]
            dyv = dy_ref[:, sl]
            col0, col1 = _pair_cols(cs, lo)
            l0 = jnp.where(tri, jnp.exp(jnp.minimum(col0 - csr[2 * p:2 * p + 1, :], 0.0)), 0.0)
            l1 = jnp.where(tri, jnp.exp(jnp.minimum(col1 - csr[2 * p + 1:2 * p + 2, :], 0.0)), 0.0)
            xb = x.astype(BF16)
            dyb = dyv.astype(BF16)
            g0 = lax.dot_general(jnp.where(lo, dyv, 0.0).astype(BF16), xb, nt_dims, preferred_element_type=F32)
            g1 = lax.dot_general(jnp.where(lo, 0.0, dyv).astype(BF16), xb, nt_dims, preferred_element_type=F32)
            gl0, gl1 = g0 * l0, g1 * l1
            dcb = dcb + gl0 + gl1
            w0, w1 = cbm * gl0, cbm * gl1
            dxd = jnp.where(lo,
                            lax.dot_general((cbm * l0).astype(BF16), dyb, tn_dims, preferred_element_type=F32),
                            lax.dot_general((cbm * l1).astype(BF16), dyb, tn_dims, preferred_element_type=F32))
            e = jnp.exp(cs)
            cs_end = cs[q - 1:q, :]
            dte = jnp.exp(cs_end - cs)
            dend = jnp.exp(cs_end)
            sf = st_ref[0, 0, p]
            sb = sf.astype(BF16)
            r = r_scr[p]
            rb = r.astype(BF16)
            dyeb = (dyv * e).astype(BF16)
            dc_acc = dc_acc + lax.dot_general(dyeb, sb, nt_dims, preferred_element_type=F32)
            dxo = dte * jnp.dot(bb, rb, preferred_element_type=F32)
            db_acc = db_acc + lax.dot_general((x * dte).astype(BF16), rb, nt_dims, preferred_element_type=F32)
            r_scr[p] = dend * r + lax.dot_general(cb_, dyeb, tn_dims, preferred_element_type=F32)
            dx = dxd + dxo
            dxp_ref[:, sl] = dx
            yoff = e * jnp.dot(cb_, sb, preferred_element_type=F32)
            w0b, w1b = w0.astype(BF16), w1.astype(BF16)
            rw = jnp.where(lo, jnp.sum(w0b.astype(F32), axis=1, keepdims=True), jnp.sum(w1b.astype(F32), axis=1, keepdims=True))
            cw = jnp.where(lo, lax.dot_general(w0b, onesq, tn_dims, preferred_element_type=F32),
                           lax.dot_general(w1b, onesq, tn_dims, preferred_element_type=F32))
            through = jnp.where(last, dend * _rowsum(r * sf), 0.0)
            suf = jnp.dot((dyv * yoff + through).astype(BF16), ones2, preferred_element_type=F32) + rw - cw
            pre = jnp.dot((dxo * x).astype(BF16), ones2, preferred_element_type=F32)
            da = _dot3(triu, suf, mm_dims, 1) + _dot3(trisl, pre, mm_dims, 1)
            qs = jnp.dot((dx * xsv).astype(BF16), ones2, preferred_element_type=F32)
            ddt_ref[:, sl] = da * an_ref[:, sl] + qs
            dan_ref[:, sl] += _rowsum(da * dtv)
        dcbb = dcb.astype(BF16)
        dc_ref[...] = dc_acc + jnp.dot(dcbb, bb, preferred_element_type=F32)
        db_ref[...] = db_acc + lax.dot_general(dcbb, cb_, tn_dims, preferred_element_type=F32)

    rev = lambda g, c: (nc - 1 - c, g)
    return pl.pallas_call(
        body, name=name, grid=(N_GROUPS, nc),
        in_specs=[pl.BlockSpec((q, hg), rev), pl.BlockSpec((q, hg), rev), pl.BlockSpec((q, hg), rev),
                  pl.BlockSpec((1, nheads, q), lambda g, c: (g, 0, nc - 1 - c)),
                  pl.BlockSpec((q, n), rev), pl.BlockSpec((q, n), rev),
                  pl.BlockSpec((q, hg), rev),
                  pl.BlockSpec((1, 1, npair, n, LANES), lambda g, c: (g, nc - 1 - c, 0, 0, 0)),
                  pl.BlockSpec((1, hg), lambda g, c: (0, g))],
        out_specs=[pl.BlockSpec((q, hg), rev), pl.BlockSpec((q, n), rev), pl.BlockSpec((q, n), rev),
                   pl.BlockSpec((q, hg), rev), pl.BlockSpec((1, hg), lambda g, c: (0, g))],
        out_shape=[jax.ShapeDtypeStruct((t, di), F32), jax.ShapeDtypeStruct((t, N_GROUPS * n), F32),
                   jax.ShapeDtypeStruct((t, N_GROUPS * n), F32), jax.ShapeDtypeStruct((t, di), F32),
                   jax.ShapeDtypeStruct((1, di), F32)],
        scratch_shapes=[pltpu.VMEM((npair, n, LANES), F32)],
        compiler_params=_cparams(dimension_semantics=("parallel", "arbitrary")),
    )(xs, dt_exp, cs_exp, cs_rows, bm, cm, dy, states, aneg_exp)


def _group_stats(w, gw):
    return [lax.rsqrt(jnp.mean(w[:, i * gw:(i + 1) * gw] ** 2, axis=-1, keepdims=True) + RMS_EPS)
            for i in range(N_GROUPS)]


def _gated_norm_fwd(y_ssd, xs, z, d_exp, g, *, name):
    di = xs.shape[1]
    gw = di // N_GROUPS

    def fn(yb, xb, zb, db, gb):
        w = (yb + db * xb) * _silu(zb)
        rs = _group_stats(w, gw)
        return [jnp.concatenate([w[:, i * gw:(i + 1) * gw] * rs[i] for i in range(N_GROUPS)], axis=1) * gb], []
    (o,), _ = _rowwise(fn, [y_ssd, xs, (z, di, 0)], [d_exp, g], [(di, BF16)], [], tm=128, name=name)
    return o


def _gated_norm_bwd(y_ssd, xs, z, d_exp, g, do, *, name):
    di = xs.shape[1]
    gw = di // N_GROUPS

    def fn(yb, xb, zb, dob, db, gb):
        yy = yb + db * xb
        sz = _silu(zb)
        w = yy * sz
        rs = _group_stats(w, gw)
        dwh = dob * gb
        wh_parts, dw_parts = [], []
        for i in range(N_GROUPS):
            sl = slice(i * gw, (i + 1) * gw)
            wh = w[:, sl] * rs[i]
            wh_parts.append(wh)
            dw_parts.append(rs[i] * (dwh[:, sl] - wh * jnp.mean(dwh[:, sl] * wh, axis=-1, keepdims=True)))
        wh = jnp.concatenate(wh_parts, axis=1)
        dw = jnp.concatenate(dw_parts, axis=1)
        dy = dw * sz
        dz = dw * yy * _dsilu(zb)
        return [dy, dz], [_rowsum(dob * wh), _rowsum(dy * xb)]
    (dy, dz), (dg, dd) = _rowwise(fn, [y_ssd, xs, (z, di, 0), do], [d_exp, g], [(di, F32), (di, BF16)], [di, di],
                                  tm=128, name=name)
    return dy, dz, dg, dd


def _ssm_act_bwd(conv, dtp, dxp, dy, dbm, dcm, ddt_exp, dt_exp, bias_exp, d_exp, expand, *, di, name):
    gn = dbm.shape[1]

    def fn(cb, dtb, dxpb, dyb, dbb, dcb, ddtb, dteb, bb, db, eb):
        dxs = dxpb * dteb + dyb * db
        dact = jnp.concatenate([dxs, dbb, dcb], axis=1)
        dconv = dact * _dsilu(cb)
        ddtp = ddtb * _sigmoid(_dot3(dtb, eb, (((1,), (0,)), ((), ())), 0) + bb)
        return [dconv, ddtp], [_rowsum(ddtp)]
    (dconv, ddtp), (dbias,) = _rowwise(fn, [conv, dtp, dxp, dy, dbm, dcm, ddt_exp, dt_exp], [bias_exp, d_exp, expand],
                                       [(di + 2 * gn, F32), (di, F32)], [di], tm=64, name=name)
    return dconv, ddtp, dbias


def _adamw(w, g, m, v, *, name):
    r, c = w.shape
    c1 = 1.0 / (1.0 - ADAM_B1 ** ADAM_STEP)
    c2 = 1.0 / (1.0 - ADAM_B2 ** ADAM_STEP)

    def fn(wb, gb, mb, vb):
        mn = ADAM_B1 * mb + (1.0 - ADAM_B1) * gb
        vn = ADAM_B2 * vb + (1.0 - ADAM_B2) * (gb * gb)
        delta = -ADAM_LR * ((mn * c1) / (jnp.sqrt(vn * c2) + ADAM_EPS) + ADAM_WD * wb)
        return [delta, mn, vn], []
    cap = max(8, ADAMW_BLOCK_ELEMS // c)
    tm = _pick(r, [p for p in (512, 256, 128, 64, 32, 16, 8) if p <= cap])
    (d, mn, vn), _ = _rowwise(fn, [w, g, m, v], [], [(c, F32)] * 3, [], tm=tm, name=name)
    return d, mn, vn


def _add_pair(sel, g, r, *, name):
    _, _, rows, cols = g.shape
    tm = _pick(rows, (512, 352, 256, 128, 64, 32, 16))

    def body(s_ref, g_ref, r_ref, o_ref):
        o_ref[...] = (g_ref[...].astype(F32) + r_ref[...].astype(F32)).astype(BF16)

    return pl.pallas_call(
        body, name=name,
        grid_spec=pltpu.PrefetchScalarGridSpec(
            num_scalar_prefetch=1, grid=(N_CHIPS, rows // tm),
            in_specs=[pl.BlockSpec((None, None, tm, cols), lambda j, i, s: (j, s[0], i, 0)),
                      pl.BlockSpec((None, tm, cols), lambda j, i, s: (j, i, 0))],
            out_specs=pl.BlockSpec((None, tm, cols), lambda j, i, s: (j, i, 0))),
        out_shape=jax.ShapeDtypeStruct((N_CHIPS, rows, cols), BF16),
        compiler_params=_cparams(dimension_semantics=("parallel", "parallel")),
    )(sel, g, r)


def _add_four(sel, p, r, *, name):
    _, rows, cols = p.shape
    tm = _pick(rows, (512, 352, 256, 128, 64, 32, 16))

    def body(s_ref, p_ref, r0, r1, r2, o_ref):
        o_ref[...] = ((p_ref[...].astype(F32) + r0[...].astype(F32)) + r1[...].astype(F32)) + r2[...].astype(F32)

    rspec = lambda k: pl.BlockSpec((None, tm, cols), lambda i, s, k=k: (k, i, 0))
    return pl.pallas_call(
        body, name=name,
        grid_spec=pltpu.PrefetchScalarGridSpec(
            num_scalar_prefetch=1, grid=(rows // tm,),
            in_specs=[pl.BlockSpec((None, tm, cols), lambda i, s: (s[0], i, 0)), rspec(0), rspec(1), rspec(2)],
            out_specs=pl.BlockSpec((None, tm, cols), lambda i, s: (s[1], i, 0))),
        out_shape=jax.ShapeDtypeStruct((2, rows, cols), F32),
        compiler_params=_cparams(dimension_semantics=("parallel",)),
    )(sel, p, r, r, r)


def _sum8(g, *, name):
    _, rows, cols = g.shape
    tm = _pick(rows, (512, 256, 128, 64, 32, 16, 8))

    def body(g_ref, o_ref):
        acc = g_ref[0]
        for k in range(1, 8):
            acc = acc + g_ref[k]
        o_ref[...] = acc

    return pl.pallas_call(
        body, name=name, grid=(rows // tm,),
        in_specs=[pl.BlockSpec((8, tm, cols), lambda i: (0, i, 0))],
        out_specs=pl.BlockSpec((tm, cols), lambda i: (i, 0)),
        out_shape=jax.ShapeDtypeStruct((rows, cols), F32),
        compiler_params=_cparams(dimension_semantics=("parallel",)),
    )(g)


def _place():
    x, y, c = lax.axis_index("x"), lax.axis_index("y"), lax.axis_index("c")
    chips = [(1 - x, y), (x, 1 - y), (1 - x, 1 - y)]
    return x, y, c, chips


def _rcopy(src, dst, send_sems, recv_sems, k, to):
    return pltpu.make_async_remote_copy(src_ref=src, dst_ref=dst, send_sem=send_sems.at[k], recv_sem=recv_sems.at[k],
                                        device_id=to, device_id_type=MESH)


def _gather_chips(packs, *, name):
    n = len(packs)

    def body(*refs):
        srcs, outs, (send_sems, recv_sems) = refs[:n], refs[n:2 * n], refs[2 * n:]
        x, y, c, chips = _place()
        sibling = (x, y, 1 - c)
        me = 2 * x + y
        first, passed = [], []
        for t, (src, out) in enumerate(zip(srcs, outs)):
            for k, (cx, cy) in enumerate(chips):
                cp = _rcopy(src.at[c], out.at[me, c], send_sems, recv_sems, 6 * t + k, (cx, cy, c))
                cp.start()
                first.append(cp)
        for t, out in enumerate(outs):
            for k, (cx, cy) in enumerate(chips):
                blk = out.at[2 * cx + cy, c]
                _rcopy(blk, blk, send_sems, recv_sems, 6 * t + k, (cx, cy, c)).wait_recv()
                fw = _rcopy(blk, blk, send_sems, recv_sems, 6 * t + 3 + k, sibling)
                fw.start()
                passed.append(fw)
        for t, out in enumerate(outs):
            for k, (cx, cy) in enumerate(chips):
                blk = out.at[2 * cx + cy, 1 - c]
                _rcopy(blk, blk, send_sems, recv_sems, 6 * t + 3 + k, sibling).wait_recv()
        for cp in first + passed:
            cp.wait_send()

    return pl.pallas_call(
        body, name=name, in_specs=[ANY] * n, out_specs=[ANY] * n,
        out_shape=[jax.ShapeDtypeStruct((N_CHIPS,) + p.shape, p.dtype) for p in packs],
        scratch_shapes=[pltpu.SemaphoreType.DMA((6 * n,)), pltpu.SemaphoreType.DMA((6 * n,))],
    )(*packs)


def _gather_devices(pack, *, name):
    rows, cols = pack.shape

    def body(src, out, send_sems, recv_sems, local_sem):
        x, y, c, chips = _place()
        sibling = (x, y, 1 - c)

        def blk(px, py, pc):
            return out.at[4 * px + 2 * py + pc]

        mine = pltpu.make_async_copy(src, blk(x, y, c), local_sem)
        mine.start()
        first = [_rcopy(src, blk(x, y, c), send_sems, recv_sems, 0, sibling)]
        first += [_rcopy(src, blk(x, y, c), send_sems, recv_sems, 1 + k, (cx, cy, c)) for k, (cx, cy) in enumerate(chips)]
        for cp in first:
            cp.start()
        passed = []
        for k, (cx, cy) in enumerate(chips):
            b = blk(cx, cy, c)
            _rcopy(b, b, send_sems, recv_sems, 1 + k, (cx, cy, c)).wait_recv()
            fw = _rcopy(b, b, send_sems, recv_sems, 4 + k, sibling)
            fw.start()
            passed.append(fw)
        b = blk(x, y, 1 - c)
        _rcopy(b, b, send_sems, recv_sems, 0, sibling).wait_recv()
        for k, (cx, cy) in enumerate(chips):
            b = blk(cx, cy, 1 - c)
            _rcopy(b, b, send_sems, recv_sems, 4 + k, sibling).wait_recv()
        for cp in first + passed:
            cp.wait_send()
        mine.wait()

    return pl.pallas_call(
        body, name=name, in_specs=[ANY], out_specs=ANY,
        out_shape=jax.ShapeDtypeStruct((8, rows, cols), pack.dtype),
        scratch_shapes=[pltpu.SemaphoreType.DMA((7,)), pltpu.SemaphoreType.DMA((7,)), pltpu.SemaphoreType.DMA],
    )(pack)


def _swap_halves(gs, *, name):
    n = len(gs)

    def body(*refs):
        srcs, outs, (send_sems, recv_sems) = refs[:n], refs[n:2 * n], refs[2 * n:]
        x, y, c, _ = _place()
        cps = [_rcopy(src.at[j, 1 - c], out.at[j], send_sems, recv_sems, N_CHIPS * t + j, (x, y, 1 - c))
               for t, (src, out) in enumerate(zip(srcs, outs)) for j in range(N_CHIPS)]
        for cp in cps:
            cp.start()
        for cp in cps:
            cp.wait()

    return pl.pallas_call(
        body, name=name, in_specs=[ANY] * n, out_specs=[ANY] * n,
        out_shape=[jax.ShapeDtypeStruct((N_CHIPS,) + g.shape[2:], g.dtype) for g in gs],
        scratch_shapes=[pltpu.SemaphoreType.DMA((N_CHIPS * n,)), pltpu.SemaphoreType.DMA((N_CHIPS * n,))],
    )(*gs)


def _join_halves(rs, *, name):
    n = len(rs)

    def body(*refs):
        srcs, outs, (send_sems, recv_sems) = refs[:n], refs[n:2 * n], refs[2 * n:]
        x, y, c, _ = _place()
        cps = [_rcopy(src.at[c], out.at[c], send_sems, recv_sems, t, (x, y, 1 - c))
               for t, (src, out) in enumerate(zip(srcs, outs))]
        for cp in cps:
            cp.start()
        for t, out in enumerate(outs):
            b = out.at[1 - c]
            _rcopy(b, b, send_sems, recv_sems, t, (x, y, 1 - c)).wait_recv()
        for cp in cps:
            cp.wait_send()

    return pl.pallas_call(
        body, name=name, in_specs=[ANY] * n, out_specs=[ANY] * n,
        out_shape=[jax.ShapeDtypeStruct(r.shape, r.dtype) for r in rs], input_output_aliases={t: t for t in range(n)},
        scratch_shapes=[pltpu.SemaphoreType.DMA((n,)), pltpu.SemaphoreType.DMA((n,))],
    )(*rs)


def _flat_rows(parts, cols):
    flat = jnp.concatenate([p.reshape(-1) for p in parts])
    n = flat.shape[0]
    rows = -(-n // cols)
    unit = 256 if rows > 256 else 8
    rows = unit * (-(-rows // unit))
    return jnp.pad(flat, (0, rows * cols - n)).reshape(rows, cols)


def _expand(v, di):
    return jnp.repeat(v, HEAD_DIM).reshape(1, di)


def kernel(x, norm_mix_g, norm_ffn_g, norm_final_g, cv_w_in, cv_b_in, cv_w_dw, cv_b_dw, cv_ln_g, cv_ln_b, cv_w_out, cv_b_out, ssm_w_in, ssm_w_conv, ssm_b_conv, ssm_dt_bias, ssm_a_log, ssm_d, ssm_norm_g, ssm_w_out, ffn_w_up, ffn_w_dw, ffn_b_dw, ffn_w_down, loss_target, m_norm_mix_g, m_norm_ffn_g, m_norm_final_g, m_cv_w_in, m_cv_b_in, m_cv_w_dw, m_cv_b_dw, m_cv_ln_g, m_cv_ln_b, m_cv_w_out, m_cv_b_out, m_ssm_w_in, m_ssm_w_conv, m_ssm_b_conv, m_ssm_dt_bias, m_ssm_a_log, m_ssm_d, m_ssm_norm_g, m_ssm_w_out, m_ffn_w_up, m_ffn_w_dw, m_ffn_b_dw, m_ffn_w_down, v_norm_mix_g, v_norm_ffn_g, v_norm_final_g, v_cv_w_in, v_cv_b_in, v_cv_w_dw, v_cv_b_dw, v_cv_ln_g, v_cv_ln_b, v_cv_w_out, v_cv_b_out, v_ssm_w_in, v_ssm_w_conv, v_ssm_b_conv, v_ssm_dt_bias, v_ssm_a_log, v_ssm_d, v_ssm_norm_g, v_ssm_w_out, v_ffn_w_up, v_ffn_w_dw, v_ffn_b_dw, v_ffn_w_down):
    weights = dict(norm_mix_g=norm_mix_g, norm_ffn_g=norm_ffn_g, norm_final_g=norm_final_g, cv_w_in=cv_w_in, cv_b_in=cv_b_in, cv_w_dw=cv_w_dw, cv_b_dw=cv_b_dw, cv_ln_g=cv_ln_g, cv_ln_b=cv_ln_b, cv_w_out=cv_w_out, cv_b_out=cv_b_out, ssm_w_in=ssm_w_in, ssm_w_conv=ssm_w_conv, ssm_b_conv=ssm_b_conv, ssm_dt_bias=ssm_dt_bias, ssm_a_log=ssm_a_log, ssm_d=ssm_d, ssm_norm_g=ssm_norm_g, ssm_w_out=ssm_w_out, ffn_w_up=ffn_w_up, ffn_w_dw=ffn_w_dw, ffn_b_dw=ffn_b_dw, ffn_w_down=ffn_w_down)
    mom_m = dict(norm_mix_g=m_norm_mix_g, norm_ffn_g=m_norm_ffn_g, norm_final_g=m_norm_final_g, cv_w_in=m_cv_w_in, cv_b_in=m_cv_b_in, cv_w_dw=m_cv_w_dw, cv_b_dw=m_cv_b_dw, cv_ln_g=m_cv_ln_g, cv_ln_b=m_cv_ln_b, cv_w_out=m_cv_w_out, cv_b_out=m_cv_b_out, ssm_w_in=m_ssm_w_in, ssm_w_conv=m_ssm_w_conv, ssm_b_conv=m_ssm_b_conv, ssm_dt_bias=m_ssm_dt_bias, ssm_a_log=m_ssm_a_log, ssm_d=m_ssm_d, ssm_norm_g=m_ssm_norm_g, ssm_w_out=m_ssm_w_out, ffn_w_up=m_ffn_w_up, ffn_w_dw=m_ffn_w_dw, ffn_b_dw=m_ffn_b_dw, ffn_w_down=m_ffn_w_down)
    mom_v = dict(norm_mix_g=v_norm_mix_g, norm_ffn_g=v_norm_ffn_g, norm_final_g=v_norm_final_g, cv_w_in=v_cv_w_in, cv_b_in=v_cv_b_in, cv_w_dw=v_cv_w_dw, cv_b_dw=v_cv_b_dw, cv_ln_g=v_cv_ln_g, cv_ln_b=v_cv_ln_b, cv_w_out=v_cv_w_out, cv_b_out=v_cv_b_out, ssm_w_in=v_ssm_w_in, ssm_w_conv=v_ssm_w_conv, ssm_b_conv=v_ssm_b_conv, ssm_dt_bias=v_ssm_dt_bias, ssm_a_log=v_ssm_a_log, ssm_d=v_ssm_d, ssm_norm_g=v_ssm_norm_g, ssm_w_out=v_ssm_w_out, ffn_w_up=v_ffn_w_up, ffn_w_dw=v_ffn_w_dw, ffn_b_dw=v_ffn_b_dw, ffn_w_down=v_ffn_w_down)
    names = list(weights)

    xt = x[0]
    tgt = loss_target[0]
    t, d = xt.shape
    depth = norm_mix_g.shape[0]
    n_cv, n_ssm = cv_w_in.shape[0], ssm_w_in.shape[0]
    di = ssm_w_out.shape[1] * N_CHIPS
    n_heads = di // HEAD_DIM
    gn = N_GROUPS * D_STATE
    ssm_in = ssm_w_in.shape[2] * N_CHIPS
    chip = 2 * lax.axis_index("x") + lax.axis_index("y")

    cq = ssm_w_in.shape[2]
    cqp = LANES * (-(-cq // LANES))
    ssm_inp = N_CHIPS * cqp
    by_col = ("cv_w_in", "ffn_w_up")
    big_names = ("cv_w_in", "cv_w_out", "ssm_w_in", "ssm_w_out", "ffn_w_up", "ffn_w_down")

    def layer_tensors(i):
        mixer = [("cv_w_in", i // 2), ("cv_w_out", i // 2)] if i % 2 == 0 else [("ssm_w_in", i // 2), ("ssm_w_out", i // 2)]
        return mixer + [("ffn_w_up", i), ("ffn_w_down", i)]

    def halves(a):
        return a.reshape((2, a.shape[0] // 2) + a.shape[1:])

    order = [key for i in range(depth) for key in layer_tensors(i)]
    shards = []
    for nm, l in order:
        w = weights[nm][l]
        if nm == "ssm_w_in":
            w = jnp.pad(w, ((0, 0), (0, cqp - cq)))
        shards.append(halves(w.astype(BF16)))
    shards = dict(zip(order, lax.optimization_barrier(shards)))
    full = {}

    def arrived(key, g):
        g = lax.dynamic_update_index_in_dim(g, shards[key], chip, 0)
        g = g.reshape((N_CHIPS, 2 * g.shape[2], g.shape[3]))
        if key[0] == "ssm_w_in":
            full[key] = jnp.concatenate([g[jj, :, :cq] for jj in range(N_CHIPS)] + [jnp.zeros((d, ssm_inp - ssm_in), BF16)],
                                        axis=1)
        else:
            full[key] = g if key[0] in by_col else g.reshape(N_CHIPS * g.shape[1], g.shape[2])

    for key, g in zip(order[:2], _gather_chips([shards[k] for k in order[:2]], name="gather_weights")):
        arrived(key, g)
    landed = {}

    def mm_fwd(key, a, **kw):
        pos = order.index(key)
        kw["b_chips"] = key[0] in by_col
        jobs = []
        if pos + 1 in landed:
            jobs.append(("pass", landed.pop(pos + 1)))
        if pos + 2 < len(order):
            jobs.append(("gather", shards[order[pos + 2]]))
        if not jobs:
            return _matmul(a, full[key], **kw)
        out, *got = _matmul(a, full[key], carry=jobs, **kw)
        for (kind, _), g in zip(jobs, got):
            if kind == "pass":
                arrived(order[pos + 1], g)
            else:
                landed[pos + 2] = g
        return out

    expand = _head_expander(n_heads)

    small_sharded = ["cv_w_dw", "ssm_w_conv", "ssm_b_conv", "ssm_norm_g", "ffn_w_dw"]
    spack = _flat_rows([weights[nm] for nm in small_sharded], LANES)
    sg = _gather_devices(spack, name="gather_small").reshape(8, -1)[::2]
    o = 0
    for nm in small_sharded:
        shp = weights[nm].shape
        n = weights[nm].size
        full[nm] = jnp.concatenate([sg[j, o:o + n].reshape(shp) for j in range(N_CHIPS)], axis=-1)
        o += n

    row = lambda v: v.reshape(1, -1)

    saved = []
    xc = xt
    for i in range(depth):
        j = i // 2
        s = {"x_in": xc}
        h = _rms_fwd(xc, row(norm_mix_g[i]), name="rms_mix_fwd")
        s["h"] = h
        if i % 2 == 0:
            u = mm_fwd(("cv_w_in", j), h, name="cv_in_fwd")
            v1 = _glu_fwd(u, row(cv_b_in[j]), name="cv_glu_fwd")
            v2 = _dwconv_fwd(v1, full["cv_w_dw"][j], row(cv_b_dw[j]), name="cv_dw_fwd")
            v4 = _ln_silu_fwd(v2, row(cv_ln_g[j]), row(cv_ln_b[j]), name="cv_ln_fwd")
            xc = mm_fwd(("cv_w_out", j), v4, bias=row(cv_b_out[j]), res=xc, name="cv_out_fwd")
            s.update(u=u, v1=v1, v2=v2, v4=v4)
        else:
            zx = mm_fwd(("ssm_w_in", j), h, name="ssm_in_fwd")
            xbc_cols = (di, di + 2 * gn)
            dtp = zx[:, 2 * di + 2 * gn:ssm_in]
            conv = _dwconv_fwd(zx, full["ssm_w_conv"][j], row(full["ssm_b_conv"][j]), cols=xbc_cols, name="ssm_dw_fwd")
            bias_exp = _expand(ssm_dt_bias[j], di)
            aneg_exp = _expand(-jnp.exp(ssm_a_log[j]), di)
            d_exp = _expand(ssm_d[j], di)
            xs, bm, cm, dt_exp, cs_exp = _ssm_act(conv, dtp, bias_exp, aneg_exp, expand, di=di, name="ssm_act_fwd")
            cs_rows = cs_exp[:, ::HEAD_DIM].T.reshape(N_GROUPS, n_heads // N_GROUPS, t)
            y_ssd, states = _ssd_fwd(xs, dt_exp, cs_exp, cs_rows, bm, cm, name="ssd_fwd")
            gnrm = _gated_norm_fwd(y_ssd, xs, zx, d_exp, row(full["ssm_norm_g"][j]), name="ssm_norm_fwd")
            xc = mm_fwd(("ssm_w_out", j), gnrm, res=xc, name="ssm_out_fwd")
            s.update(zx=zx, xbc_cols=xbc_cols, dtp=dtp, conv=conv, bias_exp=bias_exp, aneg_exp=aneg_exp,
                     d_exp=d_exp, xs=xs, bm=bm, cm=cm, dt_exp=dt_exp, cs_exp=cs_exp, cs_rows=cs_rows, y_ssd=y_ssd,
                     states=states, gnrm=gnrm)
        s["x_mid"] = xc
        h2 = _rms_fwd(xc, row(norm_ffn_g[i]), name="rms_ffn_fwd")
        u2 = mm_fwd(("ffn_w_up", i), h2, out_chips=True, name="ffn_up_fwd")
        hm = _ffn_mid_fwd(u2, full["ffn_w_dw"][i], row(ffn_b_dw[i]), name="ffn_mid_fwd")
        xc = mm_fwd(("ffn_w_down", i), hm, res=xc, name="ffn_down_fwd")
        s.update(h2=h2, u2=u2, hm=hm)
        saved.append(s)

    dx, dxb, sq, dg_final = _loss_head(xc, row(norm_final_g), tgt, name="loss_head")
    loss_part = 0.5 / d * jnp.sum(sq)
    gr = {nm: [None] * weights[nm].shape[0] for nm in names if nm != "norm_final_g"}
    reduced = {}
    sel_c = jnp.reshape(lax.axis_index("c"), (1,)).astype(jnp.int32)
    sel_j = jnp.stack([chip, lax.axis_index("c")]).astype(jnp.int32)

    waiting = {}

    def joined(r):
        key, _ = waiting.pop("join")
        reduced[key] = r.reshape(2 * r.shape[1], r.shape[2])[:, :weights[key[0]].shape[2]]

    def backward_pair(key, act, dout, dw_name, dx_name, dout_chips=False):
        col = key[0] in by_col
        jobs = [("join", waiting["join"][1])] if "join" in waiting else []
        g = _matmul(act, dout, ta=True, b_chips=dout_chips, out_chips=col, out_dtype=BF16, carry=jobs, name=dw_name)
        if jobs:
            g, r = g
            joined(r)
        if key[0] == "ssm_w_in":
            g = jnp.stack([jnp.pad(g[:, jj * cq:(jj + 1) * cq], ((0, 0), (0, cqp - cq))) for jj in range(N_CHIPS)])
        elif not col:
            g = g.reshape(N_CHIPS, g.shape[0] // N_CHIPS, g.shape[1])
        g = g.reshape(N_CHIPS, 2, g.shape[1] // 2, g.shape[2])
        (got,) = _swap_halves([g], name="grads_swap")
        pair = _add_pair(sel_c, g, got, name="grads_add2")
        dact, got3 = _matmul(dout, full[key], tb=True, a_chips=dout_chips, b_chips=col, carry=[("scatter", pair)],
                             name=dx_name)
        waiting["join"] = (key, _add_four(sel_j, pair, got3, name="grads_add4"))
        return dact

    for i in reversed(range(depth)):
        j = i // 2
        s = saved[i]
        dhm = backward_pair(("ffn_w_down", i), s["hm"], dxb, "ffn_down_dw", "ffn_down_dx")
        du2b, dw_dw, db_dw = _ffn_mid_bwd(s["u2"], dhm, full["ffn_w_dw"][i], row(ffn_b_dw[i]), name="ffn_mid_bwd")
        gr["ffn_w_dw"][i], gr["ffn_b_dw"][i] = dw_dw, db_dw[0]
        dh2 = backward_pair(("ffn_w_up", i), s["h2"], du2b, "ffn_up_dw", "ffn_up_dx", dout_chips=True)
        dx, dxb, colsum, dg = _rms_bwd(s["x_mid"], row(norm_ffn_g[i]), dh2, dx, name="rms_ffn_bwd")
        gr["norm_ffn_g"][i] = dg[0]
        if i % 2 == 0:
            gr["cv_b_out"][j] = colsum[0]
            dv4 = backward_pair(("cv_w_out", j), s["v4"], dxb, "cv_out_dw", "cv_out_dx")
            dv2, dlg, dlb = _ln_silu_bwd(s["v2"], row(cv_ln_g[j]), row(cv_ln_b[j]), dv4, name="cv_ln_bwd")
            gr["cv_ln_g"][j], gr["cv_ln_b"][j] = dlg[0], dlb[0]
            dv1, dw_dw, db_dw = _dwconv_bwd(s["v1"], dv2, full["cv_w_dw"][j], name="cv_dw_bwd")
            gr["cv_w_dw"][j], gr["cv_b_dw"][j] = dw_dw, db_dw[0]
            du, db_in = _glu_bwd(s["u"], row(cv_b_in[j]), dv1, name="cv_glu_bwd")
            gr["cv_b_in"][j] = db_in[0]
            dh = backward_pair(("cv_w_in", j), s["h"], du, "cv_in_dw", "cv_in_dx")
        else:
            dgn = backward_pair(("ssm_w_out", j), s["gnrm"], dxb, "ssm_out_dw", "ssm_out_dx")
            dy, dz, dng, ddl = _gated_norm_bwd(s["y_ssd"], s["xs"], s["zx"], s["d_exp"], row(full["ssm_norm_g"][j]),
                                               dgn, name="ssm_norm_bwd")
            gr["ssm_norm_g"][j] = dng[0]
            gr["ssm_d"][j] = ddl.reshape(n_heads, HEAD_DIM).sum(axis=1)
            dxp, dbm, dcm, ddt_exp, dan = _ssd_bwd(s["xs"], s["dt_exp"], s["cs_exp"], s["cs_rows"], s["bm"], s["cm"],
                                                   dy, s["states"], s["aneg_exp"], name="ssd_bwd")
            gr["ssm_a_log"][j] = dan[0, ::HEAD_DIM] * s["aneg_exp"][0, ::HEAD_DIM]
            dconv, ddtp, dbias = _ssm_act_bwd(s["conv"], s["dtp"], dxp, dy, dbm, dcm, ddt_exp, s["dt_exp"],
                                              s["bias_exp"], s["d_exp"], expand, di=di, name="ssm_act_bwd")
            gr["ssm_dt_bias"][j] = dbias[0, ::HEAD_DIM]
            dxbc, dw_c, db_c = _dwconv_bwd(s["zx"], dconv, full["ssm_w_conv"][j], dx_dtype=BF16, cols=s["xbc_cols"],
                                           name="ssm_dw_bwd")
            gr["ssm_w_conv"][j], gr["ssm_b_conv"][j] = dw_c, db_c[0]
            dtail = jnp.pad(ddtp[:, ::HEAD_DIM].astype(BF16), ((0, 0), (0, ssm_inp - ssm_in)))
            dzx = jnp.concatenate([dz, dxbc, dtail], axis=1)
            dh = backward_pair(("ssm_w_in", j), s["h"], dzx, "ssm_in_dw", "ssm_in_dx")
        dx, dxb, _, dg = _rms_bwd(s["x_in"], row(norm_mix_g[i]), dh, dx, name="rms_mix_bwd")
        gr["norm_mix_g"][i] = dg[0]

    joined(_join_halves([waiting["join"][1]], name="grads_join")[0])
    grads = {nm: jnp.stack([reduced[nm, l] for l in range(weights[nm].shape[0])]) for nm in big_names}

    small = [nm for nm in names if nm not in grads]
    small_parts = []
    for nm in small:
        small_parts.append(dg_final[0] if nm == "norm_final_g" else jnp.stack(gr[nm]))
    gs_pack = _flat_rows(small_parts + [loss_part.reshape(1)], LANES)
    gs = _sum8(_gather_devices(gs_pack, name="gather_small_grads"), name="sum_small_grads").reshape(-1)
    o = 0
    for nm, p in zip(small, small_parts):
        gfull = gs[o:o + p.size].reshape(p.shape)
        o += p.size
        if nm in small_sharded:
            width = weights[nm].shape[-1]
            gfull = lax.dynamic_slice_in_dim(gfull, chip * width, width, axis=gfull.ndim - 1)
        grads[nm] = gfull
    loss = gs[o]

    delta, new_m, new_v = {}, {}, {}
    for nm in big_names:
        shp = weights[nm].shape
        as2d = lambda a: a.reshape(-1, shp[-1])
        dl, mn, vn = _adamw(as2d(weights[nm]), as2d(grads[nm]), as2d(mom_m[nm]), as2d(mom_v[nm]), name="adamw_" + nm)
        delta[nm], new_m[nm], new_v[nm] = dl.reshape(shp), mn.reshape(shp), vn.reshape(shp)
    pk = lambda dct: _flat_rows([dct[nm] for nm in small], LANES)
    dl, mn, vn = _adamw(pk(weights), pk(grads), pk(mom_m), pk(mom_v), name="adamw_small")
    dl, mn, vn = dl.reshape(-1), mn.reshape(-1), vn.reshape(-1)
    o = 0
    for nm in small:
        shp, n = weights[nm].shape, weights[nm].size
        delta[nm], new_m[nm], new_v[nm] = (a[o:o + n].reshape(shp) for a in (dl, mn, vn))
        o += n

    return (loss, dx[None], *[grads[nm] for nm in names], *[delta[nm] for nm in names],
            *[new_m[nm] for nm in names], *[new_v[nm] for nm in names])
```
